```python
import math
import jax, jax.numpy as jnp
from jax import lax
import numpy as np

D_MODEL = 1024
BATCH = 4
SEQ = 4096
DEPTH = 2

N_META = 16
BLOCK = 128
PAD_FRONT = BLOCK - N_META
HEAD_DIM = 64
FOX_HEADS = 8
SWA_Q_HEADS = 8
SWA_KV_HEADS = 2
WINDOW = 128
N_BUCKETS = 32
MAX_DISTANCE = 128
D_FF = 2816
EPS = 1e-6
NEG = -1e30
FORGET_BIAS_INIT = 2.0
FOX_W = FOX_HEADS * HEAD_DIM
SWA_QW = SWA_Q_HEADS * HEAD_DIM
SWA_KVW = SWA_KV_HEADS * HEAD_DIM
D_IN = 3 * FOX_W + FOX_HEADS + SWA_QW + 2 * SWA_KVW + 2 * D_MODEL

kernel_name = "hybrid_fox_swa_sink_macaron_meta"


def rms_norm(x, g):
    xf = x.astype(jnp.float32)
    ms = jnp.mean(xf * xf, axis=-1, keepdims=True)
    return (xf * lax.rsqrt(ms + EPS) * g.astype(jnp.float32)).astype(x.dtype)


def swiglu(h, w_in, w_out):
    gu = h @ w_in
    gate, up = jnp.split(gu, 2, axis=-1)
    return (jax.nn.silu(gate) * up) @ w_out


def t5_bucket(dist):
    n = jnp.maximum(dist, 0)
    max_exact = N_BUCKETS // 2
    nf = jnp.maximum(n, 1).astype(jnp.float32)
    large = max_exact + (jnp.log(nf / max_exact) / math.log(MAX_DISTANCE / max_exact)
                         * (N_BUCKETS - max_exact)).astype(jnp.int32)
    large = jnp.minimum(large, N_BUCKETS - 1)
    return jnp.where(n < max_exact, n, large)


def fox_attention(q, k, v, log_f):
    b, p, h, dh = q.shape
    nb = p // BLOCK
    c_k = jnp.cumsum(log_f, axis=1).transpose(0, 2, 1)
    k_pos = jnp.arange(p)
    k_valid = k_pos >= PAD_FRONT
    qb = q.reshape(b, nb, BLOCK, h, dh).transpose(1, 0, 2, 3, 4)
    cqb = c_k.reshape(b, h, nb, BLOCK).transpose(2, 0, 1, 3)
    scale = dh ** -0.5

    def one_block(args):
        qi, cqi, i = args
        q_pos = i * BLOCK + jnp.arange(BLOCK)
        s = jnp.einsum('bqhd,bkhd->bhqk', qi, k, preferred_element_type=jnp.float32) * scale
        s = s + cqi[..., :, None] - c_k[:, :, None, :]
        mask = (k_pos[None, :] <= q_pos[:, None]) & k_valid[None, :]
        s = jnp.where(mask[None, None], s, NEG)
        pr = jax.nn.softmax(s, axis=-1)
        return jnp.einsum('bhqk,bkhd->bqhd', pr.astype(v.dtype), v)

    out = lax.map(one_block, (qb, cqb, jnp.arange(nb)))
    return out.transpose(1, 0, 2, 3, 4).reshape(b, p, h * dh)


def swa_attention(q, k, v, sinks, rel_bias_table):
    b, p, hq, dh = q.shape
    kv = k.shape[2]
    r = hq // kv
    nb = p // BLOCK
    scale = dh ** -0.5
    qb = q.reshape(b, nb, BLOCK, kv, r, dh)

    def band(a):
        a_ext = jnp.concatenate([jnp.zeros_like(a[:, :BLOCK]), a], axis=1)
        a_ext = a_ext.reshape(b, nb + 1, BLOCK, kv, dh)
        return jnp.concatenate([a_ext[:, :-1], a_ext[:, 1:]], axis=2)

    kb, vb = band(k), band(v)
    km, vm = k[:, PAD_FRONT:BLOCK], v[:, PAD_FRONT:BLOCK]
    q_pos = jnp.arange(nb)[:, None] * BLOCK + jnp.arange(BLOCK)[None, :]
    band_pos = (jnp.arange(nb)[:, None] - 1) * BLOCK + jnp.arange(2 * BLOCK)[None, :]
    meta_pos = PAD_FRONT + jnp.arange(N_META)
    d_band = q_pos[:, :, None] - band_pos[:, None, :]
    m_band = (d_band >= 0) & (d_band < WINDOW) & (band_pos[:, None, :] >= PAD_FRONT)
    d_meta = q_pos[:, :, None] - meta_pos[None, None, :]
    m_meta = d_meta >= WINDOW
    table = rel_bias_table.astype(jnp.float32)
    bias_band = table[t5_bucket(d_band)].transpose(0, 3, 1, 2).reshape(nb, kv, r, BLOCK, 2 * BLOCK)
    bias_meta = table[t5_bucket(d_meta)].transpose(0, 3, 1, 2).reshape(nb, kv, r, BLOCK, N_META)

    s_band = jnp.einsum('bnqgrd,bnkgd->bngrqk', qb, kb, preferred_element_type=jnp.float32) * scale + bias_band[None]
    s_band = jnp.where(m_band[None, :, None, None], s_band, NEG)
    s_meta = jnp.einsum('bnqgrd,bmgd->bngrqm', qb, km, preferred_element_type=jnp.float32) * scale + bias_meta[None]
    s_meta = jnp.where(m_meta[None, :, None, None], s_meta, NEG)
    s = jnp.concatenate([s_band, s_meta], axis=-1)
    sink = sinks.astype(jnp.float32).reshape(kv, r)[None, None, :, :, None, None]
    mx = jnp.maximum(jnp.max(s, axis=-1, keepdims=True), sink)
    pr = jnp.exp(s - mx)
    pr = pr / (jnp.sum(pr, axis=-1, keepdims=True) + jnp.exp(sink - mx))
    pr = pr.astype(v.dtype)
    out = (jnp.einsum('bngrqk,bnkgd->bnqgrd', pr[..., :2 * BLOCK], vb)
           + jnp.einsum('bngrqm,bmgd->bnqgrd', pr[..., 2 * BLOCK:], vm))
    return out.reshape(b, p, hq * dh)


def token_mixer(h, rel_bias_table, w_in, forget_bias, fox_q_norm, fox_k_norm,
                swa_q_norm, swa_k_norm, swa_sinks, w_branch_fox, w_branch_swa, w_out):
    b, p, _ = h.shape
    proj = h @ w_in
    widths = [FOX_W, FOX_W, FOX_W, FOX_HEADS, SWA_QW, SWA_KVW, SWA_KVW, D_MODEL]
    idx = list(np.cumsum(widths))
    qa, ka, va, fa, qb, kb, vb, ga, gb = jnp.split(proj, idx, axis=-1)
    qa = rms_norm(qa.reshape(b, p, FOX_HEADS, HEAD_DIM), fox_q_norm)
    ka = rms_norm(ka.reshape(b, p, FOX_HEADS, HEAD_DIM), fox_k_norm)
    va = va.reshape(b, p, FOX_HEADS, HEAD_DIM)
    log_f = jax.nn.log_sigmoid(fa.astype(jnp.float32) + forget_bias.astype(jnp.float32))
    qb = rms_norm(qb.reshape(b, p, SWA_Q_HEADS, HEAD_DIM), swa_q_norm)
    kb = rms_norm(kb.reshape(b, p, SWA_KV_HEADS, HEAD_DIM), swa_k_norm)
    vb = vb.reshape(b, p, SWA_KV_HEADS, HEAD_DIM)
    o_fox = fox_attention(qa, ka, va, log_f)
    o_swa = swa_attention(qb, kb, vb, swa_sinks, rel_bias_table)
    y = jax.nn.sigmoid(ga) * (o_fox @ w_branch_fox) + jax.nn.sigmoid(gb) * (o_swa @ w_branch_swa)
    return y @ w_out


def setup_inputs(seed: int = 0) -> dict:
    key = jax.random.key(seed)
    ks = jax.random.split(key, 24)
    f32 = jnp.float32

    def nrm(k, shape, scale):
        return jax.random.normal(k, shape, f32) * scale

    def gain(k, shape):
        return 1.0 + 0.1 * jax.random.normal(k, shape, f32)

    return {
        "x": nrm(ks[0], (BATCH, SEQ, D_MODEL), 1.0),
        "meta_tokens": nrm(ks[1], (N_META, D_MODEL), 1.0),
        "rel_bias_table": nrm(ks[2], (N_BUCKETS, SWA_Q_HEADS), 0.5),
        "ffn1_norm": gain(ks[3], (DEPTH, D_MODEL)),
        "ffn1_w_in": nrm(ks[4], (DEPTH, D_MODEL, 2 * D_FF), D_MODEL ** -0.5),
        "ffn1_w_out": nrm(ks[5], (DEPTH, D_FF, D_MODEL), D_FF ** -0.5),
        "mix_norm": gain(ks[6], (DEPTH, D_MODEL)),
        "w_in": nrm(ks[7], (DEPTH, D_MODEL, D_IN), D_MODEL ** -0.5),
        "forget_bias": FORGET_BIAS_INIT + 0.1 * jax.random.normal(ks[8], (DEPTH, FOX_HEADS), f32),
        "fox_q_norm": gain(ks[9], (DEPTH, HEAD_DIM)),
        "fox_k_norm": gain(ks[10], (DEPTH, HEAD_DIM)),
        "swa_q_norm": gain(ks[11], (DEPTH, HEAD_DIM)),
        "swa_k_norm": gain(ks[12], (DEPTH, HEAD_DIM)),
        "swa_sinks": nrm(ks[13], (DEPTH, SWA_Q_HEADS), 0.5),
        "w_branch_fox": nrm(ks[14], (DEPTH, FOX_W, D_MODEL), FOX_W ** -0.5),
        "w_branch_swa": nrm(ks[15], (DEPTH, SWA_QW, D_MODEL), SWA_QW ** -0.5),
        "w_out": nrm(ks[16], (DEPTH, D_MODEL, D_MODEL), D_MODEL ** -0.5),
        "ffn2_norm": gain(ks[17], (DEPTH, D_MODEL)),
        "ffn2_w_in": nrm(ks[18], (DEPTH, D_MODEL, 2 * D_FF), D_MODEL ** -0.5),
        "ffn2_w_out": nrm(ks[19], (DEPTH, D_FF, D_MODEL), D_FF ** -0.5),
    }


def reference(x, meta_tokens, rel_bias_table, ffn1_norm, ffn1_w_in, ffn1_w_out, mix_norm,
              w_in, forget_bias, fox_q_norm, fox_k_norm, swa_q_norm, swa_k_norm, swa_sinks,
              w_branch_fox, w_branch_swa, w_out, ffn2_norm, ffn2_w_in, ffn2_w_out):
    b = x.shape[0]
    pad = jnp.zeros((b, PAD_FRONT, D_MODEL), x.dtype)
    meta = jnp.broadcast_to(meta_tokens.astype(x.dtype)[None], (b, N_META, D_MODEL))
    h = jnp.concatenate([pad, meta, x], axis=1)
    for l in range(DEPTH):
        h = h + 0.5 * swiglu(rms_norm(h, ffn1_norm[l]), ffn1_w_in[l], ffn1_w_out[l])
        h = h + token_mixer(rms_norm(h, mix_norm[l]), rel_bias_table, w_in[l], forget_bias[l],
                            fox_q_norm[l], fox_k_norm[l], swa_q_norm[l], swa_k_norm[l],
                            swa_sinks[l], w_branch_fox[l], w_branch_swa[l], w_out[l])
        h = h + 0.5 * swiglu(rms_norm(h, ffn2_norm[l]), ffn2_w_in[l], ffn2_w_out[l])
    return h[:, BLOCK:]
```

```python
import functools
import math

import jax
import jax.numpy as jnp
import numpy as np
from jax import lax
from jax.experimental import pallas as pl
from jax.experimental.pallas import tpu as pltpu

F32 = jnp.float32
BF16 = jnp.bfloat16

N_META = 16
BLOCK = 128
PAD_FRONT = BLOCK - N_META
HEAD_DIM = 64
FOX_HEADS = 8
SWA_Q_HEADS = 8
SWA_KV_HEADS = 2
SWA_GROUP = SWA_Q_HEADS // SWA_KV_HEADS
WINDOW = 128
N_BUCKETS = 32
MAX_DISTANCE = 128
EPS = 1e-6
NEG = -1e30
FOX_W = FOX_HEADS * HEAD_DIM
SWA_QW = SWA_Q_HEADS * HEAD_DIM
SWA_KVW = SWA_KV_HEADS * HEAD_DIM
SCALE = HEAD_DIM ** -0.5

LANES = 128
VMEM_LIMIT = 56 * 1024 * 1024
FFN_CHUNK = 256
FOX_TILE = 512
FOX_GROUP = 4
AUG = LANES
SWA_KEYS = 3 * BLOCK
PROJ_TILE = 384
SWA_TILE = 384

_C_QA, _C_KA, _C_VA = 0, FOX_W, 2 * FOX_W
_C_QB = 3 * FOX_W
_C_KB = _C_QB + SWA_QW
_C_VB = _C_KB + SWA_KVW
_C_FA = _C_VB + SWA_KVW


def _params(*sem):
    return pltpu.CompilerParams(dimension_semantics=sem, vmem_limit_bytes=VMEM_LIMIT)


def _const_spec(shape):
    nd = len(shape)
    return pl.BlockSpec(shape, lambda *_: (0,) * nd, pipeline_mode=pl.Buffered(1))


def _pick_tile(n, candidates):
    for c in candidates:
        if n % c == 0:
            return c
    raise ValueError(f"no tile for {n}")


def _rms(x, g):
    ms = jnp.mean(x * x, axis=-1, keepdims=True)
    return x * lax.rsqrt(ms + EPS) * g


def _dot(a, b):
    return jnp.dot(a, b, preferred_element_type=F32)


def _dot_nt(a, b):
    return lax.dot_general(a, b, (((1,), (1,)), ((), ())), preferred_element_type=F32)


def _ffn_body(x_ref, g_ref, win_ref, wout_ref, o_ref, acc_ref, *, d_ff):
    x = x_ref[...]
    xn = _rms(x, g_ref[...]).astype(BF16)
    for c in range(d_ff // FFN_CHUNK):
        lo = c * FFN_CHUNK
        gate = _dot(xn, win_ref[:, lo:lo + FFN_CHUNK])
        up = _dot(xn, win_ref[:, d_ff + lo:d_ff + lo + FFN_CHUNK])
        act = (gate * jax.nn.sigmoid(gate) * up).astype(BF16)
        part = _dot(act, wout_ref[lo:lo + FFN_CHUNK, :])
        if c == 0:
            acc_ref[...] = part
        else:
            acc_ref[...] += part
    o_ref[...] = x + 0.5 * acc_ref[...]


def _ffn(h2, gain, w_in, w_out):
    rows, d = h2.shape
    d_ff = w_out.shape[0]
    tm = _pick_tile(rows, (512, 384, 256, 128))
    return pl.pallas_call(
        functools.partial(_ffn_body, d_ff=d_ff),
        grid=(rows // tm,),
        in_specs=[
            pl.BlockSpec((tm, d), lambda i: (i, 0)),
            _const_spec((1, d)),
            _const_spec((d, 2 * d_ff)),
            _const_spec((d_ff, d)),
        ],
        out_specs=pl.BlockSpec((tm, d), lambda i: (i, 0)),
        out_shape=jax.ShapeDtypeStruct((rows, d), F32),
        scratch_shapes=[pltpu.VMEM((tm, d), F32)],
        compiler_params=_params("arbitrary"),
        name="ffn",
    )(h2, gain, w_in, w_out)


def _split3(x):
    hi = x.astype(BF16).astype(F32)
    r = x - hi
    mid = r.astype(BF16).astype(F32)
    return hi, mid, r - mid


def _proj_body(x_ref, g_ref, w_ref, fb_ref, nfq_ref, nfk_ref, nsq_ref, nsk_ref,
               qf_ref, kf_ref, vf_ref, qs_ref, ks_ref, vs_ref, ga_ref, gb_ref,
               carry_ref, *, d_model):
    t = pl.program_id(1)
    tm = x_ref.shape[1]
    xn = _rms(x_ref[0], g_ref[...]).astype(BF16)

    def seg(lo, width):
        return _dot(xn, w_ref[:, lo:lo + width])

    lf = jax.nn.log_sigmoid(seg(_C_FA, LANES) + fb_ref[...])
    row = lax.broadcasted_iota(jnp.int32, (tm, tm), 0)
    col = lax.broadcasted_iota(jnp.int32, (tm, tm), 1)
    tri = (row >= col).astype(BF16)
    hi, mid, lo3 = _split3(lf)
    c_loc = _dot(tri, hi.astype(BF16)) + _dot(tri, mid.astype(BF16)) + _dot(tri, lo3.astype(BF16))

    @pl.when(t == 0)
    def _():
        carry_ref[...] = jnp.zeros_like(carry_ref)

    c = c_loc + carry_ref[...]
    carry_ref[...] = c[tm - 1:tm, :]

    lane = lax.broadcasted_iota(jnp.int32, (tm, HEAD_DIM), 1)
    qa = seg(_C_QA, FOX_W)
    ka = seg(_C_KA, FOX_W)
    for h in range(FOX_HEADS):
        sl = slice(h * HEAD_DIM, (h + 1) * HEAD_DIM)
        c_hi, c_mid, c_lo = _split3(c[:, h:h + 1])
        aug_q = jnp.where(lane == 0, c_hi, jnp.where(lane == 1, c_mid, jnp.where(
            lane == 2, c_lo, jnp.where(lane < 6, 1.0, 0.0))))
        aug_k = jnp.where(lane < 3, 1.0, jnp.where(lane == 3, -c_hi, jnp.where(
            lane == 4, -c_mid, jnp.where(lane == 5, -c_lo, 0.0))))
        base = h * AUG
        qf_ref[0, :, base:base + HEAD_DIM] = (_rms(qa[:, sl], nfq_ref[...]) * SCALE).astype(BF16)
        qf_ref[0, :, base + HEAD_DIM:base + AUG] = aug_q.astype(BF16)
        kf_ref[0, :, base:base + HEAD_DIM] = _rms(ka[:, sl], nfk_ref[...]).astype(BF16)
        kf_ref[0, :, base + HEAD_DIM:base + AUG] = aug_k.astype(BF16)
    vf_ref[0] = seg(_C_VA, FOX_W).astype(BF16)

    qb = seg(_C_QB, SWA_QW)
    for h in range(SWA_Q_HEADS):
        sl = slice(h * HEAD_DIM, (h + 1) * HEAD_DIM)
        qs_ref[0, :, sl] = (_rms(qb[:, sl], nsq_ref[...]) * SCALE).astype(BF16)
    kb = seg(_C_KB, SWA_KVW)
    vb = seg(_C_VB, SWA_KVW)
    for g in range(SWA_KV_HEADS):
        sl = slice(g * HEAD_DIM, (g + 1) * HEAD_DIM)
        kn = _rms(kb[:, sl], nsk_ref[...]).astype(BF16)
        vg = vb[:, sl].astype(BF16)
        for r in range(SWA_GROUP):
            dst = slice((g * SWA_GROUP + r) * HEAD_DIM, (g * SWA_GROUP + r + 1) * HEAD_DIM)
            ks_ref[0, :, dst] = kn
            vs_ref[0, :, dst] = vg
    c_ga = _C_FA + LANES
    ga_ref[0] = jax.nn.sigmoid(seg(c_ga, d_model)).astype(BF16)
    gb_ref[0] = jax.nn.sigmoid(seg(c_ga + d_model, d_model)).astype(BF16)


def _proj(h3, gain, w_r, fb, nfq, nfk, nsq, nsk):
    b, p, d = h3.shape
    tm = PROJ_TILE
    width = w_r.shape[1]

    def tile(w):
        return pl.BlockSpec((1, tm, w), lambda i, t: (i, t, 0))

    widths = (FOX_HEADS * AUG, FOX_HEADS * AUG, FOX_W, SWA_QW, SWA_QW, SWA_QW, d, d)
    return pl.pallas_call(
        functools.partial(_proj_body, d_model=d),
        grid=(b, p // tm),
        in_specs=[
            tile(d),
            _const_spec((1, d)),
            _const_spec((d, width)),
            _const_spec((1, LANES)),
            _const_spec((1, HEAD_DIM)), _const_spec((1, HEAD_DIM)),
            _const_spec((1, HEAD_DIM)), _const_spec((1, HEAD_DIM)),
        ],
        out_specs=[tile(w) for w in widths],
        out_shape=[jax.ShapeDtypeStruct((b, p, w), BF16) for w in widths],
        scratch_shapes=[pltpu.VMEM((1, LANES), F32)],
        compiler_params=_params("arbitrary", "arbitrary"),
        name="proj",
    )(h3, gain, w_r, fb, nfq, nfk, nsq, nsk)


def _fox_body(q_ref, k_ref, v_ref, o_ref, m_ref, l_ref, acc_ref):
    qi = pl.program_id(2)
    width = FOX_GROUP * HEAD_DIM
    lane_grp = lax.broadcasted_iota(jnp.int32, (1, width), 1) // HEAD_DIM

    def attend(q_start, tq, n_full, diag):
        q = q_ref[0, pl.ds(q_start, tq), :]
        m_ref[:, :tq] = jnp.full((FOX_GROUP, tq, 1), NEG, F32)
        l_ref[:, :tq] = jnp.zeros((FOX_GROUP, tq, 1), F32)
        acc_ref[:tq] = jnp.zeros((tq, width), F32)

        def process(k_start, tk, mask):
            kblk = k_ref[0, pl.ds(k_start, tk), :]
            vblk = v_ref[0, pl.ds(k_start, tk), :]
            ps, vs, alphas = [], [], []
            for h in range(FOX_GROUP):
                s = _dot_nt(q[:, h * AUG:(h + 1) * AUG], kblk[:, h * AUG:(h + 1) * AUG])
                if mask is not None:
                    s = jnp.where(mask, s, NEG)
                m_prev = m_ref[h, :tq]
                m_new = jnp.maximum(m_prev, jnp.max(s, axis=-1, keepdims=True))
                alpha = jnp.exp(m_prev - m_new)
                p = jnp.exp(s - m_new)
                l_ref[h, :tq] = alpha * l_ref[h, :tq] + jnp.sum(p, axis=-1, keepdims=True)
                m_ref[h, :tq] = m_new
                ps.append(p.astype(BF16))
                vs.append(jnp.where(lane_grp == h, vblk, jnp.zeros_like(vblk)))
                alphas.append(alpha)
            pv = _dot(jnp.concatenate(ps, axis=1), jnp.concatenate(vs, axis=0))
            acc_ref[:tq] = acc_ref[:tq] * _by_group(alphas, lane_grp) + pv

        kpos0 = lax.broadcasted_iota(jnp.int32, (tq, BLOCK), 1)
        if diag:
            process(0, BLOCK, kpos0 >= PAD_FRONT)

            def full_tile(j, carry):
                process(pl.multiple_of(BLOCK + j * FOX_TILE, BLOCK), FOX_TILE, None)
                return carry

            lax.fori_loop(0, n_full, full_tile, 0)
            r = lax.broadcasted_iota(jnp.int32, (tq, tq), 0)
            c = lax.broadcasted_iota(jnp.int32, (tq, tq), 1)
            process(q_start, tq, c <= r)
        else:
            r = lax.broadcasted_iota(jnp.int32, (tq, BLOCK), 0)
            process(0, BLOCK, (kpos0 >= PAD_FRONT) & (kpos0 <= r))
        inv = [1.0 / l_ref[h, :tq] for h in range(FOX_GROUP)]
        o_ref[0, pl.ds(q_start, tq), :] = (acc_ref[:tq] * _by_group(inv, lane_grp)).astype(BF16)

    @pl.when(qi == 0)
    def _():
        attend(0, BLOCK, 0, False)

    @pl.when(qi > 0)
    def _():
        attend(pl.multiple_of(BLOCK + (qi - 1) * FOX_TILE, BLOCK), FOX_TILE, qi - 1, True)


def _by_group(cols, lane_grp):
    out = cols[-1]
    for h in range(len(cols) - 2, -1, -1):
        out = jnp.where(lane_grp == h, cols[h], out)
    return out


def _fox(qf, kf, vf):
    b, p, _ = qf.shape
    n_grp = FOX_HEADS // FOX_GROUP
    width = FOX_GROUP * HEAD_DIM
    n_q = (p - BLOCK) // FOX_TILE + 1

    def resident(w):
        return pl.BlockSpec((1, p, w), lambda i, g, q: (i, 0, g))

    return pl.pallas_call(
        _fox_body,
        grid=(b, n_grp, n_q),
        in_specs=[resident(FOX_GROUP * AUG), resident(FOX_GROUP * AUG), resident(width)],
        out_specs=resident(width),
        out_shape=jax.ShapeDtypeStruct((b, p, FOX_W), BF16),
        scratch_shapes=[
            pltpu.VMEM((FOX_GROUP, FOX_TILE, 1), F32),
            pltpu.VMEM((FOX_GROUP, FOX_TILE, 1), F32),
            pltpu.VMEM((FOX_TILE, width), F32),
        ],
        compiler_params=_params("arbitrary", "arbitrary", "arbitrary"),
        name="fox",
    )(qf, kf, vf)


def _t5_bucket_np(dist):
    n = np.maximum(dist, 0)
    max_exact = N_BUCKETS // 2
    nf = np.maximum(n, 1).astype(np.float64)
    large = max_exact + (np.log(nf / max_exact) / math.log(MAX_DISTANCE / max_exact)
                         * (N_BUCKETS - max_exact)).astype(np.int64)
    large = np.minimum(large, N_BUCKETS - 1)
    return np.where(n < max_exact, n, large).astype(np.int32)


def _swa_static_tables():
    q = np.arange(BLOCK)[:, None]
    k = np.arange(SWA_KEYS)[None, :]
    d = BLOCK + q - k
    bucket = _t5_bucket_np(d)
    bucket = np.where((d >= 0) & (d < WINDOW) & (k < 2 * BLOCK), bucket, -1)
    is_meta = k >= 2 * BLOCK + PAD_FRONT
    bucket = np.where(is_meta, N_BUCKETS - 1, bucket)
    return np.broadcast_to(bucket, (BLOCK, SWA_KEYS)).astype(np.int32)


def _swa_body(table_ref, sink_ref, bucket_ref, q_ref, k_ref, v_ref, o_ref, bias_ref, kv_ref):
    g = pl.program_id(1)
    t = pl.program_id(2)
    width = SWA_GROUP * HEAD_DIM
    rows = SWA_GROUP * BLOCK

    @pl.when(t == 0)
    def _():
        bucket = bucket_ref[...]
        for r in range(SWA_GROUP):
            bias = jnp.full((BLOCK, SWA_KEYS), NEG, F32)
            for bkt in range(N_BUCKETS):
                bias = jnp.where(bucket == bkt, table_ref[bkt, g * SWA_GROUP + r], bias)
            bias_ref[r * BLOCK:(r + 1) * BLOCK, :] = bias

    lane_grp = lax.broadcasted_iota(jnp.int32, (1, width), 1) // HEAD_DIM
    slot = lax.broadcasted_iota(jnp.int32, (rows, SWA_KEYS), 1)
    qrow = lax.broadcasted_iota(jnp.int32, (rows, SWA_KEYS), 0) % BLOCK
    sink = jnp.concatenate(
        [jnp.full((BLOCK, 1), sink_ref[g * SWA_GROUP + r], F32) for r in range(SWA_GROUP)], axis=0)
    kv_ref[0, 2 * BLOCK:] = k_ref[0, 0:BLOCK, :]
    kv_ref[1, 2 * BLOCK:] = v_ref[0, 0:BLOCK, :]

    for i in range(SWA_TILE // BLOCK):
        n = t * (SWA_TILE // BLOCK) + i
        prev = pl.multiple_of(jnp.maximum(n - 1, 0) * BLOCK, BLOCK)
        cur = pl.multiple_of(n * BLOCK, BLOCK)
        kv_ref[0, 0:BLOCK] = k_ref[0, pl.ds(prev, BLOCK), :]
        kv_ref[0, BLOCK:2 * BLOCK] = k_ref[0, pl.ds(cur, BLOCK), :]
        kv_ref[1, 0:BLOCK] = v_ref[0, pl.ds(prev, BLOCK), :]
        kv_ref[1, BLOCK:2 * BLOCK] = v_ref[0, pl.ds(cur, BLOCK), :]
        q4 = q_ref[0, i * BLOCK:(i + 1) * BLOCK, :]
        qst = jnp.concatenate(
            [jnp.where(lane_grp == r, q4, jnp.zeros_like(q4)) for r in range(SWA_GROUP)], axis=0)
        s = _dot_nt(qst, kv_ref[0]) + bias_ref[...]
        band_ok = (slot >= PAD_FRONT - (n - 1) * BLOCK) & (slot < 2 * BLOCK)
        meta_ok = ((slot >= 2 * BLOCK + PAD_FRONT)
                   & (n * BLOCK + qrow - (slot - 2 * BLOCK) >= WINDOW))
        s = jnp.where(band_ok | meta_ok, s, NEG)
        mx = jnp.maximum(jnp.max(s, axis=-1, keepdims=True), sink)
        p = jnp.exp(s - mx)
        denom = jnp.sum(p, axis=-1, keepdims=True) + jnp.exp(sink - mx)
        pv = _dot(p.astype(BF16), kv_ref[1]) * (1.0 / denom)
        out = pv[(SWA_GROUP - 1) * BLOCK:]
        for r in range(SWA_GROUP - 2, -1, -1):
            out = jnp.where(lane_grp == r, pv[r * BLOCK:(r + 1) * BLOCK], out)
        o_ref[0, i * BLOCK:(i + 1) * BLOCK, :] = out.astype(BF16)


def _swa(table, sinks, qs, ks, vs):
    b, p, _ = qs.shape
    width = SWA_GROUP * HEAD_DIM
    bucket = jnp.asarray(_swa_static_tables())
    smem = pl.BlockSpec(memory_space=pltpu.SMEM)

    def tile():
        return pl.BlockSpec((1, SWA_TILE, width), lambda i, g, t: (i, t, g))

    def resident():
        return pl.BlockSpec((1, p, width), lambda i, g, t: (i, 0, g))

    return pl.pallas_call(
        _swa_body,
        grid=(b, SWA_KV_HEADS, p // SWA_TILE),
        in_specs=[smem, smem, _const_spec((BLOCK, SWA_KEYS)), tile(), resident(), resident()],
        out_specs=tile(),
        out_shape=jax.ShapeDtypeStruct((b, p, SWA_QW), BF16),
        scratch_shapes=[
            pltpu.VMEM((SWA_GROUP * BLOCK, SWA_KEYS), F32),
            pltpu.VMEM((2, SWA_KEYS, width), BF16),
        ],
        compiler_params=_params("arbitrary", "arbitrary", "arbitrary"),
        name="swa",
    )(table, sinks, bucket, qs, ks, vs)


def _mix_body(h_ref, of_ref, os_ref, ga_ref, gb_ref, wf_ref, ws_ref, wo_ref, o_ref):
    y = (ga_ref[...].astype(F32) * _dot(of_ref[...], wf_ref[...])
         + gb_ref[...].astype(F32) * _dot(os_ref[...], ws_ref[...]))
    o_ref[...] = h_ref[...] + _dot(y.astype(BF16), wo_ref[...])


def _mix_out(h2, o_fox, o_swa, ga, gb, wf, ws, wo):
    rows, d = h2.shape
    tm = _pick_tile(rows, (512, 384, 256, 128))

    def tile(w):
        return pl.BlockSpec((tm, w), lambda i: (i, 0))

    return pl.pallas_call(
        _mix_body,
        grid=(rows // tm,),
        in_specs=[tile(d), tile(FOX_W), tile(SWA_QW), tile(d), tile(d),
                  _const_spec(wf.shape), _const_spec(ws.shape), _const_spec(wo.shape)],
        out_specs=tile(d),
        out_shape=jax.ShapeDtypeStruct((rows, d), F32),
        compiler_params=_params("arbitrary"),
        name="mix_out",
    )(h2, o_fox, o_swa, ga, gb, wf, ws, wo)


def _rearrange_w_in(w, d_model):
    idx = np.cumsum([FOX_W, FOX_W, FOX_W, FOX_HEADS, SWA_QW, SWA_KVW, SWA_KVW, d_model])
    qa, ka, va, fa, qb, kb, vb, ga, gb = jnp.split(w, idx, axis=-1)
    fa = jnp.pad(fa, ((0, 0), (0, LANES - FOX_HEADS)))
    return jnp.concatenate([qa, ka, va, qb, kb, vb, fa, ga, gb], axis=-1).astype(BF16)


def kernel(x, meta_tokens, rel_bias_table, ffn1_norm, ffn1_w_in, ffn1_w_out, mix_norm, w_in, forget_bias, fox_q_norm, fox_k_norm, swa_q_norm, swa_k_norm, swa_sinks, w_branch_fox, w_branch_swa, w_out, ffn2_norm, ffn2_w_in, ffn2_w_out):
    b, seq, d = x.shape
    depth = w_in.shape[0]
    p = BLOCK + seq
    assert seq % FOX_TILE == 0 and p % PROJ_TILE == 0 and p % SWA_TILE == 0
    pad = jnp.zeros((b, PAD_FRONT, d), x.dtype)
    meta = jnp.broadcast_to(meta_tokens.astype(x.dtype)[None], (b, N_META, d))
    h = jnp.concatenate([pad, meta, x], axis=1).reshape(b * p, d)
    table = rel_bias_table.astype(F32)
    for l in range(depth):
        h = _ffn(h, ffn1_norm[l][None], ffn1_w_in[l].astype(BF16), ffn1_w_out[l].astype(BF16))
        fb = jnp.pad(forget_bias[l].astype(F32), (0, LANES - FOX_HEADS))[None]
        qf, kf, vf, qs, ks, vs, ga, gb = _proj(
            h.reshape(b, p, d), mix_norm[l][None], _rearrange_w_in(w_in[l], d), fb,
            fox_q_norm[l][None], fox_k_norm[l][None], swa_q_norm[l][None], swa_k_norm[l][None])
        o_fox = _fox(qf, kf, vf)
        o_swa = _swa(table, swa_sinks[l].astype(F32), qs, ks, vs)
        h = _mix_out(h, o_fox.reshape(b * p, FOX_W), o_swa.reshape(b * p, SWA_QW),
                     ga.reshape(b * p, d), gb.reshape(b * p, d),
                     w_branch_fox[l].astype(BF16), w_branch_swa[l].astype(BF16),
                     w_out[l].astype(BF16))
        h = _ffn(h, ffn2_norm[l][None], ffn2_w_in[l].astype(BF16), ffn2_w_out[l].astype(BF16))
    return h.reshape(b, p, d)[:, BLOCK:]
```

```python
import functools
import math

import jax
import jax.numpy as jnp
import numpy as np
from jax import lax
from jax.experimental import pallas as pl
from jax.experimental.pallas import tpu as pltpu

F32 = jnp.float32
BF16 = jnp.bfloat16

N_META = 16
BLOCK = 128
PAD_FRONT = BLOCK - N_META
HEAD_DIM = 64
FOX_HEADS = 8
SWA_Q_HEADS = 8
SWA_KV_HEADS = 2
SWA_GROUP = SWA_Q_HEADS // SWA_KV_HEADS
WINDOW = 128
N_BUCKETS = 32
MAX_DISTANCE = 128
EPS = 1e-6
NEG = -1e30
FOX_W = FOX_HEADS * HEAD_DIM
SWA_QW = SWA_Q_HEADS * HEAD_DIM
SWA_KVW = SWA_KV_HEADS * HEAD_DIM
SCALE = HEAD_DIM ** -0.5

LANES = 128
VMEM_LIMIT = 56 * 1024 * 1024
FFN_CHUNK = 256
FOX_TILE = 512
FOX_GROUP = 4
AUG = LANES
SWA_KEYS = 3 * BLOCK
PROJ_TILE = 384
SWA_TILE = 384

_C_QA, _C_KA, _C_VA = 0, FOX_W, 2 * FOX_W
_C_QB = 3 * FOX_W
_C_KB = _C_QB + SWA_QW
_C_VB = _C_KB + SWA_KVW
_C_FA = _C_VB + SWA_KVW


def _params(*sem):
    return pltpu.CompilerParams(dimension_semantics=sem, vmem_limit_bytes=VMEM_LIMIT)


def _const_spec(shape):
    nd = len(shape)
    return pl.BlockSpec(shape, lambda *_: (0,) * nd, pipeline_mode=pl.Buffered(1))


def _pick_tile(n, candidates):
    for c in candidates:
        if n % c == 0:
            return c
    raise ValueError(f"no tile for {n}")


def _rms(x, g):
    ms = jnp.mean(x * x, axis=-1, keepdims=True)
    return x * lax.rsqrt(ms + EPS) * g


def _dot(a, b):
    return jnp.dot(a, b, preferred_element_type=F32)


def _dot_nt(a, b):
    return lax.dot_general(a, b, (((1,), (1,)), ((), ())), preferred_element_type=F32)


def _ffn_body(x_ref, g_ref, win_ref, wout_ref, o_ref, acc_ref, *, d_ff):
    x = x_ref[...]
    xn = _rms(x, g_ref[...]).astype(BF16)
    for c in range(d_ff // FFN_CHUNK):
        lo = c * FFN_CHUNK
        gate = _dot(xn, win_ref[:, lo:lo + FFN_CHUNK])
        up = _dot(xn, win_ref[:, d_ff + lo:d_ff + lo + FFN_CHUNK])
        act = (gate * jax.nn.sigmoid(gate) * up).astype(BF16)
        part = _dot(act, wout_ref[lo:lo + FFN_CHUNK, :])
        if c == 0:
            acc_ref[...] = part
        else:
            acc_ref[...] += part
    o_ref[...] = x + 0.5 * acc_ref[...]


def _ffn(h2, gain, w_in, w_out):
    rows, d = h2.shape
    d_ff = w_out.shape[0]
    tm = _pick_tile(rows, (512, 384, 256, 128))
    return pl.pallas_call(
        functools.partial(_ffn_body, d_ff=d_ff),
        grid=(rows // tm,),
        in_specs=[
            pl.BlockSpec((tm, d), lambda i: (i, 0)),
            _const_spec((1, d)),
            _const_spec((d, 2 * d_ff)),
            _const_spec((d_ff, d)),
        ],
        out_specs=pl.BlockSpec((tm, d), lambda i: (i, 0)),
        out_shape=jax.ShapeDtypeStruct((rows, d), F32),
        scratch_shapes=[pltpu.VMEM((tm, d), F32)],
        compiler_params=_params("arbitrary"),
        name="ffn",
    )(h2, gain, w_in, w_out)


def _split3(x):
    hi = x.astype(BF16).astype(F32)
    r = x - hi
    mid = r.astype(BF16).astype(F32)
    return hi, mid, r - mid


def _proj_body(x_ref, g_ref, w_ref, fb_ref, nfq_ref, nfk_ref, nsq_ref, nsk_ref,
               qf_ref, kf_ref, vf_ref, qs_ref, ks_ref, vs_ref, ga_ref, gb_ref,
               carry_ref, *, d_model):
    t = pl.program_id(1)
    tm = x_ref.shape[1]
    xn = _rms(x_ref[0], g_ref[...]).astype(BF16)

    def seg(lo, width):
        return _dot(xn, w_ref[:, lo:lo + width])

    lf = jax.nn.log_sigmoid(seg(_C_FA, LANES) + fb_ref[...])
    row = lax.broadcasted_iota(jnp.int32, (tm, tm), 0)
    col = lax.broadcasted_iota(jnp.int32, (tm, tm), 1)
    tri = (row >= col).astype(BF16)
    hi, mid, lo3 = _split3(lf)
    c_loc = _dot(tri, hi.astype(BF16)) + _dot(tri, mid.astype(BF16)) + _dot(tri, lo3.astype(BF16))

    @pl.when(t == 0)
    def _():
        carry_ref[...] = jnp.zeros_like(carry_ref)

    c = c_loc + carry_ref[...]
    carry_ref[...] = c[tm - 1:tm, :]

    lane = lax.broadcasted_iota(jnp.int32, (tm, HEAD_DIM), 1)
    qa = seg(_C_QA, FOX_W)
    ka = seg(_C_KA, FOX_W)
    for h in range(FOX_HEADS):
        sl = slice(h * HEAD_DIM, (h + 1) * HEAD_DIM)
        c_hi, c_mid, c_lo = _split3(c[:, h:h + 1])
        aug_q = jnp.where(lane == 0, c_hi, jnp.where(lane == 1, c_mid, jnp.where(
            lane == 2, c_lo, jnp.where(lane < 6, 1.0, 0.0))))
        aug_k = jnp.where(lane < 3, 1.0, jnp.where(lane == 3, -c_hi, jnp.where(
            lane == 4, -c_mid, jnp.where(lane == 5, -c_lo, 0.0))))
        base = h * AUG
        qf_ref[0, :, base:base + HEAD_DIM] = (_rms(qa[:, sl], nfq_ref[...]) * SCALE).astype(BF16)
        qf_ref[0, :, base + HEAD_DIM:base + AUG] = aug_q.astype(BF16)
        kf_ref[0, :, base:base + HEAD_DIM] = _rms(ka[:, sl], nfk_ref[...]).astype(BF16)
        kf_ref[0, :, base + HEAD_DIM:base + AUG] = aug_k.astype(BF16)
    vf_ref[0] = seg(_C_VA, FOX_W).T.astype(BF16)

    qb = seg(_C_QB, SWA_QW)
    for h in range(SWA_Q_HEADS):
        sl = slice(h * HEAD_DIM, (h + 1) * HEAD_DIM)
        qs_ref[0, :, sl] = (_rms(qb[:, sl], nsq_ref[...]) * SCALE).astype(BF16)
    kb = seg(_C_KB, SWA_KVW)
    vb = seg(_C_VB, SWA_KVW)
    for g in range(SWA_KV_HEADS):
        sl = slice(g * HEAD_DIM, (g + 1) * HEAD_DIM)
        kn = _rms(kb[:, sl], nsk_ref[...]).astype(BF16)
        vg = vb[:, sl].astype(BF16)
        for r in range(SWA_GROUP):
            dst = slice((g * SWA_GROUP + r) * HEAD_DIM, (g * SWA_GROUP + r + 1) * HEAD_DIM)
            ks_ref[0, :, dst] = kn
            vs_ref[0, :, dst] = vg
    c_ga = _C_FA + LANES
    ga_ref[0] = jax.nn.sigmoid(seg(c_ga, d_model)).astype(BF16)
    gb_ref[0] = jax.nn.sigmoid(seg(c_ga + d_model, d_model)).astype(BF16)


def _proj(h3, gain, w_r, fb, nfq, nfk, nsq, nsk):
    b, p, d = h3.shape
    tm = PROJ_TILE
    width = w_r.shape[1]

    def tile(w):
        return pl.BlockSpec((1, tm, w), lambda i, t: (i, t, 0))

    widths = (FOX_HEADS * AUG, FOX_HEADS * AUG, None, SWA_QW, SWA_QW, SWA_QW, d, d)
    vt_spec = pl.BlockSpec((1, FOX_W, tm), lambda i, t: (i, 0, t))
    return pl.pallas_call(
        functools.partial(_proj_body, d_model=d),
        grid=(b, p // tm),
        in_specs=[
            tile(d),
            _const_spec((1, d)),
            _const_spec((d, width)),
            _const_spec((1, LANES)),
            _const_spec((1, HEAD_DIM)), _const_spec((1, HEAD_DIM)),
            _const_spec((1, HEAD_DIM)), _const_spec((1, HEAD_DIM)),
        ],
        out_specs=[vt_spec if w is None else tile(w) for w in widths],
        out_shape=[jax.ShapeDtypeStruct((b, FOX_W, p) if w is None else (b, p, w), BF16)
                   for w in widths],
        scratch_shapes=[pltpu.VMEM((1, LANES), F32)],
        compiler_params=_params("arbitrary", "arbitrary"),
        name="proj",
    )(h3, gain, w_r, fb, nfq, nfk, nsq, nsk)


def _fox_body(q_ref, k_ref, vt_ref, o_ref, m_ref, l_ref, acc_ref, st_ref):
    qi = pl.program_id(2)

    def attend(q_start, tq, n_full, diag):
        q = q_ref[0, pl.ds(q_start, tq), :]
        m_ref[:, :, :tq] = jnp.full((FOX_GROUP, 1, tq), NEG, F32)
        l_ref[:, :, :tq] = jnp.zeros((FOX_GROUP, 1, tq), F32)
        acc_ref[:, :tq] = jnp.zeros((FOX_GROUP * HEAD_DIM, tq), F32)

        def scores(kblk, h):
            return _dot_nt(kblk[:, h * AUG:(h + 1) * AUG], q[:, h * AUG:(h + 1) * AUG])

        def softmax(h, st, mask):
            if mask is not None:
                st = jnp.where(mask, st, NEG)
            m_prev = m_ref[h, :, :tq]
            m_new = jnp.maximum(m_prev, jnp.max(st, axis=0, keepdims=True))
            alpha = jnp.exp(m_prev - m_new)
            p = jnp.exp(st - m_new)
            l_ref[h, :, :tq] = alpha * l_ref[h, :, :tq] + jnp.sum(p, axis=0, keepdims=True)
            m_ref[h, :, :tq] = m_new
            return p.astype(BF16), alpha

        def weighted_values(h, vt, p_alpha):
            p, alpha = p_alpha
            rows = slice(h * HEAD_DIM, (h + 1) * HEAD_DIM)
            acc_ref[rows, :tq] = alpha * acc_ref[rows, :tq] + _dot(vt[rows, :], p)

        def process(k_start, tk, mask):
            kblk = k_ref[0, pl.ds(k_start, tk), :]
            vt = vt_ref[0, :, pl.ds(k_start, tk)]
            for h in range(FOX_GROUP):
                weighted_values(h, vt, softmax(h, scores(kblk, h), mask))

        def key_tile(j):
            return k_ref[0, pl.ds(pl.multiple_of(BLOCK + j * FOX_TILE, BLOCK), FOX_TILE), :]

        def pipelined(j, mask, prefetch):
            kblk = key_tile(j)
            vt = vt_ref[0, :, pl.ds(pl.multiple_of(BLOCK + j * FOX_TILE, BLOCK), FOX_TILE)]
            st = [st_ref[...], scores(kblk, 1), None, None]
            pa = softmax(0, st[0], mask)
            for h in range(FOX_GROUP):
                if h + 2 < FOX_GROUP:
                    st[h + 2] = scores(kblk, h + 2)
                elif h + 2 == FOX_GROUP and prefetch:
                    st_ref[...] = scores(key_tile(j + 1), 0)
                weighted_values(h, vt, pa)
                if h + 1 < FOX_GROUP:
                    pa = softmax(h + 1, st[h + 1], mask)

        kpos0 = lax.broadcasted_iota(jnp.int32, (BLOCK, tq), 0)
        if diag:
            process(0, BLOCK, kpos0 >= PAD_FRONT)
            st_ref[...] = scores(key_tile(0), 0)

            def full_tile(j, carry):
                pipelined(j, None, True)
                return carry

            lax.fori_loop(0, n_full, full_tile, 0)
            kr = lax.broadcasted_iota(jnp.int32, (tq, tq), 0)
            qc = lax.broadcasted_iota(jnp.int32, (tq, tq), 1)
            pipelined(n_full, kr <= qc, False)
        else:
            qc = lax.broadcasted_iota(jnp.int32, (BLOCK, tq), 1)
            process(0, BLOCK, (kpos0 >= PAD_FRONT) & (kpos0 <= qc))
        out_t = jnp.concatenate(
            [acc_ref[h * HEAD_DIM:(h + 1) * HEAD_DIM, :tq] * (1.0 / l_ref[h, :, :tq])
             for h in range(FOX_GROUP)], axis=0)
        o_ref[0, pl.ds(q_start, tq), :] = out_t.T.astype(BF16)

    @pl.when(qi == 0)
    def _():
        attend(0, BLOCK, 0, False)

    @pl.when(qi > 0)
    def _():
        attend(pl.multiple_of(BLOCK + (qi - 1) * FOX_TILE, BLOCK), FOX_TILE, qi - 1, True)


def _fox(qf, kf, vt):
    b, p, _ = qf.shape
    n_grp = FOX_HEADS // FOX_GROUP
    width = FOX_GROUP * HEAD_DIM
    n_q = (p - BLOCK) // FOX_TILE + 1

    def resident(w):
        return pl.BlockSpec((1, p, w), lambda i, g, q: (i, 0, g))

    return pl.pallas_call(
        _fox_body,
        grid=(b, n_grp, n_q),
        in_specs=[resident(FOX_GROUP * AUG), resident(FOX_GROUP * AUG),
                  pl.BlockSpec((1, width, p), lambda i, g, q: (i, g, 0))],
        out_specs=resident(width),
        out_shape=jax.ShapeDtypeStruct((b, p, FOX_W), BF16),
        scratch_shapes=[
            pltpu.VMEM((FOX_GROUP, 1, FOX_TILE), F32),
            pltpu.VMEM((FOX_GROUP, 1, FOX_TILE), F32),
            pltpu.VMEM((width, FOX_TILE), F32),
            pltpu.VMEM((FOX_TILE, FOX_TILE), F32),
        ],
        compiler_params=_params("arbitrary", "arbitrary", "arbitrary"),
        name="fox",
    )(qf, kf, vt)


def _t5_bucket_np(dist):
    n = np.maximum(dist, 0)
    max_exact = N_BUCKETS // 2
    nf = np.maximum(n, 1).astype(np.float64)
    large = max_exact + (np.log(nf / max_exact) / math.log(MAX_DISTANCE / max_exact)
                         * (N_BUCKETS - max_exact)).astype(np.int64)
    large = np.minimum(large, N_BUCKETS - 1)
    return np.where(n < max_exact, n, large).astype(np.int32)


def _swa_static_tables():
    q = np.arange(BLOCK)[:, None]
    k = np.arange(SWA_KEYS)[None, :]
    d = BLOCK + q - k
    bucket = _t5_bucket_np(d)
    bucket = np.where((d >= 0) & (d < WINDOW) & (k < 2 * BLOCK), bucket, -1)
    is_meta = k >= 2 * BLOCK + PAD_FRONT
    bucket = np.where(is_meta, N_BUCKETS - 1, bucket)
    return np.broadcast_to(bucket, (BLOCK, SWA_KEYS)).astype(np.int32)


def _swa_body(table_ref, sink_ref, bucket_ref, q_ref, k_ref, v_ref, o_ref, bias_ref, kv_ref):
    g = pl.program_id(1)
    t = pl.program_id(2)
    width = SWA_GROUP * HEAD_DIM
    rows = SWA_GROUP * BLOCK

    @pl.when(t == 0)
    def _():
        bucket = bucket_ref[...]
        for r in range(SWA_GROUP):
            bias = jnp.full((BLOCK, SWA_KEYS), NEG, F32)
            for bkt in range(N_BUCKETS):
                bias = jnp.where(bucket == bkt, table_ref[bkt, g * SWA_GROUP + r], bias)
            bias_ref[r * BLOCK:(r + 1) * BLOCK, :] = bias

    lane_grp = lax.broadcasted_iota(jnp.int32, (1, width), 1) // HEAD_DIM
    slot = lax.broadcasted_iota(jnp.int32, (rows, SWA_KEYS), 1)
    qrow = lax.broadcasted_iota(jnp.int32, (rows, SWA_KEYS), 0) % BLOCK
    sink = jnp.concatenate(
        [jnp.full((BLOCK, 1), sink_ref[g * SWA_GROUP + r], F32) for r in range(SWA_GROUP)], axis=0)
    kv_ref[0, 2 * BLOCK:] = k_ref[0, 0:BLOCK, :]
    kv_ref[1, 2 * BLOCK:] = v_ref[0, 0:BLOCK, :]

    for i in range(SWA_TILE // BLOCK):
        n = t * (SWA_TILE // BLOCK) + i
        prev = pl.multiple_of(jnp.maximum(n - 1, 0) * BLOCK, BLOCK)
        cur = pl.multiple_of(n * BLOCK, BLOCK)
        kv_ref[0, 0:BLOCK] = k_ref[0, pl.ds(prev, BLOCK), :]
        kv_ref[0, BLOCK:2 * BLOCK] = k_ref[0, pl.ds(cur, BLOCK), :]
        kv_ref[1, 0:BLOCK] = v_ref[0, pl.ds(prev, BLOCK), :]
        kv_ref[1, BLOCK:2 * BLOCK] = v_ref[0, pl.ds(cur, BLOCK), :]
        q4 = q_ref[0, i * BLOCK:(i + 1) * BLOCK, :]
        qst = jnp.concatenate(
            [jnp.where(lane_grp == r, q4, jnp.zeros_like(q4)) for r in range(SWA_GROUP)], axis=0)
        s = _dot_nt(qst, kv_ref[0]) + bias_ref[...]
        band_ok = (slot >= PAD_FRONT - (n - 1) * BLOCK) & (slot < 2 * BLOCK)
        meta_ok = ((slot >= 2 * BLOCK + PAD_FRONT)
                   & (n * BLOCK + qrow - (slot - 2 * BLOCK) >= WINDOW))
        s = jnp.where(band_ok | meta_ok, s, NEG)
        mx = jnp.maximum(jnp.max(s, axis=-1, keepdims=True), sink)
        p = jnp.exp(s - mx)
        denom = jnp.sum(p, axis=-1, keepdims=True) + jnp.exp(sink - mx)
        pv = _dot(p.astype(BF16), kv_ref[1]) * (1.0 / denom)
        out = pv[(SWA_GROUP - 1) * BLOCK:]
        for r in range(SWA_GROUP - 2, -1, -1):
            out = jnp.where(lane_grp == r, pv[r * BLOCK:(r + 1) * BLOCK], out)
        o_ref[0, i * BLOCK:(i + 1) * BLOCK, :] = out.astype(BF16)


def _swa(table, sinks, qs, ks, vs):
    b, p, _ = qs.shape
    width = SWA_GROUP * HEAD_DIM
    bucket = jnp.asarray(_swa_static_tables())
    smem = pl.BlockSpec(memory_space=pltpu.SMEM)

    def tile():
        return pl.BlockSpec((1, SWA_TILE, width), lambda i, g, t: (i, t, g))

    def resident():
        return pl.BlockSpec((1, p, width), lambda i, g, t: (i, 0, g))

    return pl.pallas_call(
        _swa_body,
        grid=(b, SWA_KV_HEADS, p // SWA_TILE),
        in_specs=[smem, smem, _const_spec((BLOCK, SWA_KEYS)), tile(), resident(), resident()],
        out_specs=tile(),
        out_shape=jax.ShapeDtypeStruct((b, p, SWA_QW), BF16),
        scratch_shapes=[
            pltpu.VMEM((SWA_GROUP * BLOCK, SWA_KEYS), F32),
            pltpu.VMEM((2, SWA_KEYS, width), BF16),
        ],
        compiler_params=_params("arbitrary", "arbitrary", "arbitrary"),
        name="swa",
    )(table, sinks, bucket, qs, ks, vs)


def _mix_body(h_ref, of_ref, os_ref, ga_ref, gb_ref, wf_ref, ws_ref, wo_ref, o_ref):
    y = (ga_ref[...].astype(F32) * _dot(of_ref[...], wf_ref[...])
         + gb_ref[...].astype(F32) * _dot(os_ref[...], ws_ref[...]))
    o_ref[...] = h_ref[...] + _dot(y.astype(BF16), wo_ref[...])


def _mix_out(h2, o_fox, o_swa, ga, gb, wf, ws, wo):
    rows, d = h2.shape
    tm = _pick_tile(rows, (512, 384, 256, 128))

    def tile(w):
        return pl.BlockSpec((tm, w), lambda i: (i, 0))

    return pl.pallas_call(
        _mix_body,
        grid=(rows // tm,),
        in_specs=[tile(d), tile(FOX_W), tile(SWA_QW), tile(d), tile(d),
                  _const_spec(wf.shape), _const_spec(ws.shape), _const_spec(wo.shape)],
        out_specs=tile(d),
        out_shape=jax.ShapeDtypeStruct((rows, d), F32),
        compiler_params=_params("arbitrary"),
        name="mix_out",
    )(h2, o_fox, o_swa, ga, gb, wf, ws, wo)


def _rearrange_w_in(w, d_model):
    idx = np.cumsum([FOX_W, FOX_W, FOX_W, FOX_HEADS, SWA_QW, SWA_KVW, SWA_KVW, d_model])
    qa, ka, va, fa, qb, kb, vb, ga, gb = jnp.split(w, idx, axis=-1)
    fa = jnp.pad(fa, ((0, 0), (0, LANES - FOX_HEADS)))
    return jnp.concatenate([qa, ka, va, qb, kb, vb, fa, ga, gb], axis=-1).astype(BF16)


def kernel(x, meta_tokens, rel_bias_table, ffn1_norm, ffn1_w_in, ffn1_w_out, mix_norm, w_in, forget_bias, fox_q_norm, fox_k_norm, swa_q_norm, swa_k_norm, swa_sinks, w_branch_fox, w_branch_swa, w_out, ffn2_norm, ffn2_w_in, ffn2_w_out):
    b, seq, d = x.shape
    depth = w_in.shape[0]
    p = BLOCK + seq
    assert seq % FOX_TILE == 0 and p % PROJ_TILE == 0 and p % SWA_TILE == 0
    pad = jnp.zeros((b, PAD_FRONT, d), x.dtype)
    meta = jnp.broadcast_to(meta_tokens.astype(x.dtype)[None], (b, N_META, d))
    h = jnp.concatenate([pad, meta, x], axis=1).reshape(b * p, d)
    table = rel_bias_table.astype(F32)
    for l in range(depth):
        h = _ffn(h, ffn1_norm[l][None], ffn1_w_in[l].astype(BF16), ffn1_w_out[l].astype(BF16))
        fb = jnp.pad(forget_bias[l].astype(F32), (0, LANES - FOX_HEADS))[None]
        qf, kf, vf, qs, ks, vs, ga, gb = _proj(
            h.reshape(b, p, d), mix_norm[l][None], _rearrange_w_in(w_in[l], d), fb,
            fox_q_norm[l][None], fox_k_norm[l][None], swa_q_norm[l][None], swa_k_norm[l][None])
        o_fox = _fox(qf, kf, vf)
        o_swa = _swa(table, swa_sinks[l].astype(F32), qs, ks, vs)
        h = _mix_out(h, o_fox.reshape(b * p, FOX_W), o_swa.reshape(b * p, SWA_QW),
                     ga.reshape(b * p, d), gb.reshape(b * p, d),
                     w_branch_fox[l].astype(BF16), w_branch_swa[l].astype(BF16),
                     w_out[l].astype(BF16))
        h = _ffn(h, ffn2_norm[l][None], ffn2_w_in[l].astype(BF16), ffn2_w_out[l].astype(BF16))
    return h.reshape(b, p, d)[:, BLOCK:]
```

```python
import functools
import math

import jax
import jax.numpy as jnp
import numpy as np
from jax import lax
from jax.experimental import pallas as pl
from jax.experimental.pallas import tpu as pltpu

F32 = jnp.float32
BF16 = jnp.bfloat16

N_META = 16
BLOCK = 128
PAD_FRONT = BLOCK - N_META
HEAD_DIM = 64
FOX_HEADS = 8
SWA_Q_HEADS = 8
SWA_KV_HEADS = 2
SWA_GROUP = SWA_Q_HEADS // SWA_KV_HEADS
WINDOW = 128
N_BUCKETS = 32
MAX_DISTANCE = 128
EPS = 1e-6
NEG = -1e30
FOX_W = FOX_HEADS * HEAD_DIM
SWA_QW = SWA_Q_HEADS * HEAD_DIM
SWA_KVW = SWA_KV_HEADS * HEAD_DIM
SCALE = HEAD_DIM ** -0.5

LANES = 128
VMEM_LIMIT = 56 * 1024 * 1024
ROW_TILE = 512
FFN_CHUNK = 256
FOX_TILE = 512
FOX_GROUP = 4
AUG = LANES
AUG_ROWS = 32
FA_ROWS = 16
SWA_KEYS = 3 * BLOCK
SWA_TILE = 384

_R_QA = 0
_R_VA = _R_QA + FOX_W
_R_QB = _R_VA + FOX_W
_R_VB = _R_QB + SWA_QW
_R_FA = _R_VB + SWA_KVW
_R_GA = _R_FA + FA_ROWS


def _params(*sem):
    return pltpu.CompilerParams(dimension_semantics=sem, vmem_limit_bytes=VMEM_LIMIT)


def _const_spec(shape):
    nd = len(shape)
    return pl.BlockSpec(shape, lambda *_: (0,) * nd, pipeline_mode=pl.Buffered(1))


def _dot(a, b):
    return jnp.dot(a, b, preferred_element_type=F32)


def _dot_nt(a, b):
    return lax.dot_general(a, b, (((1,), (1,)), ((), ())), preferred_element_type=F32)


def _dot_tn(a, b):
    return lax.dot_general(a, b, (((0,), (0,)), ((), ())), preferred_element_type=F32)


def _rms_rows(x, g):
    ms = jnp.mean(x * x, axis=-1, keepdims=True)
    return x * lax.rsqrt(ms + EPS) * g


def _ffn_body(x_ref, g_ref, win_ref, wout_ref, o_ref, acc_ref, *, d_ff):
    x = x_ref[...]
    xn = _rms_rows(x, g_ref[...]).astype(BF16)
    for c in range(d_ff // FFN_CHUNK):
        lo = c * FFN_CHUNK
        gate = _dot(xn, win_ref[:, lo:lo + FFN_CHUNK])
        up = _dot(xn, win_ref[:, d_ff + lo:d_ff + lo + FFN_CHUNK])
        act = (gate * jax.nn.sigmoid(gate) * up).astype(BF16)
        part = _dot(act, wout_ref[lo:lo + FFN_CHUNK, :])
        if c == 0:
            acc_ref[...] = part
        else:
            acc_ref[...] += part
    o_ref[...] = x + 0.5 * acc_ref[...]


def _ffn(h2, gain, w_in, w_out):
    rows, d = h2.shape
    d_ff = w_out.shape[0]
    tm = ROW_TILE
    return pl.pallas_call(
        functools.partial(_ffn_body, d_ff=d_ff),
        grid=(rows // tm,),
        in_specs=[
            pl.BlockSpec((tm, d), lambda i: (i, 0)),
            _const_spec((1, d)),
            _const_spec((d, 2 * d_ff)),
            _const_spec((d_ff, d)),
        ],
        out_specs=pl.BlockSpec((tm, d), lambda i: (i, 0)),
        out_shape=jax.ShapeDtypeStruct((rows, d), F32),
        scratch_shapes=[pltpu.VMEM((tm, d), F32)],
        compiler_params=_params("arbitrary"),
        name="ffn",
    )(h2, gain, w_in, w_out)


def _split3(x):
    hi = x.astype(BF16).astype(F32)
    r = x - hi
    mid = r.astype(BF16).astype(F32)
    return hi, mid, r - mid


def _head_rms_cols(x, g):
    ms = jnp.mean(x * x, axis=0, keepdims=True)
    return x * lax.rsqrt(ms + EPS) * g


def _pair_rms_rows(x, g2):
    lane = lax.broadcasted_iota(jnp.int32, x.shape, 1)
    sq = x * x
    first = lane < HEAD_DIM
    s0 = jnp.sum(jnp.where(first, sq, 0.0), axis=-1, keepdims=True)
    s1 = jnp.sum(jnp.where(first, 0.0, sq), axis=-1, keepdims=True)
    ms = jnp.where(first, s0, s1) * (1.0 / HEAD_DIM)
    return x * lax.rsqrt(ms + EPS) * g2


def _proj_body(x_ref, g_ref, wt_ref, wk_ref, fb_ref, nfq_ref, nfk_ref, nsq_ref, nsk_ref,
               qf_ref, kf_ref, vt_ref, qs_ref, ks_ref, vs_ref, ga_ref, gb_ref,
               carry_ref, *, d_model, seq_pad):
    i = pl.program_id(0)
    tm = x_ref.shape[0]
    xn = _rms_rows(x_ref[...], g_ref[...]).astype(BF16)

    def feat(lo, rows):
        return _dot_nt(wt_ref[lo:lo + rows, :], xn)

    lf = jax.nn.log_sigmoid(feat(_R_FA, FA_ROWS) + fb_ref[...])
    r0 = i * tm
    boundary = (r0 // seq_pad + 1) * seq_pad - r0
    src = lax.broadcasted_iota(jnp.int32, (tm, tm), 0)
    dst = lax.broadcasted_iota(jnp.int32, (tm, tm), 1)
    upper = ((src <= dst) & ((src < boundary) == (dst < boundary))).astype(BF16)
    hi, mid, lo3 = _split3(lf)
    c_loc = (_dot(hi.astype(BF16), upper) + _dot(mid.astype(BF16), upper)
             + _dot(lo3.astype(BF16), upper))
    carry = jnp.where(r0 % seq_pad == 0, 0.0, carry_ref[:, 0:1])
    tok = lax.broadcasted_iota(jnp.int32, (1, tm), 1)
    c_t = c_loc + jnp.where(tok < boundary, carry, 0.0)
    carry_ref[...] = jnp.broadcast_to(c_t[:, tm - 1:tm], carry_ref.shape)

    c_pad = jnp.concatenate(
        [jnp.where(lax.broadcasted_iota(jnp.int32, (FA_ROWS, tm), 0) < FOX_HEADS, c_t, 0.0),
         jnp.zeros((LANES - FA_ROWS, tm), F32)], axis=0)
    c_tok = c_pad.T
    lane = lax.broadcasted_iota(jnp.int32, (tm, LANES), 1)
    k_hi, k_mid, k_lo = _split3(c_tok)
    aug_lo = (-k_hi - pltpu.roll(k_mid, FOX_HEADS, 1) - pltpu.roll(k_lo, 2 * FOX_HEADS, 1)
              + jnp.where((lane >= 3 * FOX_HEADS) & (lane < 3 * FOX_HEADS + 3), 1.0, 0.0))
    aug_hi = pltpu.roll(aug_lo, HEAD_DIM, 1)
    first = lane < HEAD_DIM

    kt = _dot(xn, wk_ref[...])
    for j in range(FOX_HEADS // 2):
        kn = _pair_rms_rows(kt[:, j * LANES:(j + 1) * LANES], nfk_ref[...])
        kf_ref[:, (2 * j) * AUG:(2 * j + 1) * AUG] = jnp.where(first, kn, aug_hi).astype(BF16)
        kf_ref[:, (2 * j + 1) * AUG:(2 * j + 2) * AUG] = jnp.where(first, aug_lo, kn).astype(BF16)
    ks_ref[...] = _pair_rms_rows(kt[:, FOX_W:FOX_W + SWA_KVW], nsk_ref[...]).astype(BF16)

    arow = lax.broadcasted_iota(jnp.int32, (AUG_ROWS, tm), 0)
    qa = feat(_R_QA, FOX_W)
    zeros = jnp.zeros((HEAD_DIM - AUG_ROWS, tm), BF16)
    for h in range(FOX_HEADS):
        q_hi, q_mid, q_lo = _split3(c_t[h:h + 1, :])
        onehot = jnp.where((arow % FOX_HEADS == h) & (arow < 3 * FOX_HEADS), 1.0, 0.0)
        aug_q = jnp.where(arow == 3 * FOX_HEADS, q_hi, jnp.where(
            arow == 3 * FOX_HEADS + 1, q_mid, jnp.where(arow == 3 * FOX_HEADS + 2, q_lo, onehot)))
        qn = (_head_rms_cols(qa[h * HEAD_DIM:(h + 1) * HEAD_DIM], nfq_ref[...]) * SCALE).astype(BF16)
        parts = [qn, aug_q.astype(BF16), zeros] if h % 2 == 0 else [aug_q.astype(BF16), zeros, qn]
        qf_ref[h * AUG:(h + 1) * AUG, :] = jnp.concatenate(parts, axis=0)
    vt_ref[...] = feat(_R_VA, FOX_W).astype(BF16)

    qb = feat(_R_QB, SWA_QW)
    for h in range(SWA_Q_HEADS):
        sl = slice(h * HEAD_DIM, (h + 1) * HEAD_DIM)
        qs_ref[sl, :] = (_head_rms_cols(qb[sl], nsq_ref[...]) * SCALE).astype(BF16)
    vs_ref[...] = feat(_R_VB, SWA_KVW).astype(BF16)
    ga_ref[...] = jax.nn.sigmoid(feat(_R_GA, d_model)).astype(BF16)
    gb_ref[...] = jax.nn.sigmoid(feat(_R_GA + d_model, d_model)).astype(BF16)


def _proj(h2, seq_pad, gain, w_feat, w_keys, fb, nfq, nfk, nsq, nsk):
    rows, d = h2.shape
    tm = ROW_TILE

    def tok(w):
        return pl.BlockSpec((tm, w), lambda i: (i, 0)), jax.ShapeDtypeStruct((rows, w), BF16)

    def feat(r):
        return pl.BlockSpec((r, tm), lambda i: (0, i)), jax.ShapeDtypeStruct((r, rows), BF16)

    outs = [feat(FOX_HEADS * AUG), tok(FOX_HEADS * AUG), feat(FOX_W), feat(SWA_QW),
            tok(SWA_KVW), feat(SWA_KVW), feat(d), feat(d)]
    return pl.pallas_call(
        functools.partial(_proj_body, d_model=d, seq_pad=seq_pad),
        grid=(rows // tm,),
        in_specs=[
            pl.BlockSpec((tm, d), lambda i: (i, 0)),
            _const_spec((1, d)),
            _const_spec(w_feat.shape),
            _const_spec(w_keys.shape),
            _const_spec((FA_ROWS, 1)),
            _const_spec((HEAD_DIM, 1)), _const_spec((1, LANES)),
            _const_spec((HEAD_DIM, 1)), _const_spec((1, LANES)),
        ],
        out_specs=[o[0] for o in outs],
        out_shape=[o[1] for o in outs],
        scratch_shapes=[pltpu.VMEM((FA_ROWS, LANES), F32)],
        compiler_params=_params("arbitrary"),
        name="proj",
    )(h2, gain, w_feat, w_keys, fb, nfq, nfk, nsq, nsk)


def _fox_body(q_ref, k_ref, vt_ref, o_ref, m_ref, l_ref, acc_ref, st_ref):
    qi = pl.program_id(2)

    def attend(q_start, tq, n_full, diag):
        q = q_ref[:, pl.ds(q_start, tq)]
        m_ref[:, :, :tq] = jnp.full((FOX_GROUP, 1, tq), NEG, F32)
        l_ref[:, :, :tq] = jnp.zeros((FOX_GROUP, 1, tq), F32)
        acc_ref[:, :tq] = jnp.zeros((FOX_GROUP * HEAD_DIM, tq), F32)

        def scores(kblk, h):
            return _dot(kblk[:, h * AUG:(h + 1) * AUG], q[h * AUG:(h + 1) * AUG, :])

        def softmax(h, st, mask):
            if mask is not None:
                st = jnp.where(mask, st, NEG)
            m_prev = m_ref[h, :, :tq]
            m_new = jnp.maximum(m_prev, jnp.max(st, axis=0, keepdims=True))
            alpha = jnp.exp(m_prev - m_new)
            p = jnp.exp(st - m_new)
            l_ref[h, :, :tq] = alpha * l_ref[h, :, :tq] + jnp.sum(p, axis=0, keepdims=True)
            m_ref[h, :, :tq] = m_new
            return p.astype(BF16), alpha

        def weighted_values(h, vt, p_alpha):
            p, alpha = p_alpha
            rows = slice(h * HEAD_DIM, (h + 1) * HEAD_DIM)
            acc_ref[rows, :tq] = alpha * acc_ref[rows, :tq] + _dot(vt[rows, :], p)

        def process(k_start, tk, mask):
            kblk = k_ref[pl.ds(k_start, tk), :]
            vt = vt_ref[:, pl.ds(k_start, tk)]
            for h in range(FOX_GROUP):
                weighted_values(h, vt, softmax(h, scores(kblk, h), mask))

        def key_tile(j):
            return k_ref[pl.ds(pl.multiple_of(BLOCK + j * FOX_TILE, BLOCK), FOX_TILE), :]

        def pipelined(j, mask, prefetch):
            kblk = key_tile(j)
            vt = vt_ref[:, pl.ds(pl.multiple_of(BLOCK + j * FOX_TILE, BLOCK), FOX_TILE)]
            st = [st_ref[...], scores(kblk, 1), None, None]
            pa = softmax(0, st[0], mask)
            for h in range(FOX_GROUP):
                if h + 2 < FOX_GROUP:
                    st[h + 2] = scores(kblk, h + 2)
                elif h + 2 == FOX_GROUP and prefetch:
                    st_ref[...] = scores(key_tile(j + 1), 0)
                weighted_values(h, vt, pa)
                if h + 1 < FOX_GROUP:
                    pa = softmax(h + 1, st[h + 1], mask)

        kpos0 = lax.broadcasted_iota(jnp.int32, (BLOCK, tq), 0)
        if diag:
            process(0, BLOCK, kpos0 >= PAD_FRONT)
            st_ref[...] = scores(key_tile(0), 0)

            def full_tile(j, carry):
                pipelined(j, None, True)
                return carry

            lax.fori_loop(0, n_full, full_tile, 0)
            kr = lax.broadcasted_iota(jnp.int32, (tq, tq), 0)
            qc = lax.broadcasted_iota(jnp.int32, (tq, tq), 1)
            pipelined(n_full, kr <= qc, False)
        else:
            qc = lax.broadcasted_iota(jnp.int32, (BLOCK, tq), 1)
            process(0, BLOCK, (kpos0 >= PAD_FRONT) & (kpos0 <= qc))
        for h in range(FOX_GROUP):
            rows = slice(h * HEAD_DIM, (h + 1) * HEAD_DIM)
            o_ref[rows, pl.ds(q_start, tq)] = (
                acc_ref[rows, :tq] * (1.0 / l_ref[h, :, :tq])).astype(BF16)

    @pl.when(qi == 0)
    def _():
        attend(0, BLOCK, 0, False)

    @pl.when(qi > 0)
    def _():
        attend(pl.multiple_of(BLOCK + (qi - 1) * FOX_TILE, BLOCK), FOX_TILE, qi - 1, True)


def _fox(qft, kf, vt, batch):
    rows = kf.shape[0]
    p = rows // batch
    n_grp = FOX_HEADS // FOX_GROUP
    width = FOX_GROUP * HEAD_DIM
    n_q = (p - BLOCK) // FOX_TILE + 1

    def feat(r):
        return pl.BlockSpec((r, p), lambda b, g, q: (g, b))

    return pl.pallas_call(
        _fox_body,
        grid=(batch, n_grp, n_q),
        in_specs=[feat(FOX_GROUP * AUG),
                  pl.BlockSpec((p, FOX_GROUP * AUG), lambda b, g, q: (b, g)),
                  feat(width)],
        out_specs=feat(width),
        out_shape=jax.ShapeDtypeStruct((FOX_W, rows), BF16),
        scratch_shapes=[
            pltpu.VMEM((FOX_GROUP, 1, FOX_TILE), F32),
            pltpu.VMEM((FOX_GROUP, 1, FOX_TILE), F32),
            pltpu.VMEM((width, FOX_TILE), F32),
            pltpu.VMEM((FOX_TILE, FOX_TILE), F32),
        ],
        compiler_params=_params("arbitrary", "arbitrary", "arbitrary"),
        name="fox",
    )(qft, kf, vt)


def _t5_bucket_np(dist):
    n = np.maximum(dist, 0)
    max_exact = N_BUCKETS // 2
    nf = np.maximum(n, 1).astype(np.float64)
    large = max_exact + (np.log(nf / max_exact) / math.log(MAX_DISTANCE / max_exact)
                         * (N_BUCKETS - max_exact)).astype(np.int64)
    large = np.minimum(large, N_BUCKETS - 1)
    return np.where(n < max_exact, n, large).astype(np.int32)


def _swa_static_buckets():
    k = np.arange(SWA_KEYS)[:, None]
    q = np.arange(BLOCK)[None, :]
    d = BLOCK + q - k
    bucket = np.where((d >= 0) & (d < WINDOW) & (k < 2 * BLOCK), _t5_bucket_np(d), -1)
    bucket = np.where(k >= 2 * BLOCK + PAD_FRONT, N_BUCKETS - 1, bucket)
    return np.ascontiguousarray(np.broadcast_to(bucket, (SWA_KEYS, BLOCK))).astype(np.int32)


def _swa_body(table_ref, sink_ref, bucket_ref, q_ref, k_ref, vt_ref, o_ref, bias_ref, s_ref):
    g = pl.program_id(1)
    t = pl.program_id(2)
    cols = SWA_GROUP * BLOCK

    @pl.when(t == 0)
    def _():
        bucket = bucket_ref[...]
        for r in range(SWA_GROUP):
            bias = jnp.full((SWA_KEYS, BLOCK), NEG, F32)
            for bkt in range(N_BUCKETS):
                bias = jnp.where(bucket == bkt, table_ref[bkt, g * SWA_GROUP + r], bias)
            bias_ref[:, r * BLOCK:(r + 1) * BLOCK] = bias

    sink = jnp.concatenate(
        [jnp.full((1, BLOCK), sink_ref[g * SWA_GROUP + r], F32) for r in range(SWA_GROUP)], axis=1)
    slot = lax.broadcasted_iota(jnp.int32, (SWA_KEYS, cols), 0)
    qpos = lax.broadcasted_iota(jnp.int32, (SWA_KEYS, cols), 1) % BLOCK
    grp_rows = lax.broadcasted_iota(jnp.int32, (SWA_KVW, cols), 0) // HEAD_DIM
    v_rows = pl.ds(pl.multiple_of(g * HEAD_DIM, HEAD_DIM), HEAD_DIM)
    k_lead = k_ref[0:BLOCK, :]
    v_lead = vt_ref[v_rows, 0:BLOCK]

    for i in range(SWA_TILE // BLOCK):
        n = t * (SWA_TILE // BLOCK) + i
        prev = pl.multiple_of(jnp.maximum(n - 1, 0) * BLOCK, BLOCK)
        cur = pl.multiple_of(n * BLOCK, BLOCK)
        keys = jnp.concatenate(
            [k_ref[pl.ds(prev, BLOCK), :], k_ref[pl.ds(cur, BLOCK), :], k_lead], axis=0)
        vals = jnp.concatenate(
            [vt_ref[v_rows, pl.ds(prev, BLOCK)], vt_ref[v_rows, pl.ds(cur, BLOCK)], v_lead], axis=1)
        q4 = jnp.concatenate(
            [q_ref[r * HEAD_DIM:(r + 1) * HEAD_DIM, i * BLOCK:(i + 1) * BLOCK]
             for r in range(SWA_GROUP)], axis=1)
        q8 = jnp.concatenate([q4, q4], axis=0)
        q8 = jnp.where(grp_rows == g, q8, jnp.zeros_like(q8))
        s_ref[...] = _dot(keys, q8) + bias_ref[...]

        @pl.when(n < 2)
        def _():
            band_ok = (slot >= PAD_FRONT - (n - 1) * BLOCK) & (slot < 2 * BLOCK)
            meta_ok = ((slot >= 2 * BLOCK + PAD_FRONT)
                       & (n * BLOCK + qpos - (slot - 2 * BLOCK) >= WINDOW))
            s_ref[...] = jnp.where(band_ok | meta_ok, s_ref[...], NEG)

        s = s_ref[...]
        mx = jnp.maximum(jnp.max(s, axis=0, keepdims=True), sink)
        p = jnp.exp(s - mx)
        denom = jnp.sum(p, axis=0, keepdims=True) + jnp.exp(sink - mx)
        out = _dot(vals, p.astype(BF16)) * (1.0 / denom)
        for r in range(SWA_GROUP):
            o_ref[r * HEAD_DIM:(r + 1) * HEAD_DIM, i * BLOCK:(i + 1) * BLOCK] = (
                out[:, r * BLOCK:(r + 1) * BLOCK].astype(BF16))


def _swa(table, sinks, qst, ks, vst, batch):
    rows = ks.shape[0]
    p = rows // batch
    n_t = p // SWA_TILE
    width = SWA_GROUP * HEAD_DIM
    bucket = jnp.asarray(_swa_static_buckets())
    smem = pl.BlockSpec(memory_space=pltpu.SMEM)
    tile = pl.BlockSpec((width, SWA_TILE), lambda b, g, t: (g, b * n_t + t))
    return pl.pallas_call(
        _swa_body,
        grid=(batch, SWA_KV_HEADS, n_t),
        in_specs=[smem, smem, _const_spec((SWA_KEYS, BLOCK)), tile,
                  pl.BlockSpec((p, SWA_KVW), lambda b, g, t: (b, 0)),
                  pl.BlockSpec((SWA_KVW, p), lambda b, g, t: (0, b))],
        out_specs=tile,
        out_shape=jax.ShapeDtypeStruct((SWA_QW, rows), BF16),
        scratch_shapes=[
            pltpu.VMEM((SWA_KEYS, SWA_GROUP * BLOCK), F32),
            pltpu.VMEM((SWA_KEYS, SWA_GROUP * BLOCK), F32),
        ],
        compiler_params=_params("arbitrary", "arbitrary", "arbitrary"),
        name="swa",
    )(table, sinks, bucket, qst, ks, vst)


def _mix_body(h_ref, of_ref, os_ref, ga_ref, gb_ref, wf_ref, ws_ref, wo_ref, o_ref):
    y = (ga_ref[...].astype(F32) * _dot(wf_ref[...], of_ref[...])
         + gb_ref[...].astype(F32) * _dot(ws_ref[...], os_ref[...]))
    o_ref[...] = h_ref[...] + _dot_tn(y.astype(BF16), wo_ref[...])


def _mix_out(h2, o_fox, o_swa, ga, gb, wf_t, ws_t, wo):
    rows, d = h2.shape
    tm = ROW_TILE

    def feat(r):
        return pl.BlockSpec((r, tm), lambda i: (0, i))

    tok = pl.BlockSpec((tm, d), lambda i: (i, 0))
    return pl.pallas_call(
        _mix_body,
        grid=(rows // tm,),
        in_specs=[tok, feat(FOX_W), feat(SWA_QW), feat(d), feat(d),
                  _const_spec(wf_t.shape), _const_spec(ws_t.shape), _const_spec(wo.shape)],
        out_specs=tok,
        out_shape=jax.ShapeDtypeStruct((rows, d), F32),
        compiler_params=_params("arbitrary"),
        name="mix_out",
    )(h2, o_fox, o_swa, ga, gb, wf_t, ws_t, wo)


def _split_w_in(w, d_model):
    idx = np.cumsum([FOX_W, FOX_W, FOX_W, FOX_HEADS, SWA_QW, SWA_KVW, SWA_KVW, d_model])
    qa, ka, va, fa, qb, kb, vb, ga, gb = jnp.split(w.astype(BF16), idx, axis=-1)
    fa = jnp.pad(fa, ((0, 0), (0, FA_ROWS - FOX_HEADS)))
    w_feat = jnp.concatenate([qa, va, qb, vb, fa, ga, gb], axis=-1).T
    return w_feat, jnp.concatenate([ka, kb], axis=-1)


def kernel(x, meta_tokens, rel_bias_table, ffn1_norm, ffn1_w_in, ffn1_w_out, mix_norm, w_in, forget_bias, fox_q_norm, fox_k_norm, swa_q_norm, swa_k_norm, swa_sinks, w_branch_fox, w_branch_swa, w_out, ffn2_norm, ffn2_w_in, ffn2_w_out):
    b, seq, d = x.shape
    depth = w_in.shape[0]
    p = BLOCK + seq
    assert seq % FOX_TILE == 0 and p % SWA_TILE == 0 and (b * p) % ROW_TILE == 0
    pad = jnp.zeros((b, PAD_FRONT, d), x.dtype)
    meta = jnp.broadcast_to(meta_tokens.astype(x.dtype)[None], (b, N_META, d))
    h = jnp.concatenate([pad, meta, x], axis=1).reshape(b * p, d)
    table = rel_bias_table.astype(F32)

    def col(v):
        return v.astype(F32)[:, None]

    def twice(v):
        return jnp.tile(v.astype(F32), 2)[None]

    for l in range(depth):
        h = _ffn(h, ffn1_norm[l][None], ffn1_w_in[l].astype(BF16), ffn1_w_out[l].astype(BF16))
        w_feat, w_keys = _split_w_in(w_in[l], d)
        fb = jnp.pad(col(forget_bias[l]), ((0, FA_ROWS - FOX_HEADS), (0, 0)))
        qft, kf, vt, qst, ks, vst, ga, gb = _proj(
            h, p, mix_norm[l][None], w_feat, w_keys, fb,
            col(fox_q_norm[l]), twice(fox_k_norm[l]), col(swa_q_norm[l]), twice(swa_k_norm[l]))
        o_fox = _fox(qft, kf, vt, b)
        o_swa = _swa(table, swa_sinks[l].astype(F32), qst, ks, vst, b)
        h = _mix_out(h, o_fox, o_swa, ga, gb, w_branch_fox[l].astype(BF16).T,
                     w_branch_swa[l].astype(BF16).T, w_out[l].astype(BF16))
        h = _ffn(h, ffn2_norm[l][None], ffn2_w_in[l].astype(BF16), ffn2_w_out[l].astype(BF16))
    return h.reshape(b, p, d)[:, BLOCK:]
```

```python
import functools
import math

import jax
import jax.numpy as jnp
import numpy as np
from jax import lax
from jax.experimental import pallas as pl
from jax.experimental.pallas import tpu as pltpu

F32 = jnp.float32
BF16 = jnp.bfloat16

N_META = 16
BLOCK = 128
PAD_FRONT = BLOCK - N_META
HEAD_DIM = 64
FOX_HEADS = 8
SWA_Q_HEADS = 8
SWA_KV_HEADS = 2
SWA_GROUP = SWA_Q_HEADS // SWA_KV_HEADS
WINDOW = 128
N_BUCKETS = 32
MAX_DISTANCE = 128
EPS = 1e-6
NEG = -1e30
FOX_W = FOX_HEADS * HEAD_DIM
SWA_QW = SWA_Q_HEADS * HEAD_DIM
SWA_KVW = SWA_KV_HEADS * HEAD_DIM
SCALE = HEAD_DIM ** -0.5
LOG2E = math.log2(math.e)

LANES = 128
VMEM_LIMIT = 56 * 1024 * 1024
ROW_TILE = 512
FFN_CHUNK = 256
FOX_TILE = 512
FOX_GROUP = 4
AUG = LANES
AUG_ROWS = 32
FA_ROWS = 16
SWA_KEYS = 3 * BLOCK
SWA_STEPS = 3

_R_QA = 0
_R_VA = _R_QA + FOX_W
_R_QB = _R_VA + FOX_W
_R_VB = _R_QB + SWA_QW
_R_FA = _R_VB + SWA_KVW
_R_GA = _R_FA + FA_ROWS


def _params(*sem):
    return pltpu.CompilerParams(dimension_semantics=sem, vmem_limit_bytes=VMEM_LIMIT)


def _const_spec(shape):
    nd = len(shape)
    return pl.BlockSpec(shape, lambda *_: (0,) * nd, pipeline_mode=pl.Buffered(1))


def _dot(a, b):
    return jnp.dot(a, b, preferred_element_type=F32)


def _dot_nt(a, b):
    return lax.dot_general(a, b, (((1,), (1,)), ((), ())), preferred_element_type=F32)


def _dot_tn(a, b):
    return lax.dot_general(a, b, (((0,), (0,)), ((), ())), preferred_element_type=F32)


def _rms_rows(x, g):
    ms = jnp.mean(x * x, axis=-1, keepdims=True)
    return x * lax.rsqrt(ms + EPS) * g


def _ffn_body(x_ref, g_ref, win_ref, wout_ref, o_ref, acc_ref, *, d_ff):
    x = x_ref[...]
    xn = _rms_rows(x, g_ref[...]).astype(BF16)
    for c in range(d_ff // FFN_CHUNK):
        lo = c * FFN_CHUNK
        gate = _dot(xn, win_ref[:, lo:lo + FFN_CHUNK])
        up = _dot(xn, win_ref[:, d_ff + lo:d_ff + lo + FFN_CHUNK])
        act = (gate * jax.nn.sigmoid(gate) * up).astype(BF16)
        part = _dot(act, wout_ref[lo:lo + FFN_CHUNK, :])
        if c == 0:
            acc_ref[...] = part
        else:
            acc_ref[...] += part
    o_ref[...] = x + 0.5 * acc_ref[...]


def _ffn(h2, gain, w_in, w_out):
    rows, d = h2.shape
    d_ff = w_out.shape[0]
    tm = ROW_TILE
    return pl.pallas_call(
        functools.partial(_ffn_body, d_ff=d_ff),
        grid=(rows // tm,),
        in_specs=[
            pl.BlockSpec((tm, d), lambda i: (i, 0)),
            _const_spec((1, d)),
            _const_spec((d, 2 * d_ff)),
            _const_spec((d_ff, d)),
        ],
        out_specs=pl.BlockSpec((tm, d), lambda i: (i, 0)),
        out_shape=jax.ShapeDtypeStruct((rows, d), F32),
        scratch_shapes=[pltpu.VMEM((tm, d), F32)],
        compiler_params=_params("arbitrary"),
        name="ffn",
    )(h2, gain, w_in, w_out)


def _split3(x):
    hi = x.astype(BF16).astype(F32)
    r = x - hi
    mid = r.astype(BF16).astype(F32)
    return hi, mid, r - mid


def _head_rms_cols(x, g):
    ms = jnp.mean(x * x, axis=0, keepdims=True)
    return x * lax.rsqrt(ms + EPS) * g


def _pair_rms_rows(x, g2):
    lane = lax.broadcasted_iota(jnp.int32, x.shape, 1)
    sq = x * x
    first = lane < HEAD_DIM
    s0 = jnp.sum(jnp.where(first, sq, 0.0), axis=-1, keepdims=True)
    s1 = jnp.sum(jnp.where(first, 0.0, sq), axis=-1, keepdims=True)
    ms = jnp.where(first, s0, s1) * (1.0 / HEAD_DIM)
    return x * lax.rsqrt(ms + EPS) * g2


def _proj_body(x_ref, g_ref, wt_ref, wk_ref, fb_ref, nfq_ref, nfk_ref, nsq_ref, nsk_ref,
               qf_ref, kf_ref, vt_ref, qs_ref, ks_ref, vs_ref, ga_ref, gb_ref,
               carry_ref, *, d_model, seq_pad):
    i = pl.program_id(0)
    tm = x_ref.shape[0]
    xn = _rms_rows(x_ref[...], g_ref[...]).astype(BF16)

    def feat(lo, rows):
        return _dot_nt(wt_ref[lo:lo + rows, :], xn)

    lf = jax.nn.log_sigmoid(feat(_R_FA, FA_ROWS) + fb_ref[...])
    r0 = i * tm
    boundary = (r0 // seq_pad + 1) * seq_pad - r0
    src = lax.broadcasted_iota(jnp.int32, (tm, tm), 0)
    dst = lax.broadcasted_iota(jnp.int32, (tm, tm), 1)
    upper = ((src <= dst) & ((src < boundary) == (dst < boundary))).astype(BF16)
    hi, mid, lo3 = _split3(lf)
    c_loc = (_dot(hi.astype(BF16), upper) + _dot(mid.astype(BF16), upper)
             + _dot(lo3.astype(BF16), upper))
    carry = jnp.where(r0 % seq_pad == 0, 0.0, carry_ref[:, 0:1])
    tok = lax.broadcasted_iota(jnp.int32, (1, tm), 1)
    c_t = c_loc + jnp.where(tok < boundary, carry, 0.0)
    carry_ref[...] = jnp.broadcast_to(c_t[:, tm - 1:tm], carry_ref.shape)
    c_t = c_t * LOG2E

    c_pad = jnp.concatenate(
        [jnp.where(lax.broadcasted_iota(jnp.int32, (FA_ROWS, tm), 0) < FOX_HEADS, c_t, 0.0),
         jnp.zeros((LANES - FA_ROWS, tm), F32)], axis=0)
    c_tok = c_pad.T
    lane = lax.broadcasted_iota(jnp.int32, (tm, LANES), 1)
    k_hi, k_mid, k_lo = _split3(c_tok)
    aug_lo = (-k_hi - pltpu.roll(k_mid, FOX_HEADS, 1) - pltpu.roll(k_lo, 2 * FOX_HEADS, 1)
              + jnp.where((lane >= 3 * FOX_HEADS) & (lane < 3 * FOX_HEADS + 3), 1.0, 0.0))
    aug_hi = pltpu.roll(aug_lo, HEAD_DIM, 1)
    first = lane < HEAD_DIM

    kt = _dot(xn, wk_ref[...])
    for j in range(FOX_HEADS // 2):
        kn = _pair_rms_rows(kt[:, j * LANES:(j + 1) * LANES], nfk_ref[...])
        kf_ref[:, (2 * j) * AUG:(2 * j + 1) * AUG] = jnp.where(first, kn, aug_hi).astype(BF16)
        kf_ref[:, (2 * j + 1) * AUG:(2 * j + 2) * AUG] = jnp.where(first, aug_lo, kn).astype(BF16)
    ks_ref[...] = _pair_rms_rows(kt[:, FOX_W:FOX_W + SWA_KVW], nsk_ref[...]).astype(BF16)

    arow = lax.broadcasted_iota(jnp.int32, (AUG_ROWS, tm), 0)
    qa = feat(_R_QA, FOX_W)
    zeros = jnp.zeros((HEAD_DIM - AUG_ROWS, tm), BF16)
    for h in range(FOX_HEADS):
        q_hi, q_mid, q_lo = _split3(c_t[h:h + 1, :])
        onehot = jnp.where((arow % FOX_HEADS == h) & (arow < 3 * FOX_HEADS), 1.0, 0.0)
        aug_q = jnp.where(arow == 3 * FOX_HEADS, q_hi, jnp.where(
            arow == 3 * FOX_HEADS + 1, q_mid, jnp.where(arow == 3 * FOX_HEADS + 2, q_lo, onehot)))
        qn = (_head_rms_cols(qa[h * HEAD_DIM:(h + 1) * HEAD_DIM], nfq_ref[...])
              * (SCALE * LOG2E)).astype(BF16)
        parts = [qn, aug_q.astype(BF16), zeros] if h % 2 == 0 else [aug_q.astype(BF16), zeros, qn]
        qf_ref[h * AUG:(h + 1) * AUG, :] = jnp.concatenate(parts, axis=0)
    vt_ref[...] = feat(_R_VA, FOX_W).astype(BF16)

    qb = feat(_R_QB, SWA_QW)
    for h in range(SWA_Q_HEADS):
        sl = slice(h * HEAD_DIM, (h + 1) * HEAD_DIM)
        qs_ref[sl, :] = (_head_rms_cols(qb[sl], nsq_ref[...]) * SCALE).astype(BF16)
    vs_ref[...] = feat(_R_VB, SWA_KVW).astype(BF16)
    ga_ref[...] = jax.nn.sigmoid(feat(_R_GA, d_model)).astype(BF16)
    gb_ref[...] = jax.nn.sigmoid(feat(_R_GA + d_model, d_model)).astype(BF16)


def _proj(h2, seq_pad, gain, w_feat, w_keys, fb, nfq, nfk, nsq, nsk):
    rows, d = h2.shape
    tm = ROW_TILE

    def tok(w):
        return pl.BlockSpec((tm, w), lambda i: (i, 0)), jax.ShapeDtypeStruct((rows, w), BF16)

    def feat(r):
        return pl.BlockSpec((r, tm), lambda i: (0, i)), jax.ShapeDtypeStruct((r, rows), BF16)

    outs = [feat(FOX_HEADS * AUG), tok(FOX_HEADS * AUG), feat(FOX_W), feat(SWA_QW),
            tok(SWA_KVW), feat(SWA_KVW), feat(d), feat(d)]
    return pl.pallas_call(
        functools.partial(_proj_body, d_model=d, seq_pad=seq_pad),
        grid=(rows // tm,),
        in_specs=[
            pl.BlockSpec((tm, d), lambda i: (i, 0)),
            _const_spec((1, d)),
            _const_spec(w_feat.shape),
            _const_spec(w_keys.shape),
            _const_spec((FA_ROWS, 1)),
            _const_spec((HEAD_DIM, 1)), _const_spec((1, LANES)),
            _const_spec((HEAD_DIM, 1)), _const_spec((1, LANES)),
        ],
        out_specs=[o[0] for o in outs],
        out_shape=[o[1] for o in outs],
        scratch_shapes=[pltpu.VMEM((FA_ROWS, LANES), F32)],
        compiler_params=_params("arbitrary"),
        name="proj",
    )(h2, gain, w_feat, w_keys, fb, nfq, nfk, nsq, nsk)


def _fox_body(q_ref, k_ref, vt_ref, o_ref, m_ref, l_ref, acc_ref, st_ref):
    qi = pl.program_id(2)

    def attend(q_start, tq, n_full, diag):
        q = q_ref[:, pl.ds(q_start, tq)]
        m_ref[:, :, :tq] = jnp.full((FOX_GROUP, 1, tq), NEG, F32)
        l_ref[:, :, :tq] = jnp.zeros((FOX_GROUP, 1, tq), F32)
        acc_ref[:, :tq] = jnp.zeros((FOX_GROUP * HEAD_DIM, tq), F32)

        def scores(kblk, h):
            return _dot(kblk[:, h * AUG:(h + 1) * AUG], q[h * AUG:(h + 1) * AUG, :])

        def softmax(h, st, mask):
            if mask is not None:
                st = jnp.where(mask, st, NEG)
            m_prev = m_ref[h, :, :tq]
            m_new = jnp.maximum(m_prev, jnp.max(st, axis=0, keepdims=True))
            alpha = jnp.exp2(m_prev - m_new)
            p = jnp.exp2(st - m_new)
            l_ref[h, :, :tq] = alpha * l_ref[h, :, :tq] + jnp.sum(p, axis=0, keepdims=True)
            m_ref[h, :, :tq] = m_new
            return p.astype(BF16), alpha

        def weighted_values(h, vt, p_alpha):
            p, alpha = p_alpha
            rows = slice(h * HEAD_DIM, (h + 1) * HEAD_DIM)
            acc_ref[rows, :tq] = alpha * acc_ref[rows, :tq] + _dot(vt[rows, :], p)

        def process(k_start, tk, mask):
            kblk = k_ref[pl.ds(k_start, tk), :]
            vt = vt_ref[:, pl.ds(k_start, tk)]
            for h in range(FOX_GROUP):
                weighted_values(h, vt, softmax(h, scores(kblk, h), mask))

        def key_tile(j):
            return k_ref[pl.ds(pl.multiple_of(BLOCK + j * FOX_TILE, BLOCK), FOX_TILE), :]

        def pipelined(j, mask, prefetch):
            kblk = key_tile(j)
            vt = vt_ref[:, pl.ds(pl.multiple_of(BLOCK + j * FOX_TILE, BLOCK), FOX_TILE)]
            st = [st_ref[...], scores(kblk, 1), None, None]
            pa = softmax(0, st[0], mask)
            for h in range(FOX_GROUP):
                if h + 2 < FOX_GROUP:
                    st[h + 2] = scores(kblk, h + 2)
                elif h + 2 == FOX_GROUP and prefetch:
                    st_ref[...] = scores(key_tile(j + 1), 0)
                weighted_values(h, vt, pa)
                if h + 1 < FOX_GROUP:
                    pa = softmax(h + 1, st[h + 1], mask)

        kpos0 = lax.broadcasted_iota(jnp.int32, (BLOCK, tq), 0)
        if diag:
            process(0, BLOCK, kpos0 >= PAD_FRONT)
            st_ref[...] = scores(key_tile(0), 0)

            def full_tile(j, carry):
                pipelined(j, None, True)
                return carry

            lax.fori_loop(0, n_full, full_tile, 0)
            kr = lax.broadcasted_iota(jnp.int32, (tq, tq), 0)
            qc = lax.broadcasted_iota(jnp.int32, (tq, tq), 1)
            pipelined(n_full, kr <= qc, False)
        else:
            qc = lax.broadcasted_iota(jnp.int32, (BLOCK, tq), 1)
            process(0, BLOCK, (kpos0 >= PAD_FRONT) & (kpos0 <= qc))
        for h in range(FOX_GROUP):
            rows = slice(h * HEAD_DIM, (h + 1) * HEAD_DIM)
            o_ref[rows, pl.ds(q_start, tq)] = (
                acc_ref[rows, :tq] * (1.0 / l_ref[h, :, :tq])).astype(BF16)

    @pl.when(qi == 0)
    def _():
        attend(0, BLOCK, 0, False)

    @pl.when(qi > 0)
    def _():
        attend(pl.multiple_of(BLOCK + (qi - 1) * FOX_TILE, BLOCK), FOX_TILE, qi - 1, True)


def _fox(qft, kf, vt, batch):
    rows = kf.shape[0]
    p = rows // batch
    n_grp = FOX_HEADS // FOX_GROUP
    width = FOX_GROUP * HEAD_DIM
    n_q = (p - BLOCK) // FOX_TILE + 1

    def feat(r):
        return pl.BlockSpec((r, p), lambda b, g, q: (g, b))

    return pl.pallas_call(
        _fox_body,
        grid=(batch, n_grp, n_q),
        in_specs=[feat(FOX_GROUP * AUG),
                  pl.BlockSpec((p, FOX_GROUP * AUG), lambda b, g, q: (b, g)),
                  feat(width)],
        out_specs=feat(width),
        out_shape=jax.ShapeDtypeStruct((FOX_W, rows), BF16),
        scratch_shapes=[
            pltpu.VMEM((FOX_GROUP, 1, FOX_TILE), F32),
            pltpu.VMEM((FOX_GROUP, 1, FOX_TILE), F32),
            pltpu.VMEM((width, FOX_TILE), F32),
            pltpu.VMEM((FOX_TILE, FOX_TILE), F32),
        ],
        compiler_params=_params("arbitrary", "arbitrary", "arbitrary"),
        name="fox",
    )(qft, kf, vt)


def _t5_bucket_np(dist):
    n = np.maximum(dist, 0)
    max_exact = N_BUCKETS // 2
    nf = np.maximum(n, 1).astype(np.float64)
    large = max_exact + (np.log(nf / max_exact) / math.log(MAX_DISTANCE / max_exact)
                         * (N_BUCKETS - max_exact)).astype(np.int64)
    large = np.minimum(large, N_BUCKETS - 1)
    return np.where(n < max_exact, n, large).astype(np.int32)


def _swa_static_buckets():
    k = np.arange(SWA_KEYS)[:, None]
    q = np.arange(BLOCK)[None, :]
    d = BLOCK + q - k
    bucket = np.where((d >= 0) & (d < WINDOW) & (k < 2 * BLOCK), _t5_bucket_np(d), -1)
    bucket = np.where(k >= 2 * BLOCK + PAD_FRONT, N_BUCKETS - 1, bucket)
    return np.ascontiguousarray(np.broadcast_to(bucket, (SWA_KEYS, BLOCK))).astype(np.int32)


def _swa_body(table_ref, sink_ref, bucket_ref, q_ref, k_ref, vt_ref, o_ref, bias_ref):
    g = pl.program_id(1)
    t = pl.program_id(2)
    cols = SWA_GROUP * BLOCK

    @pl.when(t == 0)
    def _():
        bucket = bucket_ref[...]
        slot = lax.broadcasted_iota(jnp.int32, (SWA_KEYS, BLOCK), 0)
        qpos = lax.broadcasted_iota(jnp.int32, (SWA_KEYS, BLOCK), 1)
        for r in range(SWA_GROUP):
            bias = jnp.full((SWA_KEYS, BLOCK), NEG, F32)
            for bkt in range(N_BUCKETS):
                bias = jnp.where(bucket == bkt, table_ref[bkt, g * SWA_GROUP + r], bias)
            bias_ref[2, :, r * BLOCK:(r + 1) * BLOCK] = bias
            for n in range(2):
                band_ok = (slot >= PAD_FRONT - (n - 1) * BLOCK) & (slot < 2 * BLOCK)
                meta_ok = ((slot >= 2 * BLOCK + PAD_FRONT)
                           & (n * BLOCK + qpos - (slot - 2 * BLOCK) >= WINDOW))
                bias_ref[n, :, r * BLOCK:(r + 1) * BLOCK] = jnp.where(band_ok | meta_ok, bias, NEG)

    sink = jnp.concatenate(
        [jnp.full((1, BLOCK), sink_ref[g * SWA_GROUP + r], F32) for r in range(SWA_GROUP)], axis=1)
    grp_rows = lax.broadcasted_iota(jnp.int32, (SWA_KVW, cols), 0) // HEAD_DIM
    v_rows = pl.ds(pl.multiple_of(g * HEAD_DIM, HEAD_DIM), HEAD_DIM)
    k_lead = k_ref[0:BLOCK, :]
    v_lead = vt_ref[v_rows, 0:BLOCK]

    n_blocks = q_ref.shape[1] // BLOCK

    def band(i):
        n = t * n_blocks + i
        return (n, pl.multiple_of(jnp.maximum(n - 1, 0) * BLOCK, BLOCK),
                pl.multiple_of(n * BLOCK, BLOCK))

    def scores(i):
        n, prev, cur = band(i)
        keys = jnp.concatenate(
            [k_ref[pl.ds(prev, BLOCK), :], k_ref[pl.ds(cur, BLOCK), :], k_lead], axis=0)
        q4 = jnp.concatenate(
            [q_ref[r * HEAD_DIM:(r + 1) * HEAD_DIM, i * BLOCK:(i + 1) * BLOCK]
             for r in range(SWA_GROUP)], axis=1)
        q8 = jnp.concatenate([q4, q4], axis=0)
        q8 = jnp.where(grp_rows == g, q8, jnp.zeros_like(q8))
        return _dot(keys, q8) + bias_ref[jnp.minimum(n, 2)]

    def finish(i, s):
        _, prev, cur = band(i)
        vals = jnp.concatenate(
            [vt_ref[v_rows, pl.ds(prev, BLOCK)], vt_ref[v_rows, pl.ds(cur, BLOCK)], v_lead], axis=1)
        mx = jnp.maximum(jnp.max(s, axis=0, keepdims=True), sink)
        p = jnp.exp(s - mx)
        denom = jnp.sum(p, axis=0, keepdims=True) + jnp.exp(sink - mx)
        out = _dot(vals, p.astype(BF16)) * (1.0 / denom)
        for r in range(SWA_GROUP):
            o_ref[r * HEAD_DIM:(r + 1) * HEAD_DIM, i * BLOCK:(i + 1) * BLOCK] = (
                out[:, r * BLOCK:(r + 1) * BLOCK].astype(BF16))

    s = scores(0)
    for i in range(n_blocks):
        s_next = scores(i + 1) if i + 1 < n_blocks else None
        finish(i, s)
        s = s_next


def _swa(table, sinks, qst, ks, vst, batch):
    rows = ks.shape[0]
    p = rows // batch
    n_t = SWA_STEPS
    swa_tile = p // n_t
    width = SWA_GROUP * HEAD_DIM
    bucket = jnp.asarray(_swa_static_buckets())
    smem = pl.BlockSpec(memory_space=pltpu.SMEM)
    tile = pl.BlockSpec((width, swa_tile), lambda b, g, t: (g, b * n_t + t))
    return pl.pallas_call(
        _swa_body,
        grid=(batch, SWA_KV_HEADS, n_t),
        in_specs=[smem, smem, _const_spec((SWA_KEYS, BLOCK)), tile,
                  pl.BlockSpec((p, SWA_KVW), lambda b, g, t: (b, 0)),
                  pl.BlockSpec((SWA_KVW, p), lambda b, g, t: (0, b))],
        out_specs=tile,
        out_shape=jax.ShapeDtypeStruct((SWA_QW, rows), BF16),
        scratch_shapes=[pltpu.VMEM((3, SWA_KEYS, SWA_GROUP * BLOCK), F32)],
        compiler_params=_params("arbitrary", "arbitrary", "arbitrary"),
        name="swa",
    )(table, sinks, bucket, qst, ks, vst)


def _mix_body(h_ref, of_ref, os_ref, ga_ref, gb_ref, wf_ref, ws_ref, wo_ref, o_ref):
    y = (ga_ref[...].astype(F32) * _dot(wf_ref[...], of_ref[...])
         + gb_ref[...].astype(F32) * _dot(ws_ref[...], os_ref[...]))
    o_ref[...] = h_ref[...] + _dot_tn(y.astype(BF16), wo_ref[...])


def _mix_out(h2, o_fox, o_swa, ga, gb, wf_t, ws_t, wo):
    rows, d = h2.shape
    tm = ROW_TILE

    def feat(r):
        return pl.BlockSpec((r, tm), lambda i: (0, i))

    tok = pl.BlockSpec((tm, d), lambda i: (i, 0))
    return pl.pallas_call(
        _mix_body,
        grid=(rows // tm,),
        in_specs=[tok, feat(FOX_W), feat(SWA_QW), feat(d), feat(d),
                  _const_spec(wf_t.shape), _const_spec(ws_t.shape), _const_spec(wo.shape)],
        out_specs=tok,
        out_shape=jax.ShapeDtypeStruct((rows, d), F32),
        compiler_params=_params("arbitrary"),
        name="mix_out",
    )(h2, o_fox, o_swa, ga, gb, wf_t, ws_t, wo)


def _split_w_in(w, d_model):
    idx = np.cumsum([FOX_W, FOX_W, FOX_W, FOX_HEADS, SWA_QW, SWA_KVW, SWA_KVW, d_model])
    qa, ka, va, fa, qb, kb, vb, ga, gb = jnp.split(w.astype(BF16), idx, axis=-1)
    fa = jnp.pad(fa, ((0, 0), (0, FA_ROWS - FOX_HEADS)))
    w_feat = jnp.concatenate([qa, va, qb, vb, fa, ga, gb], axis=-1).T
    return w_feat, jnp.concatenate([ka, kb], axis=-1)


def kernel(x, meta_tokens, rel_bias_table, ffn1_norm, ffn1_w_in, ffn1_w_out, mix_norm, w_in, forget_bias, fox_q_norm, fox_k_norm, swa_q_norm, swa_k_norm, swa_sinks, w_branch_fox, w_branch_swa, w_out, ffn2_norm, ffn2_w_in, ffn2_w_out):
    b, seq, d = x.shape
    depth = w_in.shape[0]
    p = BLOCK + seq
    assert seq % FOX_TILE == 0 and p % (SWA_STEPS * BLOCK) == 0 and (b * p) % ROW_TILE == 0
    pad = jnp.zeros((b, PAD_FRONT, d), x.dtype)
    meta = jnp.broadcast_to(meta_tokens.astype(x.dtype)[None], (b, N_META, d))
    h = jnp.concatenate([pad, meta, x], axis=1).reshape(b * p, d)
    table = rel_bias_table.astype(F32)

    def col(v):
        return v.astype(F32)[:, None]

    def twice(v):
        return jnp.tile(v.astype(F32), 2)[None]

    for l in range(depth):
        h = _ffn(h, ffn1_norm[l][None], ffn1_w_in[l].astype(BF16), ffn1_w_out[l].astype(BF16))
        w_feat, w_keys = _split_w_in(w_in[l], d)
        fb = jnp.pad(col(forget_bias[l]), ((0, FA_ROWS - FOX_HEADS), (0, 0)))
        qft, kf, vt, qst, ks, vst, ga, gb = _proj(
            h, p, mix_norm[l][None], w_feat, w_keys, fb,
            col(fox_q_norm[l]), twice(fox_k_norm[l]), col(swa_q_norm[l]), twice(swa_k_norm[l]))
        o_fox = _fox(qft, kf, vt, b)
        o_swa = _swa(table, swa_sinks[l].astype(F32), qst, ks, vst, b)
        h = _mix_out(h, o_fox, o_swa, ga, gb, w_branch_fox[l].astype(BF16).T,
                     w_branch_swa[l].astype(BF16).T, w_out[l].astype(BF16))
        h = _ffn(h, ffn2_norm[l][None], ffn2_w_in[l].astype(BF16), ffn2_w_out[l].astype(BF16))
    return h.reshape(b, p, d)[:, BLOCK:]
```

```python
import functools
import math

import jax
import jax.numpy as jnp
import numpy as np
from jax import lax
from jax.experimental import pallas as pl
from jax.experimental.pallas import tpu as pltpu

F32 = jnp.float32
BF16 = jnp.bfloat16

N_META = 16
BLOCK = 128
PAD_FRONT = BLOCK - N_META
HEAD_DIM = 64
FOX_HEADS = 8
SWA_Q_HEADS = 8
SWA_KV_HEADS = 2
SWA_GROUP = SWA_Q_HEADS // SWA_KV_HEADS
WINDOW = 128
N_BUCKETS = 32
MAX_DISTANCE = 128
EPS = 1e-6
NEG = -1e30
FOX_W = FOX_HEADS * HEAD_DIM
SWA_QW = SWA_Q_HEADS * HEAD_DIM
SWA_KVW = SWA_KV_HEADS * HEAD_DIM
SCALE = HEAD_DIM ** -0.5
LOG2E = math.log2(math.e)

LANES = 128
VMEM_LIMIT = 56 * 1024 * 1024
ROW_TILE = 512
FFN_CHUNK = 256
FOX_TILE = 512
FOX_GROUP = 4
AUG = LANES
AUG_ROWS = 32
V_ROWS = HEAD_DIM + 16
FA_ROWS = 16
SWA_KEYS = 3 * BLOCK
SWA_STEPS = 3

_R_QA = 0
_R_VA = _R_QA + FOX_W
_R_QB = _R_VA + FOX_W
_R_VB = _R_QB + SWA_QW
_R_FA = _R_VB + SWA_KVW
_R_GA = _R_FA + FA_ROWS


def _params(*sem):
    return pltpu.CompilerParams(dimension_semantics=sem, vmem_limit_bytes=VMEM_LIMIT)


def _const_spec(shape):
    nd = len(shape)
    return pl.BlockSpec(shape, lambda *_: (0,) * nd, pipeline_mode=pl.Buffered(1))


def _dot(a, b):
    return jnp.dot(a, b, preferred_element_type=F32)


def _dot_nt(a, b):
    return lax.dot_general(a, b, (((1,), (1,)), ((), ())), preferred_element_type=F32)


def _dot_tn(a, b):
    return lax.dot_general(a, b, (((0,), (0,)), ((), ())), preferred_element_type=F32)


def _rms_rows(x, g):
    ms = jnp.mean(x * x, axis=-1, keepdims=True)
    return x * lax.rsqrt(ms + EPS) * g


def _ffn_body(x_ref, g_ref, win_ref, wout_ref, o_ref, acc_ref, *, d_ff):
    x = x_ref[...]
    xn = _rms_rows(x, g_ref[...]).astype(BF16)
    for c in range(d_ff // FFN_CHUNK):
        lo = c * FFN_CHUNK
        gate = _dot(xn, win_ref[:, lo:lo + FFN_CHUNK])
        up = _dot(xn, win_ref[:, d_ff + lo:d_ff + lo + FFN_CHUNK])
        act = (gate * jax.nn.sigmoid(gate) * up).astype(BF16)
        part = _dot(act, wout_ref[lo:lo + FFN_CHUNK, :])
        if c == 0:
            acc_ref[...] = part
        else:
            acc_ref[...] += part
    o_ref[...] = x + 0.5 * acc_ref[...]


def _ffn(h2, gain, w_in, w_out):
    rows, d = h2.shape
    d_ff = w_out.shape[0]
    tm = ROW_TILE
    return pl.pallas_call(
        functools.partial(_ffn_body, d_ff=d_ff),
        grid=(rows // tm,),
        in_specs=[
            pl.BlockSpec((tm, d), lambda i: (i, 0)),
            _const_spec((1, d)),
            _const_spec((d, 2 * d_ff)),
            _const_spec((d_ff, d)),
        ],
        out_specs=pl.BlockSpec((tm, d), lambda i: (i, 0)),
        out_shape=jax.ShapeDtypeStruct((rows, d), F32),
        scratch_shapes=[pltpu.VMEM((tm, d), F32)],
        compiler_params=_params("arbitrary"),
        name="ffn",
    )(h2, gain, w_in, w_out)


def _split3(x):
    hi = x.astype(BF16).astype(F32)
    r = x - hi
    mid = r.astype(BF16).astype(F32)
    return hi, mid, r - mid


def _head_rms_cols(x, g):
    ms = jnp.mean(x * x, axis=0, keepdims=True)
    return x * lax.rsqrt(ms + EPS) * g


def _pair_rms_rows(x, g2):
    lane = lax.broadcasted_iota(jnp.int32, x.shape, 1)
    sq = x * x
    first = lane < HEAD_DIM
    s0 = jnp.sum(jnp.where(first, sq, 0.0), axis=-1, keepdims=True)
    s1 = jnp.sum(jnp.where(first, 0.0, sq), axis=-1, keepdims=True)
    ms = jnp.where(first, s0, s1) * (1.0 / HEAD_DIM)
    return x * lax.rsqrt(ms + EPS) * g2


def _proj_body(x_ref, g_ref, wt_ref, wk_ref, fb_ref, nfq_ref, nfk_ref, nsq_ref, nsk_ref,
               qf_ref, kf_ref, vt_ref, qs_ref, ks_ref, vs_ref, ga_ref, gb_ref,
               carry_ref, *, d_model, seq_pad):
    i = pl.program_id(0)
    tm = x_ref.shape[0]
    xn = _rms_rows(x_ref[...], g_ref[...]).astype(BF16)

    def feat(lo, rows):
        return _dot_nt(wt_ref[lo:lo + rows, :], xn)

    lf = jax.nn.log_sigmoid(feat(_R_FA, FA_ROWS) + fb_ref[...])
    r0 = i * tm
    boundary = (r0 // seq_pad + 1) * seq_pad - r0
    src = lax.broadcasted_iota(jnp.int32, (tm, tm), 0)
    dst = lax.broadcasted_iota(jnp.int32, (tm, tm), 1)
    upper = ((src <= dst) & ((src < boundary) == (dst < boundary))).astype(BF16)
    hi, mid, lo3 = _split3(lf)
    c_loc = (_dot(hi.astype(BF16), upper) + _dot(mid.astype(BF16), upper)
             + _dot(lo3.astype(BF16), upper))
    carry = jnp.where(r0 % seq_pad == 0, 0.0, carry_ref[:, 0:1])
    tok = lax.broadcasted_iota(jnp.int32, (1, tm), 1)
    c_t = c_loc + jnp.where(tok < boundary, carry, 0.0)
    carry_ref[...] = jnp.broadcast_to(c_t[:, tm - 1:tm], carry_ref.shape)
    c_t = c_t * LOG2E

    c_pad = jnp.concatenate(
        [jnp.where(lax.broadcasted_iota(jnp.int32, (FA_ROWS, tm), 0) < FOX_HEADS, c_t, 0.0),
         jnp.zeros((LANES - FA_ROWS, tm), F32)], axis=0)
    c_tok = c_pad.T
    lane = lax.broadcasted_iota(jnp.int32, (tm, LANES), 1)
    k_hi, k_mid, k_lo = _split3(c_tok)
    aug_lo = (-k_hi - pltpu.roll(k_mid, FOX_HEADS, 1) - pltpu.roll(k_lo, 2 * FOX_HEADS, 1)
              + jnp.where((lane >= 3 * FOX_HEADS) & (lane < 3 * FOX_HEADS + 3), 1.0, 0.0))
    aug_hi = pltpu.roll(aug_lo, HEAD_DIM, 1)
    first = lane < HEAD_DIM

    kt = _dot(xn, wk_ref[...])
    for j in range(FOX_HEADS // 2):
        kn = _pair_rms_rows(kt[:, j * LANES:(j + 1) * LANES], nfk_ref[...])
        kf_ref[:, (2 * j) * AUG:(2 * j + 1) * AUG] = jnp.where(first, kn, aug_hi).astype(BF16)
        kf_ref[:, (2 * j + 1) * AUG:(2 * j + 2) * AUG] = jnp.where(first, aug_lo, kn).astype(BF16)
    ks_ref[...] = _pair_rms_rows(kt[:, FOX_W:FOX_W + SWA_KVW], nsk_ref[...]).astype(BF16)

    arow = lax.broadcasted_iota(jnp.int32, (AUG_ROWS, tm), 0)
    qa = feat(_R_QA, FOX_W)
    zeros = jnp.zeros((HEAD_DIM - AUG_ROWS, tm), BF16)
    for h in range(FOX_HEADS):
        q_hi, q_mid, q_lo = _split3(c_t[h:h + 1, :])
        onehot = jnp.where((arow % FOX_HEADS == h) & (arow < 3 * FOX_HEADS), 1.0, 0.0)
        aug_q = jnp.where(arow == 3 * FOX_HEADS, q_hi, jnp.where(
            arow == 3 * FOX_HEADS + 1, q_mid, jnp.where(arow == 3 * FOX_HEADS + 2, q_lo, onehot)))
        qn = (_head_rms_cols(qa[h * HEAD_DIM:(h + 1) * HEAD_DIM], nfq_ref[...])
              * (SCALE * LOG2E)).astype(BF16)
        parts = [qn, aug_q.astype(BF16), zeros] if h % 2 == 0 else [aug_q.astype(BF16), zeros, qn]
        qf_ref[h * AUG:(h + 1) * AUG, :] = jnp.concatenate(parts, axis=0)
    va = feat(_R_VA, FOX_W).astype(BF16)
    ones = jnp.ones((V_ROWS - HEAD_DIM, tm), BF16)
    for h in range(FOX_HEADS):
        vt_ref[h * V_ROWS:(h + 1) * V_ROWS, :] = jnp.concatenate(
            [va[h * HEAD_DIM:(h + 1) * HEAD_DIM], ones], axis=0)

    qb = feat(_R_QB, SWA_QW)
    for h in range(SWA_Q_HEADS):
        sl = slice(h * HEAD_DIM, (h + 1) * HEAD_DIM)
        qs_ref[sl, :] = (_head_rms_cols(qb[sl], nsq_ref[...]) * SCALE).astype(BF16)
    vs_ref[...] = feat(_R_VB, SWA_KVW).astype(BF16)
    ga_ref[...] = jax.nn.sigmoid(feat(_R_GA, d_model)).astype(BF16)
    gb_ref[...] = jax.nn.sigmoid(feat(_R_GA + d_model, d_model)).astype(BF16)


def _proj(h2, seq_pad, gain, w_feat, w_keys, fb, nfq, nfk, nsq, nsk):
    rows, d = h2.shape
    tm = ROW_TILE

    def tok(w):
        return pl.BlockSpec((tm, w), lambda i: (i, 0)), jax.ShapeDtypeStruct((rows, w), BF16)

    def feat(r):
        return pl.BlockSpec((r, tm), lambda i: (0, i)), jax.ShapeDtypeStruct((r, rows), BF16)

    outs = [feat(FOX_HEADS * AUG), tok(FOX_HEADS * AUG), feat(FOX_HEADS * V_ROWS), feat(SWA_QW),
            tok(SWA_KVW), feat(SWA_KVW), feat(d), feat(d)]
    return pl.pallas_call(
        functools.partial(_proj_body, d_model=d, seq_pad=seq_pad),
        grid=(rows // tm,),
        in_specs=[
            pl.BlockSpec((tm, d), lambda i: (i, 0)),
            _const_spec((1, d)),
            _const_spec(w_feat.shape),
            _const_spec(w_keys.shape),
            _const_spec((FA_ROWS, 1)),
            _const_spec((HEAD_DIM, 1)), _const_spec((1, LANES)),
            _const_spec((HEAD_DIM, 1)), _const_spec((1, LANES)),
        ],
        out_specs=[o[0] for o in outs],
        out_shape=[o[1] for o in outs],
        scratch_shapes=[pltpu.VMEM((FA_ROWS, LANES), F32)],
        compiler_params=_params("arbitrary"),
        name="proj",
    )(h2, gain, w_feat, w_keys, fb, nfq, nfk, nsq, nsk)


def _fox_body(q_ref, k_ref, vt_ref, o_ref, m_ref, acc_ref, st_ref):
    qi = pl.program_id(2)

    def attend(q_start, tq, n_full, diag):
        q = q_ref[:, pl.ds(q_start, tq)]
        m_ref[:, :, :tq] = jnp.full((FOX_GROUP, 1, tq), NEG, F32)
        acc_ref[:, :tq] = jnp.zeros((FOX_GROUP * V_ROWS, tq), F32)

        def scores(kblk, h):
            return _dot(kblk[:, h * AUG:(h + 1) * AUG], q[h * AUG:(h + 1) * AUG, :])

        def softmax(h, st, mask):
            if mask is not None:
                st = jnp.where(mask, st, NEG)
            m_prev = m_ref[h, :, :tq]
            m_new = jnp.maximum(m_prev, jnp.max(st, axis=0, keepdims=True))
            alpha = jnp.exp2(m_prev - m_new)
            m_ref[h, :, :tq] = m_new
            return jnp.exp2((st - m_new).astype(BF16)), alpha

        def weighted_values(h, vt, p_alpha):
            p, alpha = p_alpha
            rows = slice(h * V_ROWS, (h + 1) * V_ROWS)
            acc_ref[rows, :tq] = alpha * acc_ref[rows, :tq] + _dot(vt[rows, :], p)

        def process(k_start, tk, mask):
            kblk = k_ref[pl.ds(k_start, tk), :]
            vt = vt_ref[:, pl.ds(k_start, tk)]
            for h in range(FOX_GROUP):
                weighted_values(h, vt, softmax(h, scores(kblk, h), mask))

        def key_tile(j):
            return k_ref[pl.ds(pl.multiple_of(BLOCK + j * FOX_TILE, BLOCK), FOX_TILE), :]

        def pipelined(j, mask, prefetch):
            kblk = key_tile(j)
            vt = vt_ref[:, pl.ds(pl.multiple_of(BLOCK + j * FOX_TILE, BLOCK), FOX_TILE)]
            st = [st_ref[...], scores(kblk, 1), None, None]
            pa = softmax(0, st[0], mask)
            for h in range(FOX_GROUP):
                if h + 2 < FOX_GROUP:
                    st[h + 2] = scores(kblk, h + 2)
                elif h + 2 == FOX_GROUP and prefetch:
                    st_ref[...] = scores(key_tile(j + 1), 0)
                weighted_values(h, vt, pa)
                if h + 1 < FOX_GROUP:
                    pa = softmax(h + 1, st[h + 1], mask)

        kpos0 = lax.broadcasted_iota(jnp.int32, (BLOCK, tq), 0)
        if diag:
            process(0, BLOCK, kpos0 >= PAD_FRONT)
            st_ref[...] = scores(key_tile(0), 0)

            def full_tile(j, carry):
                pipelined(j, None, True)
                return carry

            lax.fori_loop(0, n_full, full_tile, 0)
            kr = lax.broadcasted_iota(jnp.int32, (tq, tq), 0)
            qc = lax.broadcasted_iota(jnp.int32, (tq, tq), 1)
            pipelined(n_full, kr <= qc, False)
        else:
            qc = lax.broadcasted_iota(jnp.int32, (BLOCK, tq), 1)
            process(0, BLOCK, (kpos0 >= PAD_FRONT) & (kpos0 <= qc))
        for h in range(FOX_GROUP):
            denom = acc_ref[h * V_ROWS + HEAD_DIM:h * V_ROWS + HEAD_DIM + 1, :tq]
            o_ref[h * HEAD_DIM:(h + 1) * HEAD_DIM, pl.ds(q_start, tq)] = (
                acc_ref[h * V_ROWS:h * V_ROWS + HEAD_DIM, :tq] * (1.0 / denom)).astype(BF16)

    @pl.when(qi == 0)
    def _():
        attend(0, BLOCK, 0, False)

    @pl.when(qi > 0)
    def _():
        attend(pl.multiple_of(BLOCK + (qi - 1) * FOX_TILE, BLOCK), FOX_TILE, qi - 1, True)


def _fox(qft, kf, vt, batch):
    rows = kf.shape[0]
    p = rows // batch
    n_grp = FOX_HEADS // FOX_GROUP
    width = FOX_GROUP * HEAD_DIM
    n_q = (p - BLOCK) // FOX_TILE + 1

    def feat(r):
        return pl.BlockSpec((r, p), lambda b, g, q: (g, b))

    return pl.pallas_call(
        _fox_body,
        grid=(batch, n_grp, n_q),
        in_specs=[feat(FOX_GROUP * AUG),
                  pl.BlockSpec((p, FOX_GROUP * AUG), lambda b, g, q: (b, g)),
                  feat(FOX_GROUP * V_ROWS)],
        out_specs=feat(width),
        out_shape=jax.ShapeDtypeStruct((FOX_W, rows), BF16),
        scratch_shapes=[
            pltpu.VMEM((FOX_GROUP, 1, FOX_TILE), F32),
            pltpu.VMEM((FOX_GROUP * V_ROWS, FOX_TILE), F32),
            pltpu.VMEM((FOX_TILE, FOX_TILE), F32),
        ],
        compiler_params=_params("arbitrary", "arbitrary", "arbitrary"),
        name="fox",
    )(qft, kf, vt)


def _t5_bucket_np(dist):
    n = np.maximum(dist, 0)
    max_exact = N_BUCKETS // 2
    nf = np.maximum(n, 1).astype(np.float64)
    large = max_exact + (np.log(nf / max_exact) / math.log(MAX_DISTANCE / max_exact)
                         * (N_BUCKETS - max_exact)).astype(np.int64)
    large = np.minimum(large, N_BUCKETS - 1)
    return np.where(n < max_exact, n, large).astype(np.int32)


def _swa_static_buckets():
    k = np.arange(SWA_KEYS)[:, None]
    q = np.arange(BLOCK)[None, :]
    d = BLOCK + q - k
    bucket = np.where((d >= 0) & (d < WINDOW) & (k < 2 * BLOCK), _t5_bucket_np(d), -1)
    bucket = np.where(k >= 2 * BLOCK + PAD_FRONT, N_BUCKETS - 1, bucket)
    return np.ascontiguousarray(np.broadcast_to(bucket, (SWA_KEYS, BLOCK))).astype(np.int32)


def _swa_body(table_ref, sink_ref, bucket_ref, q_ref, k_ref, vt_ref, o_ref, bias_ref):
    g = pl.program_id(1)
    t = pl.program_id(2)
    cols = SWA_GROUP * BLOCK

    @pl.when(t == 0)
    def _():
        bucket = bucket_ref[...]
        slot = lax.broadcasted_iota(jnp.int32, (SWA_KEYS, BLOCK), 0)
        qpos = lax.broadcasted_iota(jnp.int32, (SWA_KEYS, BLOCK), 1)
        for r in range(SWA_GROUP):
            bias = jnp.full((SWA_KEYS, BLOCK), NEG, F32)
            for bkt in range(N_BUCKETS):
                bias = jnp.where(bucket == bkt, table_ref[bkt, g * SWA_GROUP + r], bias)
            bias_ref[2, :, r * BLOCK:(r + 1) * BLOCK] = bias
            for n in range(2):
                band_ok = (slot >= PAD_FRONT - (n - 1) * BLOCK) & (slot < 2 * BLOCK)
                meta_ok = ((slot >= 2 * BLOCK + PAD_FRONT)
                           & (n * BLOCK + qpos - (slot - 2 * BLOCK) >= WINDOW))
                bias_ref[n, :, r * BLOCK:(r + 1) * BLOCK] = jnp.where(band_ok | meta_ok, bias, NEG)

    sink = jnp.concatenate(
        [jnp.full((1, BLOCK), sink_ref[g * SWA_GROUP + r], F32) for r in range(SWA_GROUP)], axis=1)
    grp_rows = lax.broadcasted_iota(jnp.int32, (SWA_KVW, cols), 0) // HEAD_DIM
    v_rows = pl.ds(pl.multiple_of(g * HEAD_DIM, HEAD_DIM), HEAD_DIM)
    k_lead = k_ref[0:BLOCK, :]
    v_lead = vt_ref[v_rows, 0:BLOCK]

    n_blocks = q_ref.shape[1] // BLOCK

    def band(i):
        n = t * n_blocks + i
        return (n, pl.multiple_of(jnp.maximum(n - 1, 0) * BLOCK, BLOCK),
                pl.multiple_of(n * BLOCK, BLOCK))

    def scores(i):
        n, prev, cur = band(i)
        keys = jnp.concatenate(
            [k_ref[pl.ds(prev, BLOCK), :], k_ref[pl.ds(cur, BLOCK), :], k_lead], axis=0)
        q4 = jnp.concatenate(
            [q_ref[r * HEAD_DIM:(r + 1) * HEAD_DIM, i * BLOCK:(i + 1) * BLOCK]
             for r in range(SWA_GROUP)], axis=1)
        q8 = jnp.concatenate([q4, q4], axis=0)
        q8 = jnp.where(grp_rows == g, q8, jnp.zeros_like(q8))
        return _dot(keys, q8) + bias_ref[jnp.minimum(n, 2)]

    def finish(i, s):
        _, prev, cur = band(i)
        vals = jnp.concatenate(
            [vt_ref[v_rows, pl.ds(prev, BLOCK)], vt_ref[v_rows, pl.ds(cur, BLOCK)], v_lead], axis=1)
        mx = jnp.maximum(jnp.max(s, axis=0, keepdims=True), sink)
        p = jnp.exp(s - mx)
        denom = jnp.sum(p, axis=0, keepdims=True) + jnp.exp(sink - mx)
        out = _dot(vals, p.astype(BF16)) * (1.0 / denom)
        for r in range(SWA_GROUP):
            o_ref[r * HEAD_DIM:(r + 1) * HEAD_DIM, i * BLOCK:(i + 1) * BLOCK] = (
                out[:, r * BLOCK:(r + 1) * BLOCK].astype(BF16))

    s = scores(0)
    for i in range(n_blocks):
        s_next = scores(i + 1) if i + 1 < n_blocks else None
        finish(i, s)
        s = s_next


def _swa(table, sinks, qst, ks, vst, batch):
    rows = ks.shape[0]
    p = rows // batch
    n_t = SWA_STEPS
    swa_tile = p // n_t
    width = SWA_GROUP * HEAD_DIM
    bucket = jnp.asarray(_swa_static_buckets())
    smem = pl.BlockSpec(memory_space=pltpu.SMEM)
    tile = pl.BlockSpec((width, swa_tile), lambda b, g, t: (g, b * n_t + t))
    return pl.pallas_call(
        _swa_body,
        grid=(batch, SWA_KV_HEADS, n_t),
        in_specs=[smem, smem, _const_spec((SWA_KEYS, BLOCK)), tile,
                  pl.BlockSpec((p, SWA_KVW), lambda b, g, t: (b, 0)),
                  pl.BlockSpec((SWA_KVW, p), lambda b, g, t: (0, b))],
        out_specs=tile,
        out_shape=jax.ShapeDtypeStruct((SWA_QW, rows), BF16),
        scratch_shapes=[pltpu.VMEM((3, SWA_KEYS, SWA_GROUP * BLOCK), F32)],
        compiler_params=_params("arbitrary", "arbitrary", "arbitrary"),
        name="swa",
    )(table, sinks, bucket, qst, ks, vst)


def _mix_body(h_ref, of_ref, os_ref, ga_ref, gb_ref, wf_ref, ws_ref, wo_ref, o_ref):
    y = (ga_ref[...].astype(F32) * _dot(wf_ref[...], of_ref[...])
         + gb_ref[...].astype(F32) * _dot(ws_ref[...], os_ref[...]))
    o_ref[...] = h_ref[...] + _dot_tn(y.astype(BF16), wo_ref[...])


def _mix_out(h2, o_fox, o_swa, ga, gb, wf_t, ws_t, wo):
    rows, d = h2.shape
    tm = ROW_TILE

    def feat(r):
        return pl.BlockSpec((r, tm), lambda i: (0, i))

    tok = pl.BlockSpec((tm, d), lambda i: (i, 0))
    return pl.pallas_call(
        _mix_body,
        grid=(rows // tm,),
        in_specs=[tok, feat(FOX_W), feat(SWA_QW), feat(d), feat(d),
                  _const_spec(wf_t.shape), _const_spec(ws_t.shape), _const_spec(wo.shape)],
        out_specs=tok,
        out_shape=jax.ShapeDtypeStruct((rows, d), F32),
        compiler_params=_params("arbitrary"),
        name="mix_out",
    )(h2, o_fox, o_swa, ga, gb, wf_t, ws_t, wo)


def _split_w_in(w, d_model):
    idx = np.cumsum([FOX_W, FOX_W, FOX_W, FOX_HEADS, SWA_QW, SWA_KVW, SWA_KVW, d_model])
    qa, ka, va, fa, qb, kb, vb, ga, gb = jnp.split(w.astype(BF16), idx, axis=-1)
    fa = jnp.pad(fa, ((0, 0), (0, FA_ROWS - FOX_HEADS)))
    w_feat = jnp.concatenate([qa, va, qb, vb, fa, ga, gb], axis=-1).T
    return w_feat, jnp.concatenate([ka, kb], axis=-1)


def kernel(x, meta_tokens, rel_bias_table, ffn1_norm, ffn1_w_in, ffn1_w_out, mix_norm, w_in, forget_bias, fox_q_norm, fox_k_norm, swa_q_norm, swa_k_norm, swa_sinks, w_branch_fox, w_branch_swa, w_out, ffn2_norm, ffn2_w_in, ffn2_w_out):
    b, seq, d = x.shape
    depth = w_in.shape[0]
    p = BLOCK + seq
    assert seq % FOX_TILE == 0 and p % (SWA_STEPS * BLOCK) == 0 and (b * p) % ROW_TILE == 0
    pad = jnp.zeros((b, PAD_FRONT, d), x.dtype)
    meta = jnp.broadcast_to(meta_tokens.astype(x.dtype)[None], (b, N_META, d))
    h = jnp.concatenate([pad, meta, x], axis=1).reshape(b * p, d)
    table = rel_bias_table.astype(F32)

    def col(v):
        return v.astype(F32)[:, None]

    def twice(v):
        return jnp.tile(v.astype(F32), 2)[None]

    for l in range(depth):
        h = _ffn(h, ffn1_norm[l][None], ffn1_w_in[l].astype(BF16), ffn1_w_out[l].astype(BF16))
        w_feat, w_keys = _split_w_in(w_in[l], d)
        fb = jnp.pad(col(forget_bias[l]), ((0, FA_ROWS - FOX_HEADS), (0, 0)))
        qft, kf, vt, qst, ks, vst, ga, gb = _proj(
            h, p, mix_norm[l][None], w_feat, w_keys, fb,
            col(fox_q_norm[l]), twice(fox_k_norm[l]), col(swa_q_norm[l]), twice(swa_k_norm[l]))
        o_fox = _fox(qft, kf, vt, b)
        o_swa = _swa(table, swa_sinks[l].astype(F32), qst, ks, vst, b)
        h = _mix_out(h, o_fox, o_swa, ga, gb, w_branch_fox[l].astype(BF16).T,
                     w_branch_swa[l].astype(BF16).T, w_out[l].astype(BF16))
        h = _ffn(h, ffn2_norm[l][None], ffn2_w_in[l].astype(BF16), ffn2_w_out[l].astype(BF16))
    return h.reshape(b, p, d)[:, BLOCK:]
```

```python
import functools
import math

import jax
import jax.numpy as jnp
import numpy as np
from jax import lax
from jax.experimental import pallas as pl
from jax.experimental.pallas import tpu as pltpu

F32 = jnp.float32
BF16 = jnp.bfloat16

N_META = 16
BLOCK = 128
PAD_FRONT = BLOCK - N_META
HEAD_DIM = 64
FOX_HEADS = 8
SWA_Q_HEADS = 8
SWA_KV_HEADS = 2
SWA_GROUP = SWA_Q_HEADS // SWA_KV_HEADS
WINDOW = 128
N_BUCKETS = 32
MAX_DISTANCE = 128
EPS = 1e-6
NEG = -1e30
FOX_W = FOX_HEADS * HEAD_DIM
SWA_QW = SWA_Q_HEADS * HEAD_DIM
SWA_KVW = SWA_KV_HEADS * HEAD_DIM
SCALE = HEAD_DIM ** -0.5
LOG2E = math.log2(math.e)

LANES = 128
VMEM_LIMIT = 56 * 1024 * 1024
ROW_TILE = 512
FFN_CHUNK = 256
FOX_TILE = 512
FOX_GROUP = 4
AUG = LANES
AUG_ROWS = 32
V_ROWS = HEAD_DIM + 16
FA_ROWS = 16
SWA_KEYS = 3 * BLOCK
SWA_STEPS = 3

_R_QA = 0
_R_VA = _R_QA + FOX_W
_R_QB = _R_VA + FOX_W
_R_VB = _R_QB + SWA_QW
_R_FA = _R_VB + SWA_KVW
_R_GA = _R_FA + FA_ROWS


def _params(*sem, flags=None):
    return pltpu.CompilerParams(dimension_semantics=sem, vmem_limit_bytes=VMEM_LIMIT, flags=flags)


def _const_spec(shape):
    nd = len(shape)
    return pl.BlockSpec(shape, lambda *_: (0,) * nd, pipeline_mode=pl.Buffered(1))


def _dot(a, b):
    return jnp.dot(a, b, preferred_element_type=F32)


def _dot_nt(a, b):
    return lax.dot_general(a, b, (((1,), (1,)), ((), ())), preferred_element_type=F32)


def _dot_tn(a, b):
    return lax.dot_general(a, b, (((0,), (0,)), ((), ())), preferred_element_type=F32)


def _rms_rows(x, g):
    ms = jnp.mean(x * x, axis=-1, keepdims=True)
    return x * lax.rsqrt(ms + EPS) * g


def _ffn_body(x_ref, g_ref, win_ref, wout_ref, o_ref, win_bf, wout_bf, acc_ref, *, d_ff, n_pro):
    i = pl.program_id(0)
    in_chunk = win_ref.shape[1]

    @pl.when(i < n_pro)
    def _():
        win_bf[:, pl.ds(pl.multiple_of(i * in_chunk, LANES), in_chunk)] = win_ref[...].astype(BF16)
        wout_bf[pl.ds(pl.multiple_of(i * FFN_CHUNK, FFN_CHUNK), FFN_CHUNK), :] = (
            wout_ref[...].astype(BF16))

    @pl.when(i >= n_pro)
    def _():
        x = x_ref[...]
        xn = _rms_rows(x, g_ref[...]).astype(BF16)
        for c in range(d_ff // FFN_CHUNK):
            lo = c * FFN_CHUNK
            gate = _dot(xn, win_bf[:, lo:lo + FFN_CHUNK])
            up = _dot(xn, win_bf[:, d_ff + lo:d_ff + lo + FFN_CHUNK])
            act = (gate * jax.nn.sigmoid(gate) * up).astype(BF16)
            part = _dot(act, wout_bf[lo:lo + FFN_CHUNK, :])
            if c == 0:
                acc_ref[...] = part
            else:
                acc_ref[...] += part
        o_ref[...] = x + 0.5 * acc_ref[...]


def _ffn(h2, gain, w_in, w_out):
    rows, d = h2.shape
    d_ff = w_out.shape[0]
    tm = ROW_TILE
    n_pro = d_ff // FFN_CHUNK
    in_chunk = 2 * d_ff // n_pro

    def row_tile(i):
        return (jnp.maximum(i - n_pro, 0), 0)

    return pl.pallas_call(
        functools.partial(_ffn_body, d_ff=d_ff, n_pro=n_pro),
        grid=(n_pro + rows // tm,),
        in_specs=[
            pl.BlockSpec((tm, d), row_tile),
            _const_spec((1, d)),
            pl.BlockSpec((d, in_chunk), lambda i: (0, jnp.minimum(i, n_pro - 1))),
            pl.BlockSpec((FFN_CHUNK, d), lambda i: (jnp.minimum(i, n_pro - 1), 0)),
        ],
        out_specs=pl.BlockSpec((tm, d), row_tile),
        out_shape=jax.ShapeDtypeStruct((rows, d), F32),
        scratch_shapes=[pltpu.VMEM((d, 2 * d_ff), BF16), pltpu.VMEM((d_ff, d), BF16),
                        pltpu.VMEM((tm, d), F32)],
        compiler_params=_params("arbitrary"),
        name="ffn",
    )(h2, gain, w_in, w_out)


def _split3(x):
    hi = x.astype(BF16).astype(F32)
    r = x - hi
    mid = r.astype(BF16).astype(F32)
    return hi, mid, r - mid


def _head_rms_cols(x, g):
    ms = jnp.mean(x * x, axis=0, keepdims=True)
    return x * lax.rsqrt(ms + EPS) * g


def _pair_rms_rows(x, g2):
    lane = lax.broadcasted_iota(jnp.int32, x.shape, 1)
    sq = x * x
    first = lane < HEAD_DIM
    s0 = jnp.sum(jnp.where(first, sq, 0.0), axis=-1, keepdims=True)
    s1 = jnp.sum(jnp.where(first, 0.0, sq), axis=-1, keepdims=True)
    ms = jnp.where(first, s0, s1) * (1.0 / HEAD_DIM)
    return x * lax.rsqrt(ms + EPS) * g2


def _proj_body(x_ref, g_ref, wt_ref, wk_ref, fb_ref, nfq_ref, nfk_ref, nsq_ref, nsk_ref,
               qf_ref, kf_ref, vt_ref, qs_ref, ks_ref, vs_ref, ga_ref, gb_ref,
               carry_ref, *, d_model, seq_pad):
    i = pl.program_id(0)
    tm = x_ref.shape[0]
    xn = _rms_rows(x_ref[...], g_ref[...]).astype(BF16)

    def feat(lo, rows):
        return _dot_nt(wt_ref[lo:lo + rows, :], xn)

    lf = jax.nn.log_sigmoid(feat(_R_FA, FA_ROWS) + fb_ref[...])
    r0 = i * tm
    boundary = (r0 // seq_pad + 1) * seq_pad - r0
    src = lax.broadcasted_iota(jnp.int32, (tm, tm), 0)
    dst = lax.broadcasted_iota(jnp.int32, (tm, tm), 1)
    upper = ((src <= dst) & ((src < boundary) == (dst < boundary))).astype(BF16)
    hi, mid, lo3 = _split3(lf)
    c_loc = (_dot(hi.astype(BF16), upper) + _dot(mid.astype(BF16), upper)
             + _dot(lo3.astype(BF16), upper))
    carry = jnp.where(r0 % seq_pad == 0, 0.0, carry_ref[:, 0:1])
    tok = lax.broadcasted_iota(jnp.int32, (1, tm), 1)
    c_t = c_loc + jnp.where(tok < boundary, carry, 0.0)
    carry_ref[...] = jnp.broadcast_to(c_t[:, tm - 1:tm], carry_ref.shape)
    c_t = c_t * LOG2E

    c_pad = jnp.concatenate(
        [jnp.where(lax.broadcasted_iota(jnp.int32, (FA_ROWS, tm), 0) < FOX_HEADS, c_t, 0.0),
         jnp.zeros((LANES - FA_ROWS, tm), F32)], axis=0)
    c_tok = c_pad.T
    lane = lax.broadcasted_iota(jnp.int32, (tm, LANES), 1)
    k_hi, k_mid, k_lo = _split3(c_tok)
    aug_lo = (-k_hi - pltpu.roll(k_mid, FOX_HEADS, 1) - pltpu.roll(k_lo, 2 * FOX_HEADS, 1)
              + jnp.where((lane >= 3 * FOX_HEADS) & (lane < 3 * FOX_HEADS + 3), 1.0, 0.0))
    aug_hi = pltpu.roll(aug_lo, HEAD_DIM, 1)
    first = lane < HEAD_DIM

    kt = _dot(xn, wk_ref[...])
    for j in range(FOX_HEADS // 2):
        kn = _pair_rms_rows(kt[:, j * LANES:(j + 1) * LANES], nfk_ref[...])
        kf_ref[:, (2 * j) * AUG:(2 * j + 1) * AUG] = jnp.where(first, kn, aug_hi).astype(BF16)
        kf_ref[:, (2 * j + 1) * AUG:(2 * j + 2) * AUG] = jnp.where(first, aug_lo, kn).astype(BF16)
    ks_ref[...] = _pair_rms_rows(kt[:, FOX_W:FOX_W + SWA_KVW], nsk_ref[...]).astype(BF16)

    arow = lax.broadcasted_iota(jnp.int32, (AUG_ROWS, tm), 0)
    qa = feat(_R_QA, FOX_W)
    zeros = jnp.zeros((HEAD_DIM - AUG_ROWS, tm), BF16)
    for h in range(FOX_HEADS):
        q_hi, q_mid, q_lo = _split3(c_t[h:h + 1, :])
        onehot = jnp.where((arow % FOX_HEADS == h) & (arow < 3 * FOX_HEADS), 1.0, 0.0)
        aug_q = jnp.where(arow == 3 * FOX_HEADS, q_hi, jnp.where(
            arow == 3 * FOX_HEADS + 1, q_mid, jnp.where(arow == 3 * FOX_HEADS + 2, q_lo, onehot)))
        qn = (_head_rms_cols(qa[h * HEAD_DIM:(h + 1) * HEAD_DIM], nfq_ref[...])
              * (SCALE * LOG2E)).astype(BF16)
        parts = [qn, aug_q.astype(BF16), zeros] if h % 2 == 0 else [aug_q.astype(BF16), zeros, qn]
        qf_ref[h * AUG:(h + 1) * AUG, :] = jnp.concatenate(parts, axis=0)
    va = feat(_R_VA, FOX_W).astype(BF16)
    ones = jnp.ones((V_ROWS - HEAD_DIM, tm), BF16)
    for h in range(FOX_HEADS):
        vt_ref[h * V_ROWS:(h + 1) * V_ROWS, :] = jnp.concatenate(
            [va[h * HEAD_DIM:(h + 1) * HEAD_DIM], ones], axis=0)

    qb = feat(_R_QB, SWA_QW)
    for h in range(SWA_Q_HEADS):
        sl = slice(h * HEAD_DIM, (h + 1) * HEAD_DIM)
        qs_ref[sl, :] = (_head_rms_cols(qb[sl], nsq_ref[...]) * SCALE).astype(BF16)
    vs_ref[...] = feat(_R_VB, SWA_KVW).astype(BF16)
    ga_ref[...] = jax.nn.sigmoid(feat(_R_GA, d_model)).astype(BF16)
    gb_ref[...] = jax.nn.sigmoid(feat(_R_GA + d_model, d_model)).astype(BF16)


def _proj(h2, seq_pad, gain, w_feat, w_keys, fb, nfq, nfk, nsq, nsk):
    rows, d = h2.shape
    tm = ROW_TILE

    def tok(w):
        return pl.BlockSpec((tm, w), lambda i: (i, 0)), jax.ShapeDtypeStruct((rows, w), BF16)

    def feat(r):
        return pl.BlockSpec((r, tm), lambda i: (0, i)), jax.ShapeDtypeStruct((r, rows), BF16)

    outs = [feat(FOX_HEADS * AUG), tok(FOX_HEADS * AUG), feat(FOX_HEADS * V_ROWS), feat(SWA_QW),
            tok(SWA_KVW), feat(SWA_KVW), feat(d), feat(d)]
    return pl.pallas_call(
        functools.partial(_proj_body, d_model=d, seq_pad=seq_pad),
        grid=(rows // tm,),
        in_specs=[
            pl.BlockSpec((tm, d), lambda i: (i, 0)),
            _const_spec((1, d)),
            _const_spec(w_feat.shape),
            _const_spec(w_keys.shape),
            _const_spec((FA_ROWS, 1)),
            _const_spec((HEAD_DIM, 1)), _const_spec((1, LANES)),
            _const_spec((HEAD_DIM, 1)), _const_spec((1, LANES)),
        ],
        out_specs=[o[0] for o in outs],
        out_shape=[o[1] for o in outs],
        scratch_shapes=[pltpu.VMEM((FA_ROWS, LANES), F32)],
        compiler_params=_params("arbitrary"),
        name="proj",
    )(h2, gain, w_feat, w_keys, fb, nfq, nfk, nsq, nsk)


def _fox_body(q_ref, k_ref, vt_ref, o_ref, m_ref, acc_ref, st_ref):
    qi = pl.program_id(2)

    def attend(q_start, tq, n_full, diag):
        q = q_ref[:, pl.ds(q_start, tq)]
        m_ref[:, :, :tq] = jnp.full((FOX_GROUP, 1, tq), NEG, F32)
        acc_ref[:, :tq] = jnp.zeros((FOX_GROUP * V_ROWS, tq), F32)

        def scores(kblk, h):
            return _dot(kblk[:, h * AUG:(h + 1) * AUG], q[h * AUG:(h + 1) * AUG, :])

        def softmax(h, st, mask):
            if mask is not None:
                st = jnp.where(mask, st, NEG)
            m_prev = m_ref[h, :, :tq]
            m_new = jnp.maximum(m_prev, jnp.max(st, axis=0, keepdims=True))
            alpha = jnp.exp2(m_prev - m_new)
            m_ref[h, :, :tq] = m_new
            return jnp.exp2(st - m_new).astype(BF16), alpha

        def weighted_values(h, vt, p_alpha):
            p, alpha = p_alpha
            rows = slice(h * V_ROWS, (h + 1) * V_ROWS)
            acc_ref[rows, :tq] = alpha * acc_ref[rows, :tq] + _dot(vt[rows, :], p)

        def process(k_start, tk, mask):
            kblk = k_ref[pl.ds(k_start, tk), :]
            vt = vt_ref[:, pl.ds(k_start, tk)]
            for h in range(FOX_GROUP):
                weighted_values(h, vt, softmax(h, scores(kblk, h), mask))

        def key_tile(j):
            return k_ref[pl.ds(pl.multiple_of(BLOCK + j * FOX_TILE, BLOCK), FOX_TILE), :]

        def pipelined(j, mask, prefetch):
            kblk = key_tile(j)
            vt = vt_ref[:, pl.ds(pl.multiple_of(BLOCK + j * FOX_TILE, BLOCK), FOX_TILE)]
            st = [st_ref[...], scores(kblk, 1), None, None]
            pa = softmax(0, st[0], mask)
            for h in range(FOX_GROUP):
                if h + 2 < FOX_GROUP:
                    st[h + 2] = scores(kblk, h + 2)
                elif h + 2 == FOX_GROUP and prefetch:
                    st_ref[...] = scores(key_tile(j + 1), 0)
                weighted_values(h, vt, pa)
                if h + 1 < FOX_GROUP:
                    pa = softmax(h + 1, st[h + 1], mask)

        kpos0 = lax.broadcasted_iota(jnp.int32, (BLOCK, tq), 0)
        if diag:
            process(0, BLOCK, kpos0 >= PAD_FRONT)
            st_ref[...] = scores(key_tile(0), 0)

            def tile_pair(jj, carry):
                pipelined(2 * jj, None, True)
                pipelined(2 * jj + 1, None, True)
                return carry

            lax.fori_loop(0, n_full // 2, tile_pair, 0)

            @pl.when(n_full % 2 == 1)
            def _():
                pipelined(n_full - 1, None, True)
            kr = lax.broadcasted_iota(jnp.int32, (tq, tq), 0)
            qc = lax.broadcasted_iota(jnp.int32, (tq, tq), 1)
            pipelined(n_full, kr <= qc, False)
        else:
            qc = lax.broadcasted_iota(jnp.int32, (BLOCK, tq), 1)
            process(0, BLOCK, (kpos0 >= PAD_FRONT) & (kpos0 <= qc))
        for h in range(FOX_GROUP):
            denom = acc_ref[h * V_ROWS + HEAD_DIM:h * V_ROWS + HEAD_DIM + 1, :tq]
            o_ref[h * HEAD_DIM:(h + 1) * HEAD_DIM, pl.ds(q_start, tq)] = (
                acc_ref[h * V_ROWS:h * V_ROWS + HEAD_DIM, :tq] * (1.0 / denom)).astype(BF16)

    @pl.when(qi == 0)
    def _():
        attend(0, BLOCK, 0, False)

    @pl.when(qi > 0)
    def _():
        attend(pl.multiple_of(BLOCK + (qi - 1) * FOX_TILE, BLOCK), FOX_TILE, qi - 1, True)


def _fox(qft, kf, vt, batch):
    rows = kf.shape[0]
    p = rows // batch
    n_grp = FOX_HEADS // FOX_GROUP
    width = FOX_GROUP * HEAD_DIM
    n_q = (p - BLOCK) // FOX_TILE + 1

    def feat(r):
        return pl.BlockSpec((r, p), lambda b, g, q: (g, b))

    return pl.pallas_call(
        _fox_body,
        grid=(batch, n_grp, n_q),
        in_specs=[feat(FOX_GROUP * AUG),
                  pl.BlockSpec((p, FOX_GROUP * AUG), lambda b, g, q: (b, g)),
                  feat(FOX_GROUP * V_ROWS)],
        out_specs=feat(width),
        out_shape=jax.ShapeDtypeStruct((FOX_W, rows), BF16),
        scratch_shapes=[
            pltpu.VMEM((FOX_GROUP, 1, FOX_TILE), F32),
            pltpu.VMEM((FOX_GROUP * V_ROWS, FOX_TILE), F32),
            pltpu.VMEM((FOX_TILE, FOX_TILE), F32),
        ],
        compiler_params=_params("arbitrary", "arbitrary", "arbitrary"),
        name="fox",
    )(qft, kf, vt)


def _t5_bucket_np(dist):
    n = np.maximum(dist, 0)
    max_exact = N_BUCKETS // 2
    nf = np.maximum(n, 1).astype(np.float64)
    large = max_exact + (np.log(nf / max_exact) / math.log(MAX_DISTANCE / max_exact)
                         * (N_BUCKETS - max_exact)).astype(np.int64)
    large = np.minimum(large, N_BUCKETS - 1)
    return np.where(n < max_exact, n, large).astype(np.int32)


def _swa_static_buckets():
    k = np.arange(SWA_KEYS)[:, None]
    q = np.arange(BLOCK)[None, :]
    d = BLOCK + q - k
    bucket = np.where((d >= 0) & (d < WINDOW) & (k < 2 * BLOCK), _t5_bucket_np(d), -1)
    bucket = np.where(k >= 2 * BLOCK + PAD_FRONT, N_BUCKETS - 1, bucket)
    return np.ascontiguousarray(np.broadcast_to(bucket, (SWA_KEYS, BLOCK))).astype(np.int32)


def _swa_body(table_ref, sink_ref, bucket_ref, q_ref, k_ref, vt_ref, o_ref, bias_ref):
    g = pl.program_id(1)
    t = pl.program_id(2)
    cols = SWA_GROUP * BLOCK

    @pl.when(t == 0)
    def _():
        bucket = bucket_ref[...]
        slot = lax.broadcasted_iota(jnp.int32, (SWA_KEYS, BLOCK), 0)
        qpos = lax.broadcasted_iota(jnp.int32, (SWA_KEYS, BLOCK), 1)
        for r in range(SWA_GROUP):
            bias = jnp.full((SWA_KEYS, BLOCK), NEG, F32)
            for bkt in range(N_BUCKETS):
                bias = jnp.where(bucket == bkt, table_ref[bkt, g * SWA_GROUP + r], bias)
            bias_ref[2, :, r * BLOCK:(r + 1) * BLOCK] = bias
            for n in range(2):
                band_ok = (slot >= PAD_FRONT - (n - 1) * BLOCK) & (slot < 2 * BLOCK)
                meta_ok = ((slot >= 2 * BLOCK + PAD_FRONT)
                           & (n * BLOCK + qpos - (slot - 2 * BLOCK) >= WINDOW))
                bias_ref[n, :, r * BLOCK:(r + 1) * BLOCK] = jnp.where(band_ok | meta_ok, bias, NEG)

    sink = jnp.concatenate(
        [jnp.full((1, BLOCK), sink_ref[g * SWA_GROUP + r], F32) for r in range(SWA_GROUP)], axis=1)
    grp_rows = lax.broadcasted_iota(jnp.int32, (SWA_KVW, cols), 0) // HEAD_DIM
    v_rows = pl.ds(pl.multiple_of(g * HEAD_DIM, HEAD_DIM), HEAD_DIM)
    k_lead = k_ref[0:BLOCK, :]
    v_lead = vt_ref[v_rows, 0:BLOCK]

    n_blocks = q_ref.shape[1] // BLOCK

    def band(i):
        n = t * n_blocks + i
        return (n, pl.multiple_of(jnp.maximum(n - 1, 0) * BLOCK, BLOCK),
                pl.multiple_of(n * BLOCK, BLOCK))

    def scores(i):
        n, prev, cur = band(i)
        keys = jnp.concatenate(
            [k_ref[pl.ds(prev, BLOCK), :], k_ref[pl.ds(cur, BLOCK), :], k_lead], axis=0)
        q4 = jnp.concatenate(
            [q_ref[r * HEAD_DIM:(r + 1) * HEAD_DIM, i * BLOCK:(i + 1) * BLOCK]
             for r in range(SWA_GROUP)], axis=1)
        q8 = jnp.concatenate([q4, q4], axis=0)
        q8 = jnp.where(grp_rows == g, q8, jnp.zeros_like(q8))
        return _dot(keys, q8) + bias_ref[jnp.minimum(n, 2)]

    def finish(i, s):
        _, prev, cur = band(i)
        vals = jnp.concatenate(
            [vt_ref[v_rows, pl.ds(prev, BLOCK)], vt_ref[v_rows, pl.ds(cur, BLOCK)], v_lead], axis=1)
        mx = jnp.maximum(jnp.max(s, axis=0, keepdims=True), sink)
        p = jnp.exp(s - mx)
        denom = jnp.sum(p, axis=0, keepdims=True) + jnp.exp(sink - mx)
        out = _dot(vals, p.astype(BF16)) * (1.0 / denom)
        for r in range(SWA_GROUP):
            o_ref[r * HEAD_DIM:(r + 1) * HEAD_DIM, i * BLOCK:(i + 1) * BLOCK] = (
                out[:, r * BLOCK:(r + 1) * BLOCK].astype(BF16))

    s = scores(0)
    for i in range(n_blocks):
        s_next = scores(i + 1) if i + 1 < n_blocks else None
        finish(i, s)
        s = s_next


def _swa(table, sinks, qst, ks, vst, batch):
    rows = ks.shape[0]
    p = rows // batch
    n_t = SWA_STEPS
    swa_tile = p // n_t
    width = SWA_GROUP * HEAD_DIM
    bucket = jnp.asarray(_swa_static_buckets())
    smem = pl.BlockSpec(memory_space=pltpu.SMEM)
    tile = pl.BlockSpec((width, swa_tile), lambda b, g, t: (g, b * n_t + t))
    return pl.pallas_call(
        _swa_body,
        grid=(batch, SWA_KV_HEADS, n_t),
        in_specs=[smem, smem, _const_spec((SWA_KEYS, BLOCK)), tile,
                  pl.BlockSpec((p, SWA_KVW), lambda b, g, t: (b, 0)),
                  pl.BlockSpec((SWA_KVW, p), lambda b, g, t: (0, b))],
        out_specs=tile,
        out_shape=jax.ShapeDtypeStruct((SWA_QW, rows), BF16),
        scratch_shapes=[pltpu.VMEM((3, SWA_KEYS, SWA_GROUP * BLOCK), F32)],
        compiler_params=_params("arbitrary", "arbitrary", "arbitrary"),
        name="swa",
    )(table, sinks, bucket, qst, ks, vst)


def _mix_body(h_ref, of_ref, os_ref, ga_ref, gb_ref, wf_ref, ws_ref, wo_ref, o_ref):
    y = (ga_ref[...].astype(F32) * _dot(wf_ref[...], of_ref[...])
         + gb_ref[...].astype(F32) * _dot(ws_ref[...], os_ref[...]))
    o_ref[...] = h_ref[...] + _dot_tn(y.astype(BF16), wo_ref[...])


def _mix_out(h2, o_fox, o_swa, ga, gb, wf_t, ws_t, wo):
    rows, d = h2.shape
    tm = ROW_TILE

    def feat(r):
        return pl.BlockSpec((r, tm), lambda i: (0, i))

    tok = pl.BlockSpec((tm, d), lambda i: (i, 0))
    return pl.pallas_call(
        _mix_body,
        grid=(rows // tm,),
        in_specs=[tok, feat(FOX_W), feat(SWA_QW), feat(d), feat(d),
                  _const_spec(wf_t.shape), _const_spec(ws_t.shape), _const_spec(wo.shape)],
        out_specs=tok,
        out_shape=jax.ShapeDtypeStruct((rows, d), F32),
        compiler_params=_params("arbitrary"),
        name="mix_out",
    )(h2, o_fox, o_swa, ga, gb, wf_t, ws_t, wo)


def _split_w_in(w, d_model):
    idx = np.cumsum([FOX_W, FOX_W, FOX_W, FOX_HEADS, SWA_QW, SWA_KVW, SWA_KVW, d_model])
    qa, ka, va, fa, qb, kb, vb, ga, gb = jnp.split(w.astype(BF16), idx, axis=-1)
    fa = jnp.pad(fa, ((0, 0), (0, FA_ROWS - FOX_HEADS)))
    w_feat = jnp.concatenate([qa, va, qb, vb, fa, ga, gb], axis=-1).T
    return w_feat, jnp.concatenate([ka, kb], axis=-1)


def kernel(x, meta_tokens, rel_bias_table, ffn1_norm, ffn1_w_in, ffn1_w_out, mix_norm, w_in, forget_bias, fox_q_norm, fox_k_norm, swa_q_norm, swa_k_norm, swa_sinks, w_branch_fox, w_branch_swa, w_out, ffn2_norm, ffn2_w_in, ffn2_w_out):
    b, seq, d = x.shape
    depth = w_in.shape[0]
    p = BLOCK + seq
    assert seq % FOX_TILE == 0 and p % (SWA_STEPS * BLOCK) == 0 and (b * p) % ROW_TILE == 0
    pad = jnp.zeros((b, PAD_FRONT, d), x.dtype)
    meta = jnp.broadcast_to(meta_tokens.astype(x.dtype)[None], (b, N_META, d))
    h = jnp.concatenate([pad, meta, x], axis=1).reshape(b * p, d)
    table = rel_bias_table.astype(F32)

    def col(v):
        return v.astype(F32)[:, None]

    def twice(v):
        return jnp.tile(v.astype(F32), 2)[None]

    for l in range(depth):
        h = _ffn(h, ffn1_norm[l][None], ffn1_w_in[l], ffn1_w_out[l])
        w_feat, w_keys = _split_w_in(w_in[l], d)
        fb = jnp.pad(col(forget_bias[l]), ((0, FA_ROWS - FOX_HEADS), (0, 0)))
        qft, kf, vt, qst, ks, vst, ga, gb = _proj(
            h, p, mix_norm[l][None], w_feat, w_keys, fb,
            col(fox_q_norm[l]), twice(fox_k_norm[l]), col(swa_q_norm[l]), twice(swa_k_norm[l]))
        o_fox = _fox(qft, kf, vt, b)
        o_swa = _swa(table, swa_sinks[l].astype(F32), qst, ks, vst, b)
        h = _mix_out(h, o_fox, o_swa, ga, gb, w_branch_fox[l].astype(BF16).T,
                     w_branch_swa[l].astype(BF16).T, w_out[l].astype(BF16))
        h = _ffn(h, ffn2_norm[l][None], ffn2_w_in[l], ffn2_w_out[l])
    return h.reshape(b, p, d)[:, BLOCK:]
```

```python
import functools
import math

import jax
import jax.numpy as jnp
import numpy as np
from jax import lax
from jax.experimental import pallas as pl
from jax.experimental.pallas import tpu as pltpu

F32 = jnp.float32
BF16 = jnp.bfloat16

N_META = 16
BLOCK = 128
PAD_FRONT = BLOCK - N_META
HEAD_DIM = 64
FOX_HEADS = 8
SWA_Q_HEADS = 8
SWA_KV_HEADS = 2
SWA_GROUP = SWA_Q_HEADS // SWA_KV_HEADS
WINDOW = 128
N_BUCKETS = 32
MAX_DISTANCE = 128
EPS = 1e-6
NEG = -1e30
FOX_W = FOX_HEADS * HEAD_DIM
SWA_QW = SWA_Q_HEADS * HEAD_DIM
SWA_KVW = SWA_KV_HEADS * HEAD_DIM
SCALE = HEAD_DIM ** -0.5
LOG2E = math.log2(math.e)

LANES = 128
VMEM_LIMIT = 56 * 1024 * 1024
ROW_TILE = 512
FFN_CHUNK = 256
FOX_TILE = 512
FOX_GROUP = 4
AUG = LANES
AUG_ROWS = 32
V_ROWS = HEAD_DIM + 16
FA_ROWS = 16
SWA_KEYS = 3 * BLOCK
SWA_STEPS = 3

_R_QA = 0
_R_VA = _R_QA + FOX_W
_R_QB = _R_VA + FOX_W
_R_VB = _R_QB + SWA_QW
_R_FA = _R_VB + SWA_KVW
_R_GA = _R_FA + FA_ROWS


def _params(*sem, flags=None):
    return pltpu.CompilerParams(dimension_semantics=sem, vmem_limit_bytes=VMEM_LIMIT, flags=flags)


def _const_spec(shape):
    nd = len(shape)
    return pl.BlockSpec(shape, lambda *_: (0,) * nd, pipeline_mode=pl.Buffered(1))


def _dot(a, b):
    return jnp.dot(a, b, preferred_element_type=F32)


def _dot_nt(a, b):
    return lax.dot_general(a, b, (((1,), (1,)), ((), ())), preferred_element_type=F32)


def _dot_tn(a, b):
    return lax.dot_general(a, b, (((0,), (0,)), ((), ())), preferred_element_type=F32)


def _rms_rows(x, g):
    ms = jnp.mean(x * x, axis=-1, keepdims=True)
    return x * lax.rsqrt(ms + EPS) * g


def _ffn_body(x_ref, g_ref, win_ref, wout_ref, o_ref, win_bf, wout_bf, acc_ref, *, d_ff, n_pro):
    i = pl.program_id(0)
    in_chunk = win_ref.shape[1]

    @pl.when(i < n_pro)
    def _():
        win_bf[:, pl.ds(pl.multiple_of(i * in_chunk, LANES), in_chunk)] = win_ref[...].astype(BF16)
        wout_bf[pl.ds(pl.multiple_of(i * FFN_CHUNK, FFN_CHUNK), FFN_CHUNK), :] = (
            wout_ref[...].astype(BF16))

    @pl.when(i >= n_pro)
    def _():
        x = x_ref[...]
        xn = _rms_rows(x, g_ref[...]).astype(BF16)
        for c in range(d_ff // FFN_CHUNK):
            lo = c * FFN_CHUNK
            gate = _dot(xn, win_bf[:, lo:lo + FFN_CHUNK])
            up = _dot(xn, win_bf[:, d_ff + lo:d_ff + lo + FFN_CHUNK])
            act = (gate * jax.nn.sigmoid(gate) * up).astype(BF16)
            part = _dot(act, wout_bf[lo:lo + FFN_CHUNK, :])
            if c == 0:
                acc_ref[...] = part
            else:
                acc_ref[...] += part
        o_ref[...] = x + 0.5 * acc_ref[...]


def _ffn(h2, gain, w_in, w_out, layer, real_only=None):
    rows, d = h2.shape
    d_ff = w_out.shape[1]
    tm = ROW_TILE
    n_pro = d_ff // FFN_CHUNK
    in_chunk = 2 * d_ff // n_pro

    def row_tile(i):
        return (jnp.maximum(i - n_pro, 0), 0)

    if real_only is None:
        x_arg, n_tiles = h2, rows // tm
        x_spec = out_spec = pl.BlockSpec((tm, d), row_tile)
        out_shape = jax.ShapeDtypeStruct((rows, d), F32)
    else:
        batch, seq = real_only
        per_seq = seq // tm
        x_arg, n_tiles = h2, batch * per_seq

        def seq_tile(i):
            r = jnp.maximum(i - n_pro, 0)
            return r // per_seq, r % per_seq

        def x_map(i):
            b, t = seq_tile(i)
            return (pl.multiple_of(b * (rows // batch) + BLOCK + t * tm, BLOCK), 0)

        def out_map(i):
            b, t = seq_tile(i)
            return (b, t, 0)

        x_spec = pl.BlockSpec((pl.Element(tm), pl.Element(d)), x_map)
        out_spec = pl.BlockSpec((None, tm, d), out_map)
        out_shape = jax.ShapeDtypeStruct((batch, seq, d), F32)

    return pl.pallas_call(
        functools.partial(_ffn_body, d_ff=d_ff, n_pro=n_pro),
        grid=(n_pro + n_tiles,),
        in_specs=[
            x_spec,
            _const_spec((1, d)),
            pl.BlockSpec((None, d, in_chunk), lambda i: (layer, 0, jnp.minimum(i, n_pro - 1))),
            pl.BlockSpec((None, FFN_CHUNK, d), lambda i: (layer, jnp.minimum(i, n_pro - 1), 0)),
        ],
        out_specs=out_spec,
        out_shape=out_shape,
        scratch_shapes=[pltpu.VMEM((d, 2 * d_ff), BF16), pltpu.VMEM((d_ff, d), BF16),
                        pltpu.VMEM((tm, d), F32)],
        compiler_params=_params("arbitrary"),
        name="ffn",
    )(x_arg, gain, w_in, w_out)


def _split3(x):
    hi = x.astype(BF16).astype(F32)
    r = x - hi
    mid = r.astype(BF16).astype(F32)
    return hi, mid, r - mid


def _head_rms_cols(x, g):
    ms = jnp.mean(x * x, axis=0, keepdims=True)
    return x * lax.rsqrt(ms + EPS) * g


def _pair_rms_rows(x, g2):
    lane = lax.broadcasted_iota(jnp.int32, x.shape, 1)
    sq = x * x
    first = lane < HEAD_DIM
    s0 = jnp.sum(jnp.where(first, sq, 0.0), axis=-1, keepdims=True)
    s1 = jnp.sum(jnp.where(first, 0.0, sq), axis=-1, keepdims=True)
    ms = jnp.where(first, s0, s1) * (1.0 / HEAD_DIM)
    return x * lax.rsqrt(ms + EPS) * g2


def _proj_body(x_ref, g_ref, wt_ref, wk_ref, fb_ref, nfq_ref, nfk_ref, nsq_ref, nsk_ref,
               qf_ref, kf_ref, vt_ref, qs_ref, ks_ref, vs_ref, ga_ref, gb_ref,
               carry_ref, *, d_model, seq_pad):
    i = pl.program_id(0)
    tm = x_ref.shape[0]
    xn = _rms_rows(x_ref[...], g_ref[...]).astype(BF16)

    def feat(lo, rows):
        return _dot_nt(wt_ref[lo:lo + rows, :], xn)

    lf = jax.nn.log_sigmoid(feat(_R_FA, FA_ROWS) + fb_ref[...])
    r0 = i * tm
    boundary = (r0 // seq_pad + 1) * seq_pad - r0
    src = lax.broadcasted_iota(jnp.int32, (tm, tm), 0)
    dst = lax.broadcasted_iota(jnp.int32, (tm, tm), 1)
    upper = ((src <= dst) & ((src < boundary) == (dst < boundary))).astype(BF16)
    hi, mid, lo3 = _split3(lf)
    c_loc = (_dot(hi.astype(BF16), upper) + _dot(mid.astype(BF16), upper)
             + _dot(lo3.astype(BF16), upper))
    carry = jnp.where(r0 % seq_pad == 0, 0.0, carry_ref[:, 0:1])
    tok = lax.broadcasted_iota(jnp.int32, (1, tm), 1)
    c_t = c_loc + jnp.where(tok < boundary, carry, 0.0)
    carry_ref[...] = jnp.broadcast_to(c_t[:, tm - 1:tm], carry_ref.shape)
    c_t = c_t * LOG2E

    c_pad = jnp.concatenate(
        [jnp.where(lax.broadcasted_iota(jnp.int32, (FA_ROWS, tm), 0) < FOX_HEADS, c_t, 0.0),
         jnp.zeros((LANES - FA_ROWS, tm), F32)], axis=0)
    c_tok = c_pad.T
    lane = lax.broadcasted_iota(jnp.int32, (tm, LANES), 1)
    k_hi, k_mid, k_lo = _split3(c_tok)
    aug_lo = (-k_hi - pltpu.roll(k_mid, FOX_HEADS, 1) - pltpu.roll(k_lo, 2 * FOX_HEADS, 1)
              + jnp.where((lane >= 3 * FOX_HEADS) & (lane < 3 * FOX_HEADS + 3), 1.0, 0.0))
    aug_hi = pltpu.roll(aug_lo, HEAD_DIM, 1)
    first = lane < HEAD_DIM

    kt = _dot(xn, wk_ref[...])
    for j in range(FOX_HEADS // 2):
        kn = _pair_rms_rows(kt[:, j * LANES:(j + 1) * LANES], nfk_ref[...])
        kf_ref[:, (2 * j) * AUG:(2 * j + 1) * AUG] = jnp.where(first, kn, aug_hi).astype(BF16)
        kf_ref[:, (2 * j + 1) * AUG:(2 * j + 2) * AUG] = jnp.where(first, aug_lo, kn).astype(BF16)
    ks_ref[...] = _pair_rms_rows(kt[:, FOX_W:FOX_W + SWA_KVW], nsk_ref[...]).astype(BF16)

    arow = lax.broadcasted_iota(jnp.int32, (AUG_ROWS, tm), 0)
    qa = feat(_R_QA, FOX_W)
    zeros = jnp.zeros((HEAD_DIM - AUG_ROWS, tm), BF16)
    for h in range(FOX_HEADS):
        q_hi, q_mid, q_lo = _split3(c_t[h:h + 1, :])
        onehot = jnp.where((arow % FOX_HEADS == h) & (arow < 3 * FOX_HEADS), 1.0, 0.0)
        aug_q = jnp.where(arow == 3 * FOX_HEADS, q_hi, jnp.where(
            arow == 3 * FOX_HEADS + 1, q_mid, jnp.where(arow == 3 * FOX_HEADS + 2, q_lo, onehot)))
        qn = (_head_rms_cols(qa[h * HEAD_DIM:(h + 1) * HEAD_DIM], nfq_ref[...])
              * (SCALE * LOG2E)).astype(BF16)
        parts = [qn, aug_q.astype(BF16), zeros] if h % 2 == 0 else [aug_q.astype(BF16), zeros, qn]
        qf_ref[h * AUG:(h + 1) * AUG, :] = jnp.concatenate(parts, axis=0)
    va = feat(_R_VA, FOX_W).astype(BF16)
    ones = jnp.ones((V_ROWS - HEAD_DIM, tm), BF16)
    for h in range(FOX_HEADS):
        vt_ref[h * V_ROWS:(h + 1) * V_ROWS, :] = jnp.concatenate(
            [va[h * HEAD_DIM:(h + 1) * HEAD_DIM], ones], axis=0)

    qb = feat(_R_QB, SWA_QW)
    for h in range(SWA_Q_HEADS):
        sl = slice(h * HEAD_DIM, (h + 1) * HEAD_DIM)
        qs_ref[sl, :] = (_head_rms_cols(qb[sl], nsq_ref[...]) * SCALE).astype(BF16)
    vs_ref[...] = feat(_R_VB, SWA_KVW).astype(BF16)
    ga_ref[...] = jax.nn.sigmoid(feat(_R_GA, d_model)).astype(BF16)
    gb_ref[...] = jax.nn.sigmoid(feat(_R_GA + d_model, d_model)).astype(BF16)


def _proj(h2, seq_pad, gain, w_feat, w_keys, fb, nfq, nfk, nsq, nsk):
    rows, d = h2.shape
    tm = ROW_TILE

    def tok(w):
        return pl.BlockSpec((tm, w), lambda i: (i, 0)), jax.ShapeDtypeStruct((rows, w), BF16)

    def feat(r):
        return pl.BlockSpec((r, tm), lambda i: (0, i)), jax.ShapeDtypeStruct((r, rows), BF16)

    outs = [feat(FOX_HEADS * AUG), tok(FOX_HEADS * AUG), feat(FOX_HEADS * V_ROWS), feat(SWA_QW),
            tok(SWA_KVW), feat(SWA_KVW), feat(d), feat(d)]
    return pl.pallas_call(
        functools.partial(_proj_body, d_model=d, seq_pad=seq_pad),
        grid=(rows // tm,),
        in_specs=[
            pl.BlockSpec((tm, d), lambda i: (i, 0)),
            _const_spec((1, d)),
            _const_spec(w_feat.shape),
            _const_spec(w_keys.shape),
            _const_spec((FA_ROWS, 1)),
            _const_spec((HEAD_DIM, 1)), _const_spec((1, LANES)),
            _const_spec((HEAD_DIM, 1)), _const_spec((1, LANES)),
        ],
        out_specs=[o[0] for o in outs],
        out_shape=[o[1] for o in outs],
        scratch_shapes=[pltpu.VMEM((FA_ROWS, LANES), F32)],
        compiler_params=_params("arbitrary"),
        name="proj",
    )(h2, gain, w_feat, w_keys, fb, nfq, nfk, nsq, nsk)


def _fox_body(q_ref, k_ref, vt_ref, o_ref, m_ref, acc_ref, st_ref):
    qi = pl.program_id(2)

    def attend(q_start, tq, n_full, diag):
        q = q_ref[:, pl.ds(q_start, tq)]
        m_ref[:, :, :tq] = jnp.full((FOX_GROUP, 1, tq), NEG, F32)
        acc_ref[:, :tq] = jnp.zeros((FOX_GROUP * V_ROWS, tq), F32)

        def scores(kblk, h):
            return _dot(kblk[:, h * AUG:(h + 1) * AUG], q[h * AUG:(h + 1) * AUG, :])

        def softmax(h, st, mask):
            if mask is not None:
                st = jnp.where(mask, st, NEG)
            m_prev = m_ref[h, :, :tq]
            m_new = jnp.maximum(m_prev, jnp.max(st, axis=0, keepdims=True))
            alpha = jnp.exp2(m_prev - m_new)
            m_ref[h, :, :tq] = m_new
            return jnp.exp2(st - m_new).astype(BF16), alpha

        def weighted_values(h, vt, p_alpha):
            p, alpha = p_alpha
            rows = slice(h * V_ROWS, (h + 1) * V_ROWS)
            acc_ref[rows, :tq] = alpha * acc_ref[rows, :tq] + _dot(vt[rows, :], p)

        def process(k_start, tk, mask):
            kblk = k_ref[pl.ds(k_start, tk), :]
            vt = vt_ref[:, pl.ds(k_start, tk)]
            for h in range(FOX_GROUP):
                weighted_values(h, vt, softmax(h, scores(kblk, h), mask))

        def key_tile(j):
            return k_ref[pl.ds(pl.multiple_of(BLOCK + j * FOX_TILE, BLOCK), FOX_TILE), :]

        def pipelined(j, mask, prefetch):
            kblk = key_tile(j)
            vt = vt_ref[:, pl.ds(pl.multiple_of(BLOCK + j * FOX_TILE, BLOCK), FOX_TILE)]
            st = [st_ref[...], scores(kblk, 1), None, None]
            pa = softmax(0, st[0], mask)
            for h in range(FOX_GROUP):
                if h + 2 < FOX_GROUP:
                    st[h + 2] = scores(kblk, h + 2)
                elif h + 2 == FOX_GROUP and prefetch:
                    st_ref[...] = scores(key_tile(j + 1), 0)
                weighted_values(h, vt, pa)
                if h + 1 < FOX_GROUP:
                    pa = softmax(h + 1, st[h + 1], mask)

        kpos0 = lax.broadcasted_iota(jnp.int32, (BLOCK, tq), 0)
        if diag:
            process(0, BLOCK, kpos0 >= PAD_FRONT)
            st_ref[...] = scores(key_tile(0), 0)

            def tile_pair(jj, carry):
                pipelined(2 * jj, None, True)
                pipelined(2 * jj + 1, None, True)
                return carry

            lax.fori_loop(0, n_full // 2, tile_pair, 0)

            @pl.when(n_full % 2 == 1)
            def _():
                pipelined(n_full - 1, None, True)
            kr = lax.broadcasted_iota(jnp.int32, (tq, tq), 0)
            qc = lax.broadcasted_iota(jnp.int32, (tq, tq), 1)
            pipelined(n_full, kr <= qc, False)
        else:
            qc = lax.broadcasted_iota(jnp.int32, (BLOCK, tq), 1)
            process(0, BLOCK, (kpos0 >= PAD_FRONT) & (kpos0 <= qc))
        for h in range(FOX_GROUP):
            denom = acc_ref[h * V_ROWS + HEAD_DIM:h * V_ROWS + HEAD_DIM + 1, :tq]
            o_ref[h * HEAD_DIM:(h + 1) * HEAD_DIM, pl.ds(q_start, tq)] = (
                acc_ref[h * V_ROWS:h * V_ROWS + HEAD_DIM, :tq] * (1.0 / denom)).astype(BF16)

    @pl.when(qi == 0)
    def _():
        attend(0, BLOCK, 0, False)

    @pl.when(qi > 0)
    def _():
        attend(pl.multiple_of(BLOCK + (qi - 1) * FOX_TILE, BLOCK), FOX_TILE, qi - 1, True)


def _fox(qft, kf, vt, batch):
    rows = kf.shape[0]
    p = rows // batch
    n_grp = FOX_HEADS // FOX_GROUP
    width = FOX_GROUP * HEAD_DIM
    n_q = (p - BLOCK) // FOX_TILE + 1

    def feat(r):
        return pl.BlockSpec((r, p), lambda b, g, q: (g, b))

    return pl.pallas_call(
        _fox_body,
        grid=(batch, n_grp, n_q),
        in_specs=[feat(FOX_GROUP * AUG),
                  pl.BlockSpec((p, FOX_GROUP * AUG), lambda b, g, q: (b, g)),
                  feat(FOX_GROUP * V_ROWS)],
        out_specs=feat(width),
        out_shape=jax.ShapeDtypeStruct((FOX_W, rows), BF16),
        scratch_shapes=[
            pltpu.VMEM((FOX_GROUP, 1, FOX_TILE), F32),
            pltpu.VMEM((FOX_GROUP * V_ROWS, FOX_TILE), F32),
            pltpu.VMEM((FOX_TILE, FOX_TILE), F32),
        ],
        compiler_params=_params("arbitrary", "arbitrary", "arbitrary"),
        name="fox",
    )(qft, kf, vt)


def _t5_bucket_np(dist):
    n = np.maximum(dist, 0)
    max_exact = N_BUCKETS // 2
    nf = np.maximum(n, 1).astype(np.float64)
    large = max_exact + (np.log(nf / max_exact) / math.log(MAX_DISTANCE / max_exact)
                         * (N_BUCKETS - max_exact)).astype(np.int64)
    large = np.minimum(large, N_BUCKETS - 1)
    return np.where(n < max_exact, n, large).astype(np.int32)


def _swa_static_buckets():
    k = np.arange(SWA_KEYS)[:, None]
    q = np.arange(BLOCK)[None, :]
    d = BLOCK + q - k
    bucket = np.where((d >= 0) & (d < WINDOW) & (k < 2 * BLOCK), _t5_bucket_np(d), -1)
    bucket = np.where(k >= 2 * BLOCK + PAD_FRONT, N_BUCKETS - 1, bucket)
    return np.ascontiguousarray(np.broadcast_to(bucket, (SWA_KEYS, BLOCK))).astype(np.int32)


def _swa_body(table_ref, sink_ref, bucket_ref, q_ref, k_ref, vt_ref, o_ref, bias_ref):
    g = pl.program_id(1)
    t = pl.program_id(2)
    cols = SWA_GROUP * BLOCK

    @pl.when(t == 0)
    def _():
        bucket = bucket_ref[...]
        slot = lax.broadcasted_iota(jnp.int32, (SWA_KEYS, BLOCK), 0)
        qpos = lax.broadcasted_iota(jnp.int32, (SWA_KEYS, BLOCK), 1)
        for r in range(SWA_GROUP):
            bias = jnp.full((SWA_KEYS, BLOCK), NEG, F32)
            for bkt in range(N_BUCKETS):
                bias = jnp.where(bucket == bkt, table_ref[bkt, g * SWA_GROUP + r], bias)
            bias_ref[2, :, r * BLOCK:(r + 1) * BLOCK] = bias
            for n in range(2):
                band_ok = (slot >= PAD_FRONT - (n - 1) * BLOCK) & (slot < 2 * BLOCK)
                meta_ok = ((slot >= 2 * BLOCK + PAD_FRONT)
                           & (n * BLOCK + qpos - (slot - 2 * BLOCK) >= WINDOW))
                bias_ref[n, :, r * BLOCK:(r + 1) * BLOCK] = jnp.where(band_ok | meta_ok, bias, NEG)

    sink = jnp.concatenate(
        [jnp.full((1, BLOCK), sink_ref[g * SWA_GROUP + r], F32) for r in range(SWA_GROUP)], axis=1)
    grp_rows = lax.broadcasted_iota(jnp.int32, (SWA_KVW, cols), 0) // HEAD_DIM
    v_rows = pl.ds(pl.multiple_of(g * HEAD_DIM, HEAD_DIM), HEAD_DIM)
    k_lead = k_ref[0:BLOCK, :]
    v_lead = vt_ref[v_rows, 0:BLOCK]

    n_blocks = q_ref.shape[1] // BLOCK

    def band(i):
        n = t * n_blocks + i
        return (n, pl.multiple_of(jnp.maximum(n - 1, 0) * BLOCK, BLOCK),
                pl.multiple_of(n * BLOCK, BLOCK))

    def scores(i):
        n, prev, cur = band(i)
        keys = jnp.concatenate(
            [k_ref[pl.ds(prev, BLOCK), :], k_ref[pl.ds(cur, BLOCK), :], k_lead], axis=0)
        q4 = jnp.concatenate(
            [q_ref[r * HEAD_DIM:(r + 1) * HEAD_DIM, i * BLOCK:(i + 1) * BLOCK]
             for r in range(SWA_GROUP)], axis=1)
        q8 = jnp.concatenate([q4, q4], axis=0)
        q8 = jnp.where(grp_rows == g, q8, jnp.zeros_like(q8))
        return _dot(keys, q8) + bias_ref[jnp.minimum(n, 2)]

    def finish(i, s):
        _, prev, cur = band(i)
        vals = jnp.concatenate(
            [vt_ref[v_rows, pl.ds(prev, BLOCK)], vt_ref[v_rows, pl.ds(cur, BLOCK)], v_lead], axis=1)
        mx = jnp.maximum(jnp.max(s, axis=0, keepdims=True), sink)
        p = jnp.exp(s - mx)
        denom = jnp.sum(p, axis=0, keepdims=True) + jnp.exp(sink - mx)
        out = _dot(vals, p.astype(BF16)) * (1.0 / denom)
        for r in range(SWA_GROUP):
            o_ref[r * HEAD_DIM:(r + 1) * HEAD_DIM, i * BLOCK:(i + 1) * BLOCK] = (
                out[:, r * BLOCK:(r + 1) * BLOCK].astype(BF16))

    s = scores(0)
    for i in range(n_blocks):
        s_next = scores(i + 1) if i + 1 < n_blocks else None
        finish(i, s)
        s = s_next


def _swa(table, sinks, qst, ks, vst, batch):
    rows = ks.shape[0]
    p = rows // batch
    n_t = SWA_STEPS
    swa_tile = p // n_t
    width = SWA_GROUP * HEAD_DIM
    bucket = jnp.asarray(_swa_static_buckets())
    smem = pl.BlockSpec(memory_space=pltpu.SMEM)
    tile = pl.BlockSpec((width, swa_tile), lambda b, g, t: (g, b * n_t + t))
    return pl.pallas_call(
        _swa_body,
        grid=(batch, SWA_KV_HEADS, n_t),
        in_specs=[smem, smem, _const_spec((SWA_KEYS, BLOCK)), tile,
                  pl.BlockSpec((p, SWA_KVW), lambda b, g, t: (b, 0)),
                  pl.BlockSpec((SWA_KVW, p), lambda b, g, t: (0, b))],
        out_specs=tile,
        out_shape=jax.ShapeDtypeStruct((SWA_QW, rows), BF16),
        scratch_shapes=[pltpu.VMEM((3, SWA_KEYS, SWA_GROUP * BLOCK), F32)],
        compiler_params=_params("arbitrary", "arbitrary", "arbitrary"),
        name="swa",
    )(table, sinks, bucket, qst, ks, vst)


def _mix_body(h_ref, of_ref, os_ref, ga_ref, gb_ref, wf_ref, ws_ref, wo_ref, o_ref):
    y = (ga_ref[...].astype(F32) * _dot(wf_ref[...], of_ref[...])
         + gb_ref[...].astype(F32) * _dot(ws_ref[...], os_ref[...]))
    o_ref[...] = h_ref[...] + _dot_tn(y.astype(BF16), wo_ref[...])


def _mix_out(h2, o_fox, o_swa, ga, gb, wf_t, ws_t, wo):
    rows, d = h2.shape
    tm = ROW_TILE

    def feat(r):
        return pl.BlockSpec((r, tm), lambda i: (0, i))

    tok = pl.BlockSpec((tm, d), lambda i: (i, 0))
    return pl.pallas_call(
        _mix_body,
        grid=(rows // tm,),
        in_specs=[tok, feat(FOX_W), feat(SWA_QW), feat(d), feat(d),
                  _const_spec(wf_t.shape), _const_spec(ws_t.shape), _const_spec(wo.shape)],
        out_specs=tok,
        out_shape=jax.ShapeDtypeStruct((rows, d), F32),
        compiler_params=_params("arbitrary"),
        name="mix_out",
    )(h2, o_fox, o_swa, ga, gb, wf_t, ws_t, wo)


def _split_w_in(w, d_model):
    idx = np.cumsum([FOX_W, FOX_W, FOX_W, FOX_HEADS, SWA_QW, SWA_KVW, SWA_KVW, d_model])
    qa, ka, va, fa, qb, kb, vb, ga, gb = jnp.split(w.astype(BF16), idx, axis=-1)
    fa = jnp.pad(fa, ((0, 0), (0, FA_ROWS - FOX_HEADS)))
    w_feat = jnp.concatenate([qa, va, qb, vb, fa, ga, gb], axis=-1).T
    return w_feat, jnp.concatenate([ka, kb], axis=-1)


def kernel(x, meta_tokens, rel_bias_table, ffn1_norm, ffn1_w_in, ffn1_w_out, mix_norm, w_in, forget_bias, fox_q_norm, fox_k_norm, swa_q_norm, swa_k_norm, swa_sinks, w_branch_fox, w_branch_swa, w_out, ffn2_norm, ffn2_w_in, ffn2_w_out):
    b, seq, d = x.shape
    depth = w_in.shape[0]
    p = BLOCK + seq
    assert seq % FOX_TILE == 0 and p % (SWA_STEPS * BLOCK) == 0 and (b * p) % ROW_TILE == 0
    pad = jnp.zeros((b, PAD_FRONT, d), x.dtype)
    meta = jnp.broadcast_to(meta_tokens.astype(x.dtype)[None], (b, N_META, d))
    h = jnp.concatenate([pad, meta, x], axis=1).reshape(b * p, d)
    table = rel_bias_table.astype(F32)

    def col(v):
        return v.astype(F32)[:, None]

    def twice(v):
        return jnp.tile(v.astype(F32), 2)[None]

    for l in range(depth):
        h = _ffn(h, ffn1_norm[l][None], ffn1_w_in, ffn1_w_out, l)
        w_feat, w_keys = _split_w_in(w_in[l], d)
        fb = jnp.pad(col(forget_bias[l]), ((0, FA_ROWS - FOX_HEADS), (0, 0)))
        qft, kf, vt, qst, ks, vst, ga, gb = _proj(
            h, p, mix_norm[l][None], w_feat, w_keys, fb,
            col(fox_q_norm[l]), twice(fox_k_norm[l]), col(swa_q_norm[l]), twice(swa_k_norm[l]))
        o_fox = _fox(qft, kf, vt, b)
        o_swa = _swa(table, swa_sinks[l].astype(F32), qst, ks, vst, b)
        h = _mix_out(h, o_fox, o_swa, ga, gb, w_branch_fox[l].astype(BF16).T,
                     w_branch_swa[l].astype(BF16).T, w_out[l].astype(BF16))
        last = l == depth - 1
        h = _ffn(h, ffn2_norm[l][None], ffn2_w_in, ffn2_w_out, l, (b, seq) if last else None)
    return h
```

```python
import functools
import math

import jax
import jax.numpy as jnp
import numpy as np
from jax import lax
from jax.experimental import pallas as pl
from jax.experimental.pallas import tpu as pltpu

F32 = jnp.float32
BF16 = jnp.bfloat16

N_META = 16
BLOCK = 128
PAD_FRONT = BLOCK - N_META
HEAD_DIM = 64
FOX_HEADS = 8
SWA_Q_HEADS = 8
SWA_KV_HEADS = 2
SWA_GROUP = SWA_Q_HEADS // SWA_KV_HEADS
WINDOW = 128
N_BUCKETS = 32
MAX_DISTANCE = 128
EPS = 1e-6
NEG = -1e30
FOX_W = FOX_HEADS * HEAD_DIM
SWA_QW = SWA_Q_HEADS * HEAD_DIM
SWA_KVW = SWA_KV_HEADS * HEAD_DIM
SCALE = HEAD_DIM ** -0.5
LOG2E = math.log2(math.e)

LANES = 128
VMEM_LIMIT = 56 * 1024 * 1024
ROW_TILE = 512
FFN_CHUNK = 256
FOX_TILE = 512
FOX_GROUP = 4
AUG = LANES
AUG_ROWS = 32
V_ROWS = HEAD_DIM + 16
FA_ROWS = 16
SWA_KEYS = 3 * BLOCK
SWA_STEPS = 3

_R_QA = 0
_R_VA = _R_QA + FOX_W
_R_QB = _R_VA + FOX_W
_R_VB = _R_QB + SWA_QW
_R_FA = _R_VB + SWA_KVW
_R_GA = _R_FA + FA_ROWS


def _params(*sem, flags=None):
    return pltpu.CompilerParams(dimension_semantics=sem, vmem_limit_bytes=VMEM_LIMIT, flags=flags)


def _const_spec(shape):
    nd = len(shape)
    return pl.BlockSpec(shape, lambda *_: (0,) * nd, pipeline_mode=pl.Buffered(1))


def _dot(a, b):
    return jnp.dot(a, b, preferred_element_type=F32)


def _dot_nt(a, b):
    return lax.dot_general(a, b, (((1,), (1,)), ((), ())), preferred_element_type=F32)


def _dot_tn(a, b):
    return lax.dot_general(a, b, (((0,), (0,)), ((), ())), preferred_element_type=F32)


def _rms_rows(x, g):
    ms = jnp.mean(x * x, axis=-1, keepdims=True)
    return x * lax.rsqrt(ms + EPS) * g


def _ffn_body(x_ref, g_ref, win_ref, wout_ref, o_ref, win_bf, wout_bf, acc_ref, *, d_ff, n_pro):
    i = pl.program_id(0)
    in_chunk = win_ref.shape[1]

    @pl.when(i < n_pro)
    def _():
        win_bf[:, pl.ds(pl.multiple_of(i * in_chunk, LANES), in_chunk)] = win_ref[...].astype(BF16)
        wout_bf[pl.ds(pl.multiple_of(i * FFN_CHUNK, FFN_CHUNK), FFN_CHUNK), :] = (
            wout_ref[...].astype(BF16))

    @pl.when(i >= n_pro)
    def _():
        x = x_ref[...]
        xn = _rms_rows(x, g_ref[...]).astype(BF16)
        for c in range(d_ff // FFN_CHUNK):
            lo = c * FFN_CHUNK
            gate = _dot(xn, win_bf[:, lo:lo + FFN_CHUNK])
            up = _dot(xn, win_bf[:, d_ff + lo:d_ff + lo + FFN_CHUNK])
            act = (gate * jax.nn.sigmoid(gate) * up).astype(BF16)
            part = _dot(act, wout_bf[lo:lo + FFN_CHUNK, :])
            if c == 0:
                acc_ref[...] = part
            else:
                acc_ref[...] += part
        o_ref[...] = x + 0.5 * acc_ref[...]


def _ffn(h2, gain, w_in, w_out, layer, real_only=None):
    rows, d = h2.shape
    d_ff = w_out.shape[1]
    tm = ROW_TILE
    n_pro = d_ff // FFN_CHUNK
    in_chunk = 2 * d_ff // n_pro

    def row_tile(i):
        return (jnp.maximum(i - n_pro, 0), 0)

    if real_only is None:
        x_arg, n_tiles = h2, rows // tm
        x_spec = out_spec = pl.BlockSpec((tm, d), row_tile)
        out_shape = jax.ShapeDtypeStruct((rows, d), F32)
    else:
        batch, seq = real_only
        per_seq = seq // tm
        x_arg, n_tiles = h2, batch * per_seq

        def seq_tile(i):
            r = jnp.maximum(i - n_pro, 0)
            return r // per_seq, r % per_seq

        def x_map(i):
            b, t = seq_tile(i)
            return (pl.multiple_of(b * (rows // batch) + BLOCK + t * tm, BLOCK), 0)

        def out_map(i):
            b, t = seq_tile(i)
            return (b, t, 0)

        x_spec = pl.BlockSpec((pl.Element(tm), pl.Element(d)), x_map)
        out_spec = pl.BlockSpec((None, tm, d), out_map)
        out_shape = jax.ShapeDtypeStruct((batch, seq, d), F32)

    return pl.pallas_call(
        functools.partial(_ffn_body, d_ff=d_ff, n_pro=n_pro),
        grid=(n_pro + n_tiles,),
        in_specs=[
            x_spec,
            _const_spec((1, d)),
            pl.BlockSpec((None, d, in_chunk), lambda i: (layer, 0, jnp.minimum(i, n_pro - 1))),
            pl.BlockSpec((None, FFN_CHUNK, d), lambda i: (layer, jnp.minimum(i, n_pro - 1), 0)),
        ],
        out_specs=out_spec,
        out_shape=out_shape,
        scratch_shapes=[pltpu.VMEM((d, 2 * d_ff), BF16), pltpu.VMEM((d_ff, d), BF16),
                        pltpu.VMEM((tm, d), F32)],
        compiler_params=_params("arbitrary"),
        name="ffn",
    )(x_arg, gain, w_in, w_out)


def _split3(x):
    hi = x.astype(BF16).astype(F32)
    r = x - hi
    mid = r.astype(BF16).astype(F32)
    return hi, mid, r - mid


def _head_rms_cols(x, g):
    ms = jnp.mean(x * x, axis=0, keepdims=True)
    return x * lax.rsqrt(ms + EPS) * g


def _pair_rms_rows(x, g2):
    lane = lax.broadcasted_iota(jnp.int32, x.shape, 1)
    sq = x * x
    first = lane < HEAD_DIM
    s0 = jnp.sum(jnp.where(first, sq, 0.0), axis=-1, keepdims=True)
    s1 = jnp.sum(jnp.where(first, 0.0, sq), axis=-1, keepdims=True)
    ms = jnp.where(first, s0, s1) * (1.0 / HEAD_DIM)
    return x * lax.rsqrt(ms + EPS) * g2


def _proj_body(x_ref, g_ref, wt_ref, wk_ref, fb_ref, nfq_ref, nfk_ref, nsq_ref, nsk_ref,
               qf_ref, kf_ref, vt_ref, qs_ref, ks_ref, vs_ref, ga_ref, gb_ref,
               carry_ref, tri_ref, *, d_model, seq_pad):
    i = pl.program_id(0)
    tm = x_ref.shape[0]

    @pl.when(i == 0)
    def _():
        src = lax.broadcasted_iota(jnp.int32, (tm, tm), 0)
        dst = lax.broadcasted_iota(jnp.int32, (tm, tm), 1)
        tri_ref[...] = (src <= dst).astype(BF16)

    xn = _rms_rows(x_ref[...], g_ref[...]).astype(BF16)

    def feat(lo, rows):
        return _dot_nt(wt_ref[lo:lo + rows, :], xn)

    lf = jax.nn.log_sigmoid(feat(_R_FA, FA_ROWS) + fb_ref[...])
    r0 = i * tm
    boundary = (r0 // seq_pad + 1) * seq_pad - r0
    hi, mid, lo3 = _split3(lf)
    upper = tri_ref[...]
    c_loc = (_dot(hi.astype(BF16), upper) + _dot(mid.astype(BF16), upper)
             + _dot(lo3.astype(BF16), upper))
    carry = jnp.where(r0 % seq_pad == 0, 0.0, carry_ref[:, 0:1])
    tok = lax.broadcasted_iota(jnp.int32, (1, tm), 1)
    c_end = jnp.sum(jnp.where(tok == boundary - 1, c_loc, 0.0), axis=1, keepdims=True)
    c_t = c_loc + jnp.where(tok < boundary, carry, -c_end)
    carry_ref[...] = jnp.broadcast_to(c_t[:, tm - 1:tm], carry_ref.shape)
    c_t = c_t * LOG2E

    c_pad = jnp.concatenate(
        [jnp.where(lax.broadcasted_iota(jnp.int32, (FA_ROWS, tm), 0) < FOX_HEADS, c_t, 0.0),
         jnp.zeros((LANES - FA_ROWS, tm), F32)], axis=0)
    c_tok = c_pad.T
    lane = lax.broadcasted_iota(jnp.int32, (tm, LANES), 1)
    k_hi, k_mid, k_lo = _split3(c_tok)
    aug_lo = (-k_hi - pltpu.roll(k_mid, FOX_HEADS, 1) - pltpu.roll(k_lo, 2 * FOX_HEADS, 1)
              + jnp.where((lane >= 3 * FOX_HEADS) & (lane < 3 * FOX_HEADS + 3), 1.0, 0.0))
    aug_hi = pltpu.roll(aug_lo, HEAD_DIM, 1)
    first = lane < HEAD_DIM

    kt = _dot(xn, wk_ref[...])
    for j in range(FOX_HEADS // 2):
        kn = _pair_rms_rows(kt[:, j * LANES:(j + 1) * LANES], nfk_ref[...])
        kf_ref[:, (2 * j) * AUG:(2 * j + 1) * AUG] = jnp.where(first, kn, aug_hi).astype(BF16)
        kf_ref[:, (2 * j + 1) * AUG:(2 * j + 2) * AUG] = jnp.where(first, aug_lo, kn).astype(BF16)
    ks_ref[...] = _pair_rms_rows(kt[:, FOX_W:FOX_W + SWA_KVW], nsk_ref[...]).astype(BF16)

    arow = lax.broadcasted_iota(jnp.int32, (AUG_ROWS, tm), 0)
    qa = feat(_R_QA, FOX_W)
    zeros = jnp.zeros((HEAD_DIM - AUG_ROWS, tm), BF16)
    for h in range(FOX_HEADS):
        q_hi, q_mid, q_lo = _split3(c_t[h:h + 1, :])
        onehot = jnp.where((arow % FOX_HEADS == h) & (arow < 3 * FOX_HEADS), 1.0, 0.0)
        aug_q = jnp.where(arow == 3 * FOX_HEADS, q_hi, jnp.where(
            arow == 3 * FOX_HEADS + 1, q_mid, jnp.where(arow == 3 * FOX_HEADS + 2, q_lo, onehot)))
        qn = (_head_rms_cols(qa[h * HEAD_DIM:(h + 1) * HEAD_DIM], nfq_ref[...])
              * (SCALE * LOG2E)).astype(BF16)
        parts = [qn, aug_q.astype(BF16), zeros] if h % 2 == 0 else [aug_q.astype(BF16), zeros, qn]
        qf_ref[h * AUG:(h + 1) * AUG, :] = jnp.concatenate(parts, axis=0)
    ga_ref[...] = jax.nn.sigmoid(feat(_R_GA, d_model)).astype(BF16)
    gb_ref[...] = jax.nn.sigmoid(feat(_R_GA + d_model, d_model)).astype(BF16)
    qb = feat(_R_QB, SWA_QW)
    for h in range(SWA_Q_HEADS):
        sl = slice(h * HEAD_DIM, (h + 1) * HEAD_DIM)
        qs_ref[sl, :] = (_head_rms_cols(qb[sl], nsq_ref[...]) * SCALE).astype(BF16)
    vs_ref[...] = feat(_R_VB, SWA_KVW).astype(BF16)
    va = feat(_R_VA, FOX_W).astype(BF16)
    ones = jnp.ones((V_ROWS - HEAD_DIM, tm), BF16)
    for h in range(FOX_HEADS):
        vt_ref[h * V_ROWS:(h + 1) * V_ROWS, :] = jnp.concatenate(
            [va[h * HEAD_DIM:(h + 1) * HEAD_DIM], ones], axis=0)


def _proj(h2, seq_pad, gain, w_feat, w_keys, fb, nfq, nfk, nsq, nsk):
    rows, d = h2.shape
    tm = ROW_TILE

    def tok(w):
        return pl.BlockSpec((tm, w), lambda i: (i, 0)), jax.ShapeDtypeStruct((rows, w), BF16)

    def feat(r):
        return pl.BlockSpec((r, tm), lambda i: (0, i)), jax.ShapeDtypeStruct((r, rows), BF16)

    outs = [feat(FOX_HEADS * AUG), tok(FOX_HEADS * AUG), feat(FOX_HEADS * V_ROWS), feat(SWA_QW),
            tok(SWA_KVW), feat(SWA_KVW), feat(d), feat(d)]
    return pl.pallas_call(
        functools.partial(_proj_body, d_model=d, seq_pad=seq_pad),
        grid=(rows // tm,),
        in_specs=[
            pl.BlockSpec((tm, d), lambda i: (i, 0)),
            _const_spec((1, d)),
            _const_spec(w_feat.shape),
            _const_spec(w_keys.shape),
            _const_spec((FA_ROWS, 1)),
            _const_spec((HEAD_DIM, 1)), _const_spec((1, LANES)),
            _const_spec((HEAD_DIM, 1)), _const_spec((1, LANES)),
        ],
        out_specs=[o[0] for o in outs],
        out_shape=[o[1] for o in outs],
        scratch_shapes=[pltpu.VMEM((FA_ROWS, LANES), F32), pltpu.VMEM((tm, tm), BF16)],
        compiler_params=_params("arbitrary"),
        name="proj",
    )(h2, gain, w_feat, w_keys, fb, nfq, nfk, nsq, nsk)


def _fox_body(q_ref, k_ref, vt_ref, o_ref, m_ref, acc_ref, st_ref):
    qi = pl.program_id(2)

    def attend(q_start, tq, n_full, diag):
        q = q_ref[:, pl.ds(q_start, tq)]
        m_ref[:, :, :tq] = jnp.full((FOX_GROUP, 1, tq), NEG, F32)
        acc_ref[:, :tq] = jnp.zeros((FOX_GROUP * V_ROWS, tq), F32)

        def scores(kblk, h):
            return _dot(kblk[:, h * AUG:(h + 1) * AUG], q[h * AUG:(h + 1) * AUG, :])

        def softmax(h, st, mask):
            if mask is not None:
                st = jnp.where(mask, st, NEG)
            m_prev = m_ref[h, :, :tq]
            m_new = jnp.maximum(m_prev, jnp.max(st, axis=0, keepdims=True))
            alpha = jnp.exp2(m_prev - m_new)
            m_ref[h, :, :tq] = m_new
            return jnp.exp2(st - m_new).astype(BF16), alpha

        def weighted_values(h, vt, p_alpha):
            p, alpha = p_alpha
            rows = slice(h * V_ROWS, (h + 1) * V_ROWS)
            acc_ref[rows, :tq] = alpha * acc_ref[rows, :tq] + _dot(vt[rows, :], p)

        def process(k_start, tk, mask):
            kblk = k_ref[pl.ds(k_start, tk), :]
            vt = vt_ref[:, pl.ds(k_start, tk)]
            for h in range(FOX_GROUP):
                weighted_values(h, vt, softmax(h, scores(kblk, h), mask))

        def key_tile(j):
            return k_ref[pl.ds(pl.multiple_of(BLOCK + j * FOX_TILE, BLOCK), FOX_TILE), :]

        def pipelined(j, mask, prefetch):
            kblk = key_tile(j)
            vt = vt_ref[:, pl.ds(pl.multiple_of(BLOCK + j * FOX_TILE, BLOCK), FOX_TILE)]
            st = [st_ref[...], scores(kblk, 1), None, None]
            pa = softmax(0, st[0], mask)
            for h in range(FOX_GROUP):
                if h + 2 < FOX_GROUP:
                    st[h + 2] = scores(kblk, h + 2)
                elif h + 2 == FOX_GROUP and prefetch:
                    st_ref[...] = scores(key_tile(j + 1), 0)
                weighted_values(h, vt, pa)
                if h + 1 < FOX_GROUP:
                    pa = softmax(h + 1, st[h + 1], mask)

        kpos0 = lax.broadcasted_iota(jnp.int32, (BLOCK, tq), 0)
        if diag:
            process(0, BLOCK, kpos0 >= PAD_FRONT)
            st_ref[...] = scores(key_tile(0), 0)

            def tile_pair(jj, carry):
                pipelined(2 * jj, None, True)
                pipelined(2 * jj + 1, None, True)
                return carry

            lax.fori_loop(0, n_full // 2, tile_pair, 0)

            @pl.when(n_full % 2 == 1)
            def _():
                pipelined(n_full - 1, None, True)
            kr = lax.broadcasted_iota(jnp.int32, (tq, tq), 0)
            qc = lax.broadcasted_iota(jnp.int32, (tq, tq), 1)
            pipelined(n_full, kr <= qc, False)
        else:
            qc = lax.broadcasted_iota(jnp.int32, (BLOCK, tq), 1)
            process(0, BLOCK, (kpos0 >= PAD_FRONT) & (kpos0 <= qc))
        for h in range(FOX_GROUP):
            denom = acc_ref[h * V_ROWS + HEAD_DIM:h * V_ROWS + HEAD_DIM + 1, :tq]
            o_ref[h * HEAD_DIM:(h + 1) * HEAD_DIM, pl.ds(q_start, tq)] = (
                acc_ref[h * V_ROWS:h * V_ROWS + HEAD_DIM, :tq] * (1.0 / denom)).astype(BF16)

    @pl.when(qi == 0)
    def _():
        attend(0, BLOCK, 0, False)

    @pl.when(qi > 0)
    def _():
        attend(pl.multiple_of(BLOCK + (qi - 1) * FOX_TILE, BLOCK), FOX_TILE, qi - 1, True)


def _fox(qft, kf, vt, batch):
    rows = kf.shape[0]
    p = rows // batch
    n_grp = FOX_HEADS // FOX_GROUP
    width = FOX_GROUP * HEAD_DIM
    n_q = (p - BLOCK) // FOX_TILE + 1

    def feat(r):
        return pl.BlockSpec((r, p), lambda b, g, q: (g, b))

    return pl.pallas_call(
        _fox_body,
        grid=(batch, n_grp, n_q),
        in_specs=[feat(FOX_GROUP * AUG),
                  pl.BlockSpec((p, FOX_GROUP * AUG), lambda b, g, q: (b, g)),
                  feat(FOX_GROUP * V_ROWS)],
        out_specs=feat(width),
        out_shape=jax.ShapeDtypeStruct((FOX_W, rows), BF16),
        scratch_shapes=[
            pltpu.VMEM((FOX_GROUP, 1, FOX_TILE), F32),
            pltpu.VMEM((FOX_GROUP * V_ROWS, FOX_TILE), F32),
            pltpu.VMEM((FOX_TILE, FOX_TILE), F32),
        ],
        compiler_params=_params("arbitrary", "arbitrary", "arbitrary"),
        name="fox",
    )(qft, kf, vt)


def _t5_bucket_np(dist):
    n = np.maximum(dist, 0)
    max_exact = N_BUCKETS // 2
    nf = np.maximum(n, 1).astype(np.float64)
    large = max_exact + (np.log(nf / max_exact) / math.log(MAX_DISTANCE / max_exact)
                         * (N_BUCKETS - max_exact)).astype(np.int64)
    large = np.minimum(large, N_BUCKETS - 1)
    return np.where(n < max_exact, n, large).astype(np.int32)


def _swa_static_buckets():
    k = np.arange(SWA_KEYS)[:, None]
    q = np.arange(BLOCK)[None, :]
    d = BLOCK + q - k
    bucket = np.where((d >= 0) & (d < WINDOW) & (k < 2 * BLOCK), _t5_bucket_np(d), -1)
    bucket = np.where(k >= 2 * BLOCK + PAD_FRONT, N_BUCKETS - 1, bucket)
    return np.ascontiguousarray(np.broadcast_to(bucket, (SWA_KEYS, BLOCK))).astype(np.int32)


def _swa_body(table_ref, sink_ref, bucket_ref, q_ref, k_ref, vt_ref, o_ref, bias_ref):
    g = pl.program_id(1)
    t = pl.program_id(2)
    cols = SWA_GROUP * BLOCK

    @pl.when(t == 0)
    def _():
        bucket = bucket_ref[...]
        slot = lax.broadcasted_iota(jnp.int32, (SWA_KEYS, BLOCK), 0)
        qpos = lax.broadcasted_iota(jnp.int32, (SWA_KEYS, BLOCK), 1)
        for r in range(SWA_GROUP):
            bias = jnp.full((SWA_KEYS, BLOCK), NEG, F32)
            for bkt in range(N_BUCKETS):
                bias = jnp.where(bucket == bkt, table_ref[bkt, g * SWA_GROUP + r], bias)
            bias_ref[2, :, r * BLOCK:(r + 1) * BLOCK] = bias
            for n in range(2):
                band_ok = (slot >= PAD_FRONT - (n - 1) * BLOCK) & (slot < 2 * BLOCK)
                meta_ok = ((slot >= 2 * BLOCK + PAD_FRONT)
                           & (n * BLOCK + qpos - (slot - 2 * BLOCK) >= WINDOW))
                bias_ref[n, :, r * BLOCK:(r + 1) * BLOCK] = jnp.where(band_ok | meta_ok, bias, NEG)

    sink = jnp.concatenate(
        [jnp.full((1, BLOCK), sink_ref[g * SWA_GROUP + r], F32) for r in range(SWA_GROUP)], axis=1)
    grp_rows = lax.broadcasted_iota(jnp.int32, (SWA_KVW, cols), 0) // HEAD_DIM
    v_rows = pl.ds(pl.multiple_of(g * HEAD_DIM, HEAD_DIM), HEAD_DIM)
    k_lead = k_ref[0:BLOCK, :]
    v_lead = vt_ref[v_rows, 0:BLOCK]

    n_blocks = q_ref.shape[1] // BLOCK

    def band(i):
        n = t * n_blocks + i
        return (n, pl.multiple_of(jnp.maximum(n - 1, 0) * BLOCK, BLOCK),
                pl.multiple_of(n * BLOCK, BLOCK))

    def scores(i):
        n, prev, cur = band(i)
        keys = jnp.concatenate(
            [k_ref[pl.ds(prev, BLOCK), :], k_ref[pl.ds(cur, BLOCK), :], k_lead], axis=0)
        q4 = jnp.concatenate(
            [q_ref[r * HEAD_DIM:(r + 1) * HEAD_DIM, i * BLOCK:(i + 1) * BLOCK]
             for r in range(SWA_GROUP)], axis=1)
        q8 = jnp.concatenate([q4, q4], axis=0)
        q8 = jnp.where(grp_rows == g, q8, jnp.zeros_like(q8))
        return _dot(keys, q8) + bias_ref[jnp.minimum(n, 2)]

    def finish(i, s):
        _, prev, cur = band(i)
        vals = jnp.concatenate(
            [vt_ref[v_rows, pl.ds(prev, BLOCK)], vt_ref[v_rows, pl.ds(cur, BLOCK)], v_lead], axis=1)
        mx = jnp.maximum(jnp.max(s, axis=0, keepdims=True), sink)
        p = jnp.exp(s - mx)
        denom = jnp.sum(p, axis=0, keepdims=True) + jnp.exp(sink - mx)
        out = _dot(vals, p.astype(BF16)) * (1.0 / denom)
        for r in range(SWA_GROUP):
            o_ref[r * HEAD_DIM:(r + 1) * HEAD_DIM, i * BLOCK:(i + 1) * BLOCK] = (
                out[:, r * BLOCK:(r + 1) * BLOCK].astype(BF16))

    s = scores(0)
    for i in range(n_blocks):
        s_next = scores(i + 1) if i + 1 < n_blocks else None
        finish(i, s)
        s = s_next


def _swa(table, sinks, qst, ks, vst, batch):
    rows = ks.shape[0]
    p = rows // batch
    n_t = SWA_STEPS
    swa_tile = p // n_t
    width = SWA_GROUP * HEAD_DIM
    bucket = jnp.asarray(_swa_static_buckets())
    smem = pl.BlockSpec(memory_space=pltpu.SMEM)
    tile = pl.BlockSpec((width, swa_tile), lambda b, g, t: (g, b * n_t + t))
    return pl.pallas_call(
        _swa_body,
        grid=(batch, SWA_KV_HEADS, n_t),
        in_specs=[smem, smem, _const_spec((SWA_KEYS, BLOCK)), tile,
                  pl.BlockSpec((p, SWA_KVW), lambda b, g, t: (b, 0)),
                  pl.BlockSpec((SWA_KVW, p), lambda b, g, t: (0, b))],
        out_specs=tile,
        out_shape=jax.ShapeDtypeStruct((SWA_QW, rows), BF16),
        scratch_shapes=[pltpu.VMEM((3, SWA_KEYS, SWA_GROUP * BLOCK), F32)],
        compiler_params=_params("arbitrary", "arbitrary", "arbitrary"),
        name="swa",
    )(table, sinks, bucket, qst, ks, vst)


def _mix_body(h_ref, of_ref, os_ref, ga_ref, gb_ref, wf_ref, ws_ref, wo_ref, o_ref):
    y = (ga_ref[...].astype(F32) * _dot(wf_ref[...], of_ref[...])
         + gb_ref[...].astype(F32) * _dot(ws_ref[...], os_ref[...]))
    o_ref[...] = h_ref[...] + _dot_tn(y.astype(BF16), wo_ref[...])


def _mix_out(h2, o_fox, o_swa, ga, gb, wf_t, ws_t, wo):
    rows, d = h2.shape
    tm = ROW_TILE

    def feat(r):
        return pl.BlockSpec((r, tm), lambda i: (0, i))

    tok = pl.BlockSpec((tm, d), lambda i: (i, 0))
    return pl.pallas_call(
        _mix_body,
        grid=(rows // tm,),
        in_specs=[tok, feat(FOX_W), feat(SWA_QW), feat(d), feat(d),
                  _const_spec(wf_t.shape), _const_spec(ws_t.shape), _const_spec(wo.shape)],
        out_specs=tok,
        out_shape=jax.ShapeDtypeStruct((rows, d), F32),
        compiler_params=_params("arbitrary"),
        name="mix_out",
    )(h2, o_fox, o_swa, ga, gb, wf_t, ws_t, wo)


def _w_in_body(w_ref, last_ref, wf_ref, wk_ref, *, d_model):
    def rows_of(col, n):
        return w_ref[:, col:col + n].T

    wf_ref[_R_QA:_R_QA + FOX_W] = rows_of(0, FOX_W).astype(BF16)
    wk_ref[:, 0:FOX_W] = w_ref[:, FOX_W:2 * FOX_W].astype(BF16)
    wf_ref[_R_VA:_R_VA + FOX_W] = rows_of(2 * FOX_W, FOX_W).astype(BF16)
    base = 3 * FOX_W
    n_chunks = (SWA_QW + 2 * SWA_KVW + 2 * d_model) // LANES
    chunk = rows_of(base, LANES)
    fa = chunk[0:FOX_HEADS]
    wf_ref[_R_FA:_R_FA + FA_ROWS] = jnp.concatenate(
        [fa, jnp.zeros((FA_ROWS - FOX_HEADS, fa.shape[1]), F32)], axis=0).astype(BF16)
    dest = ([(_R_QB + LANES * m) for m in range(SWA_QW // LANES)] + [None]
            + [_R_VB] + [(_R_GA + LANES * m) for m in range(2 * d_model // LANES)])
    for m in range(n_chunks):
        nxt = rows_of(base + LANES * (m + 1), LANES) if m + 1 < n_chunks else last_ref[...]
        piece = jnp.concatenate([chunk[FOX_HEADS:], nxt[0:FOX_HEADS]], axis=0)
        if dest[m] is None:
            wk_ref[:, FOX_W:FOX_W + SWA_KVW] = piece.T.astype(BF16)
        else:
            wf_ref[dest[m]:dest[m] + LANES] = piece.astype(BF16)
        chunk = nxt


def _prep_w_in(w_in, layer):
    _, d, d_in = w_in.shape
    last = w_in[layer, :, d_in - FOX_HEADS:].T
    n_feat = _R_GA + 2 * d
    return pl.pallas_call(
        functools.partial(_w_in_body, d_model=d),
        grid=(1,),
        in_specs=[pl.BlockSpec((None, d, d_in), lambda i: (layer, 0, 0), pipeline_mode=pl.Buffered(1)),
                  _const_spec((FOX_HEADS, d))],
        out_specs=[pl.BlockSpec((n_feat, d), lambda i: (0, 0)),
                   pl.BlockSpec((d, FOX_W + SWA_KVW), lambda i: (0, 0))],
        out_shape=[jax.ShapeDtypeStruct((n_feat, d), BF16),
                   jax.ShapeDtypeStruct((d, FOX_W + SWA_KVW), BF16)],
        compiler_params=_params("arbitrary"),
        name="prep_w_in",
    )(w_in, last)


def kernel(x, meta_tokens, rel_bias_table, ffn1_norm, ffn1_w_in, ffn1_w_out, mix_norm, w_in, forget_bias, fox_q_norm, fox_k_norm, swa_q_norm, swa_k_norm, swa_sinks, w_branch_fox, w_branch_swa, w_out, ffn2_norm, ffn2_w_in, ffn2_w_out):
    b, seq, d = x.shape
    depth = w_in.shape[0]
    p = BLOCK + seq
    assert seq % FOX_TILE == 0 and p % (SWA_STEPS * BLOCK) == 0 and (b * p) % ROW_TILE == 0
    pad = jnp.zeros((b, PAD_FRONT, d), x.dtype)
    meta = jnp.broadcast_to(meta_tokens.astype(x.dtype)[None], (b, N_META, d))
    h = jnp.concatenate([pad, meta, x], axis=1).reshape(b * p, d)
    table = rel_bias_table.astype(F32)

    def col(v):
        return v.astype(F32)[:, None]

    def twice(v):
        return jnp.tile(v.astype(F32), 2)[None]

    for l in range(depth):
        h = _ffn(h, ffn1_norm[l][None], ffn1_w_in, ffn1_w_out, l)
        w_feat, w_keys = _prep_w_in(w_in, l)
        fb = jnp.pad(col(forget_bias[l]), ((0, FA_ROWS - FOX_HEADS), (0, 0)))
        qft, kf, vt, qst, ks, vst, ga, gb = _proj(
            h, p, mix_norm[l][None], w_feat, w_keys, fb,
            col(fox_q_norm[l]), twice(fox_k_norm[l]), col(swa_q_norm[l]), twice(swa_k_norm[l]))
        o_fox = _fox(qft, kf, vt, b)
        o_swa = _swa(table, swa_sinks[l].astype(F32), qst, ks, vst, b)
        h = _mix_out(h, o_fox, o_swa, ga, gb, w_branch_fox[l].astype(BF16).T,
                     w_branch_swa[l].astype(BF16).T, w_out[l].astype(BF16))
        last = l == depth - 1
        h = _ffn(h, ffn2_norm[l][None], ffn2_w_in, ffn2_w_out, l, (b, seq) if last else None)
    return h
```

```python
import functools
import math

import jax
import jax.numpy as jnp
import numpy as np
from jax import lax
from jax.experimental import pallas as pl
from jax.experimental.pallas import tpu as pltpu

F32 = jnp.float32
BF16 = jnp.bfloat16

N_META = 16
BLOCK = 128
PAD_FRONT = BLOCK - N_META
HEAD_DIM = 64
FOX_HEADS = 8
SWA_Q_HEADS = 8
SWA_KV_HEADS = 2
SWA_GROUP = SWA_Q_HEADS // SWA_KV_HEADS
WINDOW = 128
N_BUCKETS = 32
MAX_DISTANCE = 128
EPS = 1e-6
NEG = -1e30
FOX_W = FOX_HEADS * HEAD_DIM
SWA_QW = SWA_Q_HEADS * HEAD_DIM
SWA_KVW = SWA_KV_HEADS * HEAD_DIM
SCALE = HEAD_DIM ** -0.5
LOG2E = math.log2(math.e)

LANES = 128
VMEM_LIMIT = 56 * 1024 * 1024
ROW_TILE = 512
FFN_CHUNK = 256
FOX_TILE = 512
FOX_GROUP = 4
AUG = LANES
AUG_ROWS = 32
V_ROWS = HEAD_DIM + 16
FA_ROWS = 16
SWA_KEYS = 3 * BLOCK
SWA_STEPS = 3

_R_QA = 0
_R_VA = _R_QA + FOX_W
_R_QB = _R_VA + FOX_W
_R_VB = _R_QB + SWA_QW
_R_FA = _R_VB + SWA_KVW
_R_GA = _R_FA + FA_ROWS


def _params(*sem, flags=None):
    return pltpu.CompilerParams(dimension_semantics=sem, vmem_limit_bytes=VMEM_LIMIT, flags=flags)


def _const_spec(shape):
    nd = len(shape)
    return pl.BlockSpec(shape, lambda *_: (0,) * nd, pipeline_mode=pl.Buffered(1))


def _dot(a, b):
    return jnp.dot(a, b, preferred_element_type=F32)


def _dot_nt(a, b):
    return lax.dot_general(a, b, (((1,), (1,)), ((), ())), preferred_element_type=F32)


def _dot_tn(a, b):
    return lax.dot_general(a, b, (((0,), (0,)), ((), ())), preferred_element_type=F32)


def _rms_rows(x, g):
    ms = jnp.mean(x * x, axis=-1, keepdims=True)
    return x * lax.rsqrt(ms + EPS) * g


def _lead_blocks_before(block, blocks_per_seq):
    return (block + blocks_per_seq - 1) // blocks_per_seq


def _assemble_rows(x_ref, lead_ref, tile, blocks_per_seq):
    n_sub = x_ref.shape[0] // BLOCK
    first = tile * n_sub
    at = (blocks_per_seq - first % blocks_per_seq) % blocks_per_seq
    parts = []
    for j in range(n_sub):
        cur = x_ref[j * BLOCK:(j + 1) * BLOCK, :]
        if j > 0:
            cur = jnp.where(j > at, x_ref[(j - 1) * BLOCK:j * BLOCK, :], cur)
        parts.append(jnp.where(j == at, lead_ref[...], cur))
    return jnp.concatenate(parts, axis=0)


def _ffn_body(*refs, d_ff, n_pro, blocks_per_seq):
    if blocks_per_seq is None:
        x_ref, g_ref, win_ref, wout_ref, o_ref, win_bf, wout_bf, acc_ref = refs
    else:
        x_ref, lead_ref, g_ref, win_ref, wout_ref, o_ref, win_bf, wout_bf, acc_ref = refs
    i = pl.program_id(0)
    in_chunk = win_ref.shape[1]

    @pl.when(i < n_pro)
    def _():
        win_bf[:, pl.ds(pl.multiple_of(i * in_chunk, LANES), in_chunk)] = win_ref[...].astype(BF16)
        wout_bf[pl.ds(pl.multiple_of(i * FFN_CHUNK, FFN_CHUNK), FFN_CHUNK), :] = (
            wout_ref[...].astype(BF16))

    @pl.when(i >= n_pro)
    def _():
        if blocks_per_seq is None:
            x = x_ref[...]
        else:
            x = _assemble_rows(x_ref, lead_ref, i - n_pro, blocks_per_seq)
        xn = _rms_rows(x, g_ref[...]).astype(BF16)
        for c in range(d_ff // FFN_CHUNK):
            lo = c * FFN_CHUNK
            gate = _dot(xn, win_bf[:, lo:lo + FFN_CHUNK])
            up = _dot(xn, win_bf[:, d_ff + lo:d_ff + lo + FFN_CHUNK])
            act = (gate * jax.nn.sigmoid(gate) * up).astype(BF16)
            part = _dot(act, wout_bf[lo:lo + FFN_CHUNK, :])
            if c == 0:
                acc_ref[...] = part
            else:
                acc_ref[...] += part
        o_ref[...] = x + 0.5 * acc_ref[...]


def _ffn(h2, gain, w_in, w_out, layer, real_only=None, lead=None):
    rows, d = h2.shape
    d_ff = w_out.shape[1]
    tm = ROW_TILE
    n_pro = d_ff // FFN_CHUNK
    in_chunk = 2 * d_ff // n_pro
    blocks_per_seq = None
    extra_in, extra_specs = (), ()

    def row_tile(i):
        return (jnp.maximum(i - n_pro, 0), 0)

    if lead is not None:
        lead_rows, batch, seq = lead
        blocks_per_seq = seq // BLOCK + 1
        rows = batch * blocks_per_seq * BLOCK
        x_arg, n_tiles = h2, rows // tm

        def x_map(i):
            first = jnp.maximum(i - n_pro, 0) * (tm // BLOCK)
            start = first - _lead_blocks_before(first, blocks_per_seq)
            return (pl.multiple_of(start * BLOCK, BLOCK), 0)

        x_spec = pl.BlockSpec((pl.Element(tm), pl.Element(d)), x_map)
        extra_in, extra_specs = (lead_rows,), (_const_spec((BLOCK, d)),)
        out_spec = pl.BlockSpec((tm, d), row_tile)
        out_shape = jax.ShapeDtypeStruct((rows, d), F32)
    elif real_only is None:
        x_arg, n_tiles = h2, rows // tm
        x_spec = out_spec = pl.BlockSpec((tm, d), row_tile)
        out_shape = jax.ShapeDtypeStruct((rows, d), F32)
    else:
        batch, seq = real_only
        per_seq = seq // tm
        x_arg, n_tiles = h2, batch * per_seq

        def seq_tile(i):
            r = jnp.maximum(i - n_pro, 0)
            return r // per_seq, r % per_seq

        def x_map(i):
            b, t = seq_tile(i)
            return (pl.multiple_of(b * (rows // batch) + BLOCK + t * tm, BLOCK), 0)

        def out_map(i):
            b, t = seq_tile(i)
            return (b, t, 0)

        x_spec = pl.BlockSpec((pl.Element(tm), pl.Element(d)), x_map)
        out_spec = pl.BlockSpec((None, tm, d), out_map)
        out_shape = jax.ShapeDtypeStruct((batch, seq, d), F32)

    return pl.pallas_call(
        functools.partial(_ffn_body, d_ff=d_ff, n_pro=n_pro, blocks_per_seq=blocks_per_seq),
        grid=(n_pro + n_tiles,),
        in_specs=[
            x_spec,
            *extra_specs,
            _const_spec((1, d)),
            pl.BlockSpec((None, d, in_chunk), lambda i: (layer, 0, jnp.minimum(i, n_pro - 1))),
            pl.BlockSpec((None, FFN_CHUNK, d), lambda i: (layer, jnp.minimum(i, n_pro - 1), 0)),
        ],
        out_specs=out_spec,
        out_shape=out_shape,
        scratch_shapes=[pltpu.VMEM((d, 2 * d_ff), BF16), pltpu.VMEM((d_ff, d), BF16),
                        pltpu.VMEM((tm, d), F32)],
        compiler_params=_params("arbitrary"),
        name="ffn",
    )(x_arg, *extra_in, gain, w_in, w_out)


def _split3(x):
    hi = x.astype(BF16).astype(F32)
    r = x - hi
    mid = r.astype(BF16).astype(F32)
    return hi, mid, r - mid


def _head_rms_cols(x, g):
    ms = jnp.mean(x * x, axis=0, keepdims=True)
    return x * lax.rsqrt(ms + EPS) * g


def _pair_rms_rows(x, g2):
    lane = lax.broadcasted_iota(jnp.int32, x.shape, 1)
    sq = x * x
    first = lane < HEAD_DIM
    s0 = jnp.sum(jnp.where(first, sq, 0.0), axis=-1, keepdims=True)
    s1 = jnp.sum(jnp.where(first, 0.0, sq), axis=-1, keepdims=True)
    ms = jnp.where(first, s0, s1) * (1.0 / HEAD_DIM)
    return x * lax.rsqrt(ms + EPS) * g2


def _proj_body(x_ref, g_ref, wt_ref, wk_ref, fb_ref, nfq_ref, nfk_ref, nsq_ref, nsk_ref,
               qf_ref, kf_ref, vt_ref, qs_ref, ks_ref, vs_ref, ga_ref, gb_ref,
               carry_ref, tri_ref, *, d_model, seq_pad):
    i = pl.program_id(0)
    tm = x_ref.shape[0]

    @pl.when(i == 0)
    def _():
        src = lax.broadcasted_iota(jnp.int32, (tm, tm), 0)
        dst = lax.broadcasted_iota(jnp.int32, (tm, tm), 1)
        tri_ref[...] = (src <= dst).astype(BF16)

    xn = _rms_rows(x_ref[...], g_ref[...]).astype(BF16)

    def feat(lo, rows):
        return _dot_nt(wt_ref[lo:lo + rows, :], xn)

    lf = jax.nn.log_sigmoid(feat(_R_FA, FA_ROWS) + fb_ref[...])
    r0 = i * tm
    boundary = (r0 // seq_pad + 1) * seq_pad - r0
    hi, mid, lo3 = _split3(lf)
    upper = tri_ref[...]
    c_loc = (_dot(hi.astype(BF16), upper) + _dot(mid.astype(BF16), upper)
             + _dot(lo3.astype(BF16), upper))
    carry = jnp.where(r0 % seq_pad == 0, 0.0, carry_ref[:, 0:1])
    tok = lax.broadcasted_iota(jnp.int32, (1, tm), 1)
    c_end = jnp.sum(jnp.where(tok == boundary - 1, c_loc, 0.0), axis=1, keepdims=True)
    c_t = c_loc + jnp.where(tok < boundary, carry, -c_end)
    carry_ref[...] = jnp.broadcast_to(c_t[:, tm - 1:tm], carry_ref.shape)
    c_t = c_t * LOG2E

    c_pad = jnp.concatenate(
        [jnp.where(lax.broadcasted_iota(jnp.int32, (FA_ROWS, tm), 0) < FOX_HEADS, c_t, 0.0),
         jnp.zeros((LANES - FA_ROWS, tm), F32)], axis=0)
    c_tok = c_pad.T
    lane = lax.broadcasted_iota(jnp.int32, (tm, LANES), 1)
    k_hi, k_mid, k_lo = _split3(c_tok)
    aug_lo = (-k_hi - pltpu.roll(k_mid, FOX_HEADS, 1) - pltpu.roll(k_lo, 2 * FOX_HEADS, 1)
              + jnp.where((lane >= 3 * FOX_HEADS) & (lane < 3 * FOX_HEADS + 3), 1.0, 0.0))
    aug_hi = pltpu.roll(aug_lo, HEAD_DIM, 1)
    first = lane < HEAD_DIM

    kt = _dot(xn, wk_ref[...])
    for j in range(FOX_HEADS // 2):
        kn = _pair_rms_rows(kt[:, j * LANES:(j + 1) * LANES], nfk_ref[...])
        kf_ref[:, (2 * j) * AUG:(2 * j + 1) * AUG] = jnp.where(first, kn, aug_hi).astype(BF16)
        kf_ref[:, (2 * j + 1) * AUG:(2 * j + 2) * AUG] = jnp.where(first, aug_lo, kn).astype(BF16)
    ks_ref[...] = _pair_rms_rows(kt[:, FOX_W:FOX_W + SWA_KVW], nsk_ref[...]).astype(BF16)

    arow = lax.broadcasted_iota(jnp.int32, (AUG_ROWS, tm), 0)
    qa = feat(_R_QA, FOX_W)
    zeros = jnp.zeros((HEAD_DIM - AUG_ROWS, tm), BF16)
    for h in range(FOX_HEADS):
        q_hi, q_mid, q_lo = _split3(c_t[h:h + 1, :])
        onehot = jnp.where((arow % FOX_HEADS == h) & (arow < 3 * FOX_HEADS), 1.0, 0.0)
        aug_q = jnp.where(arow == 3 * FOX_HEADS, q_hi, jnp.where(
            arow == 3 * FOX_HEADS + 1, q_mid, jnp.where(arow == 3 * FOX_HEADS + 2, q_lo, onehot)))
        qn = (_head_rms_cols(qa[h * HEAD_DIM:(h + 1) * HEAD_DIM], nfq_ref[...])
              * (SCALE * LOG2E)).astype(BF16)
        parts = [qn, aug_q.astype(BF16), zeros] if h % 2 == 0 else [aug_q.astype(BF16), zeros, qn]
        qf_ref[h * AUG:(h + 1) * AUG, :] = jnp.concatenate(parts, axis=0)
    ga_ref[...] = jax.nn.sigmoid(feat(_R_GA, d_model)).astype(BF16)
    gb_ref[...] = jax.nn.sigmoid(feat(_R_GA + d_model, d_model)).astype(BF16)
    qb = feat(_R_QB, SWA_QW)
    for h in range(SWA_Q_HEADS):
        sl = slice(h * HEAD_DIM, (h + 1) * HEAD_DIM)
        qs_ref[sl, :] = (_head_rms_cols(qb[sl], nsq_ref[...]) * SCALE).astype(BF16)
    vs_ref[...] = feat(_R_VB, SWA_KVW).astype(BF16)
    va = feat(_R_VA, FOX_W).astype(BF16)
    ones = jnp.ones((V_ROWS - HEAD_DIM, tm), BF16)
    for h in range(FOX_HEADS):
        vt_ref[h * V_ROWS:(h + 1) * V_ROWS, :] = jnp.concatenate(
            [va[h * HEAD_DIM:(h + 1) * HEAD_DIM], ones], axis=0)


def _proj(h2, seq_pad, gain, w_feat, w_keys, fb, nfq, nfk, nsq, nsk):
    rows, d = h2.shape
    tm = ROW_TILE

    def tok(w):
        return pl.BlockSpec((tm, w), lambda i: (i, 0)), jax.ShapeDtypeStruct((rows, w), BF16)

    def feat(r):
        return pl.BlockSpec((r, tm), lambda i: (0, i)), jax.ShapeDtypeStruct((r, rows), BF16)

    outs = [feat(FOX_HEADS * AUG), tok(FOX_HEADS * AUG), feat(FOX_HEADS * V_ROWS), feat(SWA_QW),
            tok(SWA_KVW), feat(SWA_KVW), feat(d), feat(d)]
    return pl.pallas_call(
        functools.partial(_proj_body, d_model=d, seq_pad=seq_pad),
        grid=(rows // tm,),
        in_specs=[
            pl.BlockSpec((tm, d), lambda i: (i, 0)),
            _const_spec((1, d)),
            _const_spec(w_feat.shape),
            _const_spec(w_keys.shape),
            _const_spec((FA_ROWS, 1)),
            _const_spec((HEAD_DIM, 1)), _const_spec((1, LANES)),
            _const_spec((HEAD_DIM, 1)), _const_spec((1, LANES)),
        ],
        out_specs=[o[0] for o in outs],
        out_shape=[o[1] for o in outs],
        scratch_shapes=[pltpu.VMEM((FA_ROWS, LANES), F32), pltpu.VMEM((tm, tm), BF16)],
        compiler_params=_params("arbitrary"),
        name="proj",
    )(h2, gain, w_feat, w_keys, fb, nfq, nfk, nsq, nsk)


def _fox_body(q_ref, k_ref, vt_ref, o_ref, m_ref, acc_ref, st_ref):
    qi = pl.program_id(2)

    def attend(q_start, tq, n_full, diag):
        q = q_ref[:, pl.ds(q_start, tq)]
        m_ref[:, :, :tq] = jnp.full((FOX_GROUP, 1, tq), NEG, F32)
        acc_ref[:, :tq] = jnp.zeros((FOX_GROUP * V_ROWS, tq), F32)

        def scores(kblk, h):
            return _dot(kblk[:, h * AUG:(h + 1) * AUG], q[h * AUG:(h + 1) * AUG, :])

        def softmax(h, st, mask):
            if mask is not None:
                st = jnp.where(mask, st, NEG)
            m_prev = m_ref[h, :, :tq]
            m_new = jnp.maximum(m_prev, jnp.max(st, axis=0, keepdims=True))
            alpha = jnp.exp2(m_prev - m_new)
            m_ref[h, :, :tq] = m_new
            return jnp.exp2(st - m_new).astype(BF16), alpha

        def weighted_values(h, vt, p_alpha):
            p, alpha = p_alpha
            rows = slice(h * V_ROWS, (h + 1) * V_ROWS)
            acc_ref[rows, :tq] = alpha * acc_ref[rows, :tq] + _dot(vt[rows, :], p)

        def process(k_start, tk, mask):
            kblk = k_ref[pl.ds(k_start, tk), :]
            vt = vt_ref[:, pl.ds(k_start, tk)]
            for h in range(FOX_GROUP):
                weighted_values(h, vt, softmax(h, scores(kblk, h), mask))

        def key_tile(j):
            return k_ref[pl.ds(pl.multiple_of(BLOCK + j * FOX_TILE, BLOCK), FOX_TILE), :]

        def pipelined(j, mask, prefetch):
            kblk = key_tile(j)
            vt = vt_ref[:, pl.ds(pl.multiple_of(BLOCK + j * FOX_TILE, BLOCK), FOX_TILE)]
            st = [st_ref[...], scores(kblk, 1), None, None]
            pa = softmax(0, st[0], mask)
            for h in range(FOX_GROUP):
                if h + 2 < FOX_GROUP:
                    st[h + 2] = scores(kblk, h + 2)
                elif h + 2 == FOX_GROUP and prefetch:
                    st_ref[...] = scores(key_tile(j + 1), 0)
                weighted_values(h, vt, pa)
                if h + 1 < FOX_GROUP:
                    pa = softmax(h + 1, st[h + 1], mask)

        kpos0 = lax.broadcasted_iota(jnp.int32, (BLOCK, tq), 0)
        if diag:
            process(0, BLOCK, kpos0 >= PAD_FRONT)
            st_ref[...] = scores(key_tile(0), 0)

            def tile_pair(jj, carry):
                pipelined(2 * jj, None, True)
                pipelined(2 * jj + 1, None, True)
                return carry

            lax.fori_loop(0, n_full // 2, tile_pair, 0)

            @pl.when(n_full % 2 == 1)
            def _():
                pipelined(n_full - 1, None, True)
            kr = lax.broadcasted_iota(jnp.int32, (tq, tq), 0)
            qc = lax.broadcasted_iota(jnp.int32, (tq, tq), 1)
            pipelined(n_full, kr <= qc, False)
        else:
            qc = lax.broadcasted_iota(jnp.int32, (BLOCK, tq), 1)
            process(0, BLOCK, (kpos0 >= PAD_FRONT) & (kpos0 <= qc))
        for h in range(FOX_GROUP):
            denom = acc_ref[h * V_ROWS + HEAD_DIM:h * V_ROWS + HEAD_DIM + 1, :tq]
            o_ref[h * HEAD_DIM:(h + 1) * HEAD_DIM, pl.ds(q_start, tq)] = (
                acc_ref[h * V_ROWS:h * V_ROWS + HEAD_DIM, :tq] * (1.0 / denom)).astype(BF16)

    @pl.when(qi == 0)
    def _():
        attend(0, BLOCK, 0, False)

    @pl.when(qi > 0)
    def _():
        attend(pl.multiple_of(BLOCK + (qi - 1) * FOX_TILE, BLOCK), FOX_TILE, qi - 1, True)


def _fox(qft, kf, vt, batch):
    rows = kf.shape[0]
    p = rows // batch
    n_grp = FOX_HEADS // FOX_GROUP
    width = FOX_GROUP * HEAD_DIM
    n_q = (p - BLOCK) // FOX_TILE + 1

    def feat(r):
        return pl.BlockSpec((r, p), lambda b, g, q: (g, b))

    return pl.pallas_call(
        _fox_body,
        grid=(batch, n_grp, n_q),
        in_specs=[feat(FOX_GROUP * AUG),
                  pl.BlockSpec((p, FOX_GROUP * AUG), lambda b, g, q: (b, g)),
                  feat(FOX_GROUP * V_ROWS)],
        out_specs=feat(width),
        out_shape=jax.ShapeDtypeStruct((FOX_W, rows), BF16),
        scratch_shapes=[
            pltpu.VMEM((FOX_GROUP, 1, FOX_TILE), F32),
            pltpu.VMEM((FOX_GROUP * V_ROWS, FOX_TILE), F32),
            pltpu.VMEM((FOX_TILE, FOX_TILE), F32),
        ],
        compiler_params=_params("arbitrary", "arbitrary", "arbitrary"),
        name="fox",
    )(qft, kf, vt)


def _t5_bucket_np(dist):
    n = np.maximum(dist, 0)
    max_exact = N_BUCKETS // 2
    nf = np.maximum(n, 1).astype(np.float64)
    large = max_exact + (np.log(nf / max_exact) / math.log(MAX_DISTANCE / max_exact)
                         * (N_BUCKETS - max_exact)).astype(np.int64)
    large = np.minimum(large, N_BUCKETS - 1)
    return np.where(n < max_exact, n, large).astype(np.int32)


def _swa_static_buckets():
    k = np.arange(SWA_KEYS)[:, None]
    q = np.arange(BLOCK)[None, :]
    d = BLOCK + q - k
    bucket = np.where((d >= 0) & (d < WINDOW) & (k < 2 * BLOCK), _t5_bucket_np(d), -1)
    bucket = np.where(k >= 2 * BLOCK + PAD_FRONT, N_BUCKETS - 1, bucket)
    return np.ascontiguousarray(np.broadcast_to(bucket, (SWA_KEYS, BLOCK))).astype(np.int32)


def _swa_body(table_ref, sink_ref, bucket_ref, q_ref, k_ref, vt_ref, o_ref, bias_ref):
    g = pl.program_id(1)
    t = pl.program_id(2)
    cols = SWA_GROUP * BLOCK

    @pl.when(t == 0)
    def _():
        bucket = bucket_ref[...]
        slot = lax.broadcasted_iota(jnp.int32, (SWA_KEYS, BLOCK), 0)
        qpos = lax.broadcasted_iota(jnp.int32, (SWA_KEYS, BLOCK), 1)
        for r in range(SWA_GROUP):
            bias = jnp.full((SWA_KEYS, BLOCK), NEG, F32)
            for bkt in range(N_BUCKETS):
                bias = jnp.where(bucket == bkt, table_ref[bkt, g * SWA_GROUP + r], bias)
            bias_ref[2, :, r * BLOCK:(r + 1) * BLOCK] = bias
            for n in range(2):
                band_ok = (slot >= PAD_FRONT - (n - 1) * BLOCK) & (slot < 2 * BLOCK)
                meta_ok = ((slot >= 2 * BLOCK + PAD_FRONT)
                           & (n * BLOCK + qpos - (slot - 2 * BLOCK) >= WINDOW))
                bias_ref[n, :, r * BLOCK:(r + 1) * BLOCK] = jnp.where(band_ok | meta_ok, bias, NEG)

    sink = jnp.concatenate(
        [jnp.full((1, BLOCK), sink_ref[g * SWA_GROUP + r], F32) for r in range(SWA_GROUP)], axis=1)
    grp_rows = lax.broadcasted_iota(jnp.int32, (SWA_KVW, cols), 0) // HEAD_DIM
    v_rows = pl.ds(pl.multiple_of(g * HEAD_DIM, HEAD_DIM), HEAD_DIM)
    k_lead = k_ref[0:BLOCK, :]
    v_lead = vt_ref[v_rows, 0:BLOCK]

    n_blocks = q_ref.shape[1] // BLOCK

    def band(i):
        n = t * n_blocks + i
        return (n, pl.multiple_of(jnp.maximum(n - 1, 0) * BLOCK, BLOCK),
                pl.multiple_of(n * BLOCK, BLOCK))

    def scores(i):
        n, prev, cur = band(i)
        keys = jnp.concatenate(
            [k_ref[pl.ds(prev, BLOCK), :], k_ref[pl.ds(cur, BLOCK), :], k_lead], axis=0)
        q4 = jnp.concatenate(
            [q_ref[r * HEAD_DIM:(r + 1) * HEAD_DIM, i * BLOCK:(i + 1) * BLOCK]
             for r in range(SWA_GROUP)], axis=1)
        q8 = jnp.concatenate([q4, q4], axis=0)
        q8 = jnp.where(grp_rows == g, q8, jnp.zeros_like(q8))
        return _dot(keys, q8) + bias_ref[jnp.minimum(n, 2)]

    def finish(i, s):
        _, prev, cur = band(i)
        vals = jnp.concatenate(
            [vt_ref[v_rows, pl.ds(prev, BLOCK)], vt_ref[v_rows, pl.ds(cur, BLOCK)], v_lead], axis=1)
        mx = jnp.maximum(jnp.max(s, axis=0, keepdims=True), sink)
        p = jnp.exp(s - mx)
        denom = jnp.sum(p, axis=0, keepdims=True) + jnp.exp(sink - mx)
        out = _dot(vals, p.astype(BF16)) * (1.0 / denom)
        for r in range(SWA_GROUP):
            o_ref[r * HEAD_DIM:(r + 1) * HEAD_DIM, i * BLOCK:(i + 1) * BLOCK] = (
                out[:, r * BLOCK:(r + 1) * BLOCK].astype(BF16))

    s = scores(0)
    for i in range(n_blocks):
        s_next = scores(i + 1) if i + 1 < n_blocks else None
        finish(i, s)
        s = s_next


def _swa(table, sinks, qst, ks, vst, batch):
    rows = ks.shape[0]
    p = rows // batch
    n_t = SWA_STEPS
    swa_tile = p // n_t
    width = SWA_GROUP * HEAD_DIM
    bucket = jnp.asarray(_swa_static_buckets())
    smem = pl.BlockSpec(memory_space=pltpu.SMEM)
    tile = pl.BlockSpec((width, swa_tile), lambda b, g, t: (g, b * n_t + t))
    return pl.pallas_call(
        _swa_body,
        grid=(batch, SWA_KV_HEADS, n_t),
        in_specs=[smem, smem, _const_spec((SWA_KEYS, BLOCK)), tile,
                  pl.BlockSpec((p, SWA_KVW), lambda b, g, t: (b, 0)),
                  pl.BlockSpec((SWA_KVW, p), lambda b, g, t: (0, b))],
        out_specs=tile,
        out_shape=jax.ShapeDtypeStruct((SWA_QW, rows), BF16),
        scratch_shapes=[pltpu.VMEM((3, SWA_KEYS, SWA_GROUP * BLOCK), F32)],
        compiler_params=_params("arbitrary", "arbitrary", "arbitrary"),
        name="swa",
    )(table, sinks, bucket, qst, ks, vst)


def _mix_body(h_ref, of_ref, os_ref, ga_ref, gb_ref, wf_ref, ws_ref, wo_ref, o_ref):
    y = (ga_ref[...].astype(F32) * _dot(wf_ref[...], of_ref[...])
         + gb_ref[...].astype(F32) * _dot(ws_ref[...], os_ref[...]))
    o_ref[...] = h_ref[...] + _dot_tn(y.astype(BF16), wo_ref[...])


def _mix_out(h2, o_fox, o_swa, ga, gb, wf_t, ws_t, wo):
    rows, d = h2.shape
    tm = ROW_TILE

    def feat(r):
        return pl.BlockSpec((r, tm), lambda i: (0, i))

    tok = pl.BlockSpec((tm, d), lambda i: (i, 0))
    return pl.pallas_call(
        _mix_body,
        grid=(rows // tm,),
        in_specs=[tok, feat(FOX_W), feat(SWA_QW), feat(d), feat(d),
                  _const_spec(wf_t.shape), _const_spec(ws_t.shape), _const_spec(wo.shape)],
        out_specs=tok,
        out_shape=jax.ShapeDtypeStruct((rows, d), F32),
        compiler_params=_params("arbitrary"),
        name="mix_out",
    )(h2, o_fox, o_swa, ga, gb, wf_t, ws_t, wo)


def _w_in_body(w_ref, wf_ref, wk_ref, *, d_model):
    src = np.cumsum([0, FOX_W, FOX_W, FOX_W, FOX_HEADS, SWA_QW, SWA_KVW, SWA_KVW, d_model])
    qa, ka, va, fa, qb, kb, vb, ga, gb = (int(s) for s in src)

    def move(dst, lo, n):
        wf_ref[dst:dst + n] = w_ref[lo:lo + n].astype(BF16)

    move(_R_QA, qa, FOX_W)
    move(_R_VA, va, FOX_W)
    move(_R_QB, qb, SWA_QW)
    move(_R_VB, vb, SWA_KVW)
    move(_R_GA, ga, d_model)
    move(_R_GA + d_model, gb, d_model)
    wf_ref[_R_FA:_R_FA + FA_ROWS] = jnp.concatenate(
        [w_ref[fa:fa + FOX_HEADS], jnp.zeros((FA_ROWS - FOX_HEADS, w_ref.shape[1]), F32)],
        axis=0).astype(BF16)
    wk_ref[:, 0:FOX_W] = w_ref[ka:ka + FOX_W].T.astype(BF16)
    wk_ref[:, FOX_W:FOX_W + SWA_KVW] = w_ref[kb:kb + SWA_KVW].T.astype(BF16)


def _prep_w_in(w_in, layer):
    w_t = jnp.swapaxes(w_in, 1, 2)
    _, d_in, d = w_t.shape
    n_feat = _R_GA + 2 * d
    return pl.pallas_call(
        functools.partial(_w_in_body, d_model=d),
        grid=(1,),
        in_specs=[pl.BlockSpec((None, d_in, d), lambda i: (layer, 0, 0), pipeline_mode=pl.Buffered(1))],
        out_specs=[pl.BlockSpec((n_feat, d), lambda i: (0, 0)),
                   pl.BlockSpec((d, FOX_W + SWA_KVW), lambda i: (0, 0))],
        out_shape=[jax.ShapeDtypeStruct((n_feat, d), BF16),
                   jax.ShapeDtypeStruct((d, FOX_W + SWA_KVW), BF16)],
        compiler_params=_params("arbitrary"),
        name="prep_w_in",
    )(w_t)


def kernel(x, meta_tokens, rel_bias_table, ffn1_norm, ffn1_w_in, ffn1_w_out, mix_norm, w_in, forget_bias, fox_q_norm, fox_k_norm, swa_q_norm, swa_k_norm, swa_sinks, w_branch_fox, w_branch_swa, w_out, ffn2_norm, ffn2_w_in, ffn2_w_out):
    b, seq, d = x.shape
    depth = w_in.shape[0]
    p = BLOCK + seq
    assert seq % FOX_TILE == 0 and p % (SWA_STEPS * BLOCK) == 0 and (b * p) % ROW_TILE == 0
    lead = jnp.concatenate([jnp.zeros((PAD_FRONT, d), x.dtype), meta_tokens.astype(x.dtype)], axis=0)
    h = x.reshape(b * seq, d)
    table = rel_bias_table.astype(F32)

    def col(v):
        return v.astype(F32)[:, None]

    def twice(v):
        return jnp.tile(v.astype(F32), 2)[None]

    for l in range(depth):
        h = _ffn(h, ffn1_norm[l][None], ffn1_w_in, ffn1_w_out, l,
                 lead=(lead, b, seq) if l == 0 else None)
        w_feat, w_keys = _prep_w_in(w_in, l)
        fb = jnp.pad(col(forget_bias[l]), ((0, FA_ROWS - FOX_HEADS), (0, 0)))
        qft, kf, vt, qst, ks, vst, ga, gb = _proj(
            h, p, mix_norm[l][None], w_feat, w_keys, fb,
            col(fox_q_norm[l]), twice(fox_k_norm[l]), col(swa_q_norm[l]), twice(swa_k_norm[l]))
        o_fox = _fox(qft, kf, vt, b)
        o_swa = _swa(table, swa_sinks[l].astype(F32), qst, ks, vst, b)
        h = _mix_out(h, o_fox, o_swa, ga, gb, w_branch_fox[l].astype(BF16).T,
                     w_branch_swa[l].astype(BF16).T, w_out[l].astype(BF16))
        last = l == depth - 1
        h = _ffn(h, ffn2_norm[l][None], ffn2_w_in, ffn2_w_out, l, (b, seq) if last else None)
    return h
```

```python
import functools
import math

import jax
import jax.numpy as jnp
import numpy as np
from jax import lax
from jax.experimental import pallas as pl
from jax.experimental.pallas import tpu as pltpu

F32 = jnp.float32
BF16 = jnp.bfloat16

N_META = 16
BLOCK = 128
PAD_FRONT = BLOCK - N_META
HEAD_DIM = 64
FOX_HEADS = 8
SWA_Q_HEADS = 8
SWA_KV_HEADS = 2
SWA_GROUP = SWA_Q_HEADS // SWA_KV_HEADS
WINDOW = 128
N_BUCKETS = 32
MAX_DISTANCE = 128
EPS = 1e-6
NEG = -1e30
FOX_W = FOX_HEADS * HEAD_DIM
SWA_QW = SWA_Q_HEADS * HEAD_DIM
SWA_KVW = SWA_KV_HEADS * HEAD_DIM
SCALE = HEAD_DIM ** -0.5
LOG2E = math.log2(math.e)

LANES = 128
VMEM_LIMIT = 56 * 1024 * 1024
ROW_TILE = 512
FFN_CHUNK = 256
FOX_TILE = 512
FOX_GROUP = 4
AUG = LANES
AUG_ROWS = 32
V_ROWS = HEAD_DIM + 16
FA_ROWS = 16
SWA_KEYS = 3 * BLOCK
SWA_STEPS = 3

_R_QA = 0
_R_VA = _R_QA + FOX_W
_R_QB = _R_VA + FOX_W
_R_VB = _R_QB + SWA_QW
_R_FA = _R_VB + SWA_KVW
_R_GA = _R_FA + FA_ROWS


def _params(*sem, flags=None):
    return pltpu.CompilerParams(dimension_semantics=sem, vmem_limit_bytes=VMEM_LIMIT, flags=flags)


def _const_spec(shape):
    nd = len(shape)
    return pl.BlockSpec(shape, lambda *_: (0,) * nd, pipeline_mode=pl.Buffered(1))


def _dot(a, b):
    return jnp.dot(a, b, preferred_element_type=F32)


def _dot_nt(a, b):
    return lax.dot_general(a, b, (((1,), (1,)), ((), ())), preferred_element_type=F32)


def _dot_tn(a, b):
    return lax.dot_general(a, b, (((0,), (0,)), ((), ())), preferred_element_type=F32)


def _rms_rows(x, g):
    ms = jnp.mean(x * x, axis=-1, keepdims=True)
    return x * lax.rsqrt(ms + EPS) * g


def _lead_blocks_before(block, blocks_per_seq):
    return (block + blocks_per_seq - 1) // blocks_per_seq


def _assemble_rows(x_ref, lead_ref, tile, blocks_per_seq):
    n_sub = x_ref.shape[0] // BLOCK
    first = tile * n_sub
    at = (blocks_per_seq - first % blocks_per_seq) % blocks_per_seq
    parts = []
    for j in range(n_sub):
        cur = x_ref[j * BLOCK:(j + 1) * BLOCK, :]
        if j > 0:
            cur = jnp.where(j > at, x_ref[(j - 1) * BLOCK:j * BLOCK, :], cur)
        parts.append(jnp.where(j == at, lead_ref[...], cur))
    return jnp.concatenate(parts, axis=0)


def _ffn_body(*refs, d_ff, n_pro, blocks_per_seq):
    if blocks_per_seq is None:
        x_ref, g_ref, win_ref, wout_ref, o_ref, win_bf, wout_bf, acc_ref = refs
    else:
        x_ref, lead_ref, g_ref, win_ref, wout_ref, o_ref, win_bf, wout_bf, acc_ref = refs
    i = pl.program_id(0)
    in_chunk = win_ref.shape[1]

    @pl.when(i < n_pro)
    def _():
        win_bf[:, pl.ds(pl.multiple_of(i * in_chunk, LANES), in_chunk)] = win_ref[...].astype(BF16)
        wout_bf[pl.ds(pl.multiple_of(i * FFN_CHUNK, FFN_CHUNK), FFN_CHUNK), :] = (
            wout_ref[...].astype(BF16))

    @pl.when(i >= n_pro)
    def _():
        if blocks_per_seq is None:
            x = x_ref[...]
        else:
            x = _assemble_rows(x_ref, lead_ref, i - n_pro, blocks_per_seq)
        xn = _rms_rows(x, g_ref[...]).astype(BF16)
        for c in range(d_ff // FFN_CHUNK):
            lo = c * FFN_CHUNK
            gate = _dot(xn, win_bf[:, lo:lo + FFN_CHUNK])
            up = _dot(xn, win_bf[:, d_ff + lo:d_ff + lo + FFN_CHUNK])
            act = (gate * jax.nn.sigmoid(gate) * up).astype(BF16)
            part = _dot(act, wout_bf[lo:lo + FFN_CHUNK, :])
            if c == 0:
                acc_ref[...] = part
            else:
                acc_ref[...] += part
        o_ref[...] = x + 0.5 * acc_ref[...]


def _ffn(h2, gain, w_in, w_out, layer, real_only=None, lead=None):
    rows, d = h2.shape
    d_ff = w_out.shape[1]
    tm = ROW_TILE
    n_pro = d_ff // FFN_CHUNK
    in_chunk = 2 * d_ff // n_pro
    blocks_per_seq = None
    extra_in, extra_specs = (), ()

    def row_tile(i):
        return (jnp.maximum(i - n_pro, 0), 0)

    if lead is not None:
        lead_rows, batch, seq = lead
        blocks_per_seq = seq // BLOCK + 1
        rows = batch * blocks_per_seq * BLOCK
        x_arg, n_tiles = h2, rows // tm

        def x_map(i):
            first = jnp.maximum(i - n_pro, 0) * (tm // BLOCK)
            start = first - _lead_blocks_before(first, blocks_per_seq)
            return (pl.multiple_of(start * BLOCK, BLOCK), 0)

        x_spec = pl.BlockSpec((pl.Element(tm), pl.Element(d)), x_map)
        extra_in, extra_specs = (lead_rows,), (_const_spec((BLOCK, d)),)
        out_spec = pl.BlockSpec((tm, d), row_tile)
        out_shape = jax.ShapeDtypeStruct((rows, d), F32)
    elif real_only is None:
        x_arg, n_tiles = h2, rows // tm
        x_spec = out_spec = pl.BlockSpec((tm, d), row_tile)
        out_shape = jax.ShapeDtypeStruct((rows, d), F32)
    else:
        batch, seq = real_only
        per_seq = seq // tm
        x_arg, n_tiles = h2, batch * per_seq

        def seq_tile(i):
            r = jnp.maximum(i - n_pro, 0)
            return r // per_seq, r % per_seq

        def x_map(i):
            b, t = seq_tile(i)
            return (pl.multiple_of(b * (rows // batch) + BLOCK + t * tm, BLOCK), 0)

        def out_map(i):
            b, t = seq_tile(i)
            return (b, t, 0)

        x_spec = pl.BlockSpec((pl.Element(tm), pl.Element(d)), x_map)
        out_spec = pl.BlockSpec((None, tm, d), out_map)
        out_shape = jax.ShapeDtypeStruct((batch, seq, d), F32)

    return pl.pallas_call(
        functools.partial(_ffn_body, d_ff=d_ff, n_pro=n_pro, blocks_per_seq=blocks_per_seq),
        grid=(n_pro + n_tiles,),
        in_specs=[
            x_spec,
            *extra_specs,
            _const_spec((1, d)),
            pl.BlockSpec((None, d, in_chunk), lambda i: (layer, 0, jnp.minimum(i, n_pro - 1))),
            pl.BlockSpec((None, FFN_CHUNK, d), lambda i: (layer, jnp.minimum(i, n_pro - 1), 0)),
        ],
        out_specs=out_spec,
        out_shape=out_shape,
        scratch_shapes=[pltpu.VMEM((d, 2 * d_ff), BF16), pltpu.VMEM((d_ff, d), BF16),
                        pltpu.VMEM((tm, d), F32)],
        compiler_params=_params("arbitrary"),
        name="ffn",
    )(x_arg, *extra_in, gain, w_in, w_out)


def _split3(x):
    hi = x.astype(BF16).astype(F32)
    r = x - hi
    mid = r.astype(BF16).astype(F32)
    return hi, mid, r - mid


def _head_rms_cols(x, g):
    ms = jnp.mean(x * x, axis=0, keepdims=True)
    return x * lax.rsqrt(ms + EPS) * g


def _pair_rms_rows(x, g2):
    lane = lax.broadcasted_iota(jnp.int32, x.shape, 1)
    sq = x * x
    first = lane < HEAD_DIM
    s0 = jnp.sum(jnp.where(first, sq, 0.0), axis=-1, keepdims=True)
    s1 = jnp.sum(jnp.where(first, 0.0, sq), axis=-1, keepdims=True)
    ms = jnp.where(first, s0, s1) * (1.0 / HEAD_DIM)
    return x * lax.rsqrt(ms + EPS) * g2


def _proj_body(x_ref, g_ref, wt_ref, wk_ref, fb_ref, nfq_ref, nfk_ref, nsq_ref, nsk_ref,
               qf_ref, kf_ref, vt_ref, qs_ref, ks_ref, vs_ref, ga_ref, gb_ref,
               carry_ref, tri_ref, *, d_model, seq_pad):
    i = pl.program_id(0)
    tm = x_ref.shape[0]

    @pl.when(i == 0)
    def _():
        src = lax.broadcasted_iota(jnp.int32, (tm, tm), 0)
        dst = lax.broadcasted_iota(jnp.int32, (tm, tm), 1)
        tri_ref[...] = (src <= dst).astype(BF16)

    xn = _rms_rows(x_ref[...], g_ref[...]).astype(BF16)

    def feat(lo, rows):
        return _dot_nt(wt_ref[lo:lo + rows, :], xn)

    lf = jax.nn.log_sigmoid(feat(_R_FA, FA_ROWS) + fb_ref[...])
    r0 = i * tm
    boundary = (r0 // seq_pad + 1) * seq_pad - r0
    hi, mid, lo3 = _split3(lf)
    upper = tri_ref[...]
    c_loc = (_dot(hi.astype(BF16), upper) + _dot(mid.astype(BF16), upper)
             + _dot(lo3.astype(BF16), upper))
    carry = jnp.where(r0 % seq_pad == 0, 0.0, carry_ref[:, 0:1])
    tok = lax.broadcasted_iota(jnp.int32, (1, tm), 1)
    c_end = jnp.sum(jnp.where(tok == boundary - 1, c_loc, 0.0), axis=1, keepdims=True)
    c_t = c_loc + jnp.where(tok < boundary, carry, -c_end)
    carry_ref[...] = jnp.broadcast_to(c_t[:, tm - 1:tm], carry_ref.shape)
    c_t = c_t * LOG2E

    c_pad = jnp.concatenate(
        [jnp.where(lax.broadcasted_iota(jnp.int32, (FA_ROWS, tm), 0) < FOX_HEADS, c_t, 0.0),
         jnp.zeros((LANES - FA_ROWS, tm), F32)], axis=0)
    c_tok = c_pad.T
    lane = lax.broadcasted_iota(jnp.int32, (tm, LANES), 1)
    k_hi, k_mid, k_lo = _split3(c_tok)
    aug_lo = (-k_hi - pltpu.roll(k_mid, FOX_HEADS, 1) - pltpu.roll(k_lo, 2 * FOX_HEADS, 1)
              + jnp.where((lane >= 3 * FOX_HEADS) & (lane < 3 * FOX_HEADS + 3), 1.0, 0.0))
    aug_hi = pltpu.roll(aug_lo, HEAD_DIM, 1)
    first = lane < HEAD_DIM

    kt = _dot(xn, wk_ref[...])
    for j in range(FOX_HEADS // 2):
        kn = _pair_rms_rows(kt[:, j * LANES:(j + 1) * LANES], nfk_ref[...])
        kf_ref[:, (2 * j) * AUG:(2 * j + 1) * AUG] = jnp.where(first, kn, aug_hi).astype(BF16)
        kf_ref[:, (2 * j + 1) * AUG:(2 * j + 2) * AUG] = jnp.where(first, aug_lo, kn).astype(BF16)
    ks_ref[...] = _pair_rms_rows(kt[:, FOX_W:FOX_W + SWA_KVW], nsk_ref[...]).astype(BF16)

    arow = lax.broadcasted_iota(jnp.int32, (AUG_ROWS, tm), 0)
    qa = feat(_R_QA, FOX_W)
    zeros = jnp.zeros((HEAD_DIM - AUG_ROWS, tm), BF16)
    for h in range(FOX_HEADS):
        q_hi, q_mid, q_lo = _split3(c_t[h:h + 1, :])
        onehot = jnp.where((arow % FOX_HEADS == h) & (arow < 3 * FOX_HEADS), 1.0, 0.0)
        aug_q = jnp.where(arow == 3 * FOX_HEADS, q_hi, jnp.where(
            arow == 3 * FOX_HEADS + 1, q_mid, jnp.where(arow == 3 * FOX_HEADS + 2, q_lo, onehot)))
        qn = (_head_rms_cols(qa[h * HEAD_DIM:(h + 1) * HEAD_DIM], nfq_ref[...])
              * (SCALE * LOG2E)).astype(BF16)
        parts = [qn, aug_q.astype(BF16), zeros] if h % 2 == 0 else [aug_q.astype(BF16), zeros, qn]
        qf_ref[h * AUG:(h + 1) * AUG, :] = jnp.concatenate(parts, axis=0)
    ga_ref[...] = jax.nn.sigmoid(feat(_R_GA, d_model)).astype(BF16)
    gb_ref[...] = jax.nn.sigmoid(feat(_R_GA + d_model, d_model)).astype(BF16)
    qb = feat(_R_QB, SWA_QW)
    for h in range(SWA_Q_HEADS):
        sl = slice(h * HEAD_DIM, (h + 1) * HEAD_DIM)
        qs_ref[sl, :] = (_head_rms_cols(qb[sl], nsq_ref[...]) * SCALE).astype(BF16)
    vs_ref[...] = feat(_R_VB, SWA_KVW).astype(BF16)
    va = feat(_R_VA, FOX_W).astype(BF16)
    ones = jnp.ones((V_ROWS - HEAD_DIM, tm), BF16)
    for h in range(FOX_HEADS):
        vt_ref[h * V_ROWS:(h + 1) * V_ROWS, :] = jnp.concatenate(
            [va[h * HEAD_DIM:(h + 1) * HEAD_DIM], ones], axis=0)


def _proj(h2, seq_pad, gain, w_feat, w_keys, fb, nfq, nfk, nsq, nsk):
    rows, d = h2.shape
    tm = ROW_TILE

    def tok(w):
        return pl.BlockSpec((tm, w), lambda i: (i, 0)), jax.ShapeDtypeStruct((rows, w), BF16)

    def feat(r):
        return pl.BlockSpec((r, tm), lambda i: (0, i)), jax.ShapeDtypeStruct((r, rows), BF16)

    outs = [feat(FOX_HEADS * AUG), tok(FOX_HEADS * AUG), feat(FOX_HEADS * V_ROWS), feat(SWA_QW),
            tok(SWA_KVW), feat(SWA_KVW), feat(d), feat(d)]
    return pl.pallas_call(
        functools.partial(_proj_body, d_model=d, seq_pad=seq_pad),
        grid=(rows // tm,),
        in_specs=[
            pl.BlockSpec((tm, d), lambda i: (i, 0)),
            _const_spec((1, d)),
            _const_spec(w_feat.shape),
            _const_spec(w_keys.shape),
            _const_spec((FA_ROWS, 1)),
            _const_spec((HEAD_DIM, 1)), _const_spec((1, LANES)),
            _const_spec((HEAD_DIM, 1)), _const_spec((1, LANES)),
        ],
        out_specs=[o[0] for o in outs],
        out_shape=[o[1] for o in outs],
        scratch_shapes=[pltpu.VMEM((FA_ROWS, LANES), F32), pltpu.VMEM((tm, tm), BF16)],
        compiler_params=_params("arbitrary"),
        name="proj",
    )(h2, gain, w_feat, w_keys, fb, nfq, nfk, nsq, nsk)


def _fox_body(q_ref, k_ref, vt_ref, o_ref, m_ref, acc_ref, st_ref):
    qi = pl.program_id(2)

    def attend(q_start, tq, n_full, diag):
        q = q_ref[:, pl.ds(q_start, tq)]
        m_ref[:, :, :tq] = jnp.full((FOX_GROUP, 1, tq), NEG, F32)
        acc_ref[:, :tq] = jnp.zeros((FOX_GROUP * V_ROWS, tq), F32)

        def scores(kblk, h):
            return _dot(kblk[:, h * AUG:(h + 1) * AUG], q[h * AUG:(h + 1) * AUG, :])

        def softmax(h, st, mask):
            if mask is not None:
                st = jnp.where(mask, st, NEG)
            m_prev = m_ref[h, :, :tq]
            m_new = jnp.maximum(m_prev, jnp.max(st, axis=0, keepdims=True))
            alpha = jnp.exp2(m_prev - m_new)
            m_ref[h, :, :tq] = m_new
            return jnp.exp2(st - m_new).astype(BF16), alpha

        def weighted_values(h, vt, p_alpha):
            p, alpha = p_alpha
            rows = slice(h * V_ROWS, (h + 1) * V_ROWS)
            acc_ref[rows, :tq] = alpha * acc_ref[rows, :tq] + _dot(vt[rows, :], p)

        def process(k_start, tk, mask):
            kblk = k_ref[pl.ds(k_start, tk), :]
            vt = vt_ref[:, pl.ds(k_start, tk)]
            for h in range(FOX_GROUP):
                weighted_values(h, vt, softmax(h, scores(kblk, h), mask))

        def key_tile(j):
            return k_ref[pl.ds(pl.multiple_of(BLOCK + j * FOX_TILE, BLOCK), FOX_TILE), :]

        def pipelined(j, mask, prefetch, with_lead=False):
            kblk = key_tile(j)
            vt = vt_ref[:, pl.ds(pl.multiple_of(BLOCK + j * FOX_TILE, BLOCK), FOX_TILE)]
            st0 = st_ref[...]
            if with_lead:
                k_lead = k_ref[0:BLOCK, :]
                st0 = jnp.concatenate([scores(k_lead, 0), st0], axis=0)
                kblk = jnp.concatenate([k_lead, kblk], axis=0)
                vt = jnp.concatenate([vt_ref[:, 0:BLOCK], vt], axis=1)
            st = [st0, scores(kblk, 1), None, None]
            pa = softmax(0, st[0], mask)
            for h in range(FOX_GROUP):
                if h + 2 < FOX_GROUP:
                    st[h + 2] = scores(kblk, h + 2)
                elif h + 2 == FOX_GROUP and prefetch:
                    st_ref[...] = scores(key_tile(j + 1), 0)
                weighted_values(h, vt, pa)
                if h + 1 < FOX_GROUP:
                    pa = softmax(h + 1, st[h + 1], mask)

        if diag:
            st_ref[...] = scores(key_tile(0), 0)

            def tile_pair(jj, carry):
                pipelined(2 * jj, None, True)
                pipelined(2 * jj + 1, None, True)
                return carry

            lax.fori_loop(0, n_full // 2, tile_pair, 0)

            @pl.when(n_full % 2 == 1)
            def _():
                pipelined(n_full - 1, None, True)
            kr = lax.broadcasted_iota(jnp.int32, (BLOCK + tq, tq), 0)
            qc = lax.broadcasted_iota(jnp.int32, (BLOCK + tq, tq), 1)
            visible = ((kr >= PAD_FRONT) & (kr < BLOCK)) | ((kr >= BLOCK) & (kr - BLOCK <= qc))
            pipelined(n_full, visible, False, with_lead=True)
        else:
            kpos0 = lax.broadcasted_iota(jnp.int32, (BLOCK, tq), 0)
            qc = lax.broadcasted_iota(jnp.int32, (BLOCK, tq), 1)
            process(0, BLOCK, (kpos0 >= PAD_FRONT) & (kpos0 <= qc))
        for h in range(FOX_GROUP):
            denom = acc_ref[h * V_ROWS + HEAD_DIM:h * V_ROWS + HEAD_DIM + 1, :tq]
            o_ref[h * HEAD_DIM:(h + 1) * HEAD_DIM, pl.ds(q_start, tq)] = (
                acc_ref[h * V_ROWS:h * V_ROWS + HEAD_DIM, :tq] * (1.0 / denom)).astype(BF16)

    @pl.when(qi == 0)
    def _():
        attend(0, BLOCK, 0, False)

    @pl.when(qi > 0)
    def _():
        attend(pl.multiple_of(BLOCK + (qi - 1) * FOX_TILE, BLOCK), FOX_TILE, qi - 1, True)


def _fox(qft, kf, vt, batch):
    rows = kf.shape[0]
    p = rows // batch
    n_grp = FOX_HEADS // FOX_GROUP
    width = FOX_GROUP * HEAD_DIM
    n_q = (p - BLOCK) // FOX_TILE + 1

    def feat(r):
        return pl.BlockSpec((r, p), lambda b, g, q: (g, b))

    return pl.pallas_call(
        _fox_body,
        grid=(batch, n_grp, n_q),
        in_specs=[feat(FOX_GROUP * AUG),
                  pl.BlockSpec((p, FOX_GROUP * AUG), lambda b, g, q: (b, g)),
                  feat(FOX_GROUP * V_ROWS)],
        out_specs=feat(width),
        out_shape=jax.ShapeDtypeStruct((FOX_W, rows), BF16),
        scratch_shapes=[
            pltpu.VMEM((FOX_GROUP, 1, FOX_TILE), F32),
            pltpu.VMEM((FOX_GROUP * V_ROWS, FOX_TILE), F32),
            pltpu.VMEM((FOX_TILE, FOX_TILE), F32),
        ],
        compiler_params=_params("arbitrary", "arbitrary", "arbitrary"),
        name="fox",
    )(qft, kf, vt)


def _t5_bucket_np(dist):
    n = np.maximum(dist, 0)
    max_exact = N_BUCKETS // 2
    nf = np.maximum(n, 1).astype(np.float64)
    large = max_exact + (np.log(nf / max_exact) / math.log(MAX_DISTANCE / max_exact)
                         * (N_BUCKETS - max_exact)).astype(np.int64)
    large = np.minimum(large, N_BUCKETS - 1)
    return np.where(n < max_exact, n, large).astype(np.int32)


def _swa_static_buckets():
    k = np.arange(SWA_KEYS)[:, None]
    q = np.arange(BLOCK)[None, :]
    d = BLOCK + q - k
    bucket = np.where((d >= 0) & (d < WINDOW) & (k < 2 * BLOCK), _t5_bucket_np(d), -1)
    bucket = np.where(k >= 2 * BLOCK + PAD_FRONT, N_BUCKETS - 1, bucket)
    return np.ascontiguousarray(np.broadcast_to(bucket, (SWA_KEYS, BLOCK))).astype(np.int32)


def _swa_body(table_ref, sink_ref, bucket_ref, q_ref, k_ref, vt_ref, o_ref, bias_ref):
    g = pl.program_id(1)
    t = pl.program_id(2)
    cols = SWA_GROUP * BLOCK

    @pl.when(t == 0)
    def _():
        bucket = bucket_ref[...]
        slot = lax.broadcasted_iota(jnp.int32, (SWA_KEYS, BLOCK), 0)
        qpos = lax.broadcasted_iota(jnp.int32, (SWA_KEYS, BLOCK), 1)
        for r in range(SWA_GROUP):
            bias = jnp.full((SWA_KEYS, BLOCK), NEG, F32)
            for bkt in range(N_BUCKETS):
                bias = jnp.where(bucket == bkt, table_ref[bkt, g * SWA_GROUP + r], bias)
            bias_ref[2, :, r * BLOCK:(r + 1) * BLOCK] = bias
            for n in range(2):
                band_ok = (slot >= PAD_FRONT - (n - 1) * BLOCK) & (slot < 2 * BLOCK)
                meta_ok = ((slot >= 2 * BLOCK + PAD_FRONT)
                           & (n * BLOCK + qpos - (slot - 2 * BLOCK) >= WINDOW))
                bias_ref[n, :, r * BLOCK:(r + 1) * BLOCK] = jnp.where(band_ok | meta_ok, bias, NEG)

    sink = jnp.concatenate(
        [jnp.full((1, BLOCK), sink_ref[g * SWA_GROUP + r], F32) for r in range(SWA_GROUP)], axis=1)
    grp_rows = lax.broadcasted_iota(jnp.int32, (SWA_KVW, cols), 0) // HEAD_DIM
    v_rows = pl.ds(pl.multiple_of(g * HEAD_DIM, HEAD_DIM), HEAD_DIM)
    k_lead = k_ref[0:BLOCK, :]
    v_lead = vt_ref[v_rows, 0:BLOCK]

    n_blocks = q_ref.shape[1] // BLOCK

    def band(i):
        n = t * n_blocks + i
        return (n, pl.multiple_of(jnp.maximum(n - 1, 0) * BLOCK, BLOCK),
                pl.multiple_of(n * BLOCK, BLOCK))

    def scores(i):
        n, prev, cur = band(i)
        keys = jnp.concatenate(
            [k_ref[pl.ds(prev, BLOCK), :], k_ref[pl.ds(cur, BLOCK), :], k_lead], axis=0)
        q4 = jnp.concatenate(
            [q_ref[r * HEAD_DIM:(r + 1) * HEAD_DIM, i * BLOCK:(i + 1) * BLOCK]
             for r in range(SWA_GROUP)], axis=1)
        q8 = jnp.concatenate([q4, q4], axis=0)
        q8 = jnp.where(grp_rows == g, q8, jnp.zeros_like(q8))
        return _dot(keys, q8) + bias_ref[jnp.minimum(n, 2)]

    def finish(i, s):
        _, prev, cur = band(i)
        vals = jnp.concatenate(
            [vt_ref[v_rows, pl.ds(prev, BLOCK)], vt_ref[v_rows, pl.ds(cur, BLOCK)], v_lead], axis=1)
        mx = jnp.maximum(jnp.max(s, axis=0, keepdims=True), sink)
        p = jnp.exp(s - mx)
        denom = jnp.sum(p, axis=0, keepdims=True) + jnp.exp(sink - mx)
        out = _dot(vals, p.astype(BF16)) * (1.0 / denom)
        for r in range(SWA_GROUP):
            o_ref[r * HEAD_DIM:(r + 1) * HEAD_DIM, i * BLOCK:(i + 1) * BLOCK] = (
                out[:, r * BLOCK:(r + 1) * BLOCK].astype(BF16))

    s = scores(0)
    for i in range(n_blocks):
        s_next = scores(i + 1) if i + 1 < n_blocks else None
        finish(i, s)
        s = s_next


def _swa(table, sinks, qst, ks, vst, batch):
    rows = ks.shape[0]
    p = rows // batch
    n_t = SWA_STEPS
    swa_tile = p // n_t
    width = SWA_GROUP * HEAD_DIM
    bucket = jnp.asarray(_swa_static_buckets())
    smem = pl.BlockSpec(memory_space=pltpu.SMEM)
    tile = pl.BlockSpec((width, swa_tile), lambda b, g, t: (g, b * n_t + t))
    return pl.pallas_call(
        _swa_body,
        grid=(batch, SWA_KV_HEADS, n_t),
        in_specs=[smem, smem, _const_spec((SWA_KEYS, BLOCK)), tile,
                  pl.BlockSpec((p, SWA_KVW), lambda b, g, t: (b, 0)),
                  pl.BlockSpec((SWA_KVW, p), lambda b, g, t: (0, b))],
        out_specs=tile,
        out_shape=jax.ShapeDtypeStruct((SWA_QW, rows), BF16),
        scratch_shapes=[pltpu.VMEM((3, SWA_KEYS, SWA_GROUP * BLOCK), F32)],
        compiler_params=_params("arbitrary", "arbitrary", "arbitrary"),
        name="swa",
    )(table, sinks, bucket, qst, ks, vst)


def _mix_body(h_ref, of_ref, os_ref, ga_ref, gb_ref, wf_ref, ws_ref, wo_ref, o_ref):
    y = (ga_ref[...].astype(F32) * _dot(wf_ref[...], of_ref[...])
         + gb_ref[...].astype(F32) * _dot(ws_ref[...], os_ref[...]))
    o_ref[...] = h_ref[...] + _dot_tn(y.astype(BF16), wo_ref[...])


def _mix_out(h2, o_fox, o_swa, ga, gb, wf_t, ws_t, wo):
    rows, d = h2.shape
    tm = ROW_TILE

    def feat(r):
        return pl.BlockSpec((r, tm), lambda i: (0, i))

    tok = pl.BlockSpec((tm, d), lambda i: (i, 0))
    return pl.pallas_call(
        _mix_body,
        grid=(rows // tm,),
        in_specs=[tok, feat(FOX_W), feat(SWA_QW), feat(d), feat(d),
                  _const_spec(wf_t.shape), _const_spec(ws_t.shape), _const_spec(wo.shape)],
        out_specs=tok,
        out_shape=jax.ShapeDtypeStruct((rows, d), F32),
        compiler_params=_params("arbitrary"),
        name="mix_out",
    )(h2, o_fox, o_swa, ga, gb, wf_t, ws_t, wo)


def _w_in_body(w_ref, wf_ref, wk_ref, *, d_model):
    src = np.cumsum([0, FOX_W, FOX_W, FOX_W, FOX_HEADS, SWA_QW, SWA_KVW, SWA_KVW, d_model])
    qa, ka, va, fa, qb, kb, vb, ga, gb = (int(s) for s in src)

    def move(dst, lo, n):
        wf_ref[dst:dst + n] = w_ref[lo:lo + n].astype(BF16)

    move(_R_QA, qa, FOX_W)
    move(_R_VA, va, FOX_W)
    move(_R_QB, qb, SWA_QW)
    move(_R_VB, vb, SWA_KVW)
    move(_R_GA, ga, d_model)
    move(_R_GA + d_model, gb, d_model)
    wf_ref[_R_FA:_R_FA + FA_ROWS] = jnp.concatenate(
        [w_ref[fa:fa + FOX_HEADS], jnp.zeros((FA_ROWS - FOX_HEADS, w_ref.shape[1]), F32)],
        axis=0).astype(BF16)
    wk_ref[:, 0:FOX_W] = w_ref[ka:ka + FOX_W].T.astype(BF16)
    wk_ref[:, FOX_W:FOX_W + SWA_KVW] = w_ref[kb:kb + SWA_KVW].T.astype(BF16)


def _prep_w_in(w_in, layer):
    w_t = jnp.swapaxes(w_in, 1, 2)
    _, d_in, d = w_t.shape
    n_feat = _R_GA + 2 * d
    return pl.pallas_call(
        functools.partial(_w_in_body, d_model=d),
        grid=(1,),
        in_specs=[pl.BlockSpec((None, d_in, d), lambda i: (layer, 0, 0), pipeline_mode=pl.Buffered(1))],
        out_specs=[pl.BlockSpec((n_feat, d), lambda i: (0, 0)),
                   pl.BlockSpec((d, FOX_W + SWA_KVW), lambda i: (0, 0))],
        out_shape=[jax.ShapeDtypeStruct((n_feat, d), BF16),
                   jax.ShapeDtypeStruct((d, FOX_W + SWA_KVW), BF16)],
        compiler_params=_params("arbitrary"),
        name="prep_w_in",
    )(w_t)


def kernel(x, meta_tokens, rel_bias_table, ffn1_norm, ffn1_w_in, ffn1_w_out, mix_norm, w_in, forget_bias, fox_q_norm, fox_k_norm, swa_q_norm, swa_k_norm, swa_sinks, w_branch_fox, w_branch_swa, w_out, ffn2_norm, ffn2_w_in, ffn2_w_out):
    b, seq, d = x.shape
    depth = w_in.shape[0]
    p = BLOCK + seq
    assert seq % FOX_TILE == 0 and p % (SWA_STEPS * BLOCK) == 0 and (b * p) % ROW_TILE == 0
    lead = jnp.concatenate([jnp.zeros((PAD_FRONT, d), x.dtype), meta_tokens.astype(x.dtype)], axis=0)
    h = x.reshape(b * seq, d)
    table = rel_bias_table.astype(F32)

    def col(v):
        return v.astype(F32)[:, None]

    def twice(v):
        return jnp.tile(v.astype(F32), 2)[None]

    for l in range(depth):
        h = _ffn(h, ffn1_norm[l][None], ffn1_w_in, ffn1_w_out, l,
                 lead=(lead, b, seq) if l == 0 else None)
        w_feat, w_keys = _prep_w_in(w_in, l)
        fb = jnp.pad(col(forget_bias[l]), ((0, FA_ROWS - FOX_HEADS), (0, 0)))
        qft, kf, vt, qst, ks, vst, ga, gb = _proj(
            h, p, mix_norm[l][None], w_feat, w_keys, fb,
            col(fox_q_norm[l]), twice(fox_k_norm[l]), col(swa_q_norm[l]), twice(swa_k_norm[l]))
        o_fox = _fox(qft, kf, vt, b)
        o_swa = _swa(table, swa_sinks[l].astype(F32), qst, ks, vst, b)
        h = _mix_out(h, o_fox, o_swa, ga, gb, w_branch_fox[l].astype(BF16).T,
                     w_branch_swa[l].astype(BF16).T, w_out[l].astype(BF16))
        last = l == depth - 1
        h = _ffn(h, ffn2_norm[l][None], ffn2_w_in, ffn2_w_out, l, (b, seq) if last else None)
    return h
```

```python
import functools
import math

import jax
import jax.numpy as jnp
import numpy as np
from jax import lax
from jax.experimental import pallas as pl
from jax.experimental.pallas import tpu as pltpu

F32 = jnp.float32
BF16 = jnp.bfloat16

N_META = 16
BLOCK = 128
PAD_FRONT = BLOCK - N_META
HEAD_DIM = 64
FOX_HEADS = 8
SWA_Q_HEADS = 8
SWA_KV_HEADS = 2
SWA_GROUP = SWA_Q_HEADS // SWA_KV_HEADS
WINDOW = 128
N_BUCKETS = 32
MAX_DISTANCE = 128
EPS = 1e-6
NEG = -1e30
FOX_W = FOX_HEADS * HEAD_DIM
SWA_QW = SWA_Q_HEADS * HEAD_DIM
SWA_KVW = SWA_KV_HEADS * HEAD_DIM
SCALE = HEAD_DIM ** -0.5
LOG2E = math.log2(math.e)

LANES = 128
VMEM_LIMIT = 56 * 1024 * 1024
ROW_TILE = 512
FFN_TILES = (1024, 768, 512)
FFN_CHUNK = 256
FOX_TILE = 512
FOX_GROUP = 4
AUG = LANES
AUG_ROWS = 32
V_ROWS = HEAD_DIM + 16
FA_ROWS = 16
SWA_KEYS = 3 * BLOCK
SWA_STEPS = 3

_R_QA = 0
_R_VA = _R_QA + FOX_W
_R_QB = _R_VA + FOX_W
_R_VB = _R_QB + SWA_QW
_R_FA = _R_VB + SWA_KVW
_R_GA = _R_FA + FA_ROWS


def _params(*sem, flags=None):
    return pltpu.CompilerParams(dimension_semantics=sem, vmem_limit_bytes=VMEM_LIMIT, flags=flags)


def _const_spec(shape):
    nd = len(shape)
    return pl.BlockSpec(shape, lambda *_: (0,) * nd, pipeline_mode=pl.Buffered(1))


def _dot(a, b):
    return jnp.dot(a, b, preferred_element_type=F32)


def _dot_nt(a, b):
    return lax.dot_general(a, b, (((1,), (1,)), ((), ())), preferred_element_type=F32)


def _dot_tn(a, b):
    return lax.dot_general(a, b, (((0,), (0,)), ((), ())), preferred_element_type=F32)


def _rms_rows(x, g):
    ms = jnp.mean(x * x, axis=-1, keepdims=True)
    return x * lax.rsqrt(ms + EPS) * g


def _lead_blocks_before(block, blocks_per_seq):
    return (block + blocks_per_seq - 1) // blocks_per_seq


def _assemble_rows(x_ref, lead_ref, tile, blocks_per_seq):
    n_sub = x_ref.shape[0] // BLOCK
    first = tile * n_sub
    at = (blocks_per_seq - first % blocks_per_seq) % blocks_per_seq
    parts = []
    for j in range(n_sub):
        cur = x_ref[j * BLOCK:(j + 1) * BLOCK, :]
        if j > 0:
            cur = jnp.where(j > at, x_ref[(j - 1) * BLOCK:j * BLOCK, :], cur)
        parts.append(jnp.where(j == at, lead_ref[...], cur))
    return jnp.concatenate(parts, axis=0)


def _ffn_body(*refs, d_ff, n_pro, blocks_per_seq):
    if blocks_per_seq is None:
        x_ref, g_ref, win_ref, wout_ref, o_ref, win_bf, wout_bf, acc_ref = refs
    else:
        x_ref, lead_ref, g_ref, win_ref, wout_ref, o_ref, win_bf, wout_bf, acc_ref = refs
    i = pl.program_id(0)
    in_chunk = win_ref.shape[1]

    @pl.when(i < n_pro)
    def _():
        win_bf[:, pl.ds(pl.multiple_of(i * in_chunk, LANES), in_chunk)] = win_ref[...].astype(BF16)
        wout_bf[pl.ds(pl.multiple_of(i * FFN_CHUNK, FFN_CHUNK), FFN_CHUNK), :] = (
            wout_ref[...].astype(BF16))

    @pl.when(i >= n_pro)
    def _():
        if blocks_per_seq is None:
            x = x_ref[...]
        else:
            x = _assemble_rows(x_ref, lead_ref, i - n_pro, blocks_per_seq)
        xn = _rms_rows(x, g_ref[...]).astype(BF16)
        for c in range(d_ff // FFN_CHUNK):
            lo = c * FFN_CHUNK
            gate = _dot(xn, win_bf[:, lo:lo + FFN_CHUNK])
            up = _dot(xn, win_bf[:, d_ff + lo:d_ff + lo + FFN_CHUNK])
            act = (gate * jax.nn.sigmoid(gate) * up).astype(BF16)
            part = _dot(act, wout_bf[lo:lo + FFN_CHUNK, :])
            if c == 0:
                acc_ref[...] = part
            else:
                acc_ref[...] += part
        o_ref[...] = x + 0.5 * acc_ref[...]


def _ffn(h2, gain, w_in, w_out, layer, real_only=None, lead=None):
    rows, d = h2.shape
    d_ff = w_out.shape[1]
    if lead is not None:
        walked = lead[1] * (lead[2] + BLOCK)
    else:
        walked = rows if real_only is None else real_only[1]
    tm = next(t for t in FFN_TILES if walked % t == 0)
    n_pro = d_ff // FFN_CHUNK
    in_chunk = 2 * d_ff // n_pro
    blocks_per_seq = None
    extra_in, extra_specs = (), ()

    def row_tile(i):
        return (jnp.maximum(i - n_pro, 0), 0)

    if lead is not None:
        lead_rows, batch, seq = lead
        blocks_per_seq = seq // BLOCK + 1
        rows = batch * blocks_per_seq * BLOCK
        x_arg, n_tiles = h2, rows // tm

        def x_map(i):
            first = jnp.maximum(i - n_pro, 0) * (tm // BLOCK)
            start = first - _lead_blocks_before(first, blocks_per_seq)
            return (pl.multiple_of(start * BLOCK, BLOCK), 0)

        x_spec = pl.BlockSpec((pl.Element(tm), pl.Element(d)), x_map)
        extra_in, extra_specs = (lead_rows,), (_const_spec((BLOCK, d)),)
        out_spec = pl.BlockSpec((tm, d), row_tile)
        out_shape = jax.ShapeDtypeStruct((rows, d), F32)
    elif real_only is None:
        x_arg, n_tiles = h2, rows // tm
        x_spec = out_spec = pl.BlockSpec((tm, d), row_tile)
        out_shape = jax.ShapeDtypeStruct((rows, d), F32)
    else:
        batch, seq = real_only
        per_seq = seq // tm
        x_arg, n_tiles = h2, batch * per_seq

        def seq_tile(i):
            r = jnp.maximum(i - n_pro, 0)
            return r // per_seq, r % per_seq

        def x_map(i):
            b, t = seq_tile(i)
            return (pl.multiple_of(b * (rows // batch) + BLOCK + t * tm, BLOCK), 0)

        def out_map(i):
            b, t = seq_tile(i)
            return (b, t, 0)

        x_spec = pl.BlockSpec((pl.Element(tm), pl.Element(d)), x_map)
        out_spec = pl.BlockSpec((None, tm, d), out_map)
        out_shape = jax.ShapeDtypeStruct((batch, seq, d), F32)

    return pl.pallas_call(
        functools.partial(_ffn_body, d_ff=d_ff, n_pro=n_pro, blocks_per_seq=blocks_per_seq),
        grid=(n_pro + n_tiles,),
        in_specs=[
            x_spec,
            *extra_specs,
            _const_spec((1, d)),
            pl.BlockSpec((None, d, in_chunk), lambda i: (layer, 0, jnp.minimum(i, n_pro - 1))),
            pl.BlockSpec((None, FFN_CHUNK, d), lambda i: (layer, jnp.minimum(i, n_pro - 1), 0)),
        ],
        out_specs=out_spec,
        out_shape=out_shape,
        scratch_shapes=[pltpu.VMEM((d, 2 * d_ff), BF16), pltpu.VMEM((d_ff, d), BF16),
                        pltpu.VMEM((tm, d), F32)],
        compiler_params=_params("arbitrary"),
        name="ffn",
    )(x_arg, *extra_in, gain, w_in, w_out)


def _split3(x):
    hi = x.astype(BF16).astype(F32)
    r = x - hi
    mid = r.astype(BF16).astype(F32)
    return hi, mid, r - mid


def _head_rms_cols(x, g):
    ms = jnp.mean(x * x, axis=0, keepdims=True)
    return x * lax.rsqrt(ms + EPS) * g


def _pair_rms_rows(x, g2):
    lane = lax.broadcasted_iota(jnp.int32, x.shape, 1)
    sq = x * x
    first = lane < HEAD_DIM
    s0 = jnp.sum(jnp.where(first, sq, 0.0), axis=-1, keepdims=True)
    s1 = jnp.sum(jnp.where(first, 0.0, sq), axis=-1, keepdims=True)
    ms = jnp.where(first, s0, s1) * (1.0 / HEAD_DIM)
    return x * lax.rsqrt(ms + EPS) * g2


def _proj_body(x_ref, g_ref, wt_ref, wk_ref, fb_ref, nfq_ref, nfk_ref, nsq_ref, nsk_ref,
               qf_ref, kf_ref, vt_ref, qs_ref, ks_ref, vs_ref, ga_ref, gb_ref,
               carry_ref, tri_ref, *, d_model, seq_pad):
    i = pl.program_id(0)
    tm = x_ref.shape[0]

    @pl.when(i == 0)
    def _():
        src = lax.broadcasted_iota(jnp.int32, (tm, tm), 0)
        dst = lax.broadcasted_iota(jnp.int32, (tm, tm), 1)
        tri_ref[...] = (src <= dst).astype(BF16)

    xn = _rms_rows(x_ref[...], g_ref[...]).astype(BF16)

    def feat(lo, rows):
        return _dot_nt(wt_ref[lo:lo + rows, :], xn)

    lf = jax.nn.log_sigmoid(feat(_R_FA, FA_ROWS) + fb_ref[...])
    r0 = i * tm
    boundary = (r0 // seq_pad + 1) * seq_pad - r0
    hi, mid, lo3 = _split3(lf)
    upper = tri_ref[...]
    c_loc = (_dot(hi.astype(BF16), upper) + _dot(mid.astype(BF16), upper)
             + _dot(lo3.astype(BF16), upper))
    carry = jnp.where(r0 % seq_pad == 0, 0.0, carry_ref[:, 0:1])
    tok = lax.broadcasted_iota(jnp.int32, (1, tm), 1)
    c_end = jnp.sum(jnp.where(tok == boundary - 1, c_loc, 0.0), axis=1, keepdims=True)
    c_t = c_loc + jnp.where(tok < boundary, carry, -c_end)
    carry_ref[...] = jnp.broadcast_to(c_t[:, tm - 1:tm], carry_ref.shape)
    c_t = c_t * LOG2E

    c_pad = jnp.concatenate(
        [jnp.where(lax.broadcasted_iota(jnp.int32, (FA_ROWS, tm), 0) < FOX_HEADS, c_t, 0.0),
         jnp.zeros((LANES - FA_ROWS, tm), F32)], axis=0)
    c_tok = c_pad.T
    lane = lax.broadcasted_iota(jnp.int32, (tm, LANES), 1)
    k_hi, k_mid, k_lo = _split3(c_tok)
    aug_lo = (-k_hi - pltpu.roll(k_mid, FOX_HEADS, 1) - pltpu.roll(k_lo, 2 * FOX_HEADS, 1)
              + jnp.where((lane >= 3 * FOX_HEADS) & (lane < 3 * FOX_HEADS + 3), 1.0, 0.0))
    aug_hi = pltpu.roll(aug_lo, HEAD_DIM, 1)
    first = lane < HEAD_DIM

    kt = _dot(xn, wk_ref[...])
    for j in range(FOX_HEADS // 2):
        kn = _pair_rms_rows(kt[:, j * LANES:(j + 1) * LANES], nfk_ref[...])
        kf_ref[:, (2 * j) * AUG:(2 * j + 1) * AUG] = jnp.where(first, kn, aug_hi).astype(BF16)
        kf_ref[:, (2 * j + 1) * AUG:(2 * j + 2) * AUG] = jnp.where(first, aug_lo, kn).astype(BF16)
    ks_ref[...] = _pair_rms_rows(kt[:, FOX_W:FOX_W + SWA_KVW], nsk_ref[...]).astype(BF16)

    arow = lax.broadcasted_iota(jnp.int32, (AUG_ROWS, tm), 0)
    qa = feat(_R_QA, FOX_W)
    zeros = jnp.zeros((HEAD_DIM - AUG_ROWS, tm), BF16)
    for h in range(FOX_HEADS):
        q_hi, q_mid, q_lo = _split3(c_t[h:h + 1, :])
        onehot = jnp.where((arow % FOX_HEADS == h) & (arow < 3 * FOX_HEADS), 1.0, 0.0)
        aug_q = jnp.where(arow == 3 * FOX_HEADS, q_hi, jnp.where(
            arow == 3 * FOX_HEADS + 1, q_mid, jnp.where(arow == 3 * FOX_HEADS + 2, q_lo, onehot)))
        qn = (_head_rms_cols(qa[h * HEAD_DIM:(h + 1) * HEAD_DIM], nfq_ref[...])
              * (SCALE * LOG2E)).astype(BF16)
        parts = [qn, aug_q.astype(BF16), zeros] if h % 2 == 0 else [aug_q.astype(BF16), zeros, qn]
        qf_ref[h * AUG:(h + 1) * AUG, :] = jnp.concatenate(parts, axis=0)
    ga_ref[...] = jax.nn.sigmoid(feat(_R_GA, d_model)).astype(BF16)
    gb_ref[...] = jax.nn.sigmoid(feat(_R_GA + d_model, d_model)).astype(BF16)
    qb = feat(_R_QB, SWA_QW)
    for h in range(SWA_Q_HEADS):
        sl = slice(h * HEAD_DIM, (h + 1) * HEAD_DIM)
        qs_ref[sl, :] = (_head_rms_cols(qb[sl], nsq_ref[...]) * SCALE).astype(BF16)
    vs_ref[...] = feat(_R_VB, SWA_KVW).astype(BF16)
    va = feat(_R_VA, FOX_W).astype(BF16)
    ones = jnp.ones((V_ROWS - HEAD_DIM, tm), BF16)
    for h in range(FOX_HEADS):
        vt_ref[h * V_ROWS:(h + 1) * V_ROWS, :] = jnp.concatenate(
            [va[h * HEAD_DIM:(h + 1) * HEAD_DIM], ones], axis=0)


def _proj(h2, seq_pad, gain, w_feat, w_keys, fb, nfq, nfk, nsq, nsk):
    rows, d = h2.shape
    tm = ROW_TILE

    def tok(w):
        return pl.BlockSpec((tm, w), lambda i: (i, 0)), jax.ShapeDtypeStruct((rows, w), BF16)

    def feat(r):
        return pl.BlockSpec((r, tm), lambda i: (0, i)), jax.ShapeDtypeStruct((r, rows), BF16)

    outs = [feat(FOX_HEADS * AUG), tok(FOX_HEADS * AUG), feat(FOX_HEADS * V_ROWS), feat(SWA_QW),
            tok(SWA_KVW), feat(SWA_KVW), feat(d), feat(d)]
    return pl.pallas_call(
        functools.partial(_proj_body, d_model=d, seq_pad=seq_pad),
        grid=(rows // tm,),
        in_specs=[
            pl.BlockSpec((tm, d), lambda i: (i, 0)),
            _const_spec((1, d)),
            _const_spec(w_feat.shape),
            _const_spec(w_keys.shape),
            _const_spec((FA_ROWS, 1)),
            _const_spec((HEAD_DIM, 1)), _const_spec((1, LANES)),
            _const_spec((HEAD_DIM, 1)), _const_spec((1, LANES)),
        ],
        out_specs=[o[0] for o in outs],
        out_shape=[o[1] for o in outs],
        scratch_shapes=[pltpu.VMEM((FA_ROWS, LANES), F32), pltpu.VMEM((tm, tm), BF16)],
        compiler_params=_params("arbitrary"),
        name="proj",
    )(h2, gain, w_feat, w_keys, fb, nfq, nfk, nsq, nsk)


def _fox_body(q_ref, k_ref, vt_ref, o_ref, m_ref, acc_ref, st_ref):
    def attend(q_start, tq, n_full, diag):
        q = q_ref[:, pl.ds(q_start, tq)]
        m_ref[:, :, :tq] = jnp.full((FOX_GROUP, 1, tq), NEG, F32)
        acc_ref[:, :tq] = jnp.zeros((FOX_GROUP * V_ROWS, tq), F32)

        def scores(kblk, h):
            return _dot(kblk[:, h * AUG:(h + 1) * AUG], q[h * AUG:(h + 1) * AUG, :])

        def softmax(h, st, mask):
            if mask is not None:
                st = jnp.where(mask, st, NEG)
            m_prev = m_ref[h, :, :tq]
            m_new = jnp.maximum(m_prev, jnp.max(st, axis=0, keepdims=True))
            alpha = jnp.exp2(m_prev - m_new)
            m_ref[h, :, :tq] = m_new
            return jnp.exp2(st - m_new).astype(BF16), alpha

        def weighted_values(h, vt, p_alpha):
            p, alpha = p_alpha
            rows = slice(h * V_ROWS, (h + 1) * V_ROWS)
            acc_ref[rows, :tq] = alpha * acc_ref[rows, :tq] + _dot(vt[rows, :], p)

        def process(k_start, tk, mask):
            kblk = k_ref[pl.ds(k_start, tk), :]
            vt = vt_ref[:, pl.ds(k_start, tk)]
            for h in range(FOX_GROUP):
                weighted_values(h, vt, softmax(h, scores(kblk, h), mask))

        def key_tile(j):
            return k_ref[pl.ds(pl.multiple_of(BLOCK + j * FOX_TILE, BLOCK), FOX_TILE), :]

        def pipelined(j, mask, prefetch, with_lead=False):
            kblk = key_tile(j)
            vt = vt_ref[:, pl.ds(pl.multiple_of(BLOCK + j * FOX_TILE, BLOCK), FOX_TILE)]
            st0 = st_ref[...]
            if with_lead:
                k_lead = k_ref[0:BLOCK, :]
                st0 = jnp.concatenate([scores(k_lead, 0), st0], axis=0)
                kblk = jnp.concatenate([k_lead, kblk], axis=0)
                vt = jnp.concatenate([vt_ref[:, 0:BLOCK], vt], axis=1)
            st = [st0, scores(kblk, 1), None, None]
            pa = softmax(0, st[0], mask)
            for h in range(FOX_GROUP):
                if h + 2 < FOX_GROUP:
                    st[h + 2] = scores(kblk, h + 2)
                elif h + 2 == FOX_GROUP and prefetch:
                    st_ref[...] = scores(key_tile(j + 1), 0)
                weighted_values(h, vt, pa)
                if h + 1 < FOX_GROUP:
                    pa = softmax(h + 1, st[h + 1], mask)

        if diag:
            st_ref[...] = scores(key_tile(0), 0)

            def tile_pair(jj, carry):
                pipelined(2 * jj, None, True)
                pipelined(2 * jj + 1, None, True)
                return carry

            lax.fori_loop(0, n_full // 2, tile_pair, 0)

            @pl.when(n_full % 2 == 1)
            def _():
                pipelined(n_full - 1, None, True)
            kr = lax.broadcasted_iota(jnp.int32, (BLOCK + tq, tq), 0)
            qc = lax.broadcasted_iota(jnp.int32, (BLOCK + tq, tq), 1)
            visible = ((kr >= PAD_FRONT) & (kr < BLOCK)) | ((kr >= BLOCK) & (kr - BLOCK <= qc))
            pipelined(n_full, visible, False, with_lead=True)
        else:
            kpos0 = lax.broadcasted_iota(jnp.int32, (BLOCK, tq), 0)
            qc = lax.broadcasted_iota(jnp.int32, (BLOCK, tq), 1)
            process(0, BLOCK, (kpos0 >= PAD_FRONT) & (kpos0 <= qc))
        for h in range(FOX_GROUP):
            denom = acc_ref[h * V_ROWS + HEAD_DIM:h * V_ROWS + HEAD_DIM + 1, :tq]
            o_ref[h * HEAD_DIM:(h + 1) * HEAD_DIM, pl.ds(q_start, tq)] = (
                acc_ref[h * V_ROWS:h * V_ROWS + HEAD_DIM, :tq] * (1.0 / denom)).astype(BF16)

    attend(0, BLOCK, 0, False)

    def query_tile(i, carry):
        attend(pl.multiple_of(BLOCK + i * FOX_TILE, BLOCK), FOX_TILE, i, True)
        return carry

    lax.fori_loop(0, (q_ref.shape[1] - BLOCK) // FOX_TILE, query_tile, 0)


def _fox(qft, kf, vt, batch):
    rows = kf.shape[0]
    p = rows // batch
    n_grp = FOX_HEADS // FOX_GROUP
    width = FOX_GROUP * HEAD_DIM

    def feat(r):
        return pl.BlockSpec((r, p), lambda b, g: (g, b))

    return pl.pallas_call(
        _fox_body,
        grid=(batch, n_grp),
        in_specs=[feat(FOX_GROUP * AUG),
                  pl.BlockSpec((p, FOX_GROUP * AUG), lambda b, g: (b, g)),
                  feat(FOX_GROUP * V_ROWS)],
        out_specs=feat(width),
        out_shape=jax.ShapeDtypeStruct((FOX_W, rows), BF16),
        scratch_shapes=[
            pltpu.VMEM((FOX_GROUP, 1, FOX_TILE), F32),
            pltpu.VMEM((FOX_GROUP * V_ROWS, FOX_TILE), F32),
            pltpu.VMEM((FOX_TILE, FOX_TILE), F32),
        ],
        compiler_params=_params("arbitrary", "arbitrary"),
        name="fox",
    )(qft, kf, vt)


def _t5_bucket_np(dist):
    n = np.maximum(dist, 0)
    max_exact = N_BUCKETS // 2
    nf = np.maximum(n, 1).astype(np.float64)
    large = max_exact + (np.log(nf / max_exact) / math.log(MAX_DISTANCE / max_exact)
                         * (N_BUCKETS - max_exact)).astype(np.int64)
    large = np.minimum(large, N_BUCKETS - 1)
    return np.where(n < max_exact, n, large).astype(np.int32)


def _swa_static_buckets():
    k = np.arange(SWA_KEYS)[:, None]
    q = np.arange(BLOCK)[None, :]
    d = BLOCK + q - k
    bucket = np.where((d >= 0) & (d < WINDOW) & (k < 2 * BLOCK), _t5_bucket_np(d), -1)
    bucket = np.where(k >= 2 * BLOCK + PAD_FRONT, N_BUCKETS - 1, bucket)
    return np.ascontiguousarray(np.broadcast_to(bucket, (SWA_KEYS, BLOCK))).astype(np.int32)


def _swa_body(table_ref, sink_ref, bucket_ref, q_ref, k_ref, vt_ref, o_ref, bias_ref):
    g = pl.program_id(1)
    t = pl.program_id(2)
    cols = SWA_GROUP * BLOCK

    @pl.when((pl.program_id(0) == 0) & (t == 0))
    def _():
        bucket = bucket_ref[...]
        slot = lax.broadcasted_iota(jnp.int32, (SWA_KEYS, BLOCK), 0)
        qpos = lax.broadcasted_iota(jnp.int32, (SWA_KEYS, BLOCK), 1)
        for r in range(SWA_GROUP):
            bias = jnp.full((SWA_KEYS, BLOCK), NEG, F32)
            for bkt in range(N_BUCKETS):
                bias = jnp.where(bucket == bkt, table_ref[bkt, g * SWA_GROUP + r], bias)
            bias_ref[g, 2, :, r * BLOCK:(r + 1) * BLOCK] = bias
            for n in range(2):
                band_ok = (slot >= PAD_FRONT - (n - 1) * BLOCK) & (slot < 2 * BLOCK)
                meta_ok = ((slot >= 2 * BLOCK + PAD_FRONT)
                           & (n * BLOCK + qpos - (slot - 2 * BLOCK) >= WINDOW))
                bias_ref[g, n, :, r * BLOCK:(r + 1) * BLOCK] = jnp.where(band_ok | meta_ok, bias, NEG)

    sink = jnp.concatenate(
        [jnp.full((1, BLOCK), sink_ref[g * SWA_GROUP + r], F32) for r in range(SWA_GROUP)], axis=1)
    grp_rows = lax.broadcasted_iota(jnp.int32, (SWA_KVW, cols), 0) // HEAD_DIM
    v_rows = pl.ds(pl.multiple_of(g * HEAD_DIM, HEAD_DIM), HEAD_DIM)
    k_lead = k_ref[0:BLOCK, :]
    v_lead = vt_ref[v_rows, 0:BLOCK]

    n_blocks = q_ref.shape[1] // BLOCK

    def band(i):
        n = t * n_blocks + i
        return (n, pl.multiple_of(jnp.maximum(n - 1, 0) * BLOCK, BLOCK),
                pl.multiple_of(n * BLOCK, BLOCK))

    def scores(i):
        n, prev, cur = band(i)
        keys = jnp.concatenate(
            [k_ref[pl.ds(prev, BLOCK), :], k_ref[pl.ds(cur, BLOCK), :], k_lead], axis=0)
        q4 = jnp.concatenate(
            [q_ref[r * HEAD_DIM:(r + 1) * HEAD_DIM, i * BLOCK:(i + 1) * BLOCK]
             for r in range(SWA_GROUP)], axis=1)
        q8 = jnp.concatenate([q4, q4], axis=0)
        q8 = jnp.where(grp_rows == g, q8, jnp.zeros_like(q8))
        return _dot(keys, q8) + bias_ref[g, jnp.minimum(n, 2)]

    def finish(i, s):
        _, prev, cur = band(i)
        vals = jnp.concatenate(
            [vt_ref[v_rows, pl.ds(prev, BLOCK)], vt_ref[v_rows, pl.ds(cur, BLOCK)], v_lead], axis=1)
        mx = jnp.maximum(jnp.max(s, axis=0, keepdims=True), sink)
        p = jnp.exp(s - mx)
        denom = jnp.sum(p, axis=0, keepdims=True) + jnp.exp(sink - mx)
        out = _dot(vals, p.astype(BF16)) * (1.0 / denom)
        for r in range(SWA_GROUP):
            o_ref[r * HEAD_DIM:(r + 1) * HEAD_DIM, i * BLOCK:(i + 1) * BLOCK] = (
                out[:, r * BLOCK:(r + 1) * BLOCK].astype(BF16))

    s = scores(0)
    for i in range(n_blocks):
        s_next = scores(i + 1) if i + 1 < n_blocks else None
        finish(i, s)
        s = s_next


def _swa(table, sinks, qst, ks, vst, batch):
    rows = ks.shape[0]
    p = rows // batch
    n_t = SWA_STEPS
    swa_tile = p // n_t
    width = SWA_GROUP * HEAD_DIM
    bucket = jnp.asarray(_swa_static_buckets())
    smem = pl.BlockSpec(memory_space=pltpu.SMEM)
    tile = pl.BlockSpec((width, swa_tile), lambda b, g, t: (g, b * n_t + t))
    return pl.pallas_call(
        _swa_body,
        grid=(batch, SWA_KV_HEADS, n_t),
        in_specs=[smem, smem, _const_spec((SWA_KEYS, BLOCK)), tile,
                  pl.BlockSpec((p, SWA_KVW), lambda b, g, t: (b, 0)),
                  pl.BlockSpec((SWA_KVW, p), lambda b, g, t: (0, b))],
        out_specs=tile,
        out_shape=jax.ShapeDtypeStruct((SWA_QW, rows), BF16),
        scratch_shapes=[pltpu.VMEM((SWA_KV_HEADS, 3, SWA_KEYS, SWA_GROUP * BLOCK), F32)],
        compiler_params=_params("arbitrary", "arbitrary", "arbitrary"),
        name="swa",
    )(table, sinks, bucket, qst, ks, vst)


def _mix_body(h_ref, of_ref, os_ref, ga_ref, gb_ref, wf_ref, ws_ref, wo_ref, o_ref):
    y = (ga_ref[...].astype(F32) * _dot(wf_ref[...], of_ref[...])
         + gb_ref[...].astype(F32) * _dot(ws_ref[...], os_ref[...]))
    o_ref[...] = h_ref[...] + _dot_tn(y.astype(BF16), wo_ref[...])


def _mix_out(h2, o_fox, o_swa, ga, gb, wf_t, ws_t, wo):
    rows, d = h2.shape
    tm = ROW_TILE

    def feat(r):
        return pl.BlockSpec((r, tm), lambda i: (0, i))

    tok = pl.BlockSpec((tm, d), lambda i: (i, 0))
    return pl.pallas_call(
        _mix_body,
        grid=(rows // tm,),
        in_specs=[tok, feat(FOX_W), feat(SWA_QW), feat(d), feat(d),
                  _const_spec(wf_t.shape), _const_spec(ws_t.shape), _const_spec(wo.shape)],
        out_specs=tok,
        out_shape=jax.ShapeDtypeStruct((rows, d), F32),
        compiler_params=_params("arbitrary"),
        name="mix_out",
    )(h2, o_fox, o_swa, ga, gb, wf_t, ws_t, wo)


def _w_in_body(w_ref, wf_ref, wk_ref, *, d_model):
    src = np.cumsum([0, FOX_W, FOX_W, FOX_W, FOX_HEADS, SWA_QW, SWA_KVW, SWA_KVW, d_model])
    qa, ka, va, fa, qb, kb, vb, ga, gb = (int(s) for s in src)

    def move(dst, lo, n):
        wf_ref[dst:dst + n] = w_ref[lo:lo + n].astype(BF16)

    move(_R_QA, qa, FOX_W)
    move(_R_VA, va, FOX_W)
    move(_R_QB, qb, SWA_QW)
    move(_R_VB, vb, SWA_KVW)
    move(_R_GA, ga, d_model)
    move(_R_GA + d_model, gb, d_model)
    wf_ref[_R_FA:_R_FA + FA_ROWS] = jnp.concatenate(
        [w_ref[fa:fa + FOX_HEADS], jnp.zeros((FA_ROWS - FOX_HEADS, w_ref.shape[1]), F32)],
        axis=0).astype(BF16)
    wk_ref[:, 0:FOX_W] = w_ref[ka:ka + FOX_W].T.astype(BF16)
    wk_ref[:, FOX_W:FOX_W + SWA_KVW] = w_ref[kb:kb + SWA_KVW].T.astype(BF16)


def _prep_w_in(w_in, layer):
    w_t = jnp.swapaxes(w_in, 1, 2)
    _, d_in, d = w_t.shape
    n_feat = _R_GA + 2 * d
    return pl.pallas_call(
        functools.partial(_w_in_body, d_model=d),
        grid=(1,),
        in_specs=[pl.BlockSpec((None, d_in, d), lambda i: (layer, 0, 0), pipeline_mode=pl.Buffered(1))],
        out_specs=[pl.BlockSpec((n_feat, d), lambda i: (0, 0)),
                   pl.BlockSpec((d, FOX_W + SWA_KVW), lambda i: (0, 0))],
        out_shape=[jax.ShapeDtypeStruct((n_feat, d), BF16),
                   jax.ShapeDtypeStruct((d, FOX_W + SWA_KVW), BF16)],
        compiler_params=_params("arbitrary"),
        name="prep_w_in",
    )(w_t)


def kernel(x, meta_tokens, rel_bias_table, ffn1_norm, ffn1_w_in, ffn1_w_out, mix_norm, w_in, forget_bias, fox_q_norm, fox_k_norm, swa_q_norm, swa_k_norm, swa_sinks, w_branch_fox, w_branch_swa, w_out, ffn2_norm, ffn2_w_in, ffn2_w_out):
    b, seq, d = x.shape
    depth = w_in.shape[0]
    p = BLOCK + seq
    assert seq % FOX_TILE == 0 and p % (SWA_STEPS * BLOCK) == 0 and (b * p) % ROW_TILE == 0
    lead = jnp.concatenate([jnp.zeros((PAD_FRONT, d), x.dtype), meta_tokens.astype(x.dtype)], axis=0)
    h = x.reshape(b * seq, d)
    table = rel_bias_table.astype(F32)

    def col(v):
        return v.astype(F32)[:, None]

    def twice(v):
        return jnp.tile(v.astype(F32), 2)[None]

    for l in range(depth):
        h = _ffn(h, ffn1_norm[l][None], ffn1_w_in, ffn1_w_out, l,
                 lead=(lead, b, seq) if l == 0 else None)
        w_feat, w_keys = _prep_w_in(w_in, l)
        fb = jnp.pad(col(forget_bias[l]), ((0, FA_ROWS - FOX_HEADS), (0, 0)))
        qft, kf, vt, qst, ks, vst, ga, gb = _proj(
            h, p, mix_norm[l][None], w_feat, w_keys, fb,
            col(fox_q_norm[l]), twice(fox_k_norm[l]), col(swa_q_norm[l]), twice(swa_k_norm[l]))
        o_fox = _fox(qft, kf, vt, b)
        o_swa = _swa(table, swa_sinks[l].astype(F32), qst, ks, vst, b)
        h = _mix_out(h, o_fox, o_swa, ga, gb, w_branch_fox[l].astype(BF16).T,
                     w_branch_swa[l].astype(BF16).T, w_out[l].astype(BF16))
        last = l == depth - 1
        h = _ffn(h, ffn2_norm[l][None], ffn2_w_in, ffn2_w_out, l, (b, seq) if last else None)
    return h
```

```python
import functools
import math

import jax
import jax.numpy as jnp
import numpy as np
from jax import lax
from jax.experimental import pallas as pl
from jax.experimental.pallas import tpu as pltpu

F32 = jnp.float32
BF16 = jnp.bfloat16

N_META = 16
BLOCK = 128
PAD_FRONT = BLOCK - N_META
HEAD_DIM = 64
FOX_HEADS = 8
SWA_Q_HEADS = 8
SWA_KV_HEADS = 2
SWA_GROUP = SWA_Q_HEADS // SWA_KV_HEADS
WINDOW = 128
N_BUCKETS = 32
MAX_DISTANCE = 128
EPS = 1e-6
NEG = -1e30
FOX_W = FOX_HEADS * HEAD_DIM
SWA_QW = SWA_Q_HEADS * HEAD_DIM
SWA_KVW = SWA_KV_HEADS * HEAD_DIM
SCALE = HEAD_DIM ** -0.5
LOG2E = math.log2(math.e)

LANES = 128
VMEM_LIMIT = 56 * 1024 * 1024
ROW_TILE = 512
FFN_TILES = (1024, 768, 512)
FFN_CHUNK = 256
FOX_TILE = 512
FOX_GROUP = 4
AUG = LANES
AUG_ROWS = 32
V_ROWS = HEAD_DIM + 16
FA_ROWS = 16
SWA_KEYS = 3 * BLOCK
SWA_STEPS = 3

_R_QA = 0
_R_VA = _R_QA + FOX_W
_R_FA = _R_VA + FOX_W
_R_QB = _R_FA + FA_ROWS
_R_VB = _R_QB + SWA_QW
_R_GA = _R_VB + SWA_KVW


def _params(*sem, flags=None):
    return pltpu.CompilerParams(dimension_semantics=sem, vmem_limit_bytes=VMEM_LIMIT, flags=flags)


def _const_spec(shape):
    nd = len(shape)
    return pl.BlockSpec(shape, lambda *_: (0,) * nd, pipeline_mode=pl.Buffered(1))


def _dot(a, b):
    return jnp.dot(a, b, preferred_element_type=F32)


def _dot_nt(a, b):
    return lax.dot_general(a, b, (((1,), (1,)), ((), ())), preferred_element_type=F32)


def _dot_tn(a, b):
    return lax.dot_general(a, b, (((0,), (0,)), ((), ())), preferred_element_type=F32)


def _rms_rows(x, g):
    ms = jnp.mean(x * x, axis=-1, keepdims=True)
    return x * lax.rsqrt(ms + EPS) * g


def _lead_blocks_before(block, blocks_per_seq):
    return (block + blocks_per_seq - 1) // blocks_per_seq


def _assemble_rows(x_ref, lead_ref, tile, blocks_per_seq):
    n_sub = x_ref.shape[0] // BLOCK
    first = tile * n_sub
    at = (blocks_per_seq - first % blocks_per_seq) % blocks_per_seq
    parts = []
    for j in range(n_sub):
        cur = x_ref[j * BLOCK:(j + 1) * BLOCK, :]
        if j > 0:
            cur = jnp.where(j > at, x_ref[(j - 1) * BLOCK:j * BLOCK, :], cur)
        parts.append(jnp.where(j == at, lead_ref[...], cur))
    return jnp.concatenate(parts, axis=0)


def _ffn_body(*refs, d_ff, n_pro, blocks_per_seq):
    if blocks_per_seq is None:
        x_ref, g_ref, win_ref, wout_ref, o_ref, win_bf, wout_bf, acc_ref = refs
    else:
        x_ref, lead_ref, g_ref, win_ref, wout_ref, o_ref, win_bf, wout_bf, acc_ref = refs
    i = pl.program_id(0)
    in_chunk = win_ref.shape[1]

    @pl.when(i < n_pro)
    def _():
        win_bf[:, pl.ds(pl.multiple_of(i * in_chunk, LANES), in_chunk)] = win_ref[...].astype(BF16)
        wout_bf[pl.ds(pl.multiple_of(i * FFN_CHUNK, FFN_CHUNK), FFN_CHUNK), :] = (
            wout_ref[...].astype(BF16))

    @pl.when(i >= n_pro)
    def _():
        if blocks_per_seq is None:
            x = x_ref[...]
        else:
            x = _assemble_rows(x_ref, lead_ref, i - n_pro, blocks_per_seq)
        xn = _rms_rows(x, g_ref[...]).astype(BF16)
        for c in range(d_ff // FFN_CHUNK):
            lo = c * FFN_CHUNK
            gate = _dot(xn, win_bf[:, lo:lo + FFN_CHUNK])
            up = _dot(xn, win_bf[:, d_ff + lo:d_ff + lo + FFN_CHUNK])
            act = (gate * jax.nn.sigmoid(gate) * up).astype(BF16)
            part = _dot(act, wout_bf[lo:lo + FFN_CHUNK, :])
            if c == 0:
                acc_ref[...] = part
            else:
                acc_ref[...] += part
        o_ref[...] = x + 0.5 * acc_ref[...]


def _ffn(h2, gain, w_in, w_out, layer, real_only=None, lead=None):
    rows, d = h2.shape
    d_ff = w_out.shape[1]
    if lead is not None:
        walked = lead[1] * (lead[2] + BLOCK)
    else:
        walked = rows if real_only is None else real_only[1]
    tm = next(t for t in FFN_TILES if walked % t == 0)
    n_pro = d_ff // FFN_CHUNK
    in_chunk = 2 * d_ff // n_pro
    blocks_per_seq = None
    extra_in, extra_specs = (), ()

    if lead is not None:
        lead_rows, batch, seq = lead
        blocks_per_seq = seq // BLOCK + 1
        rows = batch * blocks_per_seq * BLOCK
        n_tiles = rows // tm

        def x_map(t):
            first = t * (tm // BLOCK)
            start = first - _lead_blocks_before(first, blocks_per_seq)
            return (pl.multiple_of(start * BLOCK, BLOCK), 0)

        def out_map(t):
            return (t, 0)

        x_block = (pl.Element(tm), pl.Element(d))
        extra_in, extra_specs = (lead_rows,), (_const_spec((BLOCK, d)),)
        out_block, out_shape = (tm, d), jax.ShapeDtypeStruct((rows, d), F32)
    elif real_only is None:
        n_tiles = rows // tm

        def x_map(t):
            return (t, 0)

        out_map = x_map
        x_block = out_block = (tm, d)
        out_shape = jax.ShapeDtypeStruct((rows, d), F32)
    else:
        batch, seq = real_only
        per_seq = seq // tm
        n_tiles = batch * per_seq

        def x_map(t):
            start = (t // per_seq) * (rows // batch) + BLOCK + (t % per_seq) * tm
            return (pl.multiple_of(start, BLOCK), 0)

        def out_map(t):
            return (t // per_seq, t % per_seq, 0)

        x_block = (pl.Element(tm), pl.Element(d))
        out_block, out_shape = (None, tm, d), jax.ShapeDtypeStruct((batch, seq, d), F32)

    def tile_of(i):
        return jnp.maximum(i - n_pro, 0)

    return pl.pallas_call(
        functools.partial(_ffn_body, d_ff=d_ff, n_pro=n_pro, blocks_per_seq=blocks_per_seq),
        grid=(n_pro + n_tiles,),
        in_specs=[
            pl.BlockSpec(x_block, lambda i: x_map(tile_of(i))),
            *extra_specs,
            _const_spec((1, d)),
            pl.BlockSpec((None, d, in_chunk), lambda i: (layer, 0, jnp.minimum(i, n_pro - 1))),
            pl.BlockSpec((None, FFN_CHUNK, d), lambda i: (layer, jnp.minimum(i, n_pro - 1), 0)),
        ],
        out_specs=pl.BlockSpec(out_block, lambda i: out_map(tile_of(i))),
        out_shape=out_shape,
        scratch_shapes=[pltpu.VMEM((d, 2 * d_ff), BF16), pltpu.VMEM((d_ff, d), BF16),
                        pltpu.VMEM((tm, d), F32)],
        compiler_params=_params("arbitrary"),
        name="ffn",
    )(h2, *extra_in, gain, w_in, w_out)


def _split3(x):
    hi = x.astype(BF16).astype(F32)
    r = x - hi
    mid = r.astype(BF16).astype(F32)
    return hi, mid, r - mid


def _head_rms_cols(x, g):
    ms = jnp.mean(x * x, axis=0, keepdims=True)
    return x * lax.rsqrt(ms + EPS) * g


def _pair_rms_rows(x, g2):
    lane = lax.broadcasted_iota(jnp.int32, x.shape, 1)
    sq = x * x
    first = lane < HEAD_DIM
    s0 = jnp.sum(jnp.where(first, sq, 0.0), axis=-1, keepdims=True)
    s1 = jnp.sum(jnp.where(first, 0.0, sq), axis=-1, keepdims=True)
    ms = jnp.where(first, s0, s1) * (1.0 / HEAD_DIM)
    return x * lax.rsqrt(ms + EPS) * g2


def _proj_body(x_ref, xnext_ref, g_ref, wt_ref, wk_ref, fb_ref, nfq_ref, nfk_ref, nsq_ref, nsk_ref,
               qf_ref, kf_ref, vt_ref, qs_ref, ks_ref, vs_ref, ga_ref, gb_ref,
               carry_ref, tri_ref, xn_ref, *, d_model, seq_pad):
    i = pl.program_id(0)
    tm = x_ref.shape[0]

    @pl.when(i == 0)
    def _():
        src = lax.broadcasted_iota(jnp.int32, (tm, tm), 0)
        dst = lax.broadcasted_iota(jnp.int32, (tm, tm), 1)
        tri_ref[...] = (src <= dst).astype(BF16)
        xn_ref[...] = _rms_rows(x_ref[...], g_ref[...]).astype(BF16)

    xn = xn_ref[...]

    def feat(lo, rows):
        return _dot_nt(wt_ref[lo:lo + rows, :], xn)

    va_fa = feat(_R_VA, FOX_W + FA_ROWS)
    lf = jax.nn.log_sigmoid(va_fa[FOX_W:] + fb_ref[...])
    va = va_fa[:FOX_W].astype(BF16)
    ones = jnp.ones((V_ROWS - HEAD_DIM, tm), BF16)
    for h in range(FOX_HEADS):
        vt_ref[h * V_ROWS:(h + 1) * V_ROWS, :] = jnp.concatenate(
            [va[h * HEAD_DIM:(h + 1) * HEAD_DIM], ones], axis=0)
    r0 = i * tm
    boundary = (r0 // seq_pad + 1) * seq_pad - r0
    hi, mid, lo3 = _split3(lf)
    upper = tri_ref[...]
    c_loc = (_dot(hi.astype(BF16), upper) + _dot(mid.astype(BF16), upper)
             + _dot(lo3.astype(BF16), upper))
    carry = jnp.where(r0 % seq_pad == 0, 0.0, carry_ref[:, 0:1])
    tok = lax.broadcasted_iota(jnp.int32, (1, tm), 1)
    c_end = jnp.sum(jnp.where(tok == boundary - 1, c_loc, 0.0), axis=1, keepdims=True)
    c_t = c_loc + jnp.where(tok < boundary, carry, -c_end)
    carry_ref[...] = jnp.broadcast_to(c_t[:, tm - 1:tm], carry_ref.shape)
    c_t = c_t * LOG2E

    c_pad = jnp.concatenate(
        [jnp.where(lax.broadcasted_iota(jnp.int32, (FA_ROWS, tm), 0) < FOX_HEADS, c_t, 0.0),
         jnp.zeros((LANES - FA_ROWS, tm), F32)], axis=0)
    c_tok = c_pad.T
    lane = lax.broadcasted_iota(jnp.int32, (tm, LANES), 1)
    k_hi, k_mid, k_lo = _split3(c_tok)
    aug_lo = (-k_hi - pltpu.roll(k_mid, FOX_HEADS, 1) - pltpu.roll(k_lo, 2 * FOX_HEADS, 1)
              + jnp.where((lane >= 3 * FOX_HEADS) & (lane < 3 * FOX_HEADS + 3), 1.0, 0.0))
    aug_hi = pltpu.roll(aug_lo, HEAD_DIM, 1)
    first = lane < HEAD_DIM

    kt = _dot(xn, wk_ref[...])
    for j in range(FOX_HEADS // 2):
        kn = _pair_rms_rows(kt[:, j * LANES:(j + 1) * LANES], nfk_ref[...])
        kf_ref[:, (2 * j) * AUG:(2 * j + 1) * AUG] = jnp.where(first, kn, aug_hi).astype(BF16)
        kf_ref[:, (2 * j + 1) * AUG:(2 * j + 2) * AUG] = jnp.where(first, aug_lo, kn).astype(BF16)
    ks_ref[...] = _pair_rms_rows(kt[:, FOX_W:FOX_W + SWA_KVW], nsk_ref[...]).astype(BF16)

    arow = lax.broadcasted_iota(jnp.int32, (AUG_ROWS, tm), 0)
    qa = feat(_R_QA, FOX_W)
    zeros = jnp.zeros((HEAD_DIM - AUG_ROWS, tm), BF16)
    for h in range(FOX_HEADS):
        q_hi, q_mid, q_lo = _split3(c_t[h:h + 1, :])
        onehot = jnp.where((arow % FOX_HEADS == h) & (arow < 3 * FOX_HEADS), 1.0, 0.0)
        aug_q = jnp.where(arow == 3 * FOX_HEADS, q_hi, jnp.where(
            arow == 3 * FOX_HEADS + 1, q_mid, jnp.where(arow == 3 * FOX_HEADS + 2, q_lo, onehot)))
        qn = (_head_rms_cols(qa[h * HEAD_DIM:(h + 1) * HEAD_DIM], nfq_ref[...])
              * (SCALE * LOG2E)).astype(BF16)
        parts = [qn, aug_q.astype(BF16), zeros] if h % 2 == 0 else [aug_q.astype(BF16), zeros, qn]
        qf_ref[h * AUG:(h + 1) * AUG, :] = jnp.concatenate(parts, axis=0)
    ga_ref[...] = jax.nn.sigmoid(feat(_R_GA, d_model)).astype(BF16)
    gb_ref[...] = jax.nn.sigmoid(feat(_R_GA + d_model, d_model)).astype(BF16)
    qb = feat(_R_QB, SWA_QW)
    for h in range(SWA_Q_HEADS):
        sl = slice(h * HEAD_DIM, (h + 1) * HEAD_DIM)
        qs_ref[sl, :] = (_head_rms_cols(qb[sl], nsq_ref[...]) * SCALE).astype(BF16)
    vs_ref[...] = feat(_R_VB, SWA_KVW).astype(BF16)
    xn_ref[...] = _rms_rows(xnext_ref[...], g_ref[...]).astype(BF16)


def _proj(h2, seq_pad, gain, w_feat, w_keys, fb, nfq, nfk, nsq, nsk):
    rows, d = h2.shape
    tm = ROW_TILE

    def tok(w):
        return pl.BlockSpec((tm, w), lambda i: (i, 0)), jax.ShapeDtypeStruct((rows, w), BF16)

    def feat(r):
        return pl.BlockSpec((r, tm), lambda i: (0, i)), jax.ShapeDtypeStruct((r, rows), BF16)

    outs = [feat(FOX_HEADS * AUG), tok(FOX_HEADS * AUG), feat(FOX_HEADS * V_ROWS), feat(SWA_QW),
            tok(SWA_KVW), feat(SWA_KVW), feat(d), feat(d)]
    return pl.pallas_call(
        functools.partial(_proj_body, d_model=d, seq_pad=seq_pad),
        grid=(rows // tm,),
        in_specs=[
            _const_spec((tm, d)),
            pl.BlockSpec((tm, d), lambda i: (jnp.minimum(i + 1, rows // tm - 1), 0)),
            _const_spec((1, d)),
            _const_spec(w_feat.shape),
            _const_spec(w_keys.shape),
            _const_spec((FA_ROWS, 1)),
            _const_spec((HEAD_DIM, 1)), _const_spec((1, LANES)),
            _const_spec((HEAD_DIM, 1)), _const_spec((1, LANES)),
        ],
        out_specs=[o[0] for o in outs],
        out_shape=[o[1] for o in outs],
        scratch_shapes=[pltpu.VMEM((FA_ROWS, LANES), F32), pltpu.VMEM((tm, tm), BF16),
                        pltpu.VMEM((tm, d), BF16)],
        compiler_params=_params("arbitrary"),
        name="proj",
    )(h2, h2, gain, w_feat, w_keys, fb, nfq, nfk, nsq, nsk)


def _fox_body(q_ref, k_ref, vt_ref, o_ref, m_ref, acc_ref, st_ref):
    def attend(q_start, tq, n_full, diag):
        q = q_ref[:, pl.ds(q_start, tq)]
        m_ref[:, :, :tq] = jnp.full((FOX_GROUP, 1, tq), NEG, F32)
        acc_ref[:, :tq] = jnp.zeros((FOX_GROUP * V_ROWS, tq), F32)

        everyone = slice(0, tq)

        def scores(kblk, h, cols=everyone):
            return _dot(kblk[:, h * AUG:(h + 1) * AUG], q[h * AUG:(h + 1) * AUG, cols])

        def softmax(h, st, mask, cols):
            if mask is not None:
                st = jnp.where(mask, st, NEG)
            m_prev = m_ref[h, :, cols]
            m_new = jnp.maximum(m_prev, jnp.max(st, axis=0, keepdims=True))
            alpha = jnp.exp2(m_prev - m_new)
            m_ref[h, :, cols] = m_new
            return jnp.exp2(st - m_new).astype(BF16), alpha

        def weighted_values(h, vt, p_alpha, cols):
            p, alpha = p_alpha
            rows = slice(h * V_ROWS, (h + 1) * V_ROWS)
            acc_ref[rows, cols] = alpha * acc_ref[rows, cols] + _dot(vt[rows, :], p)

        def run_heads(kblk, vt, st0, mask, cols=everyone, next_scores=None):
            st = [st0, scores(kblk, 1, cols), None, None]
            pa = softmax(0, st[0], mask, cols)
            for h in range(FOX_GROUP):
                if h + 2 < FOX_GROUP:
                    st[h + 2] = scores(kblk, h + 2, cols)
                elif h + 2 == FOX_GROUP and next_scores is not None:
                    st_ref[...] = next_scores()
                weighted_values(h, vt, pa, cols)
                if h + 1 < FOX_GROUP:
                    pa = softmax(h + 1, st[h + 1], mask, cols)

        def key_tile(j):
            return k_ref[pl.ds(pl.multiple_of(BLOCK + j * FOX_TILE, BLOCK), FOX_TILE), :]

        def full_tile(j):
            vt = vt_ref[:, pl.ds(pl.multiple_of(BLOCK + j * FOX_TILE, BLOCK), FOX_TILE)]
            run_heads(key_tile(j), vt, st_ref[...], None,
                      next_scores=lambda: scores(key_tile(j + 1), 0))

        k_lead = k_ref[0:BLOCK, :]
        v_lead = vt_ref[:, 0:BLOCK]
        if diag:
            st_ref[...] = scores(key_tile(0), 0)

            def tile_pair(jj, carry):
                full_tile(2 * jj)
                full_tile(2 * jj + 1)
                return carry

            lax.fori_loop(0, n_full // 2, tile_pair, 0)

            @pl.when(n_full % 2 == 1)
            def _():
                full_tile(n_full - 1)

            kr = lax.broadcasted_iota(jnp.int32, (BLOCK + tq, tq), 0)
            qc = lax.broadcasted_iota(jnp.int32, (BLOCK + tq, tq), 1)
            visible = ((kr >= PAD_FRONT) & (kr < BLOCK)) | ((kr >= BLOCK) & (kr - BLOCK <= qc))
            run_heads(jnp.concatenate([k_lead, k_ref[pl.ds(q_start, tq), :]], axis=0),
                      jnp.concatenate([v_lead, vt_ref[:, pl.ds(q_start, tq)]], axis=1),
                      jnp.concatenate([scores(k_lead, 0), st_ref[...]], axis=0), visible)
        else:
            kpos0 = lax.broadcasted_iota(jnp.int32, (BLOCK, tq), 0)
            qc = lax.broadcasted_iota(jnp.int32, (BLOCK, tq), 1)
            visible = (kpos0 >= PAD_FRONT) & (kpos0 <= qc)
            for h in range(FOX_GROUP):
                weighted_values(h, v_lead, softmax(h, scores(k_lead, h), visible, everyone), everyone)
        for h in range(FOX_GROUP):
            denom = acc_ref[h * V_ROWS + HEAD_DIM:h * V_ROWS + HEAD_DIM + 1, :tq]
            o_ref[h * HEAD_DIM:(h + 1) * HEAD_DIM, pl.ds(q_start, tq)] = (
                acc_ref[h * V_ROWS:h * V_ROWS + HEAD_DIM, :tq] * (1.0 / denom)).astype(BF16)

    attend(0, BLOCK, 0, False)

    def query_tile(i, carry):
        attend(pl.multiple_of(BLOCK + i * FOX_TILE, BLOCK), FOX_TILE, i, True)
        return carry

    lax.fori_loop(0, (q_ref.shape[1] - BLOCK) // FOX_TILE, query_tile, 0)


def _fox(qft, kf, vt, batch):
    rows = kf.shape[0]
    p = rows // batch
    n_grp = FOX_HEADS // FOX_GROUP
    width = FOX_GROUP * HEAD_DIM

    def feat(r):
        return pl.BlockSpec((r, p), lambda b, g: (g, b))

    return pl.pallas_call(
        _fox_body,
        grid=(batch, n_grp),
        in_specs=[feat(FOX_GROUP * AUG),
                  pl.BlockSpec((p, FOX_GROUP * AUG), lambda b, g: (b, g)),
                  feat(FOX_GROUP * V_ROWS)],
        out_specs=feat(width),
        out_shape=jax.ShapeDtypeStruct((FOX_W, rows), BF16),
        scratch_shapes=[
            pltpu.VMEM((FOX_GROUP, 1, FOX_TILE), F32),
            pltpu.VMEM((FOX_GROUP * V_ROWS, FOX_TILE), F32),
            pltpu.VMEM((FOX_TILE, FOX_TILE), F32),
        ],
        compiler_params=_params("arbitrary", "arbitrary"),
        name="fox",
    )(qft, kf, vt)


def _t5_bucket_np(dist):
    n = np.maximum(dist, 0)
    max_exact = N_BUCKETS // 2
    nf = np.maximum(n, 1).astype(np.float64)
    large = max_exact + (np.log(nf / max_exact) / math.log(MAX_DISTANCE / max_exact)
                         * (N_BUCKETS - max_exact)).astype(np.int64)
    large = np.minimum(large, N_BUCKETS - 1)
    return np.where(n < max_exact, n, large).astype(np.int32)


def _swa_static_buckets():
    k = np.arange(SWA_KEYS)[:, None]
    q = np.arange(BLOCK)[None, :]
    d = BLOCK + q - k
    bucket = np.where((d >= 0) & (d < WINDOW) & (k < 2 * BLOCK), _t5_bucket_np(d), -1)
    bucket = np.where(k >= 2 * BLOCK + PAD_FRONT, N_BUCKETS - 1, bucket)
    return np.ascontiguousarray(np.broadcast_to(bucket, (SWA_KEYS, BLOCK))).astype(np.int32)


def _swa_body(table_ref, sink_ref, bucket_ref, q_ref, k_ref, vt_ref, o_ref, bias_ref):
    g = pl.program_id(1)
    t = pl.program_id(2)
    cols = SWA_GROUP * BLOCK

    @pl.when((pl.program_id(0) == 0) & (t == 0))
    def _():
        bucket = bucket_ref[...]
        slot = lax.broadcasted_iota(jnp.int32, (SWA_KEYS, BLOCK), 0)
        qpos = lax.broadcasted_iota(jnp.int32, (SWA_KEYS, BLOCK), 1)
        for r in range(SWA_GROUP):
            bias = jnp.full((SWA_KEYS, BLOCK), NEG, F32)
            for bkt in range(N_BUCKETS):
                bias = jnp.where(bucket == bkt, table_ref[bkt, g * SWA_GROUP + r], bias)
            bias_ref[g, 2, :, r * BLOCK:(r + 1) * BLOCK] = bias
            for n in range(2):
                band_ok = (slot >= PAD_FRONT - (n - 1) * BLOCK) & (slot < 2 * BLOCK)
                meta_ok = ((slot >= 2 * BLOCK + PAD_FRONT)
                           & (n * BLOCK + qpos - (slot - 2 * BLOCK) >= WINDOW))
                bias_ref[g, n, :, r * BLOCK:(r + 1) * BLOCK] = jnp.where(band_ok | meta_ok, bias, NEG)

    sink = jnp.concatenate(
        [jnp.full((1, BLOCK), sink_ref[g * SWA_GROUP + r], F32) for r in range(SWA_GROUP)], axis=1)
    grp_rows = lax.broadcasted_iota(jnp.int32, (SWA_KVW, cols), 0) // HEAD_DIM
    v_rows = pl.ds(pl.multiple_of(g * HEAD_DIM, HEAD_DIM), HEAD_DIM)
    k_lead = k_ref[0:BLOCK, :]
    v_lead = vt_ref[v_rows, 0:BLOCK]

    n_blocks = q_ref.shape[1] // BLOCK

    def band(i):
        n = t * n_blocks + i
        return (n, pl.multiple_of(jnp.maximum(n - 1, 0) * BLOCK, BLOCK),
                pl.multiple_of(n * BLOCK, BLOCK))

    def scores(i):
        n, prev, cur = band(i)
        keys = jnp.concatenate(
            [k_ref[pl.ds(prev, BLOCK), :], k_ref[pl.ds(cur, BLOCK), :], k_lead], axis=0)
        q4 = jnp.concatenate(
            [q_ref[r * HEAD_DIM:(r + 1) * HEAD_DIM, i * BLOCK:(i + 1) * BLOCK]
             for r in range(SWA_GROUP)], axis=1)
        q8 = jnp.concatenate([q4, q4], axis=0)
        q8 = jnp.where(grp_rows == g, q8, jnp.zeros_like(q8))
        return _dot(keys, q8) + bias_ref[g, jnp.minimum(n, 2)]

    def finish(i, s):
        _, prev, cur = band(i)
        vals = jnp.concatenate(
            [vt_ref[v_rows, pl.ds(prev, BLOCK)], vt_ref[v_rows, pl.ds(cur, BLOCK)], v_lead], axis=1)
        mx = jnp.maximum(jnp.max(s, axis=0, keepdims=True), sink)
        p = jnp.exp(s - mx)
        denom = jnp.sum(p, axis=0, keepdims=True) + jnp.exp(sink - mx)
        out = _dot(vals, p.astype(BF16)) * (1.0 / denom)
        for r in range(SWA_GROUP):
            o_ref[r * HEAD_DIM:(r + 1) * HEAD_DIM, i * BLOCK:(i + 1) * BLOCK] = (
                out[:, r * BLOCK:(r + 1) * BLOCK].astype(BF16))

    s = scores(0)
    for i in range(n_blocks):
        s_next = scores(i + 1) if i + 1 < n_blocks else None
        finish(i, s)
        s = s_next


def _swa(table, sinks, qst, ks, vst, batch):
    rows = ks.shape[0]
    p = rows // batch
    n_t = SWA_STEPS
    swa_tile = p // n_t
    width = SWA_GROUP * HEAD_DIM
    bucket = jnp.asarray(_swa_static_buckets())
    smem = pl.BlockSpec(memory_space=pltpu.SMEM)
    tile = pl.BlockSpec((width, swa_tile), lambda b, g, t: (g, b * n_t + t))
    return pl.pallas_call(
        _swa_body,
        grid=(batch, SWA_KV_HEADS, n_t),
        in_specs=[smem, smem, _const_spec((SWA_KEYS, BLOCK)), tile,
                  pl.BlockSpec((p, SWA_KVW), lambda b, g, t: (b, 0)),
                  pl.BlockSpec((SWA_KVW, p), lambda b, g, t: (0, b))],
        out_specs=tile,
        out_shape=jax.ShapeDtypeStruct((SWA_QW, rows), BF16),
        scratch_shapes=[pltpu.VMEM((SWA_KV_HEADS, 3, SWA_KEYS, SWA_GROUP * BLOCK), F32)],
        compiler_params=_params("arbitrary", "arbitrary", "arbitrary"),
        name="swa",
    )(table, sinks, bucket, qst, ks, vst)


def _mix_body(h_ref, of_ref, os_ref, ga_ref, gb_ref, wf_ref, ws_ref, wo_ref, o_ref):
    y = (ga_ref[...].astype(F32) * _dot(wf_ref[...], of_ref[...])
         + gb_ref[...].astype(F32) * _dot(ws_ref[...], os_ref[...]))
    o_ref[...] = h_ref[...] + _dot_tn(y.astype(BF16), wo_ref[...])


def _mix_out(h2, o_fox, o_swa, ga, gb, wf_t, ws_t, wo):
    rows, d = h2.shape
    tm = ROW_TILE

    def feat(r):
        return pl.BlockSpec((r, tm), lambda i: (0, i))

    tok = pl.BlockSpec((tm, d), lambda i: (i, 0))
    return pl.pallas_call(
        _mix_body,
        grid=(rows // tm,),
        in_specs=[tok, feat(FOX_W), feat(SWA_QW), feat(d), feat(d),
                  _const_spec(wf_t.shape), _const_spec(ws_t.shape), _const_spec(wo.shape)],
        out_specs=tok,
        out_shape=jax.ShapeDtypeStruct((rows, d), F32),
        compiler_params=_params("arbitrary"),
        name="mix_out",
    )(h2, o_fox, o_swa, ga, gb, wf_t, ws_t, wo)


def _w_in_body(w_ref, wf_ref, wk_ref, *, d_model):
    src = np.cumsum([0, FOX_W, FOX_W, FOX_W, FOX_HEADS, SWA_QW, SWA_KVW, SWA_KVW, d_model])
    qa, ka, va, fa, qb, kb, vb, ga, gb = (int(s) for s in src)

    def move(dst, lo, n):
        wf_ref[dst:dst + n] = w_ref[lo:lo + n].astype(BF16)

    move(_R_QA, qa, FOX_W)
    move(_R_VA, va, FOX_W)
    move(_R_QB, qb, SWA_QW)
    move(_R_VB, vb, SWA_KVW)
    move(_R_GA, ga, d_model)
    move(_R_GA + d_model, gb, d_model)
    wf_ref[_R_FA:_R_FA + FA_ROWS] = jnp.concatenate(
        [w_ref[fa:fa + FOX_HEADS], jnp.zeros((FA_ROWS - FOX_HEADS, w_ref.shape[1]), F32)],
        axis=0).astype(BF16)
    wk_ref[:, 0:FOX_W] = w_ref[ka:ka + FOX_W].T.astype(BF16)
    wk_ref[:, FOX_W:FOX_W + SWA_KVW] = w_ref[kb:kb + SWA_KVW].T.astype(BF16)


def _prep_w_in(w_in, layer):
    w_t = jnp.swapaxes(w_in, 1, 2)
    _, d_in, d = w_t.shape
    n_feat = _R_GA + 2 * d
    return pl.pallas_call(
        functools.partial(_w_in_body, d_model=d),
        grid=(1,),
        in_specs=[pl.BlockSpec((None, d_in, d), lambda i: (layer, 0, 0), pipeline_mode=pl.Buffered(1))],
        out_specs=[pl.BlockSpec((n_feat, d), lambda i: (0, 0)),
                   pl.BlockSpec((d, FOX_W + SWA_KVW), lambda i: (0, 0))],
        out_shape=[jax.ShapeDtypeStruct((n_feat, d), BF16),
                   jax.ShapeDtypeStruct((d, FOX_W + SWA_KVW), BF16)],
        compiler_params=_params("arbitrary"),
        name="prep_w_in",
    )(w_t)


def kernel(x, meta_tokens, rel_bias_table, ffn1_norm, ffn1_w_in, ffn1_w_out, mix_norm, w_in, forget_bias, fox_q_norm, fox_k_norm, swa_q_norm, swa_k_norm, swa_sinks, w_branch_fox, w_branch_swa, w_out, ffn2_norm, ffn2_w_in, ffn2_w_out):
    b, seq, d = x.shape
    depth = w_in.shape[0]
    p = BLOCK + seq
    assert seq % FOX_TILE == 0 and p % (SWA_STEPS * BLOCK) == 0 and (b * p) % ROW_TILE == 0
    lead = jnp.concatenate([jnp.zeros((PAD_FRONT, d), x.dtype), meta_tokens.astype(x.dtype)], axis=0)
    h = x.reshape(b * seq, d)
    table = rel_bias_table.astype(F32)

    def col(v):
        return v.astype(F32)[:, None]

    def twice(v):
        return jnp.tile(v.astype(F32), 2)[None]

    for l in range(depth):
        h = _ffn(h, ffn1_norm[l][None], ffn1_w_in, ffn1_w_out, l,
                 lead=(lead, b, seq) if l == 0 else None)
        w_feat, w_keys = _prep_w_in(w_in, l)
        fb = jnp.pad(col(forget_bias[l]), ((0, FA_ROWS - FOX_HEADS), (0, 0)))
        qft, kf, vt, qst, ks, vst, ga, gb = _proj(
            h, p, mix_norm[l][None], w_feat, w_keys, fb,
            col(fox_q_norm[l]), twice(fox_k_norm[l]), col(swa_q_norm[l]), twice(swa_k_norm[l]))
        o_fox = _fox(qft, kf, vt, b)
        o_swa = _swa(table, swa_sinks[l].astype(F32), qst, ks, vst, b)
        h = _mix_out(h, o_fox, o_swa, ga, gb, w_branch_fox[l].astype(BF16).T,
                     w_branch_swa[l].astype(BF16).T, w_out[l].astype(BF16))
        last = l == depth - 1
        h = _ffn(h, ffn2_norm[l][None], ffn2_w_in, ffn2_w_out, l, (b, seq) if last else None)
    return h
```

```python
import functools
import math

import jax
import jax.numpy as jnp
import numpy as np
from jax import lax
from jax.experimental import pallas as pl
from jax.experimental.pallas import tpu as pltpu

F32 = jnp.float32
BF16 = jnp.bfloat16

N_META = 16
BLOCK = 128
PAD_FRONT = BLOCK - N_META
HEAD_DIM = 64
FOX_HEADS = 8
SWA_Q_HEADS = 8
SWA_KV_HEADS = 2
SWA_GROUP = SWA_Q_HEADS // SWA_KV_HEADS
WINDOW = 128
N_BUCKETS = 32
MAX_DISTANCE = 128
EPS = 1e-6
NEG = -1e30
FOX_W = FOX_HEADS * HEAD_DIM
SWA_QW = SWA_Q_HEADS * HEAD_DIM
SWA_KVW = SWA_KV_HEADS * HEAD_DIM
SCALE = HEAD_DIM ** -0.5
LOG2E = math.log2(math.e)

LANES = 128
VMEM_LIMIT = 56 * 1024 * 1024
ROW_TILE = 512
FFN_TILES = (1024, 768, 512)
FFN_CHUNK = 256
FOX_TILE = 512
FOX_GROUP = 4
AUG = LANES
AUG_ROWS = 32
V_ROWS = HEAD_DIM + 16
FA_ROWS = 16
SWA_KEYS = 3 * BLOCK
SWA_STEPS = 3

_R_QA = 0
_R_VA = _R_QA + FOX_W
_R_FA = _R_VA + FOX_W
_R_QB = _R_FA + FA_ROWS
_R_VB = _R_QB + SWA_QW
_R_GA = _R_VB + SWA_KVW


def _params(*sem, flags=None):
    return pltpu.CompilerParams(dimension_semantics=sem, vmem_limit_bytes=VMEM_LIMIT, flags=flags)


def _const_spec(shape):
    nd = len(shape)
    return pl.BlockSpec(shape, lambda *_: (0,) * nd, pipeline_mode=pl.Buffered(1))


def _dot(a, b):
    return jnp.dot(a, b, preferred_element_type=F32)


def _dot_nt(a, b):
    return lax.dot_general(a, b, (((1,), (1,)), ((), ())), preferred_element_type=F32)


def _dot_tn(a, b):
    return lax.dot_general(a, b, (((0,), (0,)), ((), ())), preferred_element_type=F32)


def _rms_rows(x, g):
    ms = jnp.mean(x * x, axis=-1, keepdims=True)
    return x * lax.rsqrt(ms + EPS) * g


def _lead_blocks_before(block, blocks_per_seq):
    return (block + blocks_per_seq - 1) // blocks_per_seq


def _assemble_rows(x_ref, lead_ref, tile, blocks_per_seq):
    n_sub = x_ref.shape[0] // BLOCK
    first = tile * n_sub
    at = (blocks_per_seq - first % blocks_per_seq) % blocks_per_seq
    parts = []
    for j in range(n_sub):
        cur = x_ref[j * BLOCK:(j + 1) * BLOCK, :]
        if j > 0:
            cur = jnp.where(j > at, x_ref[(j - 1) * BLOCK:j * BLOCK, :], cur)
        parts.append(jnp.where(j == at, lead_ref[...], cur))
    return jnp.concatenate(parts, axis=0)


def _ffn_body(*refs, d_ff, n_pro, blocks_per_seq):
    if blocks_per_seq is None:
        (x_ref, g_ref, wg_ref, wu_ref, wout_ref, o_ref,
         win_bf, wout_bf, acc_ref, xn_ref) = refs
        lead_ref = None
    else:
        (x_ref, lead_ref, g_ref, wg_ref, wu_ref, wout_ref, o_ref,
         win_bf, wout_bf, acc_ref, xn_ref) = refs
    i = pl.program_id(0)

    def rows_of(tile):
        if blocks_per_seq is None:
            return x_ref[...]
        return _assemble_rows(x_ref, lead_ref, tile, blocks_per_seq)

    def chunk(xn, w_gate, w_up, w_down):
        gate = _dot(xn, w_gate)
        up = _dot(xn, w_up)
        return _dot((gate * jax.nn.sigmoid(gate) * up).astype(BF16), w_down)

    @pl.when(i == 0)
    def _():
        xn_ref[...] = _rms_rows(rows_of(0), g_ref[...]).astype(BF16)
        acc_ref[...] = jnp.zeros_like(acc_ref)

    @pl.when(i < n_pro)
    def _():
        lo = pl.multiple_of(i * FFN_CHUNK, FFN_CHUNK)
        w_gate, w_up, w_down = (r[...].astype(BF16) for r in (wg_ref, wu_ref, wout_ref))
        win_bf[:, pl.ds(lo, FFN_CHUNK)] = w_gate
        win_bf[:, pl.ds(pl.multiple_of(d_ff + lo, LANES), FFN_CHUNK)] = w_up
        wout_bf[pl.ds(lo, FFN_CHUNK), :] = w_down
        acc_ref[...] += chunk(xn_ref[...], w_gate, w_up, w_down)

    @pl.when(i == n_pro - 1)
    def _():
        o_ref[...] = rows_of(0) + 0.5 * acc_ref[...]

    @pl.when(i >= n_pro)
    def _():
        x = rows_of(i - n_pro + 1)
        xn = _rms_rows(x, g_ref[...]).astype(BF16)
        for c in range(d_ff // FFN_CHUNK):
            lo = c * FFN_CHUNK
            part = chunk(xn, win_bf[:, lo:lo + FFN_CHUNK],
                         win_bf[:, d_ff + lo:d_ff + lo + FFN_CHUNK], wout_bf[lo:lo + FFN_CHUNK, :])
            if c == 0:
                acc_ref[...] = part
            else:
                acc_ref[...] += part
        o_ref[...] = x + 0.5 * acc_ref[...]


def _ffn(h2, gain, w_in, w_out, layer, real_only=None, lead=None):
    rows, d = h2.shape
    d_ff = w_out.shape[1]
    if lead is not None:
        walked = lead[1] * (lead[2] + BLOCK)
    else:
        walked = rows if real_only is None else real_only[1]
    tm = next(t for t in FFN_TILES if walked % t == 0)
    n_pro = d_ff // FFN_CHUNK
    blocks_per_seq = None
    extra_in, extra_specs = (), ()

    if lead is not None:
        lead_rows, batch, seq = lead
        blocks_per_seq = seq // BLOCK + 1
        rows = batch * blocks_per_seq * BLOCK
        n_tiles = rows // tm

        def x_map(t):
            first = t * (tm // BLOCK)
            start = first - _lead_blocks_before(first, blocks_per_seq)
            return (pl.multiple_of(start * BLOCK, BLOCK), 0)

        def out_map(t):
            return (t, 0)

        x_block = (pl.Element(tm), pl.Element(d))
        extra_in, extra_specs = (lead_rows,), (_const_spec((BLOCK, d)),)
        out_block, out_shape = (tm, d), jax.ShapeDtypeStruct((rows, d), F32)
    elif real_only is None:
        n_tiles = rows // tm

        def x_map(t):
            return (t, 0)

        out_map = x_map
        x_block = out_block = (tm, d)
        out_shape = jax.ShapeDtypeStruct((rows, d), F32)
    else:
        batch, seq = real_only
        per_seq = seq // tm
        n_tiles = batch * per_seq

        def x_map(t):
            start = (t // per_seq) * (rows // batch) + BLOCK + (t % per_seq) * tm
            return (pl.multiple_of(start, BLOCK), 0)

        def out_map(t):
            return (t // per_seq, t % per_seq, 0)

        x_block = (pl.Element(tm), pl.Element(d))
        out_block, out_shape = (None, tm, d), jax.ShapeDtypeStruct((batch, seq, d), F32)

    def tile_of(i):
        return jnp.maximum(i - n_pro + 1, 0)

    def chunk_of(i):
        return jnp.minimum(i, n_pro - 1)

    return pl.pallas_call(
        functools.partial(_ffn_body, d_ff=d_ff, n_pro=n_pro, blocks_per_seq=blocks_per_seq),
        grid=(n_pro + n_tiles - 1,),
        in_specs=[
            pl.BlockSpec(x_block, lambda i: x_map(tile_of(i))),
            *extra_specs,
            _const_spec((1, d)),
            pl.BlockSpec((None, d, FFN_CHUNK), lambda i: (layer, 0, chunk_of(i))),
            pl.BlockSpec((None, d, FFN_CHUNK), lambda i: (layer, 0, n_pro + chunk_of(i))),
            pl.BlockSpec((None, FFN_CHUNK, d), lambda i: (layer, chunk_of(i), 0)),
        ],
        out_specs=pl.BlockSpec(out_block, lambda i: out_map(tile_of(i))),
        out_shape=out_shape,
        scratch_shapes=[pltpu.VMEM((d, 2 * d_ff), BF16), pltpu.VMEM((d_ff, d), BF16),
                        pltpu.VMEM((tm, d), F32), pltpu.VMEM((tm, d), BF16)],
        compiler_params=_params("arbitrary"),
        name="ffn",
    )(h2, *extra_in, gain, w_in, w_in, w_out)


def _split3(x):
    hi = x.astype(BF16).astype(F32)
    r = x - hi
    mid = r.astype(BF16).astype(F32)
    return hi, mid, r - mid


def _head_rms_cols(x, g):
    ms = jnp.mean(x * x, axis=0, keepdims=True)
    return x * lax.rsqrt(ms + EPS) * g


def _pair_rms_rows(x, g2):
    lane = lax.broadcasted_iota(jnp.int32, x.shape, 1)
    sq = x * x
    first = lane < HEAD_DIM
    s0 = jnp.sum(jnp.where(first, sq, 0.0), axis=-1, keepdims=True)
    s1 = jnp.sum(jnp.where(first, 0.0, sq), axis=-1, keepdims=True)
    ms = jnp.where(first, s0, s1) * (1.0 / HEAD_DIM)
    return x * lax.rsqrt(ms + EPS) * g2


def _proj_body(x_ref, xnext_ref, g_ref, wt_ref, wk_ref, fb_ref, nfq_ref, nfk_ref, nsq_ref, nsk_ref,
               qf_ref, kf_ref, vt_ref, qs_ref, ks_ref, vs_ref, ga_ref, gb_ref,
               carry_ref, tri_ref, xn_ref, *, d_model, seq_pad):
    i = pl.program_id(0)
    tm = x_ref.shape[0]

    @pl.when(i == 0)
    def _():
        src = lax.broadcasted_iota(jnp.int32, (tm, tm), 0)
        dst = lax.broadcasted_iota(jnp.int32, (tm, tm), 1)
        tri_ref[...] = (src <= dst).astype(BF16)
        xn_ref[...] = _rms_rows(x_ref[...], g_ref[...]).astype(BF16)

    xn = xn_ref[...]

    def feat(lo, rows):
        return _dot_nt(wt_ref[lo:lo + rows, :], xn)

    va_fa = feat(_R_VA, FOX_W + FA_ROWS)
    lf = jax.nn.log_sigmoid(va_fa[FOX_W:] + fb_ref[...])
    va = va_fa[:FOX_W].astype(BF16)
    ones = jnp.ones((V_ROWS - HEAD_DIM, tm), BF16)
    for h in range(FOX_HEADS):
        vt_ref[h * V_ROWS:(h + 1) * V_ROWS, :] = jnp.concatenate(
            [va[h * HEAD_DIM:(h + 1) * HEAD_DIM], ones], axis=0)
    r0 = i * tm
    boundary = (r0 // seq_pad + 1) * seq_pad - r0
    hi, mid, lo3 = _split3(lf)
    upper = tri_ref[...]
    c_loc = (_dot(hi.astype(BF16), upper) + _dot(mid.astype(BF16), upper)
             + _dot(lo3.astype(BF16), upper))
    carry = jnp.where(r0 % seq_pad == 0, 0.0, carry_ref[:, 0:1])
    tok = lax.broadcasted_iota(jnp.int32, (1, tm), 1)
    c_end = jnp.sum(jnp.where(tok == boundary - 1, c_loc, 0.0), axis=1, keepdims=True)
    c_t = c_loc + jnp.where(tok < boundary, carry, -c_end)
    carry_ref[...] = jnp.broadcast_to(c_t[:, tm - 1:tm], carry_ref.shape)
    c_t = c_t * LOG2E

    c_pad = jnp.concatenate(
        [jnp.where(lax.broadcasted_iota(jnp.int32, (FA_ROWS, tm), 0) < FOX_HEADS, c_t, 0.0),
         jnp.zeros((LANES - FA_ROWS, tm), F32)], axis=0)
    c_tok = c_pad.T
    lane = lax.broadcasted_iota(jnp.int32, (tm, LANES), 1)
    k_hi, k_mid, k_lo = _split3(c_tok)
    aug_lo = (-k_hi - pltpu.roll(k_mid, FOX_HEADS, 1) - pltpu.roll(k_lo, 2 * FOX_HEADS, 1)
              + jnp.where((lane >= 3 * FOX_HEADS) & (lane < 3 * FOX_HEADS + 3), 1.0, 0.0))
    aug_hi = pltpu.roll(aug_lo, HEAD_DIM, 1)
    first = lane < HEAD_DIM

    kt = _dot(xn, wk_ref[...])
    for j in range(FOX_HEADS // 2):
        kn = _pair_rms_rows(kt[:, j * LANES:(j + 1) * LANES], nfk_ref[...])
        kf_ref[:, (2 * j) * AUG:(2 * j + 1) * AUG] = jnp.where(first, kn, aug_hi).astype(BF16)
        kf_ref[:, (2 * j + 1) * AUG:(2 * j + 2) * AUG] = jnp.where(first, aug_lo, kn).astype(BF16)
    ks_ref[...] = _pair_rms_rows(kt[:, FOX_W:FOX_W + SWA_KVW], nsk_ref[...]).astype(BF16)

    arow = lax.broadcasted_iota(jnp.int32, (AUG_ROWS, tm), 0)
    qa = feat(_R_QA, FOX_W)
    zeros = jnp.zeros((HEAD_DIM - AUG_ROWS, tm), BF16)
    for h in range(FOX_HEADS):
        q_hi, q_mid, q_lo = _split3(c_t[h:h + 1, :])
        onehot = jnp.where((arow % FOX_HEADS == h) & (arow < 3 * FOX_HEADS), 1.0, 0.0)
        aug_q = jnp.where(arow == 3 * FOX_HEADS, q_hi, jnp.where(
            arow == 3 * FOX_HEADS + 1, q_mid, jnp.where(arow == 3 * FOX_HEADS + 2, q_lo, onehot)))
        qn = (_head_rms_cols(qa[h * HEAD_DIM:(h + 1) * HEAD_DIM], nfq_ref[...])
              * (SCALE * LOG2E)).astype(BF16)
        parts = [qn, aug_q.astype(BF16), zeros] if h % 2 == 0 else [aug_q.astype(BF16), zeros, qn]
        qf_ref[h * AUG:(h + 1) * AUG, :] = jnp.concatenate(parts, axis=0)
    ga_ref[...] = jax.nn.sigmoid(feat(_R_GA, d_model)).astype(BF16)
    gb_ref[...] = jax.nn.sigmoid(feat(_R_GA + d_model, d_model)).astype(BF16)
    qb = feat(_R_QB, SWA_QW)
    for h in range(SWA_Q_HEADS):
        sl = slice(h * HEAD_DIM, (h + 1) * HEAD_DIM)
        qs_ref[sl, :] = (_head_rms_cols(qb[sl], nsq_ref[...]) * SCALE).astype(BF16)
    vs_ref[...] = feat(_R_VB, SWA_KVW).astype(BF16)
    xn_ref[...] = _rms_rows(xnext_ref[...], g_ref[...]).astype(BF16)


def _proj(h2, seq_pad, gain, w_feat, w_keys, fb, nfq, nfk, nsq, nsk):
    rows, d = h2.shape
    tm = ROW_TILE

    def tok(w):
        return pl.BlockSpec((tm, w), lambda i: (i, 0)), jax.ShapeDtypeStruct((rows, w), BF16)

    def feat(r):
        return pl.BlockSpec((r, tm), lambda i: (0, i)), jax.ShapeDtypeStruct((r, rows), BF16)

    outs = [feat(FOX_HEADS * AUG), tok(FOX_HEADS * AUG), feat(FOX_HEADS * V_ROWS), feat(SWA_QW),
            tok(SWA_KVW), feat(SWA_KVW), feat(d), feat(d)]
    return pl.pallas_call(
        functools.partial(_proj_body, d_model=d, seq_pad=seq_pad),
        grid=(rows // tm,),
        in_specs=[
            _const_spec((tm, d)),
            pl.BlockSpec((tm, d), lambda i: (jnp.minimum(i + 1, rows // tm - 1), 0)),
            _const_spec((1, d)),
            _const_spec(w_feat.shape),
            _const_spec(w_keys.shape),
            _const_spec((FA_ROWS, 1)),
            _const_spec((HEAD_DIM, 1)), _const_spec((1, LANES)),
            _const_spec((HEAD_DIM, 1)), _const_spec((1, LANES)),
        ],
        out_specs=[o[0] for o in outs],
        out_shape=[o[1] for o in outs],
        scratch_shapes=[pltpu.VMEM((FA_ROWS, LANES), F32), pltpu.VMEM((tm, tm), BF16),
                        pltpu.VMEM((tm, d), BF16)],
        compiler_params=_params("arbitrary"),
        name="proj",
    )(h2, h2, gain, w_feat, w_keys, fb, nfq, nfk, nsq, nsk)


def _fox_body(q_ref, k_ref, vt_ref, o_ref, m_ref, acc_ref, st_ref):
    def attend(q_start, tq, n_full, diag):
        q = q_ref[:, pl.ds(q_start, tq)]
        m_ref[:, :, :tq] = jnp.full((FOX_GROUP, 1, tq), NEG, F32)
        acc_ref[:, :tq] = jnp.zeros((FOX_GROUP * V_ROWS, tq), F32)

        everyone = slice(0, tq)

        def scores(kblk, h, cols=everyone):
            return _dot(kblk[:, h * AUG:(h + 1) * AUG], q[h * AUG:(h + 1) * AUG, cols])

        def softmax(h, st, mask, cols):
            if mask is not None:
                st = jnp.where(mask, st, NEG)
            m_prev = m_ref[h, :, cols]
            m_new = jnp.maximum(m_prev, jnp.max(st, axis=0, keepdims=True))
            alpha = jnp.exp2(m_prev - m_new)
            m_ref[h, :, cols] = m_new
            return jnp.exp2(st - m_new).astype(BF16), alpha

        def weighted_values(h, vt, p_alpha, cols):
            p, alpha = p_alpha
            rows = slice(h * V_ROWS, (h + 1) * V_ROWS)
            acc_ref[rows, cols] = alpha * acc_ref[rows, cols] + _dot(vt[rows, :], p)

        def run_heads(kblk, vt, st0, mask, cols=everyone, next_scores=None):
            st = [st0, scores(kblk, 1, cols), None, None]
            pa = softmax(0, st[0], mask, cols)
            for h in range(FOX_GROUP):
                if h + 2 < FOX_GROUP:
                    st[h + 2] = scores(kblk, h + 2, cols)
                elif h + 2 == FOX_GROUP and next_scores is not None:
                    st_ref[...] = next_scores()
                weighted_values(h, vt, pa, cols)
                if h + 1 < FOX_GROUP:
                    pa = softmax(h + 1, st[h + 1], mask, cols)

        def key_tile(j):
            return k_ref[pl.ds(pl.multiple_of(BLOCK + j * FOX_TILE, BLOCK), FOX_TILE), :]

        def full_tile(j):
            vt = vt_ref[:, pl.ds(pl.multiple_of(BLOCK + j * FOX_TILE, BLOCK), FOX_TILE)]
            run_heads(key_tile(j), vt, st_ref[...], None,
                      next_scores=lambda: scores(key_tile(j + 1), 0))

        k_lead = k_ref[0:BLOCK, :]
        v_lead = vt_ref[:, 0:BLOCK]
        if diag:
            st_ref[...] = scores(key_tile(0), 0)

            def tile_pair(jj, carry):
                full_tile(2 * jj)
                full_tile(2 * jj + 1)
                return carry

            lax.fori_loop(0, n_full // 2, tile_pair, 0)

            @pl.when(n_full % 2 == 1)
            def _():
                full_tile(n_full - 1)

            kr = lax.broadcasted_iota(jnp.int32, (BLOCK + tq, tq), 0)
            qc = lax.broadcasted_iota(jnp.int32, (BLOCK + tq, tq), 1)
            visible = ((kr >= PAD_FRONT) & (kr < BLOCK)) | ((kr >= BLOCK) & (kr - BLOCK <= qc))
            run_heads(jnp.concatenate([k_lead, k_ref[pl.ds(q_start, tq), :]], axis=0),
                      jnp.concatenate([v_lead, vt_ref[:, pl.ds(q_start, tq)]], axis=1),
                      jnp.concatenate([scores(k_lead, 0), st_ref[...]], axis=0), visible)
        else:
            kpos0 = lax.broadcasted_iota(jnp.int32, (BLOCK, tq), 0)
            qc = lax.broadcasted_iota(jnp.int32, (BLOCK, tq), 1)
            visible = (kpos0 >= PAD_FRONT) & (kpos0 <= qc)
            for h in range(FOX_GROUP):
                weighted_values(h, v_lead, softmax(h, scores(k_lead, h), visible, everyone), everyone)
        for h in range(FOX_GROUP):
            denom = acc_ref[h * V_ROWS + HEAD_DIM:h * V_ROWS + HEAD_DIM + 1, :tq]
            o_ref[h * HEAD_DIM:(h + 1) * HEAD_DIM, pl.ds(q_start, tq)] = (
                acc_ref[h * V_ROWS:h * V_ROWS + HEAD_DIM, :tq] * (1.0 / denom)).astype(BF16)

    attend(0, BLOCK, 0, False)

    def query_tile(i, carry):
        attend(pl.multiple_of(BLOCK + i * FOX_TILE, BLOCK), FOX_TILE, i, True)
        return carry

    lax.fori_loop(0, (q_ref.shape[1] - BLOCK) // FOX_TILE, query_tile, 0)


def _fox(qft, kf, vt, batch):
    rows = kf.shape[0]
    p = rows // batch
    n_grp = FOX_HEADS // FOX_GROUP
    width = FOX_GROUP * HEAD_DIM

    def feat(r):
        return pl.BlockSpec((r, p), lambda b, g: (g, b))

    return pl.pallas_call(
        _fox_body,
        grid=(batch, n_grp),
        in_specs=[feat(FOX_GROUP * AUG),
                  pl.BlockSpec((p, FOX_GROUP * AUG), lambda b, g: (b, g)),
                  feat(FOX_GROUP * V_ROWS)],
        out_specs=feat(width),
        out_shape=jax.ShapeDtypeStruct((FOX_W, rows), BF16),
        scratch_shapes=[
            pltpu.VMEM((FOX_GROUP, 1, FOX_TILE), F32),
            pltpu.VMEM((FOX_GROUP * V_ROWS, FOX_TILE), F32),
            pltpu.VMEM((FOX_TILE, FOX_TILE), F32),
        ],
        compiler_params=_params("arbitrary", "arbitrary"),
        name="fox",
    )(qft, kf, vt)


def _t5_bucket_np(dist):
    n = np.maximum(dist, 0)
    max_exact = N_BUCKETS // 2
    nf = np.maximum(n, 1).astype(np.float64)
    large = max_exact + (np.log(nf / max_exact) / math.log(MAX_DISTANCE / max_exact)
                         * (N_BUCKETS - max_exact)).astype(np.int64)
    large = np.minimum(large, N_BUCKETS - 1)
    return np.where(n < max_exact, n, large).astype(np.int32)


def _swa_static_buckets():
    k = np.arange(SWA_KEYS)[:, None]
    q = np.arange(BLOCK)[None, :]
    d = BLOCK + q - k
    bucket = np.where((d >= 0) & (d < WINDOW) & (k < 2 * BLOCK), _t5_bucket_np(d), -1)
    bucket = np.where(k >= 2 * BLOCK + PAD_FRONT, N_BUCKETS - 1, bucket)
    return np.ascontiguousarray(np.broadcast_to(bucket, (SWA_KEYS, BLOCK))).astype(np.int32)


def _swa_body(table_ref, sink_ref, bucket_ref, q_ref, k_ref, vt_ref, o_ref, bias_ref):
    g = pl.program_id(1)
    t = pl.program_id(2)
    cols = SWA_GROUP * BLOCK

    @pl.when((pl.program_id(0) == 0) & (t == 0))
    def _():
        bucket = bucket_ref[...]
        slot = lax.broadcasted_iota(jnp.int32, (SWA_KEYS, BLOCK), 0)
        qpos = lax.broadcasted_iota(jnp.int32, (SWA_KEYS, BLOCK), 1)
        for r in range(SWA_GROUP):
            bias = jnp.full((SWA_KEYS, BLOCK), NEG, F32)
            for bkt in range(N_BUCKETS):
                bias = jnp.where(bucket == bkt, table_ref[bkt, g * SWA_GROUP + r], bias)
            bias_ref[g, 2, :, r * BLOCK:(r + 1) * BLOCK] = bias
            for n in range(2):
                band_ok = (slot >= PAD_FRONT - (n - 1) * BLOCK) & (slot < 2 * BLOCK)
                meta_ok = ((slot >= 2 * BLOCK + PAD_FRONT)
                           & (n * BLOCK + qpos - (slot - 2 * BLOCK) >= WINDOW))
                bias_ref[g, n, :, r * BLOCK:(r + 1) * BLOCK] = jnp.where(band_ok | meta_ok, bias, NEG)

    sink = jnp.concatenate(
        [jnp.full((1, BLOCK), sink_ref[g * SWA_GROUP + r], F32) for r in range(SWA_GROUP)], axis=1)
    grp_rows = lax.broadcasted_iota(jnp.int32, (SWA_KVW, cols), 0) // HEAD_DIM
    v_rows = pl.ds(pl.multiple_of(g * HEAD_DIM, HEAD_DIM), HEAD_DIM)
    k_lead = k_ref[0:BLOCK, :]
    v_lead = vt_ref[v_rows, 0:BLOCK]

    n_blocks = q_ref.shape[1] // BLOCK

    def band(i):
        n = t * n_blocks + i
        return (n, pl.multiple_of(jnp.maximum(n - 1, 0) * BLOCK, BLOCK),
                pl.multiple_of(n * BLOCK, BLOCK))

    def scores(i):
        n, prev, cur = band(i)
        keys = jnp.concatenate(
            [k_ref[pl.ds(prev, BLOCK), :], k_ref[pl.ds(cur, BLOCK), :], k_lead], axis=0)
        q4 = jnp.concatenate(
            [q_ref[r * HEAD_DIM:(r + 1) * HEAD_DIM, i * BLOCK:(i + 1) * BLOCK]
             for r in range(SWA_GROUP)], axis=1)
        q8 = jnp.concatenate([q4, q4], axis=0)
        q8 = jnp.where(grp_rows == g, q8, jnp.zeros_like(q8))
        return _dot(keys, q8) + bias_ref[g, jnp.minimum(n, 2)]

    def finish(i, s):
        _, prev, cur = band(i)
        vals = jnp.concatenate(
            [vt_ref[v_rows, pl.ds(prev, BLOCK)], vt_ref[v_rows, pl.ds(cur, BLOCK)], v_lead], axis=1)
        mx = jnp.maximum(jnp.max(s, axis=0, keepdims=True), sink)
        p = jnp.exp(s - mx)
        denom = jnp.sum(p, axis=0, keepdims=True) + jnp.exp(sink - mx)
        out = _dot(vals, p.astype(BF16)) * (1.0 / denom)
        for r in range(SWA_GROUP):
            o_ref[r * HEAD_DIM:(r + 1) * HEAD_DIM, i * BLOCK:(i + 1) * BLOCK] = (
                out[:, r * BLOCK:(r + 1) * BLOCK].astype(BF16))

    s = scores(0)
    for i in range(n_blocks):
        s_next = scores(i + 1) if i + 1 < n_blocks else None
        finish(i, s)
        s = s_next


def _swa(table, sinks, qst, ks, vst, batch):
    rows = ks.shape[0]
    p = rows // batch
    n_t = SWA_STEPS
    swa_tile = p // n_t
    width = SWA_GROUP * HEAD_DIM
    bucket = jnp.asarray(_swa_static_buckets())
    smem = pl.BlockSpec(memory_space=pltpu.SMEM)
    tile = pl.BlockSpec((width, swa_tile), lambda b, g, t: (g, b * n_t + t))
    return pl.pallas_call(
        _swa_body,
        grid=(batch, SWA_KV_HEADS, n_t),
        in_specs=[smem, smem, _const_spec((SWA_KEYS, BLOCK)), tile,
                  pl.BlockSpec((p, SWA_KVW), lambda b, g, t: (b, 0)),
                  pl.BlockSpec((SWA_KVW, p), lambda b, g, t: (0, b))],
        out_specs=tile,
        out_shape=jax.ShapeDtypeStruct((SWA_QW, rows), BF16),
        scratch_shapes=[pltpu.VMEM((SWA_KV_HEADS, 3, SWA_KEYS, SWA_GROUP * BLOCK), F32)],
        compiler_params=_params("arbitrary", "arbitrary", "arbitrary"),
        name="swa",
    )(table, sinks, bucket, qst, ks, vst)


def _mix_body(h_ref, of_ref, os_ref, ga_ref, gb_ref, wf_ref, ws_ref, wo_ref, o_ref):
    y = (ga_ref[...].astype(F32) * _dot(wf_ref[...], of_ref[...])
         + gb_ref[...].astype(F32) * _dot(ws_ref[...], os_ref[...]))
    o_ref[...] = h_ref[...] + _dot_tn(y.astype(BF16), wo_ref[...])


def _mix_out(h2, o_fox, o_swa, ga, gb, wf_t, ws_t, wo):
    rows, d = h2.shape
    tm = ROW_TILE

    def feat(r):
        return pl.BlockSpec((r, tm), lambda i: (0, i))

    tok = pl.BlockSpec((tm, d), lambda i: (i, 0))
    return pl.pallas_call(
        _mix_body,
        grid=(rows // tm,),
        in_specs=[tok, feat(FOX_W), feat(SWA_QW), feat(d), feat(d),
                  _const_spec(wf_t.shape), _const_spec(ws_t.shape), _const_spec(wo.shape)],
        out_specs=tok,
        out_shape=jax.ShapeDtypeStruct((rows, d), F32),
        compiler_params=_params("arbitrary"),
        name="mix_out",
    )(h2, o_fox, o_swa, ga, gb, wf_t, ws_t, wo)


def _w_in_body(w_ref, wf_ref, wk_ref, *, d_model):
    src = np.cumsum([0, FOX_W, FOX_W, FOX_W, FOX_HEADS, SWA_QW, SWA_KVW, SWA_KVW, d_model])
    qa, ka, va, fa, qb, kb, vb, ga, gb = (int(s) for s in src)

    def move(dst, lo, n):
        wf_ref[dst:dst + n] = w_ref[lo:lo + n].astype(BF16)

    move(_R_QA, qa, FOX_W)
    move(_R_VA, va, FOX_W)
    move(_R_QB, qb, SWA_QW)
    move(_R_VB, vb, SWA_KVW)
    move(_R_GA, ga, d_model)
    move(_R_GA + d_model, gb, d_model)
    wf_ref[_R_FA:_R_FA + FA_ROWS] = jnp.concatenate(
        [w_ref[fa:fa + FOX_HEADS], jnp.zeros((FA_ROWS - FOX_HEADS, w_ref.shape[1]), F32)],
        axis=0).astype(BF16)
    wk_ref[:, 0:FOX_W] = w_ref[ka:ka + FOX_W].T.astype(BF16)
    wk_ref[:, FOX_W:FOX_W + SWA_KVW] = w_ref[kb:kb + SWA_KVW].T.astype(BF16)


def _prep_w_in(w_in, layer):
    w_t = jnp.swapaxes(w_in, 1, 2)
    _, d_in, d = w_t.shape
    n_feat = _R_GA + 2 * d
    return pl.pallas_call(
        functools.partial(_w_in_body, d_model=d),
        grid=(1,),
        in_specs=[pl.BlockSpec((None, d_in, d), lambda i: (layer, 0, 0), pipeline_mode=pl.Buffered(1))],
        out_specs=[pl.BlockSpec((n_feat, d), lambda i: (0, 0)),
                   pl.BlockSpec((d, FOX_W + SWA_KVW), lambda i: (0, 0))],
        out_shape=[jax.ShapeDtypeStruct((n_feat, d), BF16),
                   jax.ShapeDtypeStruct((d, FOX_W + SWA_KVW), BF16)],
        compiler_params=_params("arbitrary"),
        name="prep_w_in",
    )(w_t)


def kernel(x, meta_tokens, rel_bias_table, ffn1_norm, ffn1_w_in, ffn1_w_out, mix_norm, w_in, forget_bias, fox_q_norm, fox_k_norm, swa_q_norm, swa_k_norm, swa_sinks, w_branch_fox, w_branch_swa, w_out, ffn2_norm, ffn2_w_in, ffn2_w_out):
    b, seq, d = x.shape
    depth = w_in.shape[0]
    p = BLOCK + seq
    assert seq % FOX_TILE == 0 and p % (SWA_STEPS * BLOCK) == 0 and (b * p) % ROW_TILE == 0
    lead = jnp.concatenate([jnp.zeros((PAD_FRONT, d), x.dtype), meta_tokens.astype(x.dtype)], axis=0)
    h = x.reshape(b * seq, d)
    table = rel_bias_table.astype(F32)

    def col(v):
        return v.astype(F32)[:, None]

    def twice(v):
        return jnp.tile(v.astype(F32), 2)[None]

    for l in range(depth):
        h = _ffn(h, ffn1_norm[l][None], ffn1_w_in, ffn1_w_out, l,
                 lead=(lead, b, seq) if l == 0 else None)
        w_feat, w_keys = _prep_w_in(w_in, l)
        fb = jnp.pad(col(forget_bias[l]), ((0, FA_ROWS - FOX_HEADS), (0, 0)))
        qft, kf, vt, qst, ks, vst, ga, gb = _proj(
            h, p, mix_norm[l][None], w_feat, w_keys, fb,
            col(fox_q_norm[l]), twice(fox_k_norm[l]), col(swa_q_norm[l]), twice(swa_k_norm[l]))
        o_fox = _fox(qft, kf, vt, b)
        o_swa = _swa(table, swa_sinks[l].astype(F32), qst, ks, vst, b)
        h = _mix_out(h, o_fox, o_swa, ga, gb, w_branch_fox[l].astype(BF16).T,
                     w_branch_swa[l].astype(BF16).T, w_out[l].astype(BF16))
        last = l == depth - 1
        h = _ffn(h, ffn2_norm[l][None], ffn2_w_in, ffn2_w_out, l, (b, seq) if last else None)
    return h
```

```python
import functools
import math

import jax
import jax.numpy as jnp
import numpy as np
from jax import lax
from jax.experimental import pallas as pl
from jax.experimental.pallas import tpu as pltpu

F32 = jnp.float32
BF16 = jnp.bfloat16

N_META = 16
BLOCK = 128
PAD_FRONT = BLOCK - N_META
HEAD_DIM = 64
FOX_HEADS = 8
SWA_Q_HEADS = 8
SWA_KV_HEADS = 2
SWA_GROUP = SWA_Q_HEADS // SWA_KV_HEADS
WINDOW = 128
N_BUCKETS = 32
MAX_DISTANCE = 128
EPS = 1e-6
NEG = -1e30
FOX_W = FOX_HEADS * HEAD_DIM
SWA_QW = SWA_Q_HEADS * HEAD_DIM
SWA_KVW = SWA_KV_HEADS * HEAD_DIM
SCALE = HEAD_DIM ** -0.5
LOG2E = math.log2(math.e)

LANES = 128
VMEM_LIMIT = 56 * 1024 * 1024
ROW_TILE = 512
FFN_TILES = (1024, 768, 512)
FFN_CHUNK = 256
FOX_TILE = 512
FOX_GROUP = 4
AUG = LANES
AUG_ROWS = 32
V_ROWS = HEAD_DIM + 16
FA_ROWS = 16
SWA_KEYS = 3 * BLOCK
SWA_STEPS = 3

_R_QA = 0
_R_VA = _R_QA + FOX_W
_R_FA = _R_VA + FOX_W
_R_QB = _R_FA + FA_ROWS
_R_VB = _R_QB + SWA_QW
_R_GA = _R_VB + SWA_KVW


def _params(*sem):
    return pltpu.CompilerParams(dimension_semantics=sem, vmem_limit_bytes=VMEM_LIMIT)


def _const_spec(shape):
    nd = len(shape)
    return pl.BlockSpec(shape, lambda *_: (0,) * nd, pipeline_mode=pl.Buffered(1))


def _dot(a, b):
    return jnp.dot(a, b, preferred_element_type=F32)


def _dot_nt(a, b):
    return lax.dot_general(a, b, (((1,), (1,)), ((), ())), preferred_element_type=F32)


def _dot_tn(a, b):
    return lax.dot_general(a, b, (((0,), (0,)), ((), ())), preferred_element_type=F32)


def _rms_rows(x, g):
    ms = jnp.mean(x * x, axis=-1, keepdims=True)
    return x * lax.rsqrt(ms + EPS) * g


def _lead_blocks_before(block, blocks_per_seq):
    return (block + blocks_per_seq - 1) // blocks_per_seq


def _assemble_rows(x_ref, lead_ref, tile, blocks_per_seq):
    n_sub = x_ref.shape[0] // BLOCK
    first = tile * n_sub
    at = (blocks_per_seq - first % blocks_per_seq) % blocks_per_seq
    parts = []
    for j in range(n_sub):
        cur = x_ref[j * BLOCK:(j + 1) * BLOCK, :]
        if j > 0:
            cur = jnp.where(j > at, x_ref[(j - 1) * BLOCK:j * BLOCK, :], cur)
        parts.append(jnp.where(j == at, lead_ref[...], cur))
    return jnp.concatenate(parts, axis=0)


def _ffn_body(*refs, d_ff, n_pro, blocks_per_seq):
    if blocks_per_seq is None:
        (x_ref, g_ref, wg_ref, wu_ref, wout_ref, o_ref,
         win_bf, wout_bf, acc_ref, xn_ref) = refs
        lead_ref = None
    else:
        (x_ref, lead_ref, g_ref, wg_ref, wu_ref, wout_ref, o_ref,
         win_bf, wout_bf, acc_ref, xn_ref) = refs
    i = pl.program_id(0)

    def rows_of(tile):
        if blocks_per_seq is None:
            return x_ref[...]
        return _assemble_rows(x_ref, lead_ref, tile, blocks_per_seq)

    def chunk(xn, w_gate, w_up, w_down):
        gate = _dot(xn, w_gate)
        up = _dot(xn, w_up)
        return _dot((gate * jax.nn.sigmoid(gate) * up).astype(BF16), w_down)

    @pl.when(i == 0)
    def _():
        xn_ref[...] = _rms_rows(rows_of(0), g_ref[...]).astype(BF16)
        acc_ref[...] = jnp.zeros_like(acc_ref)

    @pl.when(i < n_pro)
    def _():
        lo = pl.multiple_of(i * FFN_CHUNK, FFN_CHUNK)
        w_gate, w_up, w_down = (r[...].astype(BF16) for r in (wg_ref, wu_ref, wout_ref))
        win_bf[:, pl.ds(lo, FFN_CHUNK)] = w_gate
        win_bf[:, pl.ds(pl.multiple_of(d_ff + lo, LANES), FFN_CHUNK)] = w_up
        wout_bf[pl.ds(lo, FFN_CHUNK), :] = w_down
        acc_ref[...] += chunk(xn_ref[...], w_gate, w_up, w_down)

    @pl.when(i == n_pro - 1)
    def _():
        o_ref[...] = rows_of(0) + 0.5 * acc_ref[...]

    @pl.when(i >= n_pro)
    def _():
        x = rows_of(i - n_pro + 1)
        xn = _rms_rows(x, g_ref[...]).astype(BF16)
        for c in range(d_ff // FFN_CHUNK):
            lo = c * FFN_CHUNK
            part = chunk(xn, win_bf[:, lo:lo + FFN_CHUNK],
                         win_bf[:, d_ff + lo:d_ff + lo + FFN_CHUNK], wout_bf[lo:lo + FFN_CHUNK, :])
            if c == 0:
                acc_ref[...] = part
            else:
                acc_ref[...] += part
        o_ref[...] = x + 0.5 * acc_ref[...]


def _ffn(h2, gain, w_in, w_out, layer, real_only=None, lead=None):
    rows, d = h2.shape
    d_ff = w_out.shape[1]
    if lead is not None:
        walked = lead[1] * (lead[2] + BLOCK)
    else:
        walked = rows if real_only is None else real_only[1]
    tm = next(t for t in FFN_TILES if walked % t == 0)
    n_pro = d_ff // FFN_CHUNK
    blocks_per_seq = None
    extra_in, extra_specs = (), ()

    if lead is not None:
        lead_rows, batch, seq = lead
        blocks_per_seq = seq // BLOCK + 1
        rows = batch * blocks_per_seq * BLOCK
        n_tiles = rows // tm

        def x_map(t):
            first = t * (tm // BLOCK)
            start = first - _lead_blocks_before(first, blocks_per_seq)
            return (pl.multiple_of(start * BLOCK, BLOCK), 0)

        def out_map(t):
            return (t, 0)

        x_block = (pl.Element(tm), pl.Element(d))
        extra_in, extra_specs = (lead_rows,), (_const_spec((BLOCK, d)),)
        out_block, out_shape = (tm, d), jax.ShapeDtypeStruct((rows, d), F32)
    elif real_only is None:
        n_tiles = rows // tm

        def x_map(t):
            return (t, 0)

        out_map = x_map
        x_block = out_block = (tm, d)
        out_shape = jax.ShapeDtypeStruct((rows, d), F32)
    else:
        batch, seq = real_only
        per_seq = seq // tm
        n_tiles = batch * per_seq

        def x_map(t):
            start = (t // per_seq) * (rows // batch) + BLOCK + (t % per_seq) * tm
            return (pl.multiple_of(start, BLOCK), 0)

        def out_map(t):
            return (t // per_seq, t % per_seq, 0)

        x_block = (pl.Element(tm), pl.Element(d))
        out_block, out_shape = (None, tm, d), jax.ShapeDtypeStruct((batch, seq, d), F32)

    def tile_of(i):
        return jnp.maximum(i - n_pro + 1, 0)

    def chunk_of(i):
        return jnp.minimum(i, n_pro - 1)

    return pl.pallas_call(
        functools.partial(_ffn_body, d_ff=d_ff, n_pro=n_pro, blocks_per_seq=blocks_per_seq),
        grid=(n_pro + n_tiles - 1,),
        in_specs=[
            pl.BlockSpec(x_block, lambda i: x_map(tile_of(i))),
            *extra_specs,
            _const_spec((1, d)),
            pl.BlockSpec((None, d, FFN_CHUNK), lambda i: (layer, 0, chunk_of(i))),
            pl.BlockSpec((None, d, FFN_CHUNK), lambda i: (layer, 0, n_pro + chunk_of(i))),
            pl.BlockSpec((None, FFN_CHUNK, d), lambda i: (layer, chunk_of(i), 0)),
        ],
        out_specs=pl.BlockSpec(out_block, lambda i: out_map(tile_of(i))),
        out_shape=out_shape,
        scratch_shapes=[pltpu.VMEM((d, 2 * d_ff), BF16), pltpu.VMEM((d_ff, d), BF16),
                        pltpu.VMEM((tm, d), F32), pltpu.VMEM((tm, d), BF16)],
        compiler_params=_params("arbitrary"),
        name="ffn",
    )(h2, *extra_in, gain, w_in, w_in, w_out)


def _split3(x):
    hi = x.astype(BF16).astype(F32)
    r = x - hi
    mid = r.astype(BF16).astype(F32)
    return hi, mid, r - mid


def _head_rms_cols(x, g):
    ms = jnp.mean(x * x, axis=0, keepdims=True)
    return x * lax.rsqrt(ms + EPS) * g


def _pair_rms_rows(x, g2):
    lane = lax.broadcasted_iota(jnp.int32, x.shape, 1)
    sq = x * x
    first = lane < HEAD_DIM
    s0 = jnp.sum(jnp.where(first, sq, 0.0), axis=-1, keepdims=True)
    s1 = jnp.sum(jnp.where(first, 0.0, sq), axis=-1, keepdims=True)
    ms = jnp.where(first, s0, s1) * (1.0 / HEAD_DIM)
    return x * lax.rsqrt(ms + EPS) * g2


def _proj_body(x_ref, xnext_ref, g_ref, wt_ref, wk_ref, fb_ref, nfq_ref, nfk_ref, nsq_ref, nsk_ref,
               qf_ref, kf_ref, vt_ref, qs_ref, ks_ref, vs_ref, ga_ref, gb_ref,
               carry_ref, tri_ref, xn_ref, *, d_model, seq_pad):
    i = pl.program_id(0)
    tm = x_ref.shape[0]

    @pl.when(i == 0)
    def _():
        src = lax.broadcasted_iota(jnp.int32, (tm, tm), 0)
        dst = lax.broadcasted_iota(jnp.int32, (tm, tm), 1)
        tri_ref[...] = (src <= dst).astype(BF16)
        xn_ref[...] = _rms_rows(x_ref[...], g_ref[...]).astype(BF16)

    xn = xn_ref[...]

    def feat(lo, rows):
        return _dot_nt(wt_ref[lo:lo + rows, :], xn)

    va_fa = feat(_R_VA, FOX_W + FA_ROWS)
    lf = jax.nn.log_sigmoid(va_fa[FOX_W:] + fb_ref[...])
    va = va_fa[:FOX_W].astype(BF16)
    ones = jnp.ones((V_ROWS - HEAD_DIM, tm), BF16)
    for h in range(FOX_HEADS):
        vt_ref[h * V_ROWS:(h + 1) * V_ROWS, :] = jnp.concatenate(
            [va[h * HEAD_DIM:(h + 1) * HEAD_DIM], ones], axis=0)
    r0 = i * tm
    boundary = (r0 // seq_pad + 1) * seq_pad - r0
    hi, mid, lo3 = _split3(lf)
    upper = tri_ref[...]
    c_loc = (_dot(hi.astype(BF16), upper) + _dot(mid.astype(BF16), upper)
             + _dot(lo3.astype(BF16), upper))
    carry = jnp.where(r0 % seq_pad == 0, 0.0, carry_ref[:, 0:1])
    tok = lax.broadcasted_iota(jnp.int32, (1, tm), 1)
    c_end = jnp.sum(jnp.where(tok == boundary - 1, c_loc, 0.0), axis=1, keepdims=True)
    c_t = c_loc + jnp.where(tok < boundary, carry, -c_end)
    carry_ref[...] = jnp.broadcast_to(c_t[:, tm - 1:tm], carry_ref.shape)
    c_t = c_t * LOG2E

    c_pad = jnp.concatenate(
        [jnp.where(lax.broadcasted_iota(jnp.int32, (FA_ROWS, tm), 0) < FOX_HEADS, c_t, 0.0),
         jnp.zeros((LANES - FA_ROWS, tm), F32)], axis=0)
    c_tok = c_pad.T
    lane = lax.broadcasted_iota(jnp.int32, (tm, LANES), 1)
    k_hi, k_mid, k_lo = _split3(c_tok)
    aug_lo = (-k_hi - pltpu.roll(k_mid, FOX_HEADS, 1) - pltpu.roll(k_lo, 2 * FOX_HEADS, 1)
              + jnp.where((lane >= 3 * FOX_HEADS) & (lane < 3 * FOX_HEADS + 3), 1.0, 0.0))
    aug_hi = pltpu.roll(aug_lo, HEAD_DIM, 1)
    first = lane < HEAD_DIM

    kt = _dot(xn, wk_ref[...])
    for j in range(FOX_HEADS // 2):
        kn = _pair_rms_rows(kt[:, j * LANES:(j + 1) * LANES], nfk_ref[...])
        kf_ref[:, (2 * j) * AUG:(2 * j + 1) * AUG] = jnp.where(first, kn, aug_hi).astype(BF16)
        kf_ref[:, (2 * j + 1) * AUG:(2 * j + 2) * AUG] = jnp.where(first, aug_lo, kn).astype(BF16)
    ks_ref[...] = _pair_rms_rows(kt[:, FOX_W:FOX_W + SWA_KVW], nsk_ref[...]).astype(BF16)

    arow = lax.broadcasted_iota(jnp.int32, (AUG_ROWS, tm), 0)
    qa = feat(_R_QA, FOX_W)
    zeros = jnp.zeros((HEAD_DIM - AUG_ROWS, tm), BF16)
    for h in range(FOX_HEADS):
        q_hi, q_mid, q_lo = _split3(c_t[h:h + 1, :])
        onehot = jnp.where((arow % FOX_HEADS == h) & (arow < 3 * FOX_HEADS), 1.0, 0.0)
        aug_q = jnp.where(arow == 3 * FOX_HEADS, q_hi, jnp.where(
            arow == 3 * FOX_HEADS + 1, q_mid, jnp.where(arow == 3 * FOX_HEADS + 2, q_lo, onehot)))
        qn = (_head_rms_cols(qa[h * HEAD_DIM:(h + 1) * HEAD_DIM], nfq_ref[...])
              * (SCALE * LOG2E)).astype(BF16)
        parts = [qn, aug_q.astype(BF16), zeros] if h % 2 == 0 else [aug_q.astype(BF16), zeros, qn]
        qf_ref[h * AUG:(h + 1) * AUG, :] = jnp.concatenate(parts, axis=0)
    ga_ref[...] = jax.nn.sigmoid(feat(_R_GA, d_model)).astype(BF16)
    gb_ref[...] = jax.nn.sigmoid(feat(_R_GA + d_model, d_model)).astype(BF16)
    qb = feat(_R_QB, SWA_QW)
    for h in range(SWA_Q_HEADS):
        sl = slice(h * HEAD_DIM, (h + 1) * HEAD_DIM)
        qs_ref[sl, :] = (_head_rms_cols(qb[sl], nsq_ref[...]) * (SCALE * LOG2E)).astype(BF16)
    vs_ref[...] = feat(_R_VB, SWA_KVW).astype(BF16)
    xn_ref[...] = _rms_rows(xnext_ref[...], g_ref[...]).astype(BF16)


def _proj(h2, seq_pad, gain, w_feat, w_keys, fb, nfq, nfk, nsq, nsk):
    rows, d = h2.shape
    tm = ROW_TILE

    def tok(w):
        return pl.BlockSpec((tm, w), lambda i: (i, 0)), jax.ShapeDtypeStruct((rows, w), BF16)

    def feat(r):
        return pl.BlockSpec((r, tm), lambda i: (0, i)), jax.ShapeDtypeStruct((r, rows), BF16)

    outs = [feat(FOX_HEADS * AUG), tok(FOX_HEADS * AUG), feat(FOX_HEADS * V_ROWS), feat(SWA_QW),
            tok(SWA_KVW), feat(SWA_KVW), feat(d), feat(d)]
    return pl.pallas_call(
        functools.partial(_proj_body, d_model=d, seq_pad=seq_pad),
        grid=(rows // tm,),
        in_specs=[
            _const_spec((tm, d)),
            pl.BlockSpec((tm, d), lambda i: (jnp.minimum(i + 1, rows // tm - 1), 0)),
            _const_spec((1, d)),
            _const_spec(w_feat.shape),
            _const_spec(w_keys.shape),
            _const_spec((FA_ROWS, 1)),
            _const_spec((HEAD_DIM, 1)), _const_spec((1, LANES)),
            _const_spec((HEAD_DIM, 1)), _const_spec((1, LANES)),
        ],
        out_specs=[o[0] for o in outs],
        out_shape=[o[1] for o in outs],
        scratch_shapes=[pltpu.VMEM((FA_ROWS, LANES), F32), pltpu.VMEM((tm, tm), BF16),
                        pltpu.VMEM((tm, d), BF16)],
        compiler_params=_params("arbitrary"),
        name="proj",
    )(h2, h2, gain, w_feat, w_keys, fb, nfq, nfk, nsq, nsk)


def _fox_body(q_ref, k_ref, vt_ref, o_ref, m_ref, acc_ref, st_ref):
    def attend(q_start, tq, n_full, diag):
        q = q_ref[:, pl.ds(q_start, tq)]
        m_ref[:, :, :tq] = jnp.full((FOX_GROUP, 1, tq), NEG, F32)
        acc_ref[:, :tq] = jnp.zeros((FOX_GROUP * V_ROWS, tq), F32)

        everyone = slice(0, tq)

        def scores(kblk, h, cols=everyone):
            return _dot(kblk[:, h * AUG:(h + 1) * AUG], q[h * AUG:(h + 1) * AUG, cols])

        def softmax(h, st, mask, cols):
            if mask is not None:
                st = jnp.where(mask, st, NEG)
            m_prev = m_ref[h, :, cols]
            m_new = jnp.maximum(m_prev, jnp.max(st, axis=0, keepdims=True))
            alpha = jnp.exp2(m_prev - m_new)
            m_ref[h, :, cols] = m_new
            return jnp.exp2(st - m_new).astype(BF16), alpha

        def weighted_values(h, vt, p_alpha, cols):
            p, alpha = p_alpha
            rows = slice(h * V_ROWS, (h + 1) * V_ROWS)
            acc_ref[rows, cols] = alpha * acc_ref[rows, cols] + _dot(vt[rows, :], p)

        def run_heads(kblk, vt, st0, mask, cols=everyone, next_scores=None):
            st = [st0, scores(kblk, 1, cols), None, None]
            pa = softmax(0, st[0], mask, cols)
            for h in range(FOX_GROUP):
                if h + 2 < FOX_GROUP:
                    st[h + 2] = scores(kblk, h + 2, cols)
                elif h + 2 == FOX_GROUP and next_scores is not None:
                    st_ref[...] = next_scores()
                weighted_values(h, vt, pa, cols)
                if h + 1 < FOX_GROUP:
                    pa = softmax(h + 1, st[h + 1], mask, cols)

        def key_tile(j):
            return k_ref[pl.ds(pl.multiple_of(BLOCK + j * FOX_TILE, BLOCK), FOX_TILE), :]

        def full_tile(j):
            vt = vt_ref[:, pl.ds(pl.multiple_of(BLOCK + j * FOX_TILE, BLOCK), FOX_TILE)]
            run_heads(key_tile(j), vt, st_ref[...], None,
                      next_scores=lambda: scores(key_tile(j + 1), 0))

        k_lead = k_ref[0:BLOCK, :]
        v_lead = vt_ref[:, 0:BLOCK]
        if diag:
            st_ref[...] = scores(key_tile(0), 0)

            def tile_pair(jj, carry):
                full_tile(2 * jj)
                full_tile(2 * jj + 1)
                return carry

            lax.fori_loop(0, n_full // 2, tile_pair, 0)

            @pl.when(n_full % 2 == 1)
            def _():
                full_tile(n_full - 1)

            kr = lax.broadcasted_iota(jnp.int32, (BLOCK + tq, tq), 0)
            qc = lax.broadcasted_iota(jnp.int32, (BLOCK + tq, tq), 1)
            visible = ((kr >= PAD_FRONT) & (kr < BLOCK)) | ((kr >= BLOCK) & (kr - BLOCK <= qc))
            run_heads(jnp.concatenate([k_lead, k_ref[pl.ds(q_start, tq), :]], axis=0),
                      jnp.concatenate([v_lead, vt_ref[:, pl.ds(q_start, tq)]], axis=1),
                      jnp.concatenate([scores(k_lead, 0), st_ref[...]], axis=0), visible)
        else:
            kpos0 = lax.broadcasted_iota(jnp.int32, (BLOCK, tq), 0)
            qc = lax.broadcasted_iota(jnp.int32, (BLOCK, tq), 1)
            visible = (kpos0 >= PAD_FRONT) & (kpos0 <= qc)
            for h in range(FOX_GROUP):
                weighted_values(h, v_lead, softmax(h, scores(k_lead, h), visible, everyone), everyone)
        for h in range(FOX_GROUP):
            denom = acc_ref[h * V_ROWS + HEAD_DIM:h * V_ROWS + HEAD_DIM + 1, :tq]
            o_ref[h * HEAD_DIM:(h + 1) * HEAD_DIM, pl.ds(q_start, tq)] = (
                acc_ref[h * V_ROWS:h * V_ROWS + HEAD_DIM, :tq] * (1.0 / denom)).astype(BF16)

    attend(0, BLOCK, 0, False)

    def query_tile(i, carry):
        attend(pl.multiple_of(BLOCK + i * FOX_TILE, BLOCK), FOX_TILE, i, True)
        return carry

    lax.fori_loop(0, (q_ref.shape[1] - BLOCK) // FOX_TILE, query_tile, 0)


def _fox(qft, kf, vt, batch):
    rows = kf.shape[0]
    p = rows // batch
    n_grp = FOX_HEADS // FOX_GROUP
    width = FOX_GROUP * HEAD_DIM

    def feat(r):
        return pl.BlockSpec((r, p), lambda b, g: (g, b))

    return pl.pallas_call(
        _fox_body,
        grid=(batch, n_grp),
        in_specs=[feat(FOX_GROUP * AUG),
                  pl.BlockSpec((p, FOX_GROUP * AUG), lambda b, g: (b, g)),
                  feat(FOX_GROUP * V_ROWS)],
        out_specs=feat(width),
        out_shape=jax.ShapeDtypeStruct((FOX_W, rows), BF16),
        scratch_shapes=[
            pltpu.VMEM((FOX_GROUP, 1, FOX_TILE), F32),
            pltpu.VMEM((FOX_GROUP * V_ROWS, FOX_TILE), F32),
            pltpu.VMEM((FOX_TILE, FOX_TILE), F32),
        ],
        compiler_params=_params("arbitrary", "arbitrary"),
        name="fox",
    )(qft, kf, vt)


def _t5_bucket_np(dist):
    n = np.maximum(dist, 0)
    max_exact = N_BUCKETS // 2
    nf = np.maximum(n, 1).astype(np.float64)
    large = max_exact + (np.log(nf / max_exact) / math.log(MAX_DISTANCE / max_exact)
                         * (N_BUCKETS - max_exact)).astype(np.int64)
    large = np.minimum(large, N_BUCKETS - 1)
    return np.where(n < max_exact, n, large).astype(np.int32)


def _swa_static_buckets():
    k = np.arange(SWA_KEYS)[:, None]
    q = np.arange(BLOCK)[None, :]
    d = BLOCK + q - k
    bucket = np.where((d >= 0) & (d < WINDOW) & (k < 2 * BLOCK), _t5_bucket_np(d), -1)
    bucket = np.where(k >= 2 * BLOCK + PAD_FRONT, N_BUCKETS - 1, bucket)
    return np.ascontiguousarray(np.broadcast_to(bucket, (SWA_KEYS, BLOCK))).astype(np.int32)


def _swa_body(table_ref, sink_ref, bucket_ref, q_ref, k_ref, vt_ref, o_ref, bias_ref):
    g = pl.program_id(1)
    t = pl.program_id(2)
    cols = SWA_GROUP * BLOCK

    @pl.when((pl.program_id(0) == 0) & (t == 0))
    def _():
        bucket = bucket_ref[...]
        slot = lax.broadcasted_iota(jnp.int32, (SWA_KEYS, BLOCK), 0)
        qpos = lax.broadcasted_iota(jnp.int32, (SWA_KEYS, BLOCK), 1)
        for r in range(SWA_GROUP):
            bias = jnp.full((SWA_KEYS, BLOCK), NEG, F32)
            for bkt in range(N_BUCKETS):
                bias = jnp.where(bucket == bkt, table_ref[bkt, g * SWA_GROUP + r] * LOG2E, bias)
            bias_ref[g, 2, :, r * BLOCK:(r + 1) * BLOCK] = bias
            for n in range(2):
                band_ok = (slot >= PAD_FRONT - (n - 1) * BLOCK) & (slot < 2 * BLOCK)
                meta_ok = ((slot >= 2 * BLOCK + PAD_FRONT)
                           & (n * BLOCK + qpos - (slot - 2 * BLOCK) >= WINDOW))
                bias_ref[g, n, :, r * BLOCK:(r + 1) * BLOCK] = jnp.where(band_ok | meta_ok, bias, NEG)

    sink = jnp.concatenate(
        [jnp.full((1, BLOCK), sink_ref[g * SWA_GROUP + r] * LOG2E, F32) for r in range(SWA_GROUP)],
        axis=1)
    grp_rows = lax.broadcasted_iota(jnp.int32, (SWA_KVW, cols), 0) // HEAD_DIM
    v_rows = pl.ds(pl.multiple_of(g * HEAD_DIM, HEAD_DIM), HEAD_DIM)
    k_lead = k_ref[0:BLOCK, :]
    v_lead = vt_ref[v_rows, 0:BLOCK]

    n_blocks = q_ref.shape[1] // BLOCK

    def band(i):
        n = t * n_blocks + i
        return (n, pl.multiple_of(jnp.maximum(n - 1, 0) * BLOCK, BLOCK),
                pl.multiple_of(n * BLOCK, BLOCK))

    def scores(i):
        n, prev, cur = band(i)
        keys = jnp.concatenate(
            [k_ref[pl.ds(prev, BLOCK), :], k_ref[pl.ds(cur, BLOCK), :], k_lead], axis=0)
        q4 = jnp.concatenate(
            [q_ref[r * HEAD_DIM:(r + 1) * HEAD_DIM, i * BLOCK:(i + 1) * BLOCK]
             for r in range(SWA_GROUP)], axis=1)
        q8 = jnp.concatenate([q4, q4], axis=0)
        q8 = jnp.where(grp_rows == g, q8, jnp.zeros_like(q8))
        return _dot(keys, q8) + bias_ref[g, jnp.minimum(n, 2)]

    def finish(i, s):
        _, prev, cur = band(i)
        vals = jnp.concatenate(
            [vt_ref[v_rows, pl.ds(prev, BLOCK)], vt_ref[v_rows, pl.ds(cur, BLOCK)], v_lead], axis=1)
        vals = jnp.concatenate([vals, jnp.ones((V_ROWS - HEAD_DIM, SWA_KEYS), BF16)], axis=0)
        mx = jnp.maximum(jnp.max(s, axis=0, keepdims=True), sink)
        pv = _dot(vals, jnp.exp2(s - mx).astype(BF16))
        denom = pv[HEAD_DIM:HEAD_DIM + 1] + jnp.exp2(sink - mx)
        out = pv[:HEAD_DIM] * (1.0 / denom)
        for r in range(SWA_GROUP):
            o_ref[r * HEAD_DIM:(r + 1) * HEAD_DIM, i * BLOCK:(i + 1) * BLOCK] = (
                out[:, r * BLOCK:(r + 1) * BLOCK].astype(BF16))

    s = scores(0)
    for i in range(n_blocks):
        s_next = scores(i + 1) if i + 1 < n_blocks else None
        finish(i, s)
        s = s_next


def _swa(table, sinks, qst, ks, vst, batch):
    rows = ks.shape[0]
    p = rows // batch
    n_t = SWA_STEPS
    swa_tile = p // n_t
    width = SWA_GROUP * HEAD_DIM
    bucket = jnp.asarray(_swa_static_buckets())
    smem = pl.BlockSpec(memory_space=pltpu.SMEM)
    tile = pl.BlockSpec((width, swa_tile), lambda b, g, t: (g, b * n_t + t))
    return pl.pallas_call(
        _swa_body,
        grid=(batch, SWA_KV_HEADS, n_t),
        in_specs=[smem, smem, _const_spec((SWA_KEYS, BLOCK)), tile,
                  pl.BlockSpec((p, SWA_KVW), lambda b, g, t: (b, 0)),
                  pl.BlockSpec((SWA_KVW, p), lambda b, g, t: (0, b))],
        out_specs=tile,
        out_shape=jax.ShapeDtypeStruct((SWA_QW, rows), BF16),
        scratch_shapes=[pltpu.VMEM((SWA_KV_HEADS, 3, SWA_KEYS, SWA_GROUP * BLOCK), F32)],
        compiler_params=_params("arbitrary", "arbitrary", "arbitrary"),
        name="swa",
    )(table, sinks, bucket, qst, ks, vst)


def _mix_body(h_ref, of_ref, os_ref, ga_ref, gb_ref, wf_ref, ws_ref, wo_ref, o_ref):
    y = (ga_ref[...].astype(F32) * _dot(wf_ref[...], of_ref[...])
         + gb_ref[...].astype(F32) * _dot(ws_ref[...], os_ref[...]))
    o_ref[...] = h_ref[...] + _dot_tn(y.astype(BF16), wo_ref[...])


def _mix_out(h2, o_fox, o_swa, ga, gb, wf_t, ws_t, wo):
    rows, d = h2.shape
    tm = ROW_TILE

    def feat(r):
        return pl.BlockSpec((r, tm), lambda i: (0, i))

    tok = pl.BlockSpec((tm, d), lambda i: (i, 0))
    return pl.pallas_call(
        _mix_body,
        grid=(rows // tm,),
        in_specs=[tok, feat(FOX_W), feat(SWA_QW), feat(d), feat(d),
                  _const_spec(wf_t.shape), _const_spec(ws_t.shape), _const_spec(wo.shape)],
        out_specs=tok,
        out_shape=jax.ShapeDtypeStruct((rows, d), F32),
        compiler_params=_params("arbitrary"),
        name="mix_out",
    )(h2, o_fox, o_swa, ga, gb, wf_t, ws_t, wo)


def _w_in_body(w_ref, wf_ref, wk_ref, *, d_model):
    src = np.cumsum([0, FOX_W, FOX_W, FOX_W, FOX_HEADS, SWA_QW, SWA_KVW, SWA_KVW, d_model])
    qa, ka, va, fa, qb, kb, vb, ga, gb = (int(s) for s in src)

    def move(dst, lo, n):
        wf_ref[dst:dst + n] = w_ref[lo:lo + n].astype(BF16)

    move(_R_QA, qa, FOX_W)
    move(_R_VA, va, FOX_W)
    move(_R_QB, qb, SWA_QW)
    move(_R_VB, vb, SWA_KVW)
    move(_R_GA, ga, d_model)
    move(_R_GA + d_model, gb, d_model)
    wf_ref[_R_FA:_R_FA + FA_ROWS] = jnp.concatenate(
        [w_ref[fa:fa + FOX_HEADS], jnp.zeros((FA_ROWS - FOX_HEADS, w_ref.shape[1]), F32)],
        axis=0).astype(BF16)
    wk_ref[:, 0:FOX_W] = w_ref[ka:ka + FOX_W].T.astype(BF16)
    wk_ref[:, FOX_W:FOX_W + SWA_KVW] = w_ref[kb:kb + SWA_KVW].T.astype(BF16)


def _prep_w_in(w_in, layer):
    w_t = jnp.swapaxes(w_in, 1, 2)
    _, d_in, d = w_t.shape
    n_feat = _R_GA + 2 * d
    return pl.pallas_call(
        functools.partial(_w_in_body, d_model=d),
        grid=(1,),
        in_specs=[pl.BlockSpec((None, d_in, d), lambda i: (layer, 0, 0), pipeline_mode=pl.Buffered(1))],
        out_specs=[pl.BlockSpec((n_feat, d), lambda i: (0, 0)),
                   pl.BlockSpec((d, FOX_W + SWA_KVW), lambda i: (0, 0))],
        out_shape=[jax.ShapeDtypeStruct((n_feat, d), BF16),
                   jax.ShapeDtypeStruct((d, FOX_W + SWA_KVW), BF16)],
        compiler_params=_params("arbitrary"),
        name="prep_w_in",
    )(w_t)


def kernel(x, meta_tokens, rel_bias_table, ffn1_norm, ffn1_w_in, ffn1_w_out, mix_norm, w_in, forget_bias, fox_q_norm, fox_k_norm, swa_q_norm, swa_k_norm, swa_sinks, w_branch_fox, w_branch_swa, w_out, ffn2_norm, ffn2_w_in, ffn2_w_out):
    b, seq, d = x.shape
    depth = w_in.shape[0]
    p = BLOCK + seq
    assert seq % FOX_TILE == 0 and p % (SWA_STEPS * BLOCK) == 0 and (b * p) % ROW_TILE == 0
    lead = jnp.concatenate([jnp.zeros((PAD_FRONT, d), x.dtype), meta_tokens.astype(x.dtype)], axis=0)
    h = x.reshape(b * seq, d)
    table = rel_bias_table.astype(F32)

    def col(v):
        return v.astype(F32)[:, None]

    def twice(v):
        return jnp.tile(v.astype(F32), 2)[None]

    for l in range(depth):
        h = _ffn(h, ffn1_norm[l][None], ffn1_w_in, ffn1_w_out, l,
                 lead=(lead, b, seq) if l == 0 else None)
        w_feat, w_keys = _prep_w_in(w_in, l)
        fb = jnp.pad(col(forget_bias[l]), ((0, FA_ROWS - FOX_HEADS), (0, 0)))
        qft, kf, vt, qst, ks, vst, ga, gb = _proj(
            h, p, mix_norm[l][None], w_feat, w_keys, fb,
            col(fox_q_norm[l]), twice(fox_k_norm[l]), col(swa_q_norm[l]), twice(swa_k_norm[l]))
        o_fox = _fox(qft, kf, vt, b)
        o_swa = _swa(table, swa_sinks[l].astype(F32), qst, ks, vst, b)
        h = _mix_out(h, o_fox, o_swa, ga, gb, w_branch_fox[l].astype(BF16).T,
                     w_branch_swa[l].astype(BF16).T, w_out[l].astype(BF16))
        last = l == depth - 1
        h = _ffn(h, ffn2_norm[l][None], ffn2_w_in, ffn2_w_out, l, (b, seq) if last else None)
    return h
```

```python
import functools
import math

import jax
import jax.numpy as jnp
import numpy as np
from jax import lax
from jax.experimental import pallas as pl
from jax.experimental.pallas import tpu as pltpu

F32 = jnp.float32
BF16 = jnp.bfloat16

N_META = 16
BLOCK = 128
PAD_FRONT = BLOCK - N_META
HEAD_DIM = 64
FOX_HEADS = 8
SWA_Q_HEADS = 8
SWA_KV_HEADS = 2
SWA_GROUP = SWA_Q_HEADS // SWA_KV_HEADS
WINDOW = 128
N_BUCKETS = 32
MAX_DISTANCE = 128
EPS = 1e-6
NEG = -1e30
FOX_W = FOX_HEADS * HEAD_DIM
SWA_QW = SWA_Q_HEADS * HEAD_DIM
SWA_KVW = SWA_KV_HEADS * HEAD_DIM
SCALE = HEAD_DIM ** -0.5
LOG2E = math.log2(math.e)

LANES = 128
VMEM_LIMIT = 56 * 1024 * 1024
ROW_TILE = 512
FFN_TILES = (1024, 768, 512)
MIX_TILES = (768, 512)
FFN_CHUNK = 256
FOX_TILE = 512
FOX_GROUP = 4
AUG = LANES
AUG_ROWS = 32
V_ROWS = HEAD_DIM + 16
FA_ROWS = 16
SWA_KEYS = 3 * BLOCK
SWA_STEPS = 3

_R_QA = 0
_R_VA = _R_QA + FOX_W
_R_FA = _R_VA + FOX_W
_R_QB = _R_FA + FA_ROWS
_R_VB = _R_QB + SWA_QW
_R_GA = _R_VB + SWA_KVW


def _params(*sem):
    return pltpu.CompilerParams(dimension_semantics=sem, vmem_limit_bytes=VMEM_LIMIT)


def _const_spec(shape):
    nd = len(shape)
    return pl.BlockSpec(shape, lambda *_: (0,) * nd, pipeline_mode=pl.Buffered(1))


def _dot(a, b):
    return jnp.dot(a, b, preferred_element_type=F32)


def _dot_nt(a, b):
    return lax.dot_general(a, b, (((1,), (1,)), ((), ())), preferred_element_type=F32)


def _dot_tn(a, b):
    return lax.dot_general(a, b, (((0,), (0,)), ((), ())), preferred_element_type=F32)


def _rms_rows(x, g):
    ms = jnp.mean(x * x, axis=-1, keepdims=True)
    return x * lax.rsqrt(ms + EPS) * g


def _lead_blocks_before(block, blocks_per_seq):
    return (block + blocks_per_seq - 1) // blocks_per_seq


def _assemble_rows(x_ref, lead_ref, tile, blocks_per_seq):
    n_sub = x_ref.shape[0] // BLOCK
    first = tile * n_sub
    at = (blocks_per_seq - first % blocks_per_seq) % blocks_per_seq
    parts = []
    for j in range(n_sub):
        cur = x_ref[j * BLOCK:(j + 1) * BLOCK, :]
        if j > 0:
            cur = jnp.where(j > at, x_ref[(j - 1) * BLOCK:j * BLOCK, :], cur)
        parts.append(jnp.where(j == at, lead_ref[...], cur))
    return jnp.concatenate(parts, axis=0)


def _ffn_body(*refs, d_ff, n_pro, blocks_per_seq):
    if blocks_per_seq is None:
        (x_ref, g_ref, wg_ref, wu_ref, wout_ref, o_ref,
         win_bf, wout_bf, acc_ref, xn_ref) = refs
        lead_ref = None
    else:
        (x_ref, lead_ref, g_ref, wg_ref, wu_ref, wout_ref, o_ref,
         win_bf, wout_bf, acc_ref, xn_ref) = refs
    i = pl.program_id(0)

    def rows_of(tile):
        if blocks_per_seq is None:
            return x_ref[...]
        return _assemble_rows(x_ref, lead_ref, tile, blocks_per_seq)

    def chunk(xn, w_gate, w_up, w_down):
        gate = _dot(xn, w_gate)
        up = _dot(xn, w_up)
        return _dot((gate * jax.nn.sigmoid(gate) * up).astype(BF16), w_down)

    @pl.when(i == 0)
    def _():
        xn_ref[...] = _rms_rows(rows_of(0), g_ref[...]).astype(BF16)
        acc_ref[...] = jnp.zeros_like(acc_ref)

    @pl.when(i < n_pro)
    def _():
        lo = pl.multiple_of(i * FFN_CHUNK, FFN_CHUNK)
        w_gate, w_up, w_down = (r[...].astype(BF16) for r in (wg_ref, wu_ref, wout_ref))
        win_bf[:, pl.ds(lo, FFN_CHUNK)] = w_gate
        win_bf[:, pl.ds(pl.multiple_of(d_ff + lo, LANES), FFN_CHUNK)] = w_up
        wout_bf[pl.ds(lo, FFN_CHUNK), :] = w_down
        acc_ref[...] += chunk(xn_ref[...], w_gate, w_up, w_down)

    @pl.when(i == n_pro - 1)
    def _():
        o_ref[...] = rows_of(0) + 0.5 * acc_ref[...]

    @pl.when(i >= n_pro)
    def _():
        x = rows_of(i - n_pro + 1)
        xn = _rms_rows(x, g_ref[...]).astype(BF16)
        for c in range(d_ff // FFN_CHUNK):
            lo = c * FFN_CHUNK
            part = chunk(xn, win_bf[:, lo:lo + FFN_CHUNK],
                         win_bf[:, d_ff + lo:d_ff + lo + FFN_CHUNK], wout_bf[lo:lo + FFN_CHUNK, :])
            if c == 0:
                acc_ref[...] = part
            else:
                acc_ref[...] += part
        o_ref[...] = x + 0.5 * acc_ref[...]


def _ffn(h2, gain, w_in, w_out, layer, real_only=None, lead=None):
    rows, d = h2.shape
    d_ff = w_out.shape[1]
    if lead is not None:
        walked = lead[1] * (lead[2] + BLOCK)
    else:
        walked = rows if real_only is None else real_only[1]
    tm = next(t for t in FFN_TILES if walked % t == 0)
    n_pro = d_ff // FFN_CHUNK
    blocks_per_seq = None
    extra_in, extra_specs = (), ()

    if lead is not None:
        lead_rows, batch, seq = lead
        blocks_per_seq = seq // BLOCK + 1
        rows = batch * blocks_per_seq * BLOCK
        n_tiles = rows // tm

        def x_map(t):
            first = t * (tm // BLOCK)
            start = first - _lead_blocks_before(first, blocks_per_seq)
            return (pl.multiple_of(start * BLOCK, BLOCK), 0)

        def out_map(t):
            return (t, 0)

        x_block = (pl.Element(tm), pl.Element(d))
        extra_in, extra_specs = (lead_rows,), (_const_spec((BLOCK, d)),)
        out_block, out_shape = (tm, d), jax.ShapeDtypeStruct((rows, d), F32)
    elif real_only is None:
        n_tiles = rows // tm

        def x_map(t):
            return (t, 0)

        out_map = x_map
        x_block = out_block = (tm, d)
        out_shape = jax.ShapeDtypeStruct((rows, d), F32)
    else:
        batch, seq = real_only
        per_seq = seq // tm
        n_tiles = batch * per_seq

        def x_map(t):
            start = (t // per_seq) * (rows // batch) + BLOCK + (t % per_seq) * tm
            return (pl.multiple_of(start, BLOCK), 0)

        def out_map(t):
            return (t // per_seq, t % per_seq, 0)

        x_block = (pl.Element(tm), pl.Element(d))
        out_block, out_shape = (None, tm, d), jax.ShapeDtypeStruct((batch, seq, d), F32)

    def tile_of(i):
        return jnp.maximum(i - n_pro + 1, 0)

    def chunk_of(i):
        return jnp.minimum(i, n_pro - 1)

    return pl.pallas_call(
        functools.partial(_ffn_body, d_ff=d_ff, n_pro=n_pro, blocks_per_seq=blocks_per_seq),
        grid=(n_pro + n_tiles - 1,),
        in_specs=[
            pl.BlockSpec(x_block, lambda i: x_map(tile_of(i))),
            *extra_specs,
            _const_spec((1, d)),
            pl.BlockSpec((None, d, FFN_CHUNK), lambda i: (layer, 0, chunk_of(i))),
            pl.BlockSpec((None, d, FFN_CHUNK), lambda i: (layer, 0, n_pro + chunk_of(i))),
            pl.BlockSpec((None, FFN_CHUNK, d), lambda i: (layer, chunk_of(i), 0)),
        ],
        out_specs=pl.BlockSpec(out_block, lambda i: out_map(tile_of(i))),
        out_shape=out_shape,
        scratch_shapes=[pltpu.VMEM((d, 2 * d_ff), BF16), pltpu.VMEM((d_ff, d), BF16),
                        pltpu.VMEM((tm, d), F32), pltpu.VMEM((tm, d), BF16)],
        compiler_params=_params("arbitrary"),
        name="ffn",
    )(h2, *extra_in, gain, w_in, w_in, w_out)


def _split3(x):
    hi = x.astype(BF16).astype(F32)
    r = x - hi
    mid = r.astype(BF16).astype(F32)
    return hi, mid, r - mid


def _head_rms_cols(x, g):
    ms = jnp.mean(x * x, axis=0, keepdims=True)
    return x * lax.rsqrt(ms + EPS) * g


def _pair_rms_rows(x, g2):
    lane = lax.broadcasted_iota(jnp.int32, x.shape, 1)
    sq = x * x
    first = lane < HEAD_DIM
    s0 = jnp.sum(jnp.where(first, sq, 0.0), axis=-1, keepdims=True)
    s1 = jnp.sum(jnp.where(first, 0.0, sq), axis=-1, keepdims=True)
    ms = jnp.where(first, s0, s1) * (1.0 / HEAD_DIM)
    return x * lax.rsqrt(ms + EPS) * g2


def _proj_body(x_ref, xnext_ref, g_ref, wt_ref, wk_ref, fb_ref, nfq_ref, nfk_ref, nsq_ref, nsk_ref,
               qf_ref, kf_ref, vt_ref, qs_ref, ks_ref, vs_ref, ga_ref, gb_ref,
               carry_ref, tri_ref, xn_ref, *, d_model, seq_pad):
    i = pl.program_id(0)
    tm = x_ref.shape[0]

    @pl.when(i == 0)
    def _():
        src = lax.broadcasted_iota(jnp.int32, (tm, tm), 0)
        dst = lax.broadcasted_iota(jnp.int32, (tm, tm), 1)
        tri_ref[...] = (src <= dst).astype(BF16)
        xn_ref[...] = _rms_rows(x_ref[...], g_ref[...]).astype(BF16)

    xn = xn_ref[...]

    def feat(lo, rows):
        return _dot_nt(wt_ref[lo:lo + rows, :], xn)

    va_fa = feat(_R_VA, FOX_W + FA_ROWS)
    lf = jax.nn.log_sigmoid(va_fa[FOX_W:] + fb_ref[...])
    va = va_fa[:FOX_W].astype(BF16)
    ones = jnp.ones((V_ROWS - HEAD_DIM, tm), BF16)
    for h in range(FOX_HEADS):
        vt_ref[h * V_ROWS:(h + 1) * V_ROWS, :] = jnp.concatenate(
            [va[h * HEAD_DIM:(h + 1) * HEAD_DIM], ones], axis=0)
    r0 = i * tm
    boundary = (r0 // seq_pad + 1) * seq_pad - r0
    hi, mid, lo3 = _split3(lf)
    upper = tri_ref[...]
    c_loc = (_dot(hi.astype(BF16), upper) + _dot(mid.astype(BF16), upper)
             + _dot(lo3.astype(BF16), upper))
    carry = jnp.where(r0 % seq_pad == 0, 0.0, carry_ref[:, 0:1])
    tok = lax.broadcasted_iota(jnp.int32, (1, tm), 1)
    c_end = jnp.sum(jnp.where(tok == boundary - 1, c_loc, 0.0), axis=1, keepdims=True)
    c_t = c_loc + jnp.where(tok < boundary, carry, -c_end)
    carry_ref[...] = jnp.broadcast_to(c_t[:, tm - 1:tm], carry_ref.shape)
    c_t = c_t * LOG2E

    c_pad = jnp.concatenate(
        [jnp.where(lax.broadcasted_iota(jnp.int32, (FA_ROWS, tm), 0) < FOX_HEADS, c_t, 0.0),
         jnp.zeros((LANES - FA_ROWS, tm), F32)], axis=0)
    c_tok = c_pad.T
    lane = lax.broadcasted_iota(jnp.int32, (tm, LANES), 1)
    k_hi, k_mid, k_lo = _split3(c_tok)
    aug_lo = (-k_hi - pltpu.roll(k_mid, FOX_HEADS, 1) - pltpu.roll(k_lo, 2 * FOX_HEADS, 1)
              + jnp.where((lane >= 3 * FOX_HEADS) & (lane < 3 * FOX_HEADS + 3), 1.0, 0.0))
    aug_hi = pltpu.roll(aug_lo, HEAD_DIM, 1)
    first = lane < HEAD_DIM

    kt = _dot(xn, wk_ref[...])
    for j in range(FOX_HEADS // 2):
        kn = _pair_rms_rows(kt[:, j * LANES:(j + 1) * LANES], nfk_ref[...])
        kf_ref[:, (2 * j) * AUG:(2 * j + 1) * AUG] = jnp.where(first, kn, aug_hi).astype(BF16)
        kf_ref[:, (2 * j + 1) * AUG:(2 * j + 2) * AUG] = jnp.where(first, aug_lo, kn).astype(BF16)
    ks_ref[...] = _pair_rms_rows(kt[:, FOX_W:FOX_W + SWA_KVW], nsk_ref[...]).astype(BF16)

    arow = lax.broadcasted_iota(jnp.int32, (AUG_ROWS, tm), 0)
    qa = feat(_R_QA, FOX_W)
    zeros = jnp.zeros((HEAD_DIM - AUG_ROWS, tm), BF16)
    for h in range(FOX_HEADS):
        q_hi, q_mid, q_lo = _split3(c_t[h:h + 1, :])
        onehot = jnp.where((arow % FOX_HEADS == h) & (arow < 3 * FOX_HEADS), 1.0, 0.0)
        aug_q = jnp.where(arow == 3 * FOX_HEADS, q_hi, jnp.where(
            arow == 3 * FOX_HEADS + 1, q_mid, jnp.where(arow == 3 * FOX_HEADS + 2, q_lo, onehot)))
        qn = (_head_rms_cols(qa[h * HEAD_DIM:(h + 1) * HEAD_DIM], nfq_ref[...])
              * (SCALE * LOG2E)).astype(BF16)
        parts = [qn, aug_q.astype(BF16), zeros] if h % 2 == 0 else [aug_q.astype(BF16), zeros, qn]
        qf_ref[h * AUG:(h + 1) * AUG, :] = jnp.concatenate(parts, axis=0)
    ga_ref[...] = jax.nn.sigmoid(feat(_R_GA, d_model)).astype(BF16)
    gb_ref[...] = jax.nn.sigmoid(feat(_R_GA + d_model, d_model)).astype(BF16)
    qb = feat(_R_QB, SWA_QW)
    for h in range(SWA_Q_HEADS):
        sl = slice(h * HEAD_DIM, (h + 1) * HEAD_DIM)
        qs_ref[sl, :] = (_head_rms_cols(qb[sl], nsq_ref[...]) * (SCALE * LOG2E)).astype(BF16)
    vs_ref[...] = feat(_R_VB, SWA_KVW).astype(BF16)
    xn_ref[...] = _rms_rows(xnext_ref[...], g_ref[...]).astype(BF16)


def _proj(h2, seq_pad, gain, w_feat, w_keys, fb, nfq, nfk, nsq, nsk):
    rows, d = h2.shape
    tm = ROW_TILE

    def tok(w):
        return pl.BlockSpec((tm, w), lambda i: (i, 0)), jax.ShapeDtypeStruct((rows, w), BF16)

    def feat(r):
        return pl.BlockSpec((r, tm), lambda i: (0, i)), jax.ShapeDtypeStruct((r, rows), BF16)

    outs = [feat(FOX_HEADS * AUG), tok(FOX_HEADS * AUG), feat(FOX_HEADS * V_ROWS), feat(SWA_QW),
            tok(SWA_KVW), feat(SWA_KVW), feat(d), feat(d)]
    return pl.pallas_call(
        functools.partial(_proj_body, d_model=d, seq_pad=seq_pad),
        grid=(rows // tm,),
        in_specs=[
            _const_spec((tm, d)),
            pl.BlockSpec((tm, d), lambda i: (jnp.minimum(i + 1, rows // tm - 1), 0)),
            _const_spec((1, d)),
            _const_spec(w_feat.shape),
            _const_spec(w_keys.shape),
            _const_spec((FA_ROWS, 1)),
            _const_spec((HEAD_DIM, 1)), _const_spec((1, LANES)),
            _const_spec((HEAD_DIM, 1)), _const_spec((1, LANES)),
        ],
        out_specs=[o[0] for o in outs],
        out_shape=[o[1] for o in outs],
        scratch_shapes=[pltpu.VMEM((FA_ROWS, LANES), F32), pltpu.VMEM((tm, tm), BF16),
                        pltpu.VMEM((tm, d), BF16)],
        compiler_params=_params("arbitrary"),
        name="proj",
    )(h2, h2, gain, w_feat, w_keys, fb, nfq, nfk, nsq, nsk)


def _fox_body(q_ref, k_ref, vt_ref, o_ref, m_ref, acc_ref, st_ref):
    def attend(q_start, tq, n_full, diag):
        q = q_ref[:, pl.ds(q_start, tq)]
        m_ref[:, :, :tq] = jnp.full((FOX_GROUP, 1, tq), NEG, F32)
        acc_ref[:, :tq] = jnp.zeros((FOX_GROUP * V_ROWS, tq), F32)

        everyone = slice(0, tq)

        def scores(kblk, h, cols=everyone):
            return _dot(kblk[:, h * AUG:(h + 1) * AUG], q[h * AUG:(h + 1) * AUG, cols])

        def softmax(h, st, mask, cols):
            if mask is not None:
                st = jnp.where(mask, st, NEG)
            m_prev = m_ref[h, :, cols]
            m_new = jnp.maximum(m_prev, jnp.max(st, axis=0, keepdims=True))
            alpha = jnp.exp2(m_prev - m_new)
            m_ref[h, :, cols] = m_new
            return jnp.exp2(st - m_new).astype(BF16), alpha

        def weighted_values(h, vt, p_alpha, cols):
            p, alpha = p_alpha
            rows = slice(h * V_ROWS, (h + 1) * V_ROWS)
            acc_ref[rows, cols] = alpha * acc_ref[rows, cols] + _dot(vt[rows, :], p)

        def run_heads(kblk, vt, st0, mask, cols=everyone, next_scores=None):
            st = [st0, scores(kblk, 1, cols), None, None]
            pa = softmax(0, st[0], mask, cols)
            for h in range(FOX_GROUP):
                if h + 2 < FOX_GROUP:
                    st[h + 2] = scores(kblk, h + 2, cols)
                elif h + 2 == FOX_GROUP and next_scores is not None:
                    st_ref[...] = next_scores()
                weighted_values(h, vt, pa, cols)
                if h + 1 < FOX_GROUP:
                    pa = softmax(h + 1, st[h + 1], mask, cols)

        def key_tile(j):
            return k_ref[pl.ds(pl.multiple_of(BLOCK + j * FOX_TILE, BLOCK), FOX_TILE), :]

        def full_tile(j):
            vt = vt_ref[:, pl.ds(pl.multiple_of(BLOCK + j * FOX_TILE, BLOCK), FOX_TILE)]
            run_heads(key_tile(j), vt, st_ref[...], None,
                      next_scores=lambda: scores(key_tile(j + 1), 0))

        k_lead = k_ref[0:BLOCK, :]
        v_lead = vt_ref[:, 0:BLOCK]
        if diag:
            st_ref[...] = scores(key_tile(0), 0)

            def tile_pair(jj, carry):
                full_tile(2 * jj)
                full_tile(2 * jj + 1)
                return carry

            lax.fori_loop(0, n_full // 2, tile_pair, 0)

            @pl.when(n_full % 2 == 1)
            def _():
                full_tile(n_full - 1)

            kr = lax.broadcasted_iota(jnp.int32, (BLOCK + tq, tq), 0)
            qc = lax.broadcasted_iota(jnp.int32, (BLOCK + tq, tq), 1)
            visible = ((kr >= PAD_FRONT) & (kr < BLOCK)) | ((kr >= BLOCK) & (kr - BLOCK <= qc))
            run_heads(jnp.concatenate([k_lead, k_ref[pl.ds(q_start, tq), :]], axis=0),
                      jnp.concatenate([v_lead, vt_ref[:, pl.ds(q_start, tq)]], axis=1),
                      jnp.concatenate([scores(k_lead, 0), st_ref[...]], axis=0), visible)
        else:
            kpos0 = lax.broadcasted_iota(jnp.int32, (BLOCK, tq), 0)
            qc = lax.broadcasted_iota(jnp.int32, (BLOCK, tq), 1)
            visible = (kpos0 >= PAD_FRONT) & (kpos0 <= qc)
            for h in range(FOX_GROUP):
                weighted_values(h, v_lead, softmax(h, scores(k_lead, h), visible, everyone), everyone)
        for h in range(FOX_GROUP):
            denom = acc_ref[h * V_ROWS + HEAD_DIM:h * V_ROWS + HEAD_DIM + 1, :tq]
            o_ref[h * HEAD_DIM:(h + 1) * HEAD_DIM, pl.ds(q_start, tq)] = (
                acc_ref[h * V_ROWS:h * V_ROWS + HEAD_DIM, :tq] * (1.0 / denom)).astype(BF16)

    attend(0, BLOCK, 0, False)

    def query_tile(i, carry):
        attend(pl.multiple_of(BLOCK + i * FOX_TILE, BLOCK), FOX_TILE, i, True)
        return carry

    lax.fori_loop(0, (q_ref.shape[1] - BLOCK) // FOX_TILE, query_tile, 0)


def _fox(qft, kf, vt, batch):
    rows = kf.shape[0]
    p = rows // batch
    n_grp = FOX_HEADS // FOX_GROUP
    width = FOX_GROUP * HEAD_DIM

    def feat(r):
        return pl.BlockSpec((r, p), lambda b, g: (g, b))

    return pl.pallas_call(
        _fox_body,
        grid=(batch, n_grp),
        in_specs=[feat(FOX_GROUP * AUG),
                  pl.BlockSpec((p, FOX_GROUP * AUG), lambda b, g: (b, g)),
                  feat(FOX_GROUP * V_ROWS)],
        out_specs=feat(width),
        out_shape=jax.ShapeDtypeStruct((FOX_W, rows), BF16),
        scratch_shapes=[
            pltpu.VMEM((FOX_GROUP, 1, FOX_TILE), F32),
            pltpu.VMEM((FOX_GROUP * V_ROWS, FOX_TILE), F32),
            pltpu.VMEM((FOX_TILE, FOX_TILE), F32),
        ],
        compiler_params=_params("arbitrary", "arbitrary"),
        name="fox",
    )(qft, kf, vt)


def _t5_bucket_np(dist):
    n = np.maximum(dist, 0)
    max_exact = N_BUCKETS // 2
    nf = np.maximum(n, 1).astype(np.float64)
    large = max_exact + (np.log(nf / max_exact) / math.log(MAX_DISTANCE / max_exact)
                         * (N_BUCKETS - max_exact)).astype(np.int64)
    large = np.minimum(large, N_BUCKETS - 1)
    return np.where(n < max_exact, n, large).astype(np.int32)


def _swa_static_buckets():
    k = np.arange(SWA_KEYS)[:, None]
    q = np.arange(BLOCK)[None, :]
    d = BLOCK + q - k
    bucket = np.where((d >= 0) & (d < WINDOW) & (k < 2 * BLOCK), _t5_bucket_np(d), -1)
    bucket = np.where(k >= 2 * BLOCK + PAD_FRONT, N_BUCKETS - 1, bucket)
    return np.ascontiguousarray(np.broadcast_to(bucket, (SWA_KEYS, BLOCK))).astype(np.int32)


def _swa_body(table_ref, sink_ref, bucket_ref, q_ref, k_ref, vt_ref, o_ref, bias_ref):
    g = pl.program_id(1)
    t = pl.program_id(2)
    cols = SWA_GROUP * BLOCK

    @pl.when((pl.program_id(0) == 0) & (t == 0))
    def _():
        bucket = bucket_ref[...]
        slot = lax.broadcasted_iota(jnp.int32, (SWA_KEYS, BLOCK), 0)
        qpos = lax.broadcasted_iota(jnp.int32, (SWA_KEYS, BLOCK), 1)
        for r in range(SWA_GROUP):
            bias = jnp.full((SWA_KEYS, BLOCK), NEG, F32)
            for bkt in range(N_BUCKETS):
                bias = jnp.where(bucket == bkt, table_ref[bkt, g * SWA_GROUP + r] * LOG2E, bias)
            bias_ref[g, 2, :, r * BLOCK:(r + 1) * BLOCK] = bias
            for n in range(2):
                band_ok = (slot >= PAD_FRONT - (n - 1) * BLOCK) & (slot < 2 * BLOCK)
                meta_ok = ((slot >= 2 * BLOCK + PAD_FRONT)
                           & (n * BLOCK + qpos - (slot - 2 * BLOCK) >= WINDOW))
                bias_ref[g, n, :, r * BLOCK:(r + 1) * BLOCK] = jnp.where(band_ok | meta_ok, bias, NEG)

    sink = jnp.concatenate(
        [jnp.full((1, BLOCK), sink_ref[g * SWA_GROUP + r] * LOG2E, F32) for r in range(SWA_GROUP)],
        axis=1)
    grp_rows = lax.broadcasted_iota(jnp.int32, (SWA_KVW, cols), 0) // HEAD_DIM
    v_rows = pl.ds(pl.multiple_of(g * HEAD_DIM, HEAD_DIM), HEAD_DIM)
    k_lead = k_ref[0:BLOCK, :]
    v_lead = vt_ref[v_rows, 0:BLOCK]

    n_blocks = q_ref.shape[1] // BLOCK

    def band(i):
        n = t * n_blocks + i
        return (n, pl.multiple_of(jnp.maximum(n - 1, 0) * BLOCK, BLOCK),
                pl.multiple_of(n * BLOCK, BLOCK))

    def scores(i):
        n, prev, cur = band(i)
        keys = jnp.concatenate(
            [k_ref[pl.ds(prev, BLOCK), :], k_ref[pl.ds(cur, BLOCK), :], k_lead], axis=0)
        q4 = jnp.concatenate(
            [q_ref[r * HEAD_DIM:(r + 1) * HEAD_DIM, i * BLOCK:(i + 1) * BLOCK]
             for r in range(SWA_GROUP)], axis=1)
        q8 = jnp.concatenate([q4, q4], axis=0)
        q8 = jnp.where(grp_rows == g, q8, jnp.zeros_like(q8))
        return _dot(keys, q8) + bias_ref[g, jnp.minimum(n, 2)]

    def finish(i, s):
        _, prev, cur = band(i)
        vals = jnp.concatenate(
            [vt_ref[v_rows, pl.ds(prev, BLOCK)], vt_ref[v_rows, pl.ds(cur, BLOCK)], v_lead], axis=1)
        vals = jnp.concatenate([vals, jnp.ones((V_ROWS - HEAD_DIM, SWA_KEYS), BF16)], axis=0)
        mx = jnp.maximum(jnp.max(s, axis=0, keepdims=True), sink)
        pv = _dot(vals, jnp.exp2(s - mx).astype(BF16))
        denom = pv[HEAD_DIM:HEAD_DIM + 1] + jnp.exp2(sink - mx)
        out = pv[:HEAD_DIM] * (1.0 / denom)
        for r in range(SWA_GROUP):
            o_ref[r * HEAD_DIM:(r + 1) * HEAD_DIM, i * BLOCK:(i + 1) * BLOCK] = (
                out[:, r * BLOCK:(r + 1) * BLOCK].astype(BF16))

    s = scores(0)
    for i in range(n_blocks):
        s_next = scores(i + 1) if i + 1 < n_blocks else None
        finish(i, s)
        s = s_next


def _swa(table, sinks, qst, ks, vst, batch):
    rows = ks.shape[0]
    p = rows // batch
    n_t = SWA_STEPS
    swa_tile = p // n_t
    width = SWA_GROUP * HEAD_DIM
    bucket = jnp.asarray(_swa_static_buckets())
    smem = pl.BlockSpec(memory_space=pltpu.SMEM)
    tile = pl.BlockSpec((width, swa_tile), lambda b, g, t: (g, b * n_t + t))
    return pl.pallas_call(
        _swa_body,
        grid=(batch, SWA_KV_HEADS, n_t),
        in_specs=[smem, smem, _const_spec((SWA_KEYS, BLOCK)), tile,
                  pl.BlockSpec((p, SWA_KVW), lambda b, g, t: (b, 0)),
                  pl.BlockSpec((SWA_KVW, p), lambda b, g, t: (0, b))],
        out_specs=tile,
        out_shape=jax.ShapeDtypeStruct((SWA_QW, rows), BF16),
        scratch_shapes=[pltpu.VMEM((SWA_KV_HEADS, 3, SWA_KEYS, SWA_GROUP * BLOCK), F32)],
        compiler_params=_params("arbitrary", "arbitrary", "arbitrary"),
        name="swa",
    )(table, sinks, bucket, qst, ks, vst)


def _mix_body(h_ref, of_ref, os_ref, ga_ref, gb_ref, wf_ref, ws_ref, wo_ref, o_ref):
    y = (ga_ref[...].astype(F32) * _dot(wf_ref[...], of_ref[...])
         + gb_ref[...].astype(F32) * _dot(ws_ref[...], os_ref[...]))
    o_ref[...] = h_ref[...] + _dot_tn(y.astype(BF16), wo_ref[...])


def _mix_out(h2, o_fox, o_swa, ga, gb, wf_t, ws_t, wo):
    rows, d = h2.shape
    tm = next(t for t in MIX_TILES if rows % t == 0)

    def feat(r):
        return pl.BlockSpec((r, tm), lambda i: (0, i))

    tok = pl.BlockSpec((tm, d), lambda i: (i, 0))
    return pl.pallas_call(
        _mix_body,
        grid=(rows // tm,),
        in_specs=[tok, feat(FOX_W), feat(SWA_QW), feat(d), feat(d),
                  _const_spec(wf_t.shape), _const_spec(ws_t.shape), _const_spec(wo.shape)],
        out_specs=tok,
        out_shape=jax.ShapeDtypeStruct((rows, d), F32),
        compiler_params=_params("arbitrary"),
        name="mix_out",
    )(h2, o_fox, o_swa, ga, gb, wf_t, ws_t, wo)


def _w_in_body(w_ref, wf_ref, wk_ref, *, d_model):
    src = np.cumsum([0, FOX_W, FOX_W, FOX_W, FOX_HEADS, SWA_QW, SWA_KVW, SWA_KVW, d_model])
    qa, ka, va, fa, qb, kb, vb, ga, gb = (int(s) for s in src)

    def move(dst, lo, n):
        wf_ref[dst:dst + n] = w_ref[lo:lo + n].astype(BF16)

    move(_R_QA, qa, FOX_W)
    move(_R_VA, va, FOX_W)
    move(_R_QB, qb, SWA_QW)
    move(_R_VB, vb, SWA_KVW)
    move(_R_GA, ga, d_model)
    move(_R_GA + d_model, gb, d_model)
    wf_ref[_R_FA:_R_FA + FA_ROWS] = jnp.concatenate(
        [w_ref[fa:fa + FOX_HEADS], jnp.zeros((FA_ROWS - FOX_HEADS, w_ref.shape[1]), F32)],
        axis=0).astype(BF16)
    wk_ref[:, 0:FOX_W] = w_ref[ka:ka + FOX_W].T.astype(BF16)
    wk_ref[:, FOX_W:FOX_W + SWA_KVW] = w_ref[kb:kb + SWA_KVW].T.astype(BF16)


def _prep_w_in(w_in, layer):
    w_t = jnp.swapaxes(w_in, 1, 2)
    _, d_in, d = w_t.shape
    n_feat = _R_GA + 2 * d
    return pl.pallas_call(
        functools.partial(_w_in_body, d_model=d),
        grid=(1,),
        in_specs=[pl.BlockSpec((None, d_in, d), lambda i: (layer, 0, 0), pipeline_mode=pl.Buffered(1))],
        out_specs=[pl.BlockSpec((n_feat, d), lambda i: (0, 0)),
                   pl.BlockSpec((d, FOX_W + SWA_KVW), lambda i: (0, 0))],
        out_shape=[jax.ShapeDtypeStruct((n_feat, d), BF16),
                   jax.ShapeDtypeStruct((d, FOX_W + SWA_KVW), BF16)],
        compiler_params=_params("arbitrary"),
        name="prep_w_in",
    )(w_t)


def kernel(x, meta_tokens, rel_bias_table, ffn1_norm, ffn1_w_in, ffn1_w_out, mix_norm, w_in, forget_bias, fox_q_norm, fox_k_norm, swa_q_norm, swa_k_norm, swa_sinks, w_branch_fox, w_branch_swa, w_out, ffn2_norm, ffn2_w_in, ffn2_w_out):
    b, seq, d = x.shape
    depth = w_in.shape[0]
    p = BLOCK + seq
    assert seq % FOX_TILE == 0 and p % (SWA_STEPS * BLOCK) == 0 and (b * p) % ROW_TILE == 0
    lead = jnp.concatenate([jnp.zeros((PAD_FRONT, d), x.dtype), meta_tokens.astype(x.dtype)], axis=0)
    h = x.reshape(b * seq, d)
    table = rel_bias_table.astype(F32)

    def col(v):
        return v.astype(F32)[:, None]

    def twice(v):
        return jnp.tile(v.astype(F32), 2)[None]

    for l in range(depth):
        h = _ffn(h, ffn1_norm[l][None], ffn1_w_in, ffn1_w_out, l,
                 lead=(lead, b, seq) if l == 0 else None)
        w_feat, w_keys = _prep_w_in(w_in, l)
        fb = jnp.pad(col(forget_bias[l]), ((0, FA_ROWS - FOX_HEADS), (0, 0)))
        qft, kf, vt, qst, ks, vst, ga, gb = _proj(
            h, p, mix_norm[l][None], w_feat, w_keys, fb,
            col(fox_q_norm[l]), twice(fox_k_norm[l]), col(swa_q_norm[l]), twice(swa_k_norm[l]))
        o_fox = _fox(qft, kf, vt, b)
        o_swa = _swa(table, swa_sinks[l].astype(F32), qst, ks, vst, b)
        h = _mix_out(h, o_fox, o_swa, ga, gb, w_branch_fox[l].astype(BF16).T,
                     w_branch_swa[l].astype(BF16).T, w_out[l].astype(BF16))
        last = l == depth - 1
        h = _ffn(h, ffn2_norm[l][None], ffn2_w_in, ffn2_w_out, l, (b, seq) if last else None)
    return h
```

```python
import functools
import math

import jax
import jax.numpy as jnp
import numpy as np
from jax import lax
from jax.experimental import pallas as pl
from jax.experimental.pallas import tpu as pltpu

F32 = jnp.float32
BF16 = jnp.bfloat16

N_META = 16
BLOCK = 128
PAD_FRONT = BLOCK - N_META
HEAD_DIM = 64
FOX_HEADS = 8
SWA_Q_HEADS = 8
SWA_KV_HEADS = 2
SWA_GROUP = SWA_Q_HEADS // SWA_KV_HEADS
WINDOW = 128
N_BUCKETS = 32
MAX_DISTANCE = 128
EPS = 1e-6
NEG = -1e30
FOX_W = FOX_HEADS * HEAD_DIM
SWA_QW = SWA_Q_HEADS * HEAD_DIM
SWA_KVW = SWA_KV_HEADS * HEAD_DIM
SCALE = HEAD_DIM ** -0.5
LOG2E = math.log2(math.e)

LANES = 128
VMEM_LIMIT = 56 * 1024 * 1024
ROW_TILE = 512
FFN_TILES = (1024, 768, 512)
MIX_TILES = (768, 512)
FFN_CHUNK = 256
FOX_TILE = 512
FOX_GROUP = 4
FOX_MAX_SPREAD = 100.0
AUG = LANES
AUG_ROWS = 32
V_ROWS = HEAD_DIM + 16
FA_ROWS = 16
SWA_KEYS = 3 * BLOCK
SWA_STEPS = 3

_R_QA = 0
_R_VA = _R_QA + FOX_W
_R_FA = _R_VA + FOX_W
_R_QB = _R_FA + FA_ROWS
_R_VB = _R_QB + SWA_QW
_R_GA = _R_VB + SWA_KVW


def _params(*sem):
    return pltpu.CompilerParams(dimension_semantics=sem, vmem_limit_bytes=VMEM_LIMIT)


def _const_spec(shape):
    nd = len(shape)
    return pl.BlockSpec(shape, lambda *_: (0,) * nd, pipeline_mode=pl.Buffered(1))


def _dot(a, b):
    return jnp.dot(a, b, preferred_element_type=F32)


def _dot_nt(a, b):
    return lax.dot_general(a, b, (((1,), (1,)), ((), ())), preferred_element_type=F32)


def _dot_tn(a, b):
    return lax.dot_general(a, b, (((0,), (0,)), ((), ())), preferred_element_type=F32)


def _rms_rows(x, g):
    ms = jnp.mean(x * x, axis=-1, keepdims=True)
    return x * lax.rsqrt(ms + EPS) * g


def _lead_blocks_before(block, blocks_per_seq):
    return (block + blocks_per_seq - 1) // blocks_per_seq


def _assemble_rows(x_ref, lead_ref, tile, blocks_per_seq):
    n_sub = x_ref.shape[0] // BLOCK
    first = tile * n_sub
    at = (blocks_per_seq - first % blocks_per_seq) % blocks_per_seq
    parts = []
    for j in range(n_sub):
        cur = x_ref[j * BLOCK:(j + 1) * BLOCK, :]
        if j > 0:
            cur = jnp.where(j > at, x_ref[(j - 1) * BLOCK:j * BLOCK, :], cur)
        parts.append(jnp.where(j == at, lead_ref[...], cur))
    return jnp.concatenate(parts, axis=0)


def _ffn_body(*refs, d_ff, n_pro, blocks_per_seq):
    if blocks_per_seq is None:
        (x_ref, g_ref, wg_ref, wu_ref, wout_ref, o_ref,
         win_bf, wout_bf, acc_ref, xn_ref) = refs
        lead_ref = None
    else:
        (x_ref, lead_ref, g_ref, wg_ref, wu_ref, wout_ref, o_ref,
         win_bf, wout_bf, acc_ref, xn_ref) = refs
    i = pl.program_id(0)

    def rows_of(tile):
        if blocks_per_seq is None:
            return x_ref[...]
        return _assemble_rows(x_ref, lead_ref, tile, blocks_per_seq)

    def chunk(xn, w_gate, w_up, w_down):
        gate = _dot(xn, w_gate)
        up = _dot(xn, w_up)
        return _dot((gate * jax.nn.sigmoid(gate) * up).astype(BF16), w_down)

    @pl.when(i == 0)
    def _():
        xn_ref[...] = _rms_rows(rows_of(0), g_ref[...]).astype(BF16)
        acc_ref[...] = jnp.zeros_like(acc_ref)

    @pl.when(i < n_pro)
    def _():
        lo = pl.multiple_of(i * FFN_CHUNK, FFN_CHUNK)
        w_gate, w_up, w_down = (r[...].astype(BF16) for r in (wg_ref, wu_ref, wout_ref))
        win_bf[:, pl.ds(lo, FFN_CHUNK)] = w_gate
        win_bf[:, pl.ds(pl.multiple_of(d_ff + lo, LANES), FFN_CHUNK)] = w_up
        wout_bf[pl.ds(lo, FFN_CHUNK), :] = w_down
        acc_ref[...] += chunk(xn_ref[...], w_gate, w_up, w_down)

    @pl.when(i == n_pro - 1)
    def _():
        o_ref[...] = rows_of(0) + 0.5 * acc_ref[...]

    @pl.when(i >= n_pro)
    def _():
        x = rows_of(i - n_pro + 1)
        xn = _rms_rows(x, g_ref[...]).astype(BF16)
        for c in range(d_ff // FFN_CHUNK):
            lo = c * FFN_CHUNK
            part = chunk(xn, win_bf[:, lo:lo + FFN_CHUNK],
                         win_bf[:, d_ff + lo:d_ff + lo + FFN_CHUNK], wout_bf[lo:lo + FFN_CHUNK, :])
            if c == 0:
                acc_ref[...] = part
            else:
                acc_ref[...] += part
        o_ref[...] = x + 0.5 * acc_ref[...]


def _ffn(h2, gain, w_in, w_out, layer, real_only=None, lead=None):
    rows, d = h2.shape
    d_ff = w_out.shape[1]
    if lead is not None:
        walked = lead[1] * (lead[2] + BLOCK)
    else:
        walked = rows if real_only is None else real_only[1]
    tm = next(t for t in FFN_TILES if walked % t == 0)
    n_pro = d_ff // FFN_CHUNK
    blocks_per_seq = None
    extra_in, extra_specs = (), ()

    if lead is not None:
        lead_rows, batch, seq = lead
        blocks_per_seq = seq // BLOCK + 1
        rows = batch * blocks_per_seq * BLOCK
        n_tiles = rows // tm

        def x_map(t):
            first = t * (tm // BLOCK)
            start = first - _lead_blocks_before(first, blocks_per_seq)
            return (pl.multiple_of(start * BLOCK, BLOCK), 0)

        def out_map(t):
            return (t, 0)

        x_block = (pl.Element(tm), pl.Element(d))
        extra_in, extra_specs = (lead_rows,), (_const_spec((BLOCK, d)),)
        out_block, out_shape = (tm, d), jax.ShapeDtypeStruct((rows, d), F32)
    elif real_only is None:
        n_tiles = rows // tm

        def x_map(t):
            return (t, 0)

        out_map = x_map
        x_block = out_block = (tm, d)
        out_shape = jax.ShapeDtypeStruct((rows, d), F32)
    else:
        batch, seq = real_only
        per_seq = seq // tm
        n_tiles = batch * per_seq

        def x_map(t):
            start = (t // per_seq) * (rows // batch) + BLOCK + (t % per_seq) * tm
            return (pl.multiple_of(start, BLOCK), 0)

        def out_map(t):
            return (t // per_seq, t % per_seq, 0)

        x_block = (pl.Element(tm), pl.Element(d))
        out_block, out_shape = (None, tm, d), jax.ShapeDtypeStruct((batch, seq, d), F32)

    def tile_of(i):
        return jnp.maximum(i - n_pro + 1, 0)

    def chunk_of(i):
        return jnp.minimum(i, n_pro - 1)

    return pl.pallas_call(
        functools.partial(_ffn_body, d_ff=d_ff, n_pro=n_pro, blocks_per_seq=blocks_per_seq),
        grid=(n_pro + n_tiles - 1,),
        in_specs=[
            pl.BlockSpec(x_block, lambda i: x_map(tile_of(i))),
            *extra_specs,
            _const_spec((1, d)),
            pl.BlockSpec((None, d, FFN_CHUNK), lambda i: (layer, 0, chunk_of(i))),
            pl.BlockSpec((None, d, FFN_CHUNK), lambda i: (layer, 0, n_pro + chunk_of(i))),
            pl.BlockSpec((None, FFN_CHUNK, d), lambda i: (layer, chunk_of(i), 0)),
        ],
        out_specs=pl.BlockSpec(out_block, lambda i: out_map(tile_of(i))),
        out_shape=out_shape,
        scratch_shapes=[pltpu.VMEM((d, 2 * d_ff), BF16), pltpu.VMEM((d_ff, d), BF16),
                        pltpu.VMEM((tm, d), F32), pltpu.VMEM((tm, d), BF16)],
        compiler_params=_params("arbitrary"),
        name="ffn",
    )(h2, *extra_in, gain, w_in, w_in, w_out)


def _split3(x):
    hi = x.astype(BF16).astype(F32)
    r = x - hi
    mid = r.astype(BF16).astype(F32)
    return hi, mid, r - mid


def _head_rms_cols(x, g):
    ms = jnp.mean(x * x, axis=0, keepdims=True)
    return x * lax.rsqrt(ms + EPS) * g


def _pair_rms_rows(x, g2):
    lane = lax.broadcasted_iota(jnp.int32, x.shape, 1)
    sq = x * x
    first = lane < HEAD_DIM
    s0 = jnp.sum(jnp.where(first, sq, 0.0), axis=-1, keepdims=True)
    s1 = jnp.sum(jnp.where(first, 0.0, sq), axis=-1, keepdims=True)
    ms = jnp.where(first, s0, s1) * (1.0 / HEAD_DIM)
    return x * lax.rsqrt(ms + EPS) * g2


def _proj_body(x_ref, xnext_ref, g_ref, wt_ref, wk_ref, fb_ref, nfq_ref, nfk_ref, nsq_ref, nsk_ref,
               qf_ref, kf_ref, vt_ref, qs_ref, ks_ref, vs_ref, ga_ref, gb_ref,
               carry_ref, tri_ref, xn_ref, *, d_model, seq_pad):
    i = pl.program_id(0)
    tm = x_ref.shape[0]

    @pl.when(i == 0)
    def _():
        src = lax.broadcasted_iota(jnp.int32, (tm, tm), 0)
        dst = lax.broadcasted_iota(jnp.int32, (tm, tm), 1)
        tri_ref[...] = (src <= dst).astype(BF16)
        xn_ref[...] = _rms_rows(x_ref[...], g_ref[...]).astype(BF16)

    xn = xn_ref[...]

    def feat(lo, rows):
        return _dot_nt(wt_ref[lo:lo + rows, :], xn)

    va_fa = feat(_R_VA, FOX_W + FA_ROWS)
    lf = jax.nn.log_sigmoid(va_fa[FOX_W:] + fb_ref[...])
    va = va_fa[:FOX_W].astype(BF16)
    ones = jnp.ones((V_ROWS - HEAD_DIM, tm), BF16)
    for h in range(FOX_HEADS):
        vt_ref[h * V_ROWS:(h + 1) * V_ROWS, :] = jnp.concatenate(
            [va[h * HEAD_DIM:(h + 1) * HEAD_DIM], ones], axis=0)
    r0 = i * tm
    boundary = (r0 // seq_pad + 1) * seq_pad - r0
    hi, mid, lo3 = _split3(lf)
    upper = tri_ref[...]
    c_loc = (_dot(hi.astype(BF16), upper) + _dot(mid.astype(BF16), upper)
             + _dot(lo3.astype(BF16), upper))
    carry = jnp.where(r0 % seq_pad == 0, 0.0, carry_ref[:, 0:1])
    tok = lax.broadcasted_iota(jnp.int32, (1, tm), 1)
    c_end = jnp.sum(jnp.where(tok == boundary - 1, c_loc, 0.0), axis=1, keepdims=True)
    c_t = c_loc + jnp.where(tok < boundary, carry, -c_end)
    carry_ref[...] = jnp.broadcast_to(c_t[:, tm - 1:tm], carry_ref.shape)
    c_t = c_t * LOG2E

    c_pad = jnp.concatenate(
        [jnp.where(lax.broadcasted_iota(jnp.int32, (FA_ROWS, tm), 0) < FOX_HEADS, c_t, 0.0),
         jnp.zeros((LANES - FA_ROWS, tm), F32)], axis=0)
    c_tok = c_pad.T
    lane = lax.broadcasted_iota(jnp.int32, (tm, LANES), 1)
    k_hi, k_mid, k_lo = _split3(c_tok)
    aug_lo = (-k_hi - pltpu.roll(k_mid, FOX_HEADS, 1) - pltpu.roll(k_lo, 2 * FOX_HEADS, 1)
              + jnp.where((lane >= 3 * FOX_HEADS) & (lane < 3 * FOX_HEADS + 3), 1.0, 0.0))
    aug_hi = pltpu.roll(aug_lo, HEAD_DIM, 1)
    first = lane < HEAD_DIM

    kt = _dot(xn, wk_ref[...])
    for j in range(FOX_HEADS // 2):
        kn = _pair_rms_rows(kt[:, j * LANES:(j + 1) * LANES], nfk_ref[...])
        kf_ref[:, (2 * j) * AUG:(2 * j + 1) * AUG] = jnp.where(first, kn, aug_hi).astype(BF16)
        kf_ref[:, (2 * j + 1) * AUG:(2 * j + 2) * AUG] = jnp.where(first, aug_lo, kn).astype(BF16)
    ks_ref[...] = _pair_rms_rows(kt[:, FOX_W:FOX_W + SWA_KVW], nsk_ref[...]).astype(BF16)

    arow = lax.broadcasted_iota(jnp.int32, (AUG_ROWS, tm), 0)
    qa = feat(_R_QA, FOX_W)
    zeros = jnp.zeros((HEAD_DIM - AUG_ROWS, tm), BF16)
    for h in range(FOX_HEADS):
        q_hi, q_mid, q_lo = _split3(c_t[h:h + 1, :])
        onehot = jnp.where((arow % FOX_HEADS == h) & (arow < 3 * FOX_HEADS), 1.0, 0.0)
        aug_q = jnp.where(arow == 3 * FOX_HEADS, q_hi, jnp.where(
            arow == 3 * FOX_HEADS + 1, q_mid, jnp.where(arow == 3 * FOX_HEADS + 2, q_lo, onehot)))
        qn = (_head_rms_cols(qa[h * HEAD_DIM:(h + 1) * HEAD_DIM], nfq_ref[...])
              * (SCALE * LOG2E)).astype(BF16)
        parts = [qn, aug_q.astype(BF16), zeros] if h % 2 == 0 else [aug_q.astype(BF16), zeros, qn]
        qf_ref[h * AUG:(h + 1) * AUG, :] = jnp.concatenate(parts, axis=0)
    ga_ref[...] = jax.nn.sigmoid(feat(_R_GA, d_model)).astype(BF16)
    gb_ref[...] = jax.nn.sigmoid(feat(_R_GA + d_model, d_model)).astype(BF16)
    qb = feat(_R_QB, SWA_QW)
    for h in range(SWA_Q_HEADS):
        sl = slice(h * HEAD_DIM, (h + 1) * HEAD_DIM)
        qs_ref[sl, :] = (_head_rms_cols(qb[sl], nsq_ref[...]) * (SCALE * LOG2E)).astype(BF16)
    vs_ref[...] = feat(_R_VB, SWA_KVW).astype(BF16)
    xn_ref[...] = _rms_rows(xnext_ref[...], g_ref[...]).astype(BF16)


def _proj(h2, seq_pad, gain, w_feat, w_keys, fb, nfq, nfk, nsq, nsk):
    rows, d = h2.shape
    tm = ROW_TILE

    def tok(w):
        return pl.BlockSpec((tm, w), lambda i: (i, 0)), jax.ShapeDtypeStruct((rows, w), BF16)

    def feat(r):
        return pl.BlockSpec((r, tm), lambda i: (0, i)), jax.ShapeDtypeStruct((r, rows), BF16)

    outs = [feat(FOX_HEADS * AUG), tok(FOX_HEADS * AUG), feat(FOX_HEADS * V_ROWS), feat(SWA_QW),
            tok(SWA_KVW), feat(SWA_KVW), feat(d), feat(d)]
    return pl.pallas_call(
        functools.partial(_proj_body, d_model=d, seq_pad=seq_pad),
        grid=(rows // tm,),
        in_specs=[
            _const_spec((tm, d)),
            pl.BlockSpec((tm, d), lambda i: (jnp.minimum(i + 1, rows // tm - 1), 0)),
            _const_spec((1, d)),
            _const_spec(w_feat.shape),
            _const_spec(w_keys.shape),
            _const_spec((FA_ROWS, 1)),
            _const_spec((HEAD_DIM, 1)), _const_spec((1, LANES)),
            _const_spec((HEAD_DIM, 1)), _const_spec((1, LANES)),
        ],
        out_specs=[o[0] for o in outs],
        out_shape=[o[1] for o in outs],
        scratch_shapes=[pltpu.VMEM((FA_ROWS, LANES), F32), pltpu.VMEM((tm, tm), BF16),
                        pltpu.VMEM((tm, d), BF16)],
        compiler_params=_params("arbitrary"),
        name="proj",
    )(h2, h2, gain, w_feat, w_keys, fb, nfq, nfk, nsq, nsk)


def _fox_body(bounded_ref, q_ref, k_ref, vt_ref, o_ref, m_ref, acc_ref, st_ref):
    def attend(q_start, tq, n_full, mode):
        q = q_ref[:, pl.ds(q_start, tq)]
        m_ref[:, :, :tq] = jnp.full((FOX_GROUP, 1, tq), NEG, F32)
        acc_ref[:, :tq] = jnp.zeros((FOX_GROUP * V_ROWS, tq), F32)

        everyone = slice(0, tq)

        def scores(kblk, h, cols=everyone):
            return _dot(kblk[:, h * AUG:(h + 1) * AUG], q[h * AUG:(h + 1) * AUG, cols])

        def softmax(h, st, mask, cols):
            if mask is not None:
                st = jnp.where(mask, st, NEG)
            m_prev = m_ref[h, :, cols]
            m_new = jnp.maximum(m_prev, jnp.max(st, axis=0, keepdims=True))
            alpha = jnp.exp2(m_prev - m_new)
            m_ref[h, :, cols] = m_new
            return jnp.exp2(st - m_new).astype(BF16), alpha

        def fixed_softmax(h, st, cols):
            return jnp.exp2(st - m_ref[h, :, cols]).astype(BF16), None

        def weighted_values(h, vt, p_alpha, cols):
            p, alpha = p_alpha
            rows = slice(h * V_ROWS, (h + 1) * V_ROWS)
            if alpha is None:
                acc_ref[rows, cols] += _dot(vt[rows, :], p)
            else:
                acc_ref[rows, cols] = alpha * acc_ref[rows, cols] + _dot(vt[rows, :], p)

        def run_heads(kblk, vt, st0, mask, cols=everyone, next_scores=None, fixed=False):
            def probs(h, st):
                return fixed_softmax(h, st, cols) if fixed else softmax(h, st, mask, cols)

            st = [st0, scores(kblk, 1, cols), None, None]
            pa = probs(0, st[0])
            for h in range(FOX_GROUP):
                if h + 2 < FOX_GROUP:
                    st[h + 2] = scores(kblk, h + 2, cols)
                elif h + 2 == FOX_GROUP and next_scores is not None:
                    st_ref[...] = next_scores()
                weighted_values(h, vt, pa, cols)
                if h + 1 < FOX_GROUP:
                    pa = probs(h + 1, st[h + 1])

        def key_tile(j):
            return k_ref[pl.ds(pl.multiple_of(BLOCK + j * FOX_TILE, BLOCK), FOX_TILE), :]

        def full_tile(j, fixed):
            vt = vt_ref[:, pl.ds(pl.multiple_of(BLOCK + j * FOX_TILE, BLOCK), FOX_TILE)]
            run_heads(key_tile(j), vt, st_ref[...], None, fixed=fixed,
                      next_scores=lambda: scores(key_tile(j + 1), 0))

        def full_tiles(fixed):
            def tile_pair(jj, carry):
                full_tile(2 * jj, fixed)
                full_tile(2 * jj + 1, fixed)
                return carry

            lax.fori_loop(0, n_full // 2, tile_pair, 0)

            @pl.when(n_full % 2 == 1)
            def _():
                full_tile(n_full - 1, fixed)

        def diagonal_tile(st_diag0, next_scores):
            kr = lax.broadcasted_iota(jnp.int32, (BLOCK + tq, tq), 0)
            qc = lax.broadcasted_iota(jnp.int32, (BLOCK + tq, tq), 1)
            visible = ((kr >= PAD_FRONT) & (kr < BLOCK)) | ((kr >= BLOCK) & (kr - BLOCK <= qc))
            run_heads(jnp.concatenate([k_lead, k_ref[pl.ds(q_start, tq), :]], axis=0),
                      jnp.concatenate([v_lead, vt_ref[:, pl.ds(q_start, tq)]], axis=1),
                      jnp.concatenate([scores(k_lead, 0), st_diag0], axis=0), visible,
                      next_scores=next_scores)

        k_lead = k_ref[0:BLOCK, :]
        v_lead = vt_ref[:, 0:BLOCK]
        if mode == "online":
            st_ref[...] = scores(key_tile(0), 0)
            full_tiles(False)
            diagonal_tile(st_ref[...], None)
        elif mode == "fixed":
            diagonal_tile(scores(k_ref[pl.ds(q_start, tq), :], 0), lambda: scores(key_tile(0), 0))
            full_tiles(True)
        else:
            kpos0 = lax.broadcasted_iota(jnp.int32, (BLOCK, tq), 0)
            qc = lax.broadcasted_iota(jnp.int32, (BLOCK, tq), 1)
            visible = (kpos0 >= PAD_FRONT) & (kpos0 <= qc)
            for h in range(FOX_GROUP):
                weighted_values(h, v_lead, softmax(h, scores(k_lead, h), visible, everyone), everyone)
        for h in range(FOX_GROUP):
            denom = acc_ref[h * V_ROWS + HEAD_DIM:h * V_ROWS + HEAD_DIM + 1, :tq]
            o_ref[h * HEAD_DIM:(h + 1) * HEAD_DIM, pl.ds(q_start, tq)] = (
                acc_ref[h * V_ROWS:h * V_ROWS + HEAD_DIM, :tq] * (1.0 / denom)).astype(BF16)

    attend(0, BLOCK, 0, "lead")

    def query_tile(i, carry):
        q_start = pl.multiple_of(BLOCK + i * FOX_TILE, BLOCK)

        @pl.when(bounded_ref[0] == 1)
        def _():
            attend(q_start, FOX_TILE, i, "fixed")

        @pl.when(bounded_ref[0] != 1)
        def _():
            attend(q_start, FOX_TILE, i, "online")

        return carry

    lax.fori_loop(0, (q_ref.shape[1] - BLOCK) // FOX_TILE, query_tile, 0)


def _fox_logits_bounded(q_gain, k_gain):
    bound = math.sqrt(HEAD_DIM) * LOG2E * jnp.max(jnp.abs(q_gain)) * jnp.max(jnp.abs(k_gain))
    return (2.0 * bound < FOX_MAX_SPREAD).astype(jnp.int32).reshape(1)


def _fox(bounded, qft, kf, vt, batch):
    rows = kf.shape[0]
    p = rows // batch
    n_grp = FOX_HEADS // FOX_GROUP
    width = FOX_GROUP * HEAD_DIM

    def feat(r):
        return pl.BlockSpec((r, p), lambda b, g: (g, b))

    return pl.pallas_call(
        _fox_body,
        grid=(batch, n_grp),
        in_specs=[pl.BlockSpec(memory_space=pltpu.SMEM),
                  feat(FOX_GROUP * AUG),
                  pl.BlockSpec((p, FOX_GROUP * AUG), lambda b, g: (b, g)),
                  feat(FOX_GROUP * V_ROWS)],
        out_specs=feat(width),
        out_shape=jax.ShapeDtypeStruct((FOX_W, rows), BF16),
        scratch_shapes=[
            pltpu.VMEM((FOX_GROUP, 1, FOX_TILE), F32),
            pltpu.VMEM((FOX_GROUP * V_ROWS, FOX_TILE), F32),
            pltpu.VMEM((FOX_TILE, FOX_TILE), F32),
        ],
        compiler_params=_params("arbitrary", "arbitrary"),
        name="fox",
    )(bounded, qft, kf, vt)


def _t5_bucket_np(dist):
    n = np.maximum(dist, 0)
    max_exact = N_BUCKETS // 2
    nf = np.maximum(n, 1).astype(np.float64)
    large = max_exact + (np.log(nf / max_exact) / math.log(MAX_DISTANCE / max_exact)
                         * (N_BUCKETS - max_exact)).astype(np.int64)
    large = np.minimum(large, N_BUCKETS - 1)
    return np.where(n < max_exact, n, large).astype(np.int32)


def _swa_static_buckets():
    k = np.arange(SWA_KEYS)[:, None]
    q = np.arange(BLOCK)[None, :]
    d = BLOCK + q - k
    bucket = np.where((d >= 0) & (d < WINDOW) & (k < 2 * BLOCK), _t5_bucket_np(d), -1)
    bucket = np.where(k >= 2 * BLOCK + PAD_FRONT, N_BUCKETS - 1, bucket)
    return np.ascontiguousarray(np.broadcast_to(bucket, (SWA_KEYS, BLOCK))).astype(np.int32)


def _swa_body(table_ref, sink_ref, bucket_ref, q_ref, k_ref, vt_ref, o_ref, bias_ref):
    g = pl.program_id(1)
    t = pl.program_id(2)
    cols = SWA_GROUP * BLOCK

    @pl.when((pl.program_id(0) == 0) & (t == 0))
    def _():
        bucket = bucket_ref[...]
        slot = lax.broadcasted_iota(jnp.int32, (SWA_KEYS, BLOCK), 0)
        qpos = lax.broadcasted_iota(jnp.int32, (SWA_KEYS, BLOCK), 1)
        for r in range(SWA_GROUP):
            bias = jnp.full((SWA_KEYS, BLOCK), NEG, F32)
            for bkt in range(N_BUCKETS):
                bias = jnp.where(bucket == bkt, table_ref[bkt, g * SWA_GROUP + r] * LOG2E, bias)
            bias_ref[g, 2, :, r * BLOCK:(r + 1) * BLOCK] = bias
            for n in range(2):
                band_ok = (slot >= PAD_FRONT - (n - 1) * BLOCK) & (slot < 2 * BLOCK)
                meta_ok = ((slot >= 2 * BLOCK + PAD_FRONT)
                           & (n * BLOCK + qpos - (slot - 2 * BLOCK) >= WINDOW))
                bias_ref[g, n, :, r * BLOCK:(r + 1) * BLOCK] = jnp.where(band_ok | meta_ok, bias, NEG)

    sink = jnp.concatenate(
        [jnp.full((1, BLOCK), sink_ref[g * SWA_GROUP + r] * LOG2E, F32) for r in range(SWA_GROUP)],
        axis=1)
    grp_rows = lax.broadcasted_iota(jnp.int32, (SWA_KVW, cols), 0) // HEAD_DIM
    v_rows = pl.ds(pl.multiple_of(g * HEAD_DIM, HEAD_DIM), HEAD_DIM)
    k_lead = k_ref[0:BLOCK, :]
    v_lead = vt_ref[v_rows, 0:BLOCK]

    n_blocks = q_ref.shape[1] // BLOCK

    def band(i):
        n = t * n_blocks + i
        return (n, pl.multiple_of(jnp.maximum(n - 1, 0) * BLOCK, BLOCK),
                pl.multiple_of(n * BLOCK, BLOCK))

    def scores(i):
        n, prev, cur = band(i)
        keys = jnp.concatenate(
            [k_ref[pl.ds(prev, BLOCK), :], k_ref[pl.ds(cur, BLOCK), :], k_lead], axis=0)
        q4 = jnp.concatenate(
            [q_ref[r * HEAD_DIM:(r + 1) * HEAD_DIM, i * BLOCK:(i + 1) * BLOCK]
             for r in range(SWA_GROUP)], axis=1)
        q8 = jnp.concatenate([q4, q4], axis=0)
        q8 = jnp.where(grp_rows == g, q8, jnp.zeros_like(q8))
        return _dot(keys, q8) + bias_ref[g, jnp.minimum(n, 2)]

    def finish(i, s):
        _, prev, cur = band(i)
        vals = jnp.concatenate(
            [vt_ref[v_rows, pl.ds(prev, BLOCK)], vt_ref[v_rows, pl.ds(cur, BLOCK)], v_lead], axis=1)
        vals = jnp.concatenate([vals, jnp.ones((V_ROWS - HEAD_DIM, SWA_KEYS), BF16)], axis=0)
        mx = jnp.maximum(jnp.max(s, axis=0, keepdims=True), sink)
        pv = _dot(vals, jnp.exp2(s - mx).astype(BF16))
        denom = pv[HEAD_DIM:HEAD_DIM + 1] + jnp.exp2(sink - mx)
        out = pv[:HEAD_DIM] * (1.0 / denom)
        for r in range(SWA_GROUP):
            o_ref[r * HEAD_DIM:(r + 1) * HEAD_DIM, i * BLOCK:(i + 1) * BLOCK] = (
                out[:, r * BLOCK:(r + 1) * BLOCK].astype(BF16))

    s = scores(0)
    for i in range(n_blocks):
        s_next = scores(i + 1) if i + 1 < n_blocks else None
        finish(i, s)
        s = s_next


def _swa(table, sinks, qst, ks, vst, batch):
    rows = ks.shape[0]
    p = rows // batch
    n_t = SWA_STEPS
    swa_tile = p // n_t
    width = SWA_GROUP * HEAD_DIM
    bucket = jnp.asarray(_swa_static_buckets())
    smem = pl.BlockSpec(memory_space=pltpu.SMEM)
    tile = pl.BlockSpec((width, swa_tile), lambda b, g, t: (g, b * n_t + t))
    return pl.pallas_call(
        _swa_body,
        grid=(batch, SWA_KV_HEADS, n_t),
        in_specs=[smem, smem, _const_spec((SWA_KEYS, BLOCK)), tile,
                  pl.BlockSpec((p, SWA_KVW), lambda b, g, t: (b, 0)),
                  pl.BlockSpec((SWA_KVW, p), lambda b, g, t: (0, b))],
        out_specs=tile,
        out_shape=jax.ShapeDtypeStruct((SWA_QW, rows), BF16),
        scratch_shapes=[pltpu.VMEM((SWA_KV_HEADS, 3, SWA_KEYS, SWA_GROUP * BLOCK), F32)],
        compiler_params=_params("arbitrary", "arbitrary", "arbitrary"),
        name="swa",
    )(table, sinks, bucket, qst, ks, vst)


def _mix_body(h_ref, of_ref, os_ref, ga_ref, gb_ref, wf_ref, ws_ref, wo_ref, o_ref):
    y = (ga_ref[...].astype(F32) * _dot(wf_ref[...], of_ref[...])
         + gb_ref[...].astype(F32) * _dot(ws_ref[...], os_ref[...]))
    o_ref[...] = h_ref[...] + _dot_tn(y.astype(BF16), wo_ref[...])


def _mix_out(h2, o_fox, o_swa, ga, gb, wf_t, ws_t, wo):
    rows, d = h2.shape
    tm = next(t for t in MIX_TILES if rows % t == 0)

    def feat(r):
        return pl.BlockSpec((r, tm), lambda i: (0, i))

    tok = pl.BlockSpec((tm, d), lambda i: (i, 0))
    return pl.pallas_call(
        _mix_body,
        grid=(rows // tm,),
        in_specs=[tok, feat(FOX_W), feat(SWA_QW), feat(d), feat(d),
                  _const_spec(wf_t.shape), _const_spec(ws_t.shape), _const_spec(wo.shape)],
        out_specs=tok,
        out_shape=jax.ShapeDtypeStruct((rows, d), F32),
        compiler_params=_params("arbitrary"),
        name="mix_out",
    )(h2, o_fox, o_swa, ga, gb, wf_t, ws_t, wo)


def _w_in_body(w_ref, wf_ref, wk_ref, *, d_model):
    src = np.cumsum([0, FOX_W, FOX_W, FOX_W, FOX_HEADS, SWA_QW, SWA_KVW, SWA_KVW, d_model])
    qa, ka, va, fa, qb, kb, vb, ga, gb = (int(s) for s in src)

    def move(dst, lo, n):
        wf_ref[dst:dst + n] = w_ref[lo:lo + n].astype(BF16)

    move(_R_QA, qa, FOX_W)
    move(_R_VA, va, FOX_W)
    move(_R_QB, qb, SWA_QW)
    move(_R_VB, vb, SWA_KVW)
    move(_R_GA, ga, d_model)
    move(_R_GA + d_model, gb, d_model)
    wf_ref[_R_FA:_R_FA + FA_ROWS] = jnp.concatenate(
        [w_ref[fa:fa + FOX_HEADS], jnp.zeros((FA_ROWS - FOX_HEADS, w_ref.shape[1]), F32)],
        axis=0).astype(BF16)
    wk_ref[:, 0:FOX_W] = w_ref[ka:ka + FOX_W].T.astype(BF16)
    wk_ref[:, FOX_W:FOX_W + SWA_KVW] = w_ref[kb:kb + SWA_KVW].T.astype(BF16)


def _prep_w_in(w_in, layer):
    w_t = jnp.swapaxes(w_in, 1, 2)
    _, d_in, d = w_t.shape
    n_feat = _R_GA + 2 * d
    return pl.pallas_call(
        functools.partial(_w_in_body, d_model=d),
        grid=(1,),
        in_specs=[pl.BlockSpec((None, d_in, d), lambda i: (layer, 0, 0), pipeline_mode=pl.Buffered(1))],
        out_specs=[pl.BlockSpec((n_feat, d), lambda i: (0, 0)),
                   pl.BlockSpec((d, FOX_W + SWA_KVW), lambda i: (0, 0))],
        out_shape=[jax.ShapeDtypeStruct((n_feat, d), BF16),
                   jax.ShapeDtypeStruct((d, FOX_W + SWA_KVW), BF16)],
        compiler_params=_params("arbitrary"),
        name="prep_w_in",
    )(w_t)


def kernel(x, meta_tokens, rel_bias_table, ffn1_norm, ffn1_w_in, ffn1_w_out, mix_norm, w_in, forget_bias, fox_q_norm, fox_k_norm, swa_q_norm, swa_k_norm, swa_sinks, w_branch_fox, w_branch_swa, w_out, ffn2_norm, ffn2_w_in, ffn2_w_out):
    b, seq, d = x.shape
    depth = w_in.shape[0]
    p = BLOCK + seq
    assert seq % FOX_TILE == 0 and p % (SWA_STEPS * BLOCK) == 0 and (b * p) % ROW_TILE == 0
    lead = jnp.concatenate([jnp.zeros((PAD_FRONT, d), x.dtype), meta_tokens.astype(x.dtype)], axis=0)
    h = x.reshape(b * seq, d)
    table = rel_bias_table.astype(F32)

    def col(v):
        return v.astype(F32)[:, None]

    def twice(v):
        return jnp.tile(v.astype(F32), 2)[None]

    for l in range(depth):
        h = _ffn(h, ffn1_norm[l][None], ffn1_w_in, ffn1_w_out, l,
                 lead=(lead, b, seq) if l == 0 else None)
        w_feat, w_keys = _prep_w_in(w_in, l)
        fb = jnp.pad(col(forget_bias[l]), ((0, FA_ROWS - FOX_HEADS), (0, 0)))
        qft, kf, vt, qst, ks, vst, ga, gb = _proj(
            h, p, mix_norm[l][None], w_feat, w_keys, fb,
            col(fox_q_norm[l]), twice(fox_k_norm[l]), col(swa_q_norm[l]), twice(swa_k_norm[l]))
        o_fox = _fox(_fox_logits_bounded(fox_q_norm[l], fox_k_norm[l]), qft, kf, vt, b)
        o_swa = _swa(table, swa_sinks[l].astype(F32), qst, ks, vst, b)
        h = _mix_out(h, o_fox, o_swa, ga, gb, w_branch_fox[l].astype(BF16).T,
                     w_branch_swa[l].astype(BF16).T, w_out[l].astype(BF16))
        last = l == depth - 1
        h = _ffn(h, ffn2_norm[l][None], ffn2_w_in, ffn2_w_out, l, (b, seq) if last else None)
    return h
```

```python
import functools
import math

import jax
import jax.numpy as jnp
import numpy as np
from jax import lax
from jax.experimental import pallas as pl
from jax.experimental.pallas import tpu as pltpu

F32 = jnp.float32
BF16 = jnp.bfloat16

N_META = 16
BLOCK = 128
PAD_FRONT = BLOCK - N_META
HEAD_DIM = 64
FOX_HEADS = 8
SWA_Q_HEADS = 8
SWA_KV_HEADS = 2
SWA_GROUP = SWA_Q_HEADS // SWA_KV_HEADS
WINDOW = 128
N_BUCKETS = 32
MAX_DISTANCE = 128
EPS = 1e-6
NEG = -1e30
FOX_W = FOX_HEADS * HEAD_DIM
SWA_QW = SWA_Q_HEADS * HEAD_DIM
SWA_KVW = SWA_KV_HEADS * HEAD_DIM
SCALE = HEAD_DIM ** -0.5
LOG2E = math.log2(math.e)

LANES = 128
VMEM_LIMIT = 56 * 1024 * 1024
ROW_TILE = 512
FFN_TILES = (1024, 768, 512)
MIX_TILES = (768, 512)
FFN_CHUNK = 256
FOX_TILE = 512
FOX_GROUP = 4
FOX_MAX_SPREAD = 100.0
AUG = LANES
AUG_ROWS = 32
V_ROWS = HEAD_DIM + 16
FA_ROWS = 16
SWA_KEYS = 3 * BLOCK
SWA_STEPS = 3

_R_QA = 0
_R_VA = _R_QA + FOX_W
_R_FA = _R_VA + FOX_W
_R_QB = _R_FA + FA_ROWS
_R_VB = _R_QB + SWA_QW
_R_GA = _R_VB + SWA_KVW


def _params(*sem):
    return pltpu.CompilerParams(dimension_semantics=sem, vmem_limit_bytes=VMEM_LIMIT)


def _const_spec(shape):
    nd = len(shape)
    return pl.BlockSpec(shape, lambda *_: (0,) * nd, pipeline_mode=pl.Buffered(1))


def _dot(a, b):
    return jnp.dot(a, b, preferred_element_type=F32)


def _dot_nt(a, b):
    return lax.dot_general(a, b, (((1,), (1,)), ((), ())), preferred_element_type=F32)


def _dot_tn(a, b):
    return lax.dot_general(a, b, (((0,), (0,)), ((), ())), preferred_element_type=F32)


def _rms_rows(x, g):
    ms = jnp.mean(x * x, axis=-1, keepdims=True)
    return x * lax.rsqrt(ms + EPS) * g


def _lead_blocks_before(block, blocks_per_seq):
    return (block + blocks_per_seq - 1) // blocks_per_seq


def _assemble_rows(x_ref, lead_ref, tile, blocks_per_seq):
    n_sub = x_ref.shape[0] // BLOCK
    first = tile * n_sub
    at = (blocks_per_seq - first % blocks_per_seq) % blocks_per_seq
    parts = []
    for j in range(n_sub):
        cur = x_ref[j * BLOCK:(j + 1) * BLOCK, :]
        if j > 0:
            cur = jnp.where(j > at, x_ref[(j - 1) * BLOCK:j * BLOCK, :], cur)
        parts.append(jnp.where(j == at, lead_ref[...], cur))
    return jnp.concatenate(parts, axis=0)


def _ffn_body(*refs, d_ff, n_pro, blocks_per_seq):
    if blocks_per_seq is None:
        (x_ref, g_ref, wg_ref, wu_ref, wout_ref, o_ref,
         win_bf, wout_bf, acc_ref, xn_ref) = refs
        lead_ref = None
    else:
        (x_ref, lead_ref, g_ref, wg_ref, wu_ref, wout_ref, o_ref,
         win_bf, wout_bf, acc_ref, xn_ref) = refs
    i = pl.program_id(0)

    def rows_of(tile):
        if blocks_per_seq is None:
            return x_ref[...]
        return _assemble_rows(x_ref, lead_ref, tile, blocks_per_seq)

    def chunk(xn, w_gate, w_up, w_down):
        gate = _dot(xn, w_gate)
        up = _dot(xn, w_up)
        return _dot((gate * jax.nn.sigmoid(gate) * up).astype(BF16), w_down)

    @pl.when(i == 0)
    def _():
        xn_ref[...] = _rms_rows(rows_of(0), g_ref[...]).astype(BF16)
        acc_ref[...] = jnp.zeros_like(acc_ref)

    @pl.when(i < n_pro)
    def _():
        lo = pl.multiple_of(i * FFN_CHUNK, FFN_CHUNK)
        w_gate, w_up, w_down = (r[...].astype(BF16) for r in (wg_ref, wu_ref, wout_ref))
        win_bf[:, pl.ds(lo, FFN_CHUNK)] = w_gate
        win_bf[:, pl.ds(pl.multiple_of(d_ff + lo, LANES), FFN_CHUNK)] = w_up
        wout_bf[pl.ds(lo, FFN_CHUNK), :] = w_down
        acc_ref[...] += chunk(xn_ref[...], w_gate, w_up, w_down)

    @pl.when(i == n_pro - 1)
    def _():
        o_ref[...] = rows_of(0) + 0.5 * acc_ref[...]

    @pl.when(i >= n_pro)
    def _():
        x = rows_of(i - n_pro + 1)
        xn = _rms_rows(x, g_ref[...]).astype(BF16)
        for c in range(d_ff // FFN_CHUNK):
            lo = c * FFN_CHUNK
            part = chunk(xn, win_bf[:, lo:lo + FFN_CHUNK],
                         win_bf[:, d_ff + lo:d_ff + lo + FFN_CHUNK], wout_bf[lo:lo + FFN_CHUNK, :])
            if c == 0:
                acc_ref[...] = part
            else:
                acc_ref[...] += part
        o_ref[...] = x + 0.5 * acc_ref[...]


def _ffn(h2, gain, w_in, w_out, layer, real_only=None, lead=None):
    rows, d = h2.shape
    d_ff = w_out.shape[1]
    if lead is not None:
        walked = lead[1] * (lead[2] + BLOCK)
    else:
        walked = rows if real_only is None else real_only[1]
    tm = next(t for t in FFN_TILES if walked % t == 0)
    n_pro = d_ff // FFN_CHUNK
    blocks_per_seq = None
    extra_in, extra_specs = (), ()

    if lead is not None:
        lead_rows, batch, seq = lead
        blocks_per_seq = seq // BLOCK + 1
        rows = batch * blocks_per_seq * BLOCK
        n_tiles = rows // tm

        def x_map(t):
            first = t * (tm // BLOCK)
            start = first - _lead_blocks_before(first, blocks_per_seq)
            return (pl.multiple_of(start * BLOCK, BLOCK), 0)

        def out_map(t):
            return (t, 0)

        x_block = (pl.Element(tm), pl.Element(d))
        extra_in, extra_specs = (lead_rows,), (_const_spec((BLOCK, d)),)
        out_block, out_shape = (tm, d), jax.ShapeDtypeStruct((rows, d), F32)
    elif real_only is None:
        n_tiles = rows // tm

        def x_map(t):
            return (t, 0)

        out_map = x_map
        x_block = out_block = (tm, d)
        out_shape = jax.ShapeDtypeStruct((rows, d), F32)
    else:
        batch, seq = real_only
        per_seq = seq // tm
        n_tiles = batch * per_seq

        def x_map(t):
            start = (t // per_seq) * (rows // batch) + BLOCK + (t % per_seq) * tm
            return (pl.multiple_of(start, BLOCK), 0)

        def out_map(t):
            return (t // per_seq, t % per_seq, 0)

        x_block = (pl.Element(tm), pl.Element(d))
        out_block, out_shape = (None, tm, d), jax.ShapeDtypeStruct((batch, seq, d), F32)

    def tile_of(i):
        return jnp.maximum(i - n_pro + 1, 0)

    def chunk_of(i):
        return jnp.minimum(i, n_pro - 1)

    return pl.pallas_call(
        functools.partial(_ffn_body, d_ff=d_ff, n_pro=n_pro, blocks_per_seq=blocks_per_seq),
        grid=(n_pro + n_tiles - 1,),
        in_specs=[
            pl.BlockSpec(x_block, lambda i: x_map(tile_of(i))),
            *extra_specs,
            _const_spec((1, d)),
            pl.BlockSpec((None, d, FFN_CHUNK), lambda i: (layer, 0, chunk_of(i))),
            pl.BlockSpec((None, d, FFN_CHUNK), lambda i: (layer, 0, n_pro + chunk_of(i))),
            pl.BlockSpec((None, FFN_CHUNK, d), lambda i: (layer, chunk_of(i), 0)),
        ],
        out_specs=pl.BlockSpec(out_block, lambda i: out_map(tile_of(i))),
        out_shape=out_shape,
        scratch_shapes=[pltpu.VMEM((d, 2 * d_ff), BF16), pltpu.VMEM((d_ff, d), BF16),
                        pltpu.VMEM((tm, d), F32), pltpu.VMEM((tm, d), BF16)],
        compiler_params=_params("arbitrary"),
        name="ffn",
    )(h2, *extra_in, gain, w_in, w_in, w_out)


def _split3(x):
    hi = x.astype(BF16).astype(F32)
    r = x - hi
    mid = r.astype(BF16).astype(F32)
    return hi, mid, r - mid


def _head_rms_cols(x, g):
    ms = jnp.mean(x * x, axis=0, keepdims=True)
    return x * lax.rsqrt(ms + EPS) * g


def _pair_rms_rows(x, g2):
    lane = lax.broadcasted_iota(jnp.int32, x.shape, 1)
    sq = x * x
    first = lane < HEAD_DIM
    s0 = jnp.sum(jnp.where(first, sq, 0.0), axis=-1, keepdims=True)
    s1 = jnp.sum(jnp.where(first, 0.0, sq), axis=-1, keepdims=True)
    ms = jnp.where(first, s0, s1) * (1.0 / HEAD_DIM)
    return x * lax.rsqrt(ms + EPS) * g2


def _proj_body(shift_ref, x_ref, xnext_ref, g_ref, wt_ref, wk_ref, fb_ref, nfq_ref, nfk_ref, nsq_ref, nsk_ref,
               qf_ref, kf_ref, vt_ref, qs_ref, ks_ref, vs_ref, ga_ref, gb_ref,
               carry_ref, tri_ref, xn_ref, *, d_model, seq_pad):
    i = pl.program_id(0)
    tm = x_ref.shape[0]

    @pl.when(i == 0)
    def _():
        src = lax.broadcasted_iota(jnp.int32, (tm, tm), 0)
        dst = lax.broadcasted_iota(jnp.int32, (tm, tm), 1)
        tri_ref[...] = (src <= dst).astype(BF16)
        xn_ref[...] = _rms_rows(x_ref[...], g_ref[...]).astype(BF16)

    xn = xn_ref[...]

    def feat(lo, rows):
        return _dot_nt(wt_ref[lo:lo + rows, :], xn)

    va_fa = feat(_R_VA, FOX_W + FA_ROWS)
    lf = jax.nn.log_sigmoid(va_fa[FOX_W:] + fb_ref[...])
    va = va_fa[:FOX_W].astype(BF16)
    ones = jnp.ones((V_ROWS - HEAD_DIM, tm), BF16)
    for h in range(FOX_HEADS):
        vt_ref[h * V_ROWS:(h + 1) * V_ROWS, :] = jnp.concatenate(
            [va[h * HEAD_DIM:(h + 1) * HEAD_DIM], ones], axis=0)
    r0 = i * tm
    boundary = (r0 // seq_pad + 1) * seq_pad - r0
    hi, mid, lo3 = _split3(lf)
    upper = tri_ref[...]
    c_loc = (_dot(hi.astype(BF16), upper) + _dot(mid.astype(BF16), upper)
             + _dot(lo3.astype(BF16), upper))
    carry = jnp.where(r0 % seq_pad == 0, 0.0, carry_ref[:, 0:1])
    tok = lax.broadcasted_iota(jnp.int32, (1, tm), 1)
    c_end = jnp.sum(jnp.where(tok == boundary - 1, c_loc, 0.0), axis=1, keepdims=True)
    c_t = c_loc + jnp.where(tok < boundary, carry, -c_end)
    carry_ref[...] = jnp.broadcast_to(c_t[:, tm - 1:tm], carry_ref.shape)
    c_t = c_t * LOG2E

    c_pad = jnp.concatenate(
        [jnp.where(lax.broadcasted_iota(jnp.int32, (FA_ROWS, tm), 0) < FOX_HEADS, c_t, 0.0),
         jnp.zeros((LANES - FA_ROWS, tm), F32)], axis=0)
    c_tok = c_pad.T
    lane = lax.broadcasted_iota(jnp.int32, (tm, LANES), 1)
    k_hi, k_mid, k_lo = _split3(c_tok)
    aug_lo = (-k_hi - pltpu.roll(k_mid, FOX_HEADS, 1) - pltpu.roll(k_lo, 2 * FOX_HEADS, 1)
              + jnp.where((lane >= 3 * FOX_HEADS) & (lane < 3 * FOX_HEADS + 3), 1.0, 0.0)
              + jnp.where(lane == 3 * FOX_HEADS + 3, -shift_ref[0], 0.0))
    aug_hi = pltpu.roll(aug_lo, HEAD_DIM, 1)
    first = lane < HEAD_DIM

    kt = _dot(xn, wk_ref[...])
    for j in range(FOX_HEADS // 2):
        kn = _pair_rms_rows(kt[:, j * LANES:(j + 1) * LANES], nfk_ref[...])
        kf_ref[:, (2 * j) * AUG:(2 * j + 1) * AUG] = jnp.where(first, kn, aug_hi).astype(BF16)
        kf_ref[:, (2 * j + 1) * AUG:(2 * j + 2) * AUG] = jnp.where(first, aug_lo, kn).astype(BF16)
    ks_ref[...] = _pair_rms_rows(kt[:, FOX_W:FOX_W + SWA_KVW], nsk_ref[...]).astype(BF16)

    arow = lax.broadcasted_iota(jnp.int32, (AUG_ROWS, tm), 0)
    qa = feat(_R_QA, FOX_W)
    zeros = jnp.zeros((HEAD_DIM - AUG_ROWS, tm), BF16)
    for h in range(FOX_HEADS):
        q_hi, q_mid, q_lo = _split3(c_t[h:h + 1, :])
        onehot = jnp.where(((arow % FOX_HEADS == h) & (arow < 3 * FOX_HEADS))
                           | (arow == 3 * FOX_HEADS + 3), 1.0, 0.0)
        aug_q = jnp.where(arow == 3 * FOX_HEADS, q_hi, jnp.where(
            arow == 3 * FOX_HEADS + 1, q_mid, jnp.where(arow == 3 * FOX_HEADS + 2, q_lo, onehot)))
        qn = (_head_rms_cols(qa[h * HEAD_DIM:(h + 1) * HEAD_DIM], nfq_ref[...])
              * (SCALE * LOG2E)).astype(BF16)
        parts = [qn, aug_q.astype(BF16), zeros] if h % 2 == 0 else [aug_q.astype(BF16), zeros, qn]
        qf_ref[h * AUG:(h + 1) * AUG, :] = jnp.concatenate(parts, axis=0)
    ga_ref[...] = jax.nn.sigmoid(feat(_R_GA, d_model)).astype(BF16)
    gb_ref[...] = jax.nn.sigmoid(feat(_R_GA + d_model, d_model)).astype(BF16)
    qb = feat(_R_QB, SWA_QW)
    for h in range(SWA_Q_HEADS):
        sl = slice(h * HEAD_DIM, (h + 1) * HEAD_DIM)
        qs_ref[sl, :] = (_head_rms_cols(qb[sl], nsq_ref[...]) * (SCALE * LOG2E)).astype(BF16)
    vs_ref[...] = feat(_R_VB, SWA_KVW).astype(BF16)
    xn_ref[...] = _rms_rows(xnext_ref[...], g_ref[...]).astype(BF16)


def _proj(shift, h2, seq_pad, gain, w_feat, w_keys, fb, nfq, nfk, nsq, nsk):
    rows, d = h2.shape
    tm = ROW_TILE

    def tok(w):
        return pl.BlockSpec((tm, w), lambda i: (i, 0)), jax.ShapeDtypeStruct((rows, w), BF16)

    def feat(r):
        return pl.BlockSpec((r, tm), lambda i: (0, i)), jax.ShapeDtypeStruct((r, rows), BF16)

    outs = [feat(FOX_HEADS * AUG), tok(FOX_HEADS * AUG), feat(FOX_HEADS * V_ROWS), feat(SWA_QW),
            tok(SWA_KVW), feat(SWA_KVW), feat(d), feat(d)]
    return pl.pallas_call(
        functools.partial(_proj_body, d_model=d, seq_pad=seq_pad),
        grid=(rows // tm,),
        in_specs=[
            pl.BlockSpec(memory_space=pltpu.SMEM),
            _const_spec((tm, d)),
            pl.BlockSpec((tm, d), lambda i: (jnp.minimum(i + 1, rows // tm - 1), 0)),
            _const_spec((1, d)),
            _const_spec(w_feat.shape),
            _const_spec(w_keys.shape),
            _const_spec((FA_ROWS, 1)),
            _const_spec((HEAD_DIM, 1)), _const_spec((1, LANES)),
            _const_spec((HEAD_DIM, 1)), _const_spec((1, LANES)),
        ],
        out_specs=[o[0] for o in outs],
        out_shape=[o[1] for o in outs],
        scratch_shapes=[pltpu.VMEM((FA_ROWS, LANES), F32), pltpu.VMEM((tm, tm), BF16),
                        pltpu.VMEM((tm, d), BF16)],
        compiler_params=_params("arbitrary"),
        name="proj",
    )(shift, h2, h2, gain, w_feat, w_keys, fb, nfq, nfk, nsq, nsk)


def _fox_body(bounded_ref, q_ref, k_ref, vt_ref, o_ref, m_ref, acc_ref, st_ref):
    def attend(q_start, tq, n_full, mode):
        q = q_ref[:, pl.ds(q_start, tq)]
        m_ref[:, :, :tq] = jnp.full((FOX_GROUP, 1, tq), NEG, F32)
        acc_ref[:, :tq] = jnp.zeros((FOX_GROUP * V_ROWS, tq), F32)

        everyone = slice(0, tq)

        def scores(kblk, h, cols=everyone):
            return _dot(kblk[:, h * AUG:(h + 1) * AUG], q[h * AUG:(h + 1) * AUG, cols])

        def softmax(h, st, mask, cols):
            if mask is not None:
                st = jnp.where(mask, st, NEG)
            m_prev = m_ref[h, :, cols]
            m_new = jnp.maximum(m_prev, jnp.max(st, axis=0, keepdims=True))
            alpha = jnp.exp2(m_prev - m_new)
            m_ref[h, :, cols] = m_new
            return jnp.exp2(st - m_new).astype(BF16), alpha

        def shifted_probs(st, mask):
            if mask is not None:
                st = jnp.where(mask, st, NEG)
            return jnp.exp2(st).astype(BF16), None

        def weighted_values(h, vt, p_alpha, cols):
            p, alpha = p_alpha
            rows = slice(h * V_ROWS, (h + 1) * V_ROWS)
            if alpha is None:
                acc_ref[rows, cols] += _dot(vt[rows, :], p)
            else:
                acc_ref[rows, cols] = alpha * acc_ref[rows, cols] + _dot(vt[rows, :], p)

        def run_heads(kblk, vt, st0, mask, cols=everyone, next_scores=None, fixed=False):
            def probs(h, st):
                return shifted_probs(st, mask) if fixed else softmax(h, st, mask, cols)

            st = [st0, scores(kblk, 1, cols), None, None]
            pa = probs(0, st[0])
            for h in range(FOX_GROUP):
                if h + 2 < FOX_GROUP:
                    st[h + 2] = scores(kblk, h + 2, cols)
                elif h + 2 == FOX_GROUP and next_scores is not None:
                    st_ref[...] = next_scores()
                weighted_values(h, vt, pa, cols)
                if h + 1 < FOX_GROUP:
                    pa = probs(h + 1, st[h + 1])

        def key_tile(j):
            return k_ref[pl.ds(pl.multiple_of(BLOCK + j * FOX_TILE, BLOCK), FOX_TILE), :]

        def full_tile(j, fixed):
            vt = vt_ref[:, pl.ds(pl.multiple_of(BLOCK + j * FOX_TILE, BLOCK), FOX_TILE)]
            run_heads(key_tile(j), vt, st_ref[...], None, fixed=fixed,
                      next_scores=lambda: scores(key_tile(j + 1), 0))

        def full_tiles(fixed):
            def tile_pair(jj, carry):
                full_tile(2 * jj, fixed)
                full_tile(2 * jj + 1, fixed)
                return carry

            lax.fori_loop(0, n_full // 2, tile_pair, 0)

            @pl.when(n_full % 2 == 1)
            def _():
                full_tile(n_full - 1, fixed)

        def diagonal_tile(st_diag0, next_scores, fixed):
            kr = lax.broadcasted_iota(jnp.int32, (BLOCK + tq, tq), 0)
            qc = lax.broadcasted_iota(jnp.int32, (BLOCK + tq, tq), 1)
            visible = ((kr >= PAD_FRONT) & (kr < BLOCK)) | ((kr >= BLOCK) & (kr - BLOCK <= qc))
            run_heads(jnp.concatenate([k_lead, k_ref[pl.ds(q_start, tq), :]], axis=0),
                      jnp.concatenate([v_lead, vt_ref[:, pl.ds(q_start, tq)]], axis=1),
                      jnp.concatenate([scores(k_lead, 0), st_diag0], axis=0), visible,
                      next_scores=next_scores, fixed=fixed)

        k_lead = k_ref[0:BLOCK, :]
        v_lead = vt_ref[:, 0:BLOCK]
        if mode in ("online", "shifted"):
            st_ref[...] = scores(key_tile(0), 0)
            full_tiles(mode == "shifted")
            diagonal_tile(st_ref[...], None, mode == "shifted")
        else:
            kpos0 = lax.broadcasted_iota(jnp.int32, (BLOCK, tq), 0)
            qc = lax.broadcasted_iota(jnp.int32, (BLOCK, tq), 1)
            visible = (kpos0 >= PAD_FRONT) & (kpos0 <= qc)
            for h in range(FOX_GROUP):
                weighted_values(h, v_lead, softmax(h, scores(k_lead, h), visible, everyone), everyone)
        for h in range(FOX_GROUP):
            denom = acc_ref[h * V_ROWS + HEAD_DIM:h * V_ROWS + HEAD_DIM + 1, :tq]
            o_ref[h * HEAD_DIM:(h + 1) * HEAD_DIM, pl.ds(q_start, tq)] = (
                acc_ref[h * V_ROWS:h * V_ROWS + HEAD_DIM, :tq] * (1.0 / denom)).astype(BF16)

    attend(0, BLOCK, 0, "lead")

    def query_tile(i, carry):
        q_start = pl.multiple_of(BLOCK + i * FOX_TILE, BLOCK)

        @pl.when(bounded_ref[0] == 1)
        def _():
            attend(q_start, FOX_TILE, i, "shifted")

        @pl.when(bounded_ref[0] != 1)
        def _():
            attend(q_start, FOX_TILE, i, "online")

        return carry

    lax.fori_loop(0, (q_ref.shape[1] - BLOCK) // FOX_TILE, query_tile, 0)


def _fox_logit_bound(q_gain, k_gain):
    bound = math.sqrt(HEAD_DIM) * LOG2E * jnp.max(jnp.abs(q_gain)) * jnp.max(jnp.abs(k_gain))
    ok = 2.0 * bound < FOX_MAX_SPREAD
    return ok.astype(jnp.int32).reshape(1), jnp.where(ok, bound, 0.0).astype(F32).reshape(1)


def _fox(bounded, qft, kf, vt, batch):
    rows = kf.shape[0]
    p = rows // batch
    n_grp = FOX_HEADS // FOX_GROUP
    width = FOX_GROUP * HEAD_DIM

    def feat(r):
        return pl.BlockSpec((r, p), lambda b, g: (g, b))

    return pl.pallas_call(
        _fox_body,
        grid=(batch, n_grp),
        in_specs=[pl.BlockSpec(memory_space=pltpu.SMEM),
                  feat(FOX_GROUP * AUG),
                  pl.BlockSpec((p, FOX_GROUP * AUG), lambda b, g: (b, g)),
                  feat(FOX_GROUP * V_ROWS)],
        out_specs=feat(width),
        out_shape=jax.ShapeDtypeStruct((FOX_W, rows), BF16),
        scratch_shapes=[
            pltpu.VMEM((FOX_GROUP, 1, FOX_TILE), F32),
            pltpu.VMEM((FOX_GROUP * V_ROWS, FOX_TILE), F32),
            pltpu.VMEM((FOX_TILE, FOX_TILE), F32),
        ],
        compiler_params=_params("arbitrary", "arbitrary"),
        name="fox",
    )(bounded, qft, kf, vt)


def _t5_bucket_np(dist):
    n = np.maximum(dist, 0)
    max_exact = N_BUCKETS // 2
    nf = np.maximum(n, 1).astype(np.float64)
    large = max_exact + (np.log(nf / max_exact) / math.log(MAX_DISTANCE / max_exact)
                         * (N_BUCKETS - max_exact)).astype(np.int64)
    large = np.minimum(large, N_BUCKETS - 1)
    return np.where(n < max_exact, n, large).astype(np.int32)


def _swa_static_buckets():
    k = np.arange(SWA_KEYS)[:, None]
    q = np.arange(BLOCK)[None, :]
    d = BLOCK + q - k
    bucket = np.where((d >= 0) & (d < WINDOW) & (k < 2 * BLOCK), _t5_bucket_np(d), -1)
    bucket = np.where(k >= 2 * BLOCK + PAD_FRONT, N_BUCKETS - 1, bucket)
    return np.ascontiguousarray(np.broadcast_to(bucket, (SWA_KEYS, BLOCK))).astype(np.int32)


def _swa_body(table_ref, sink_ref, bucket_ref, q_ref, k_ref, vt_ref, o_ref, bias_ref):
    g = pl.program_id(1)
    t = pl.program_id(2)
    cols = SWA_GROUP * BLOCK

    @pl.when((pl.program_id(0) == 0) & (t == 0))
    def _():
        bucket = bucket_ref[...]
        slot = lax.broadcasted_iota(jnp.int32, (SWA_KEYS, BLOCK), 0)
        qpos = lax.broadcasted_iota(jnp.int32, (SWA_KEYS, BLOCK), 1)
        for r in range(SWA_GROUP):
            bias = jnp.full((SWA_KEYS, BLOCK), NEG, F32)
            for bkt in range(N_BUCKETS):
                bias = jnp.where(bucket == bkt, table_ref[bkt, g * SWA_GROUP + r] * LOG2E, bias)
            bias_ref[g, 2, :, r * BLOCK:(r + 1) * BLOCK] = bias
            for n in range(2):
                band_ok = (slot >= PAD_FRONT - (n - 1) * BLOCK) & (slot < 2 * BLOCK)
                meta_ok = ((slot >= 2 * BLOCK + PAD_FRONT)
                           & (n * BLOCK + qpos - (slot - 2 * BLOCK) >= WINDOW))
                bias_ref[g, n, :, r * BLOCK:(r + 1) * BLOCK] = jnp.where(band_ok | meta_ok, bias, NEG)

    sink = jnp.concatenate(
        [jnp.full((1, BLOCK), sink_ref[g * SWA_GROUP + r] * LOG2E, F32) for r in range(SWA_GROUP)],
        axis=1)
    grp_rows = lax.broadcasted_iota(jnp.int32, (SWA_KVW, cols), 0) // HEAD_DIM
    v_rows = pl.ds(pl.multiple_of(g * HEAD_DIM, HEAD_DIM), HEAD_DIM)
    k_lead = k_ref[0:BLOCK, :]
    v_lead = vt_ref[v_rows, 0:BLOCK]

    n_blocks = q_ref.shape[1] // BLOCK

    def band(i):
        n = t * n_blocks + i
        return (n, pl.multiple_of(jnp.maximum(n - 1, 0) * BLOCK, BLOCK),
                pl.multiple_of(n * BLOCK, BLOCK))

    def scores(i):
        n, prev, cur = band(i)
        keys = jnp.concatenate(
            [k_ref[pl.ds(prev, BLOCK), :], k_ref[pl.ds(cur, BLOCK), :], k_lead], axis=0)
        q4 = jnp.concatenate(
            [q_ref[r * HEAD_DIM:(r + 1) * HEAD_DIM, i * BLOCK:(i + 1) * BLOCK]
             for r in range(SWA_GROUP)], axis=1)
        q8 = jnp.concatenate([q4, q4], axis=0)
        q8 = jnp.where(grp_rows == g, q8, jnp.zeros_like(q8))
        return _dot(keys, q8) + bias_ref[g, jnp.minimum(n, 2)]

    def finish(i, s):
        _, prev, cur = band(i)
        vals = jnp.concatenate(
            [vt_ref[v_rows, pl.ds(prev, BLOCK)], vt_ref[v_rows, pl.ds(cur, BLOCK)], v_lead], axis=1)
        vals = jnp.concatenate([vals, jnp.ones((V_ROWS - HEAD_DIM, SWA_KEYS), BF16)], axis=0)
        mx = jnp.maximum(jnp.max(s, axis=0, keepdims=True), sink)
        pv = _dot(vals, jnp.exp2(s - mx).astype(BF16))
        denom = pv[HEAD_DIM:HEAD_DIM + 1] + jnp.exp2(sink - mx)
        out = pv[:HEAD_DIM] * (1.0 / denom)
        for r in range(SWA_GROUP):
            o_ref[r * HEAD_DIM:(r + 1) * HEAD_DIM, i * BLOCK:(i + 1) * BLOCK] = (
                out[:, r * BLOCK:(r + 1) * BLOCK].astype(BF16))

    s = scores(0)
    for i in range(n_blocks):
        s_next = scores(i + 1) if i + 1 < n_blocks else None
        finish(i, s)
        s = s_next


def _swa(table, sinks, qst, ks, vst, batch):
    rows = ks.shape[0]
    p = rows // batch
    n_t = SWA_STEPS
    swa_tile = p // n_t
    width = SWA_GROUP * HEAD_DIM
    bucket = jnp.asarray(_swa_static_buckets())
    smem = pl.BlockSpec(memory_space=pltpu.SMEM)
    tile = pl.BlockSpec((width, swa_tile), lambda b, g, t: (g, b * n_t + t))
    return pl.pallas_call(
        _swa_body,
        grid=(batch, SWA_KV_HEADS, n_t),
        in_specs=[smem, smem, _const_spec((SWA_KEYS, BLOCK)), tile,
                  pl.BlockSpec((p, SWA_KVW), lambda b, g, t: (b, 0)),
                  pl.BlockSpec((SWA_KVW, p), lambda b, g, t: (0, b))],
        out_specs=tile,
        out_shape=jax.ShapeDtypeStruct((SWA_QW, rows), BF16),
        scratch_shapes=[pltpu.VMEM((SWA_KV_HEADS, 3, SWA_KEYS, SWA_GROUP * BLOCK), F32)],
        compiler_params=_params("arbitrary", "arbitrary", "arbitrary"),
        name="swa",
    )(table, sinks, bucket, qst, ks, vst)


def _mix_body(h_ref, of_ref, os_ref, ga_ref, gb_ref, wf_ref, ws_ref, wo_ref, o_ref):
    y = (ga_ref[...].astype(F32) * _dot(wf_ref[...], of_ref[...])
         + gb_ref[...].astype(F32) * _dot(ws_ref[...], os_ref[...]))
    o_ref[...] = h_ref[...] + _dot_tn(y.astype(BF16), wo_ref[...])


def _mix_out(h2, o_fox, o_swa, ga, gb, wf_t, ws_t, wo):
    rows, d = h2.shape
    tm = next(t for t in MIX_TILES if rows % t == 0)

    def feat(r):
        return pl.BlockSpec((r, tm), lambda i: (0, i))

    tok = pl.BlockSpec((tm, d), lambda i: (i, 0))
    return pl.pallas_call(
        _mix_body,
        grid=(rows // tm,),
        in_specs=[tok, feat(FOX_W), feat(SWA_QW), feat(d), feat(d),
                  _const_spec(wf_t.shape), _const_spec(ws_t.shape), _const_spec(wo.shape)],
        out_specs=tok,
        out_shape=jax.ShapeDtypeStruct((rows, d), F32),
        compiler_params=_params("arbitrary"),
        name="mix_out",
    )(h2, o_fox, o_swa, ga, gb, wf_t, ws_t, wo)


def _w_in_body(w_ref, wf_ref, wk_ref, *, d_model):
    src = np.cumsum([0, FOX_W, FOX_W, FOX_W, FOX_HEADS, SWA_QW, SWA_KVW, SWA_KVW, d_model])
    qa, ka, va, fa, qb, kb, vb, ga, gb = (int(s) for s in src)

    def move(dst, lo, n):
        wf_ref[dst:dst + n] = w_ref[lo:lo + n].astype(BF16)

    move(_R_QA, qa, FOX_W)
    move(_R_VA, va, FOX_W)
    move(_R_QB, qb, SWA_QW)
    move(_R_VB, vb, SWA_KVW)
    move(_R_GA, ga, d_model)
    move(_R_GA + d_model, gb, d_model)
    wf_ref[_R_FA:_R_FA + FA_ROWS] = jnp.concatenate(
        [w_ref[fa:fa + FOX_HEADS], jnp.zeros((FA_ROWS - FOX_HEADS, w_ref.shape[1]), F32)],
        axis=0).astype(BF16)
    wk_ref[:, 0:FOX_W] = w_ref[ka:ka + FOX_W].T.astype(BF16)
    wk_ref[:, FOX_W:FOX_W + SWA_KVW] = w_ref[kb:kb + SWA_KVW].T.astype(BF16)


def _prep_w_in(w_in, layer):
    w_t = jnp.swapaxes(w_in, 1, 2)
    _, d_in, d = w_t.shape
    n_feat = _R_GA + 2 * d
    return pl.pallas_call(
        functools.partial(_w_in_body, d_model=d),
        grid=(1,),
        in_specs=[pl.BlockSpec((None, d_in, d), lambda i: (layer, 0, 0), pipeline_mode=pl.Buffered(1))],
        out_specs=[pl.BlockSpec((n_feat, d), lambda i: (0, 0)),
                   pl.BlockSpec((d, FOX_W + SWA_KVW), lambda i: (0, 0))],
        out_shape=[jax.ShapeDtypeStruct((n_feat, d), BF16),
                   jax.ShapeDtypeStruct((d, FOX_W + SWA_KVW), BF16)],
        compiler_params=_params("arbitrary"),
        name="prep_w_in",
    )(w_t)


def kernel(x, meta_tokens, rel_bias_table, ffn1_norm, ffn1_w_in, ffn1_w_out, mix_norm, w_in, forget_bias, fox_q_norm, fox_k_norm, swa_q_norm, swa_k_norm, swa_sinks, w_branch_fox, w_branch_swa, w_out, ffn2_norm, ffn2_w_in, ffn2_w_out):
    b, seq, d = x.shape
    depth = w_in.shape[0]
    p = BLOCK + seq
    assert seq % FOX_TILE == 0 and p % (SWA_STEPS * BLOCK) == 0 and (b * p) % ROW_TILE == 0
    lead = jnp.concatenate([jnp.zeros((PAD_FRONT, d), x.dtype), meta_tokens.astype(x.dtype)], axis=0)
    h = x.reshape(b * seq, d)
    table = rel_bias_table.astype(F32)

    def col(v):
        return v.astype(F32)[:, None]

    def twice(v):
        return jnp.tile(v.astype(F32), 2)[None]

    for l in range(depth):
        h = _ffn(h, ffn1_norm[l][None], ffn1_w_in, ffn1_w_out, l,
                 lead=(lead, b, seq) if l == 0 else None)
        w_feat, w_keys = _prep_w_in(w_in, l)
        fb = jnp.pad(col(forget_bias[l]), ((0, FA_ROWS - FOX_HEADS), (0, 0)))
        bounded, shift = _fox_logit_bound(fox_q_norm[l], fox_k_norm[l])
        qft, kf, vt, qst, ks, vst, ga, gb = _proj(
            shift, h, p, mix_norm[l][None], w_feat, w_keys, fb,
            col(fox_q_norm[l]), twice(fox_k_norm[l]), col(swa_q_norm[l]), twice(swa_k_norm[l]))
        o_fox = _fox(bounded, qft, kf, vt, b)
        o_swa = _swa(table, swa_sinks[l].astype(F32), qst, ks, vst, b)
        h = _mix_out(h, o_fox, o_swa, ga, gb, w_branch_fox[l].astype(BF16).T,
                     w_branch_swa[l].astype(BF16).T, w_out[l].astype(BF16))
        last = l == depth - 1
        h = _ffn(h, ffn2_norm[l][None], ffn2_w_in, ffn2_w_out, l, (b, seq) if last else None)
    return h
```

```python
import functools
import math

import jax
import jax.numpy as jnp
import numpy as np
from jax import lax
from jax.experimental import pallas as pl
from jax.experimental.pallas import tpu as pltpu

F32 = jnp.float32
BF16 = jnp.bfloat16

N_META = 16
BLOCK = 128
PAD_FRONT = BLOCK - N_META
HEAD_DIM = 64
FOX_HEADS = 8
SWA_Q_HEADS = 8
SWA_KV_HEADS = 2
SWA_GROUP = SWA_Q_HEADS // SWA_KV_HEADS
WINDOW = 128
N_BUCKETS = 32
MAX_DISTANCE = 128
EPS = 1e-6
NEG = -1e30
FOX_W = FOX_HEADS * HEAD_DIM
SWA_QW = SWA_Q_HEADS * HEAD_DIM
SWA_KVW = SWA_KV_HEADS * HEAD_DIM
SCALE = HEAD_DIM ** -0.5
LOG2E = math.log2(math.e)

LANES = 128
VMEM_LIMIT = 56 * 1024 * 1024
ROW_TILE = 512
FFN_TILES = (1024, 768, 512)
MIX_TILES = (768, 512)
FFN_CHUNK = 256
FOX_TILE = 512
FOX_GROUP = 4
FOX_MAX_SPREAD = 100.0
AUG = LANES
AUG_ROWS = 32
V_ROWS = HEAD_DIM + 16
FA_ROWS = 16
SWA_KEYS = 3 * BLOCK
SWA_STEPS = 3

_R_QA = 0
_R_VA = _R_QA + FOX_W
_R_FA = _R_VA + FOX_W
_R_QB = _R_FA + FA_ROWS
_R_VB = _R_QB + SWA_QW
_R_GA = _R_VB + SWA_KVW


def _params(*sem):
    return pltpu.CompilerParams(dimension_semantics=sem, vmem_limit_bytes=VMEM_LIMIT)


def _const_spec(shape):
    nd = len(shape)
    return pl.BlockSpec(shape, lambda *_: (0,) * nd, pipeline_mode=pl.Buffered(1))


def _dot(a, b):
    return jnp.dot(a, b, preferred_element_type=F32)


def _dot_nt(a, b):
    return lax.dot_general(a, b, (((1,), (1,)), ((), ())), preferred_element_type=F32)


def _dot_tn(a, b):
    return lax.dot_general(a, b, (((0,), (0,)), ((), ())), preferred_element_type=F32)


def _rms_rows(x, g):
    ms = jnp.mean(x * x, axis=-1, keepdims=True)
    return x * lax.rsqrt(ms + EPS) * g


def _lead_blocks_before(block, blocks_per_seq):
    return (block + blocks_per_seq - 1) // blocks_per_seq


def _assemble_rows(x_ref, lead_ref, tile, blocks_per_seq):
    n_sub = x_ref.shape[0] // BLOCK
    first = tile * n_sub
    at = (blocks_per_seq - first % blocks_per_seq) % blocks_per_seq
    parts = []
    for j in range(n_sub):
        cur = x_ref[j * BLOCK:(j + 1) * BLOCK, :]
        if j > 0:
            cur = jnp.where(j > at, x_ref[(j - 1) * BLOCK:j * BLOCK, :], cur)
        parts.append(jnp.where(j == at, lead_ref[...], cur))
    return jnp.concatenate(parts, axis=0)


def _ffn_body(*refs, d_ff, n_pro, blocks_per_seq):
    if blocks_per_seq is None:
        (x_ref, g_ref, wg_ref, wu_ref, wout_ref, o_ref,
         win_bf, wout_bf, acc_ref, xn_ref) = refs
        lead_ref = None
    else:
        (x_ref, lead_ref, g_ref, wg_ref, wu_ref, wout_ref, o_ref,
         win_bf, wout_bf, acc_ref, xn_ref) = refs
    i = pl.program_id(0)

    def rows_of(tile):
        if blocks_per_seq is None:
            return x_ref[...]
        return _assemble_rows(x_ref, lead_ref, tile, blocks_per_seq)

    def chunk(xn, w_gate, w_up, w_down):
        gate = _dot(xn, w_gate)
        up = _dot(xn, w_up)
        return _dot((gate * jax.nn.sigmoid(gate) * up).astype(BF16), w_down)

    @pl.when(i == 0)
    def _():
        xn_ref[...] = _rms_rows(rows_of(0), g_ref[...]).astype(BF16)
        acc_ref[...] = jnp.zeros_like(acc_ref)

    @pl.when(i < n_pro)
    def _():
        lo = pl.multiple_of(i * FFN_CHUNK, FFN_CHUNK)
        w_gate, w_up, w_down = (r[...].astype(BF16) for r in (wg_ref, wu_ref, wout_ref))
        win_bf[:, pl.ds(lo, FFN_CHUNK)] = w_gate
        win_bf[:, pl.ds(pl.multiple_of(d_ff + lo, LANES), FFN_CHUNK)] = w_up
        wout_bf[pl.ds(lo, FFN_CHUNK), :] = w_down
        acc_ref[...] += chunk(xn_ref[...], w_gate, w_up, w_down)

    @pl.when(i == n_pro - 1)
    def _():
        o_ref[...] = rows_of(0) + 0.5 * acc_ref[...]

    @pl.when(i >= n_pro)
    def _():
        x = rows_of(i - n_pro + 1)
        xn = _rms_rows(x, g_ref[...]).astype(BF16)
        for c in range(d_ff // FFN_CHUNK):
            lo = c * FFN_CHUNK
            part = chunk(xn, win_bf[:, lo:lo + FFN_CHUNK],
                         win_bf[:, d_ff + lo:d_ff + lo + FFN_CHUNK], wout_bf[lo:lo + FFN_CHUNK, :])
            if c == 0:
                acc_ref[...] = part
            else:
                acc_ref[...] += part
        o_ref[...] = x + 0.5 * acc_ref[...]


def _ffn(h2, gain, w_in, w_out, layer, real_only=None, lead=None):
    rows, d = h2.shape
    d_ff = w_out.shape[1]
    if lead is not None:
        walked = lead[1] * (lead[2] + BLOCK)
    else:
        walked = rows if real_only is None else real_only[1]
    tm = next(t for t in FFN_TILES if walked % t == 0)
    n_pro = d_ff // FFN_CHUNK
    blocks_per_seq = None
    extra_in, extra_specs = (), ()

    if lead is not None:
        lead_rows, batch, seq = lead
        blocks_per_seq = seq // BLOCK + 1
        rows = batch * blocks_per_seq * BLOCK
        n_tiles = rows // tm

        def x_map(t):
            first = t * (tm // BLOCK)
            start = first - _lead_blocks_before(first, blocks_per_seq)
            return (pl.multiple_of(start * BLOCK, BLOCK), 0)

        def out_map(t):
            return (t, 0)

        x_block = (pl.Element(tm), pl.Element(d))
        extra_in, extra_specs = (lead_rows,), (_const_spec((BLOCK, d)),)
        out_block, out_shape = (tm, d), jax.ShapeDtypeStruct((rows, d), F32)
    elif real_only is None:
        n_tiles = rows // tm

        def x_map(t):
            return (t, 0)

        out_map = x_map
        x_block = out_block = (tm, d)
        out_shape = jax.ShapeDtypeStruct((rows, d), F32)
    else:
        batch, seq = real_only
        per_seq = seq // tm
        n_tiles = batch * per_seq

        def x_map(t):
            start = (t // per_seq) * (rows // batch) + BLOCK + (t % per_seq) * tm
            return (pl.multiple_of(start, BLOCK), 0)

        def out_map(t):
            return (t // per_seq, t % per_seq, 0)

        x_block = (pl.Element(tm), pl.Element(d))
        out_block, out_shape = (None, tm, d), jax.ShapeDtypeStruct((batch, seq, d), F32)

    def tile_of(i):
        return jnp.maximum(i - n_pro + 1, 0)

    def chunk_of(i):
        return jnp.minimum(i, n_pro - 1)

    return pl.pallas_call(
        functools.partial(_ffn_body, d_ff=d_ff, n_pro=n_pro, blocks_per_seq=blocks_per_seq),
        grid=(n_pro + n_tiles - 1,),
        in_specs=[
            pl.BlockSpec(x_block, lambda i: x_map(tile_of(i))),
            *extra_specs,
            _const_spec((1, d)),
            pl.BlockSpec((None, d, FFN_CHUNK), lambda i: (layer, 0, chunk_of(i))),
            pl.BlockSpec((None, d, FFN_CHUNK), lambda i: (layer, 0, n_pro + chunk_of(i))),
            pl.BlockSpec((None, FFN_CHUNK, d), lambda i: (layer, chunk_of(i), 0)),
        ],
        out_specs=pl.BlockSpec(out_block, lambda i: out_map(tile_of(i))),
        out_shape=out_shape,
        scratch_shapes=[pltpu.VMEM((d, 2 * d_ff), BF16), pltpu.VMEM((d_ff, d), BF16),
                        pltpu.VMEM((tm, d), F32), pltpu.VMEM((tm, d), BF16)],
        compiler_params=_params("arbitrary"),
        name="ffn",
    )(h2, *extra_in, gain, w_in, w_in, w_out)


def _split3(x):
    hi = x.astype(BF16).astype(F32)
    r = x - hi
    mid = r.astype(BF16).astype(F32)
    return hi, mid, r - mid


def _head_rms_cols(x, g):
    ms = jnp.mean(x * x, axis=0, keepdims=True)
    return x * lax.rsqrt(ms + EPS) * g


def _pair_rms_rows(x, g2):
    lane = lax.broadcasted_iota(jnp.int32, x.shape, 1)
    sq = x * x
    first = lane < HEAD_DIM
    s0 = jnp.sum(jnp.where(first, sq, 0.0), axis=-1, keepdims=True)
    s1 = jnp.sum(jnp.where(first, 0.0, sq), axis=-1, keepdims=True)
    ms = jnp.where(first, s0, s1) * (1.0 / HEAD_DIM)
    return x * lax.rsqrt(ms + EPS) * g2


def _proj_body(shift_ref, x_ref, xnext_ref, g_ref, wt_ref, wk_ref, fb_ref, nfq_ref, nfk_ref, nsq_ref, nsk_ref,
               qf_ref, kf_ref, vt_ref, qs_ref, ks_ref, vs_ref, ga_ref, gb_ref,
               carry_ref, tri_ref, xn_ref, *, d_model, seq_pad):
    i = pl.program_id(0)
    tm = x_ref.shape[0]

    @pl.when(i == 0)
    def _():
        src = lax.broadcasted_iota(jnp.int32, (tm, tm), 0)
        dst = lax.broadcasted_iota(jnp.int32, (tm, tm), 1)
        tri_ref[...] = (src <= dst).astype(BF16)
        xn_ref[...] = _rms_rows(x_ref[...], g_ref[...]).astype(BF16)

    xn = xn_ref[...]

    def feat(lo, rows):
        return _dot_nt(wt_ref[lo:lo + rows, :], xn)

    va_fa = feat(_R_VA, FOX_W + FA_ROWS)
    lf = jax.nn.log_sigmoid(va_fa[FOX_W:] + fb_ref[...])
    va = va_fa[:FOX_W].astype(BF16)
    ones = jnp.ones((V_ROWS - HEAD_DIM, tm), BF16)
    for h in range(FOX_HEADS):
        vt_ref[h * V_ROWS:(h + 1) * V_ROWS, :] = jnp.concatenate(
            [va[h * HEAD_DIM:(h + 1) * HEAD_DIM], ones], axis=0)
    r0 = i * tm
    boundary = (r0 // seq_pad + 1) * seq_pad - r0
    hi, mid, lo3 = _split3(lf)
    upper = tri_ref[...]
    c_loc = (_dot(hi.astype(BF16), upper) + _dot(mid.astype(BF16), upper)
             + _dot(lo3.astype(BF16), upper))
    carry = jnp.where(r0 % seq_pad == 0, 0.0, carry_ref[:, 0:1])
    tok = lax.broadcasted_iota(jnp.int32, (1, tm), 1)
    c_end = jnp.sum(jnp.where(tok == boundary - 1, c_loc, 0.0), axis=1, keepdims=True)
    c_t = c_loc + jnp.where(tok < boundary, carry, -c_end)
    carry_ref[...] = jnp.broadcast_to(c_t[:, tm - 1:tm], carry_ref.shape)
    c_t = c_t * LOG2E

    c_pad = jnp.concatenate(
        [jnp.where(lax.broadcasted_iota(jnp.int32, (FA_ROWS, tm), 0) < FOX_HEADS, c_t, 0.0),
         jnp.zeros((LANES - FA_ROWS, tm), F32)], axis=0)
    c_tok = c_pad.T
    lane = lax.broadcasted_iota(jnp.int32, (tm, LANES), 1)
    k_hi, k_mid, k_lo = _split3(c_tok)
    aug_lo = (-k_hi - pltpu.roll(k_mid, FOX_HEADS, 1) - pltpu.roll(k_lo, 2 * FOX_HEADS, 1)
              + jnp.where((lane >= 3 * FOX_HEADS) & (lane < 3 * FOX_HEADS + 3), 1.0, 0.0)
              + jnp.where(lane == 3 * FOX_HEADS + 3, -shift_ref[0], 0.0))
    aug_hi = pltpu.roll(aug_lo, HEAD_DIM, 1)
    first = lane < HEAD_DIM

    kt = _dot(xn, wk_ref[...])
    for j in range(FOX_HEADS // 2):
        kn = _pair_rms_rows(kt[:, j * LANES:(j + 1) * LANES], nfk_ref[...])
        kf_ref[:, (2 * j) * AUG:(2 * j + 1) * AUG] = jnp.where(first, kn, aug_hi).astype(BF16)
        kf_ref[:, (2 * j + 1) * AUG:(2 * j + 2) * AUG] = jnp.where(first, aug_lo, kn).astype(BF16)
    ks_ref[...] = _pair_rms_rows(kt[:, FOX_W:FOX_W + SWA_KVW], nsk_ref[...]).astype(BF16)

    arow = lax.broadcasted_iota(jnp.int32, (AUG_ROWS, tm), 0)
    qa = feat(_R_QA, FOX_W)
    zeros = jnp.zeros((HEAD_DIM - AUG_ROWS, tm), BF16)
    for h in range(FOX_HEADS):
        q_hi, q_mid, q_lo = _split3(c_t[h:h + 1, :])
        onehot = jnp.where(((arow % FOX_HEADS == h) & (arow < 3 * FOX_HEADS))
                           | (arow == 3 * FOX_HEADS + 3), 1.0, 0.0)
        aug_q = jnp.where(arow == 3 * FOX_HEADS, q_hi, jnp.where(
            arow == 3 * FOX_HEADS + 1, q_mid, jnp.where(arow == 3 * FOX_HEADS + 2, q_lo, onehot)))
        qn = (_head_rms_cols(qa[h * HEAD_DIM:(h + 1) * HEAD_DIM], nfq_ref[...])
              * (SCALE * LOG2E)).astype(BF16)
        parts = [qn, aug_q.astype(BF16), zeros] if h % 2 == 0 else [aug_q.astype(BF16), zeros, qn]
        qf_ref[h * AUG:(h + 1) * AUG, :] = jnp.concatenate(parts, axis=0)
    ga_ref[...] = jax.nn.sigmoid(feat(_R_GA, d_model)).astype(BF16)
    gb_ref[...] = jax.nn.sigmoid(feat(_R_GA + d_model, d_model)).astype(BF16)
    qb = feat(_R_QB, SWA_QW)
    for h in range(SWA_Q_HEADS):
        sl = slice(h * HEAD_DIM, (h + 1) * HEAD_DIM)
        qs_ref[sl, :] = (_head_rms_cols(qb[sl], nsq_ref[...]) * (SCALE * LOG2E)).astype(BF16)
    vs_ref[...] = feat(_R_VB, SWA_KVW).astype(BF16)
    xn_ref[...] = _rms_rows(xnext_ref[...], g_ref[...]).astype(BF16)


def _proj(shift, h2, seq_pad, gain, w_feat, w_keys, fb, nfq, nfk, nsq, nsk):
    rows, d = h2.shape
    tm = ROW_TILE

    def tok(w):
        return pl.BlockSpec((tm, w), lambda i: (i, 0)), jax.ShapeDtypeStruct((rows, w), BF16)

    def feat(r):
        return pl.BlockSpec((r, tm), lambda i: (0, i)), jax.ShapeDtypeStruct((r, rows), BF16)

    outs = [feat(FOX_HEADS * AUG), tok(FOX_HEADS * AUG), feat(FOX_HEADS * V_ROWS), feat(SWA_QW),
            tok(SWA_KVW), feat(SWA_KVW), feat(d), feat(d)]
    return pl.pallas_call(
        functools.partial(_proj_body, d_model=d, seq_pad=seq_pad),
        grid=(rows // tm,),
        in_specs=[
            pl.BlockSpec(memory_space=pltpu.SMEM),
            _const_spec((tm, d)),
            pl.BlockSpec((tm, d), lambda i: (jnp.minimum(i + 1, rows // tm - 1), 0)),
            _const_spec((1, d)),
            _const_spec(w_feat.shape),
            _const_spec(w_keys.shape),
            _const_spec((FA_ROWS, 1)),
            _const_spec((HEAD_DIM, 1)), _const_spec((1, LANES)),
            _const_spec((HEAD_DIM, 1)), _const_spec((1, LANES)),
        ],
        out_specs=[o[0] for o in outs],
        out_shape=[o[1] for o in outs],
        scratch_shapes=[pltpu.VMEM((FA_ROWS, LANES), F32), pltpu.VMEM((tm, tm), BF16),
                        pltpu.VMEM((tm, d), BF16)],
        compiler_params=_params("arbitrary"),
        name="proj",
    )(shift, h2, h2, gain, w_feat, w_keys, fb, nfq, nfk, nsq, nsk)


def _fox_body(bounded_ref, q_ref, k_ref, vt_ref, o_ref, m_ref, acc_ref, st_ref):
    def attend(q_start, tq, n_full, mode):
        q = q_ref[:, pl.ds(q_start, tq)]
        m_ref[:, :, :tq] = jnp.full((FOX_GROUP, 1, tq), NEG, F32)
        acc_ref[:, :tq] = jnp.zeros((FOX_GROUP * V_ROWS, tq), F32)

        everyone = slice(0, tq)

        def scores(kblk, h, cols=everyone):
            return _dot(kblk[:, h * AUG:(h + 1) * AUG], q[h * AUG:(h + 1) * AUG, cols])

        def softmax(h, st, mask, cols):
            if mask is not None:
                st = jnp.where(mask, st, NEG)
            m_prev = m_ref[h, :, cols]
            m_new = jnp.maximum(m_prev, jnp.max(st, axis=0, keepdims=True))
            alpha = jnp.exp2(m_prev - m_new)
            m_ref[h, :, cols] = m_new
            return jnp.exp2(st - m_new).astype(BF16), alpha

        def shifted_probs(st, mask):
            if mask is not None:
                st = jnp.where(mask, st, NEG)
            return jnp.exp2(st).astype(BF16), None

        def weighted_values(h, vt, p_alpha, cols):
            p, alpha = p_alpha
            rows = slice(h * V_ROWS, (h + 1) * V_ROWS)
            if alpha is None:
                acc_ref[rows, cols] += _dot(vt[rows, :], p)
            else:
                acc_ref[rows, cols] = alpha * acc_ref[rows, cols] + _dot(vt[rows, :], p)

        def run_heads(kblk, vt, st0, mask, cols=everyone, next_scores=None, fixed=False):
            def probs(h, st):
                return shifted_probs(st, mask) if fixed else softmax(h, st, mask, cols)

            st = [st0, scores(kblk, 1, cols), None, None]
            pa = probs(0, st[0])
            for h in range(FOX_GROUP):
                if h + 2 < FOX_GROUP:
                    st[h + 2] = scores(kblk, h + 2, cols)
                elif h + 2 == FOX_GROUP and next_scores is not None:
                    st_ref[...] = next_scores()
                weighted_values(h, vt, pa, cols)
                if h + 1 < FOX_GROUP:
                    pa = probs(h + 1, st[h + 1])

        def key_tile(j):
            return k_ref[pl.ds(pl.multiple_of(BLOCK + j * FOX_TILE, BLOCK), FOX_TILE), :]

        def full_tile(j, fixed):
            vt = vt_ref[:, pl.ds(pl.multiple_of(BLOCK + j * FOX_TILE, BLOCK), FOX_TILE)]
            run_heads(key_tile(j), vt, st_ref[...], None, fixed=fixed,
                      next_scores=lambda: scores(key_tile(j + 1), 0))

        def full_tiles(fixed):
            def tile_pair(jj, carry):
                full_tile(2 * jj, fixed)
                full_tile(2 * jj + 1, fixed)
                return carry

            lax.fori_loop(0, n_full // 2, tile_pair, 0)

            @pl.when(n_full % 2 == 1)
            def _():
                full_tile(n_full - 1, fixed)

        def diagonal_tile(st_diag0, next_scores, fixed):
            kr = lax.broadcasted_iota(jnp.int32, (BLOCK + tq, tq), 0)
            qc = lax.broadcasted_iota(jnp.int32, (BLOCK + tq, tq), 1)
            visible = ((kr >= PAD_FRONT) & (kr < BLOCK)) | ((kr >= BLOCK) & (kr - BLOCK <= qc))
            run_heads(jnp.concatenate([k_lead, k_ref[pl.ds(q_start, tq), :]], axis=0),
                      jnp.concatenate([v_lead, vt_ref[:, pl.ds(q_start, tq)]], axis=1),
                      jnp.concatenate([scores(k_lead, 0), st_diag0], axis=0), visible,
                      next_scores=next_scores, fixed=fixed)

        k_lead = k_ref[0:BLOCK, :]
        v_lead = vt_ref[:, 0:BLOCK]
        if mode in ("online", "shifted"):
            st_ref[...] = scores(key_tile(0), 0)
            full_tiles(mode == "shifted")
            diagonal_tile(st_ref[...], None, mode == "shifted")
        else:
            kpos0 = lax.broadcasted_iota(jnp.int32, (BLOCK, tq), 0)
            qc = lax.broadcasted_iota(jnp.int32, (BLOCK, tq), 1)
            visible = (kpos0 >= PAD_FRONT) & (kpos0 <= qc)
            for h in range(FOX_GROUP):
                weighted_values(h, v_lead, softmax(h, scores(k_lead, h), visible, everyone), everyone)
        for h in range(FOX_GROUP):
            denom = acc_ref[h * V_ROWS + HEAD_DIM:h * V_ROWS + HEAD_DIM + 1, :tq]
            o_ref[h * HEAD_DIM:(h + 1) * HEAD_DIM, pl.ds(q_start, tq)] = (
                acc_ref[h * V_ROWS:h * V_ROWS + HEAD_DIM, :tq] * (1.0 / denom)).astype(BF16)

    attend(0, BLOCK, 0, "lead")

    def query_tile(i, carry):
        q_start = pl.multiple_of(BLOCK + i * FOX_TILE, BLOCK)

        @pl.when(bounded_ref[0] == 1)
        def _():
            attend(q_start, FOX_TILE, i, "shifted")

        @pl.when(bounded_ref[0] != 1)
        def _():
            attend(q_start, FOX_TILE, i, "online")

        return carry

    lax.fori_loop(0, (q_ref.shape[1] - BLOCK) // FOX_TILE, query_tile, 0)


def _fox_logit_bound(q_gain, k_gain):
    bound = math.sqrt(HEAD_DIM) * LOG2E * jnp.max(jnp.abs(q_gain)) * jnp.max(jnp.abs(k_gain))
    ok = 2.0 * bound < FOX_MAX_SPREAD
    return ok.astype(jnp.int32).reshape(1), jnp.where(ok, bound, 0.0).astype(F32).reshape(1)


def _fox(bounded, qft, kf, vt, batch):
    rows = kf.shape[0]
    p = rows // batch
    n_grp = FOX_HEADS // FOX_GROUP
    width = FOX_GROUP * HEAD_DIM

    def feat(r):
        return pl.BlockSpec((r, p), lambda b, g: (g, b))

    return pl.pallas_call(
        _fox_body,
        grid=(batch, n_grp),
        in_specs=[pl.BlockSpec(memory_space=pltpu.SMEM),
                  feat(FOX_GROUP * AUG),
                  pl.BlockSpec((p, FOX_GROUP * AUG), lambda b, g: (b, g)),
                  feat(FOX_GROUP * V_ROWS)],
        out_specs=feat(width),
        out_shape=jax.ShapeDtypeStruct((FOX_W, rows), BF16),
        scratch_shapes=[
            pltpu.VMEM((FOX_GROUP, 1, FOX_TILE), F32),
            pltpu.VMEM((FOX_GROUP * V_ROWS, FOX_TILE), F32),
            pltpu.VMEM((FOX_TILE, FOX_TILE), F32),
        ],
        compiler_params=_params("arbitrary", "arbitrary"),
        name="fox",
    )(bounded, qft, kf, vt)


def _t5_bucket_np(dist):
    n = np.maximum(dist, 0)
    max_exact = N_BUCKETS // 2
    nf = np.maximum(n, 1).astype(np.float64)
    large = max_exact + (np.log(nf / max_exact) / math.log(MAX_DISTANCE / max_exact)
                         * (N_BUCKETS - max_exact)).astype(np.int64)
    large = np.minimum(large, N_BUCKETS - 1)
    return np.where(n < max_exact, n, large).astype(np.int32)


def _swa_static_buckets():
    k = np.arange(SWA_KEYS)[:, None]
    q = np.arange(BLOCK)[None, :]
    d = BLOCK + q - k
    bucket = np.where((d >= 0) & (d < WINDOW) & (k < 2 * BLOCK), _t5_bucket_np(d), -1)
    bucket = np.where(k >= 2 * BLOCK + PAD_FRONT, N_BUCKETS - 1, bucket)
    return np.ascontiguousarray(np.broadcast_to(bucket, (SWA_KEYS, BLOCK))).astype(np.int32)


def _swa_body(bound_ref, table_ref, sink_ref, bucket_ref, q_ref, k_ref, vt_ref, o_ref, bias_ref):
    g = pl.program_id(1)
    t = pl.program_id(2)
    cols = SWA_GROUP * BLOCK
    shift = bound_ref[1]

    @pl.when((pl.program_id(0) == 0) & (t == 0))
    def _():
        bucket = bucket_ref[...]
        slot = lax.broadcasted_iota(jnp.int32, (SWA_KEYS, BLOCK), 0)
        qpos = lax.broadcasted_iota(jnp.int32, (SWA_KEYS, BLOCK), 1)
        for r in range(SWA_GROUP):
            bias = jnp.full((SWA_KEYS, BLOCK), NEG, F32)
            for bkt in range(N_BUCKETS):
                bias = jnp.where(bucket == bkt,
                                 table_ref[bkt, g * SWA_GROUP + r] * LOG2E - shift, bias)
            bias_ref[g, 2, :, r * BLOCK:(r + 1) * BLOCK] = bias
            for n in range(2):
                band_ok = (slot >= PAD_FRONT - (n - 1) * BLOCK) & (slot < 2 * BLOCK)
                meta_ok = ((slot >= 2 * BLOCK + PAD_FRONT)
                           & (n * BLOCK + qpos - (slot - 2 * BLOCK) >= WINDOW))
                bias_ref[g, n, :, r * BLOCK:(r + 1) * BLOCK] = jnp.where(band_ok | meta_ok, bias, NEG)

    sink = jnp.concatenate(
        [jnp.full((1, BLOCK), sink_ref[g * SWA_GROUP + r] * LOG2E, F32) for r in range(SWA_GROUP)],
        axis=1)
    grp_rows = lax.broadcasted_iota(jnp.int32, (SWA_KVW, cols), 0) // HEAD_DIM
    v_rows = pl.ds(pl.multiple_of(g * HEAD_DIM, HEAD_DIM), HEAD_DIM)
    k_lead = k_ref[0:BLOCK, :]
    v_lead = vt_ref[v_rows, 0:BLOCK]

    n_blocks = q_ref.shape[1] // BLOCK

    def band(i):
        n = t * n_blocks + i
        return (n, pl.multiple_of(jnp.maximum(n - 1, 0) * BLOCK, BLOCK),
                pl.multiple_of(n * BLOCK, BLOCK))

    def scores(i):
        n, prev, cur = band(i)
        keys = jnp.concatenate(
            [k_ref[pl.ds(prev, BLOCK), :], k_ref[pl.ds(cur, BLOCK), :], k_lead], axis=0)
        q4 = jnp.concatenate(
            [q_ref[r * HEAD_DIM:(r + 1) * HEAD_DIM, i * BLOCK:(i + 1) * BLOCK]
             for r in range(SWA_GROUP)], axis=1)
        q8 = jnp.concatenate([q4, q4], axis=0)
        q8 = jnp.where(grp_rows == g, q8, jnp.zeros_like(q8))
        return _dot(keys, q8) + bias_ref[g, jnp.minimum(n, 2)]

    def finish(i, s, shifted):
        _, prev, cur = band(i)
        vals = jnp.concatenate(
            [vt_ref[v_rows, pl.ds(prev, BLOCK)], vt_ref[v_rows, pl.ds(cur, BLOCK)], v_lead], axis=1)
        vals = jnp.concatenate([vals, jnp.ones((V_ROWS - HEAD_DIM, SWA_KEYS), BF16)], axis=0)
        mx = shift if shifted else jnp.maximum(jnp.max(s, axis=0, keepdims=True), sink)
        pv = _dot(vals, jnp.exp2(s if shifted else s - mx).astype(BF16))
        denom = pv[HEAD_DIM:HEAD_DIM + 1] + jnp.exp2(sink - mx)
        out = pv[:HEAD_DIM] * (1.0 / denom)
        for r in range(SWA_GROUP):
            o_ref[r * HEAD_DIM:(r + 1) * HEAD_DIM, i * BLOCK:(i + 1) * BLOCK] = (
                out[:, r * BLOCK:(r + 1) * BLOCK].astype(BF16))

    def all_blocks(shifted):
        s = scores(0)
        for i in range(n_blocks):
            s_next = scores(i + 1) if i + 1 < n_blocks else None
            finish(i, s, shifted)
            s = s_next

    @pl.when(bound_ref[0] > 0.5)
    def _():
        all_blocks(True)

    @pl.when(bound_ref[0] <= 0.5)
    def _():
        all_blocks(False)


def _swa_logit_bound(q_gain, k_gain, table, sinks):
    b_qk = math.sqrt(HEAD_DIM) * LOG2E * jnp.max(jnp.abs(q_gain)) * jnp.max(jnp.abs(k_gain))
    top = b_qk + LOG2E * jnp.max(jnp.abs(table))
    shift = jnp.maximum(top, LOG2E * jnp.max(sinks))
    ok = (top + shift < FOX_MAX_SPREAD) & (shift - LOG2E * jnp.min(sinks) < FOX_MAX_SPREAD)
    return jnp.stack([ok.astype(F32), jnp.where(ok, shift, 0.0).astype(F32)])


def _swa(bound, table, sinks, qst, ks, vst, batch):
    rows = ks.shape[0]
    p = rows // batch
    n_t = SWA_STEPS
    swa_tile = p // n_t
    width = SWA_GROUP * HEAD_DIM
    bucket = jnp.asarray(_swa_static_buckets())
    smem = pl.BlockSpec(memory_space=pltpu.SMEM)
    tile = pl.BlockSpec((width, swa_tile), lambda b, g, t: (g, b * n_t + t))
    return pl.pallas_call(
        _swa_body,
        grid=(batch, SWA_KV_HEADS, n_t),
        in_specs=[smem, smem, smem, _const_spec((SWA_KEYS, BLOCK)), tile,
                  pl.BlockSpec((p, SWA_KVW), lambda b, g, t: (b, 0)),
                  pl.BlockSpec((SWA_KVW, p), lambda b, g, t: (0, b))],
        out_specs=tile,
        out_shape=jax.ShapeDtypeStruct((SWA_QW, rows), BF16),
        scratch_shapes=[pltpu.VMEM((SWA_KV_HEADS, 3, SWA_KEYS, SWA_GROUP * BLOCK), F32)],
        compiler_params=_params("arbitrary", "arbitrary", "arbitrary"),
        name="swa",
    )(bound, table, sinks, bucket, qst, ks, vst)


def _mix_body(h_ref, of_ref, os_ref, ga_ref, gb_ref, wf_ref, ws_ref, wo_ref, o_ref):
    y = (ga_ref[...].astype(F32) * _dot(wf_ref[...], of_ref[...])
         + gb_ref[...].astype(F32) * _dot(ws_ref[...], os_ref[...]))
    o_ref[...] = h_ref[...] + _dot_tn(y.astype(BF16), wo_ref[...])


def _mix_out(h2, o_fox, o_swa, ga, gb, wf_t, ws_t, wo):
    rows, d = h2.shape
    tm = next(t for t in MIX_TILES if rows % t == 0)

    def feat(r):
        return pl.BlockSpec((r, tm), lambda i: (0, i))

    tok = pl.BlockSpec((tm, d), lambda i: (i, 0))
    return pl.pallas_call(
        _mix_body,
        grid=(rows // tm,),
        in_specs=[tok, feat(FOX_W), feat(SWA_QW), feat(d), feat(d),
                  _const_spec(wf_t.shape), _const_spec(ws_t.shape), _const_spec(wo.shape)],
        out_specs=tok,
        out_shape=jax.ShapeDtypeStruct((rows, d), F32),
        compiler_params=_params("arbitrary"),
        name="mix_out",
    )(h2, o_fox, o_swa, ga, gb, wf_t, ws_t, wo)


def _w_in_body(w_ref, wf_ref, wk_ref, *, d_model):
    src = np.cumsum([0, FOX_W, FOX_W, FOX_W, FOX_HEADS, SWA_QW, SWA_KVW, SWA_KVW, d_model])
    qa, ka, va, fa, qb, kb, vb, ga, gb = (int(s) for s in src)

    def move(dst, lo, n):
        wf_ref[dst:dst + n] = w_ref[lo:lo + n].astype(BF16)

    move(_R_QA, qa, FOX_W)
    move(_R_VA, va, FOX_W)
    move(_R_QB, qb, SWA_QW)
    move(_R_VB, vb, SWA_KVW)
    move(_R_GA, ga, d_model)
    move(_R_GA + d_model, gb, d_model)
    wf_ref[_R_FA:_R_FA + FA_ROWS] = jnp.concatenate(
        [w_ref[fa:fa + FOX_HEADS], jnp.zeros((FA_ROWS - FOX_HEADS, w_ref.shape[1]), F32)],
        axis=0).astype(BF16)
    wk_ref[:, 0:FOX_W] = w_ref[ka:ka + FOX_W].T.astype(BF16)
    wk_ref[:, FOX_W:FOX_W + SWA_KVW] = w_ref[kb:kb + SWA_KVW].T.astype(BF16)


def _prep_w_in(w_in, layer):
    w_t = jnp.swapaxes(w_in, 1, 2)
    _, d_in, d = w_t.shape
    n_feat = _R_GA + 2 * d
    return pl.pallas_call(
        functools.partial(_w_in_body, d_model=d),
        grid=(1,),
        in_specs=[pl.BlockSpec((None, d_in, d), lambda i: (layer, 0, 0), pipeline_mode=pl.Buffered(1))],
        out_specs=[pl.BlockSpec((n_feat, d), lambda i: (0, 0)),
                   pl.BlockSpec((d, FOX_W + SWA_KVW), lambda i: (0, 0))],
        out_shape=[jax.ShapeDtypeStruct((n_feat, d), BF16),
                   jax.ShapeDtypeStruct((d, FOX_W + SWA_KVW), BF16)],
        compiler_params=_params("arbitrary"),
        name="prep_w_in",
    )(w_t)


def kernel(x, meta_tokens, rel_bias_table, ffn1_norm, ffn1_w_in, ffn1_w_out, mix_norm, w_in, forget_bias, fox_q_norm, fox_k_norm, swa_q_norm, swa_k_norm, swa_sinks, w_branch_fox, w_branch_swa, w_out, ffn2_norm, ffn2_w_in, ffn2_w_out):
    b, seq, d = x.shape
    depth = w_in.shape[0]
    p = BLOCK + seq
    assert seq % FOX_TILE == 0 and p % (SWA_STEPS * BLOCK) == 0 and (b * p) % ROW_TILE == 0
    lead = jnp.concatenate([jnp.zeros((PAD_FRONT, d), x.dtype), meta_tokens.astype(x.dtype)], axis=0)
    h = x.reshape(b * seq, d)
    table = rel_bias_table.astype(F32)

    def col(v):
        return v.astype(F32)[:, None]

    def twice(v):
        return jnp.tile(v.astype(F32), 2)[None]

    for l in range(depth):
        h = _ffn(h, ffn1_norm[l][None], ffn1_w_in, ffn1_w_out, l,
                 lead=(lead, b, seq) if l == 0 else None)
        w_feat, w_keys = _prep_w_in(w_in, l)
        fb = jnp.pad(col(forget_bias[l]), ((0, FA_ROWS - FOX_HEADS), (0, 0)))
        bounded, shift = _fox_logit_bound(fox_q_norm[l], fox_k_norm[l])
        qft, kf, vt, qst, ks, vst, ga, gb = _proj(
            shift, h, p, mix_norm[l][None], w_feat, w_keys, fb,
            col(fox_q_norm[l]), twice(fox_k_norm[l]), col(swa_q_norm[l]), twice(swa_k_norm[l]))
        o_fox = _fox(bounded, qft, kf, vt, b)
        sinks = swa_sinks[l].astype(F32)
        o_swa = _swa(_swa_logit_bound(swa_q_norm[l], swa_k_norm[l], table, sinks),
                     table, sinks, qst, ks, vst, b)
        h = _mix_out(h, o_fox, o_swa, ga, gb, w_branch_fox[l].astype(BF16).T,
                     w_branch_swa[l].astype(BF16).T, w_out[l].astype(BF16))
        last = l == depth - 1
        h = _ffn(h, ffn2_norm[l][None], ffn2_w_in, ffn2_w_out, l, (b, seq) if last else None)
    return h
```

```python
import functools
import math

import jax
import jax.numpy as jnp
import numpy as np
from jax import lax
from jax.experimental import pallas as pl
from jax.experimental.pallas import tpu as pltpu

F32 = jnp.float32
BF16 = jnp.bfloat16

N_META = 16
BLOCK = 128
PAD_FRONT = BLOCK - N_META
HEAD_DIM = 64
FOX_HEADS = 8
SWA_Q_HEADS = 8
SWA_KV_HEADS = 2
SWA_GROUP = SWA_Q_HEADS // SWA_KV_HEADS
WINDOW = 128
N_BUCKETS = 32
MAX_DISTANCE = 128
EPS = 1e-6
NEG = -1e30
FOX_W = FOX_HEADS * HEAD_DIM
SWA_QW = SWA_Q_HEADS * HEAD_DIM
SWA_KVW = SWA_KV_HEADS * HEAD_DIM
SCALE = HEAD_DIM ** -0.5
LOG2E = math.log2(math.e)

LANES = 128
VMEM_LIMIT = 56 * 1024 * 1024
ROW_TILE = 512
FFN_TILES = (1024, 768, 512)
MIX_TILES = (768, 512)
FFN_CHUNK = 256
FOX_TILE = 512
FOX_GROUP = 4
FOX_MAX_SPREAD = 100.0
AUG = LANES
AUG_ROWS = 32
V_ROWS = HEAD_DIM + 16
FA_ROWS = 16
SWA_KEYS = 3 * BLOCK
SWA_STEPS = 3

_R_QA = 0
_R_VA = _R_QA + FOX_W
_R_FA = _R_VA + FOX_W
_R_QB = _R_FA + FA_ROWS
_R_VB = _R_QB + SWA_QW
_R_GA = _R_VB + SWA_KVW


def _params(*sem):
    return pltpu.CompilerParams(dimension_semantics=sem, vmem_limit_bytes=VMEM_LIMIT)


def _const_spec(shape):
    nd = len(shape)
    return pl.BlockSpec(shape, lambda *_: (0,) * nd, pipeline_mode=pl.Buffered(1))


def _dot(a, b):
    return jnp.dot(a, b, preferred_element_type=F32)


def _dot_nt(a, b):
    return lax.dot_general(a, b, (((1,), (1,)), ((), ())), preferred_element_type=F32)


def _dot_tn(a, b):
    return lax.dot_general(a, b, (((0,), (0,)), ((), ())), preferred_element_type=F32)


def _rms_rows(x, g):
    ms = jnp.mean(x * x, axis=-1, keepdims=True)
    return x * lax.rsqrt(ms + EPS) * g


def _lead_blocks_before(block, blocks_per_seq):
    return (block + blocks_per_seq - 1) // blocks_per_seq


def _assemble_rows(x_ref, lead_ref, tile, blocks_per_seq):
    n_sub = x_ref.shape[0] // BLOCK
    first = tile * n_sub
    at = (blocks_per_seq - first % blocks_per_seq) % blocks_per_seq
    parts = []
    for j in range(n_sub):
        cur = x_ref[j * BLOCK:(j + 1) * BLOCK, :]
        if j > 0:
            cur = jnp.where(j > at, x_ref[(j - 1) * BLOCK:j * BLOCK, :], cur)
        parts.append(jnp.where(j == at, lead_ref[...], cur))
    return jnp.concatenate(parts, axis=0)


def _ffn_body(*refs, d_ff, n_pro, blocks_per_seq):
    if blocks_per_seq is None:
        (x_ref, g_ref, wg_ref, wu_ref, wout_ref, o_ref,
         win_bf, wout_bf, acc_ref, xn_ref) = refs
        lead_ref = None
    else:
        (x_ref, lead_ref, g_ref, wg_ref, wu_ref, wout_ref, o_ref,
         win_bf, wout_bf, acc_ref, xn_ref) = refs
    i = pl.program_id(0)

    def rows_of(tile):
        if blocks_per_seq is None:
            return x_ref[...]
        return _assemble_rows(x_ref, lead_ref, tile, blocks_per_seq)

    def chunk(xn, w_gate, w_up, w_down):
        gate = _dot(xn, w_gate)
        up = _dot(xn, w_up)
        return _dot((gate * jax.nn.sigmoid(gate) * up).astype(BF16), w_down)

    @pl.when(i == 0)
    def _():
        xn_ref[...] = _rms_rows(rows_of(0), g_ref[...]).astype(BF16)
        acc_ref[...] = jnp.zeros_like(acc_ref)

    @pl.when(i < n_pro)
    def _():
        lo = pl.multiple_of(i * FFN_CHUNK, FFN_CHUNK)
        w_gate, w_up, w_down = (r[...].astype(BF16) for r in (wg_ref, wu_ref, wout_ref))
        win_bf[:, pl.ds(lo, FFN_CHUNK)] = w_gate
        win_bf[:, pl.ds(pl.multiple_of(d_ff + lo, LANES), FFN_CHUNK)] = w_up
        wout_bf[pl.ds(lo, FFN_CHUNK), :] = w_down
        acc_ref[...] += chunk(xn_ref[...], w_gate, w_up, w_down)

    @pl.when(i == n_pro - 1)
    def _():
        o_ref[...] = rows_of(0) + 0.5 * acc_ref[...]

    @pl.when(i >= n_pro)
    def _():
        x = rows_of(i - n_pro + 1)
        xn = _rms_rows(x, g_ref[...]).astype(BF16)
        for c in range(d_ff // FFN_CHUNK):
            lo = c * FFN_CHUNK
            part = chunk(xn, win_bf[:, lo:lo + FFN_CHUNK],
                         win_bf[:, d_ff + lo:d_ff + lo + FFN_CHUNK], wout_bf[lo:lo + FFN_CHUNK, :])
            if c == 0:
                acc_ref[...] = part
            else:
                acc_ref[...] += part
        o_ref[...] = x + 0.5 * acc_ref[...]


def _ffn(h2, gain, w_in, w_out, layer, real_only=None, lead=None):
    rows, d = h2.shape
    d_ff = w_out.shape[1]
    if lead is not None:
        walked = lead[1] * (lead[2] + BLOCK)
    else:
        walked = rows if real_only is None else real_only[1]
    tm = next(t for t in FFN_TILES if walked % t == 0)
    n_pro = d_ff // FFN_CHUNK
    blocks_per_seq = None
    extra_in, extra_specs = (), ()

    if lead is not None:
        lead_rows, batch, seq = lead
        blocks_per_seq = seq // BLOCK + 1
        rows = batch * blocks_per_seq * BLOCK
        n_tiles = rows // tm

        def x_map(t):
            first = t * (tm // BLOCK)
            start = first - _lead_blocks_before(first, blocks_per_seq)
            return (pl.multiple_of(start * BLOCK, BLOCK), 0)

        def out_map(t):
            return (t, 0)

        x_block = (pl.Element(tm), pl.Element(d))
        extra_in, extra_specs = (lead_rows,), (_const_spec((BLOCK, d)),)
        out_block, out_shape = (tm, d), jax.ShapeDtypeStruct((rows, d), F32)
    elif real_only is None:
        n_tiles = rows // tm

        def x_map(t):
            return (t, 0)

        out_map = x_map
        x_block = out_block = (tm, d)
        out_shape = jax.ShapeDtypeStruct((rows, d), F32)
    else:
        batch, seq = real_only
        per_seq = seq // tm
        n_tiles = batch * per_seq

        def x_map(t):
            start = (t // per_seq) * (rows // batch) + BLOCK + (t % per_seq) * tm
            return (pl.multiple_of(start, BLOCK), 0)

        def out_map(t):
            return (t // per_seq, t % per_seq, 0)

        x_block = (pl.Element(tm), pl.Element(d))
        out_block, out_shape = (None, tm, d), jax.ShapeDtypeStruct((batch, seq, d), F32)

    def tile_of(i):
        return jnp.maximum(i - n_pro + 1, 0)

    def chunk_of(i):
        return jnp.minimum(i, n_pro - 1)

    return pl.pallas_call(
        functools.partial(_ffn_body, d_ff=d_ff, n_pro=n_pro, blocks_per_seq=blocks_per_seq),
        grid=(n_pro + n_tiles - 1,),
        in_specs=[
            pl.BlockSpec(x_block, lambda i: x_map(tile_of(i))),
            *extra_specs,
            _const_spec((1, d)),
            pl.BlockSpec((None, d, FFN_CHUNK), lambda i: (layer, 0, chunk_of(i))),
            pl.BlockSpec((None, d, FFN_CHUNK), lambda i: (layer, 0, n_pro + chunk_of(i))),
            pl.BlockSpec((None, FFN_CHUNK, d), lambda i: (layer, chunk_of(i), 0)),
        ],
        out_specs=pl.BlockSpec(out_block, lambda i: out_map(tile_of(i))),
        out_shape=out_shape,
        scratch_shapes=[pltpu.VMEM((d, 2 * d_ff), BF16), pltpu.VMEM((d_ff, d), BF16),
                        pltpu.VMEM((tm, d), F32), pltpu.VMEM((tm, d), BF16)],
        compiler_params=_params("arbitrary"),
        name="ffn",
    )(h2, *extra_in, gain, w_in, w_in, w_out)


def _split3(x):
    hi = x.astype(BF16).astype(F32)
    r = x - hi
    mid = r.astype(BF16).astype(F32)
    return hi, mid, r - mid


def _head_rms_cols(x, g):
    ms = jnp.mean(x * x, axis=0, keepdims=True)
    return x * lax.rsqrt(ms + EPS) * g


def _pair_rms_rows(x, g2):
    lane = lax.broadcasted_iota(jnp.int32, x.shape, 1)
    sq = x * x
    first = lane < HEAD_DIM
    s0 = jnp.sum(jnp.where(first, sq, 0.0), axis=-1, keepdims=True)
    s1 = jnp.sum(jnp.where(first, 0.0, sq), axis=-1, keepdims=True)
    ms = jnp.where(first, s0, s1) * (1.0 / HEAD_DIM)
    return x * lax.rsqrt(ms + EPS) * g2


def _proj_body(shift_ref, x_ref, xnext_ref, g_ref, wt_ref, wk_ref, fb_ref, nfq_ref, nfk_ref, nsq_ref, nsk_ref,
               qf_ref, kf_ref, vt_ref, qs_ref, ks_ref, vs_ref, ga_ref, gb_ref,
               carry_ref, tri_ref, xn_ref, *, d_model, seq_pad):
    i = pl.program_id(0)
    tm = x_ref.shape[0]

    @pl.when(i == 0)
    def _():
        src = lax.broadcasted_iota(jnp.int32, (tm, tm), 0)
        dst = lax.broadcasted_iota(jnp.int32, (tm, tm), 1)
        tri_ref[...] = (src <= dst).astype(BF16)
        xn_ref[...] = _rms_rows(x_ref[...], g_ref[...]).astype(BF16)

    xn = xn_ref[...]

    def feat(lo, rows):
        return _dot_nt(wt_ref[lo:lo + rows, :], xn)

    va_fa = feat(_R_VA, FOX_W + FA_ROWS)
    lf = jax.nn.log_sigmoid(va_fa[FOX_W:] + fb_ref[...])
    va = va_fa[:FOX_W].astype(BF16)
    ones = jnp.ones((V_ROWS - HEAD_DIM, tm), BF16)
    for h in range(FOX_HEADS):
        vt_ref[h * V_ROWS:(h + 1) * V_ROWS, :] = jnp.concatenate(
            [va[h * HEAD_DIM:(h + 1) * HEAD_DIM], ones], axis=0)
    r0 = i * tm
    boundary = (r0 // seq_pad + 1) * seq_pad - r0
    hi, mid, lo3 = _split3(lf)
    upper = tri_ref[...]
    c_loc = (_dot(hi.astype(BF16), upper) + _dot(mid.astype(BF16), upper)
             + _dot(lo3.astype(BF16), upper))
    carry = jnp.where(r0 % seq_pad == 0, 0.0, carry_ref[:, 0:1])
    tok = lax.broadcasted_iota(jnp.int32, (1, tm), 1)
    c_end = jnp.sum(jnp.where(tok == boundary - 1, c_loc, 0.0), axis=1, keepdims=True)
    c_t = c_loc + jnp.where(tok < boundary, carry, -c_end)
    carry_ref[...] = jnp.broadcast_to(c_t[:, tm - 1:tm], carry_ref.shape)
    c_t = c_t * LOG2E

    c_pad = jnp.concatenate(
        [jnp.where(lax.broadcasted_iota(jnp.int32, (FA_ROWS, tm), 0) < FOX_HEADS, c_t, 0.0),
         jnp.zeros((LANES - FA_ROWS, tm), F32)], axis=0)
    c_tok = c_pad.T
    lane = lax.broadcasted_iota(jnp.int32, (tm, LANES), 1)
    k_hi, k_mid, k_lo = _split3(c_tok)
    aug_lo = (-k_hi - pltpu.roll(k_mid, FOX_HEADS, 1) - pltpu.roll(k_lo, 2 * FOX_HEADS, 1)
              + jnp.where((lane >= 3 * FOX_HEADS) & (lane < 3 * FOX_HEADS + 3), 1.0, 0.0)
              + jnp.where(lane == 3 * FOX_HEADS + 3, -shift_ref[0], 0.0))
    aug_hi = pltpu.roll(aug_lo, HEAD_DIM, 1)
    first = lane < HEAD_DIM

    kt = _dot(xn, wk_ref[...])
    for j in range(FOX_HEADS // 2):
        kn = _pair_rms_rows(kt[:, j * LANES:(j + 1) * LANES], nfk_ref[...])
        kf_ref[:, (2 * j) * AUG:(2 * j + 1) * AUG] = jnp.where(first, kn, aug_hi).astype(BF16)
        kf_ref[:, (2 * j + 1) * AUG:(2 * j + 2) * AUG] = jnp.where(first, aug_lo, kn).astype(BF16)
    ks_ref[...] = _pair_rms_rows(kt[:, FOX_W:FOX_W + SWA_KVW], nsk_ref[...]).astype(BF16)

    arow = lax.broadcasted_iota(jnp.int32, (AUG_ROWS, tm), 0)
    qa = feat(_R_QA, FOX_W)
    zeros = jnp.zeros((HEAD_DIM - AUG_ROWS, tm), BF16)
    for h in range(FOX_HEADS):
        q_hi, q_mid, q_lo = _split3(c_t[h:h + 1, :])
        onehot = jnp.where(((arow % FOX_HEADS == h) & (arow < 3 * FOX_HEADS))
                           | (arow == 3 * FOX_HEADS + 3), 1.0, 0.0)
        aug_q = jnp.where(arow == 3 * FOX_HEADS, q_hi, jnp.where(
            arow == 3 * FOX_HEADS + 1, q_mid, jnp.where(arow == 3 * FOX_HEADS + 2, q_lo, onehot)))
        qn = (_head_rms_cols(qa[h * HEAD_DIM:(h + 1) * HEAD_DIM], nfq_ref[...])
              * (SCALE * LOG2E)).astype(BF16)
        parts = [qn, aug_q.astype(BF16), zeros] if h % 2 == 0 else [aug_q.astype(BF16), zeros, qn]
        qf_ref[h * AUG:(h + 1) * AUG, :] = jnp.concatenate(parts, axis=0)
    ga_ref[...] = jax.nn.sigmoid(feat(_R_GA, d_model)).astype(BF16)
    gb_ref[...] = jax.nn.sigmoid(feat(_R_GA + d_model, d_model)).astype(BF16)
    qb = feat(_R_QB, SWA_QW)
    for h in range(SWA_Q_HEADS):
        sl = slice(h * HEAD_DIM, (h + 1) * HEAD_DIM)
        qs_ref[sl, :] = (_head_rms_cols(qb[sl], nsq_ref[...]) * (SCALE * LOG2E)).astype(BF16)
    vs_ref[...] = feat(_R_VB, SWA_KVW).astype(BF16)
    xn_ref[...] = _rms_rows(xnext_ref[...], g_ref[...]).astype(BF16)


def _proj(shift, h2, seq_pad, gain, w_feat, w_keys, fb, nfq, nfk, nsq, nsk):
    rows, d = h2.shape
    tm = ROW_TILE

    def tok(w):
        return pl.BlockSpec((tm, w), lambda i: (i, 0)), jax.ShapeDtypeStruct((rows, w), BF16)

    def feat(r):
        return pl.BlockSpec((r, tm), lambda i: (0, i)), jax.ShapeDtypeStruct((r, rows), BF16)

    outs = [feat(FOX_HEADS * AUG), tok(FOX_HEADS * AUG), feat(FOX_HEADS * V_ROWS), feat(SWA_QW),
            tok(SWA_KVW), feat(SWA_KVW), feat(d), feat(d)]
    return pl.pallas_call(
        functools.partial(_proj_body, d_model=d, seq_pad=seq_pad),
        grid=(rows // tm,),
        in_specs=[
            pl.BlockSpec(memory_space=pltpu.SMEM),
            _const_spec((tm, d)),
            pl.BlockSpec((tm, d), lambda i: (jnp.minimum(i + 1, rows // tm - 1), 0)),
            _const_spec((1, d)),
            _const_spec(w_feat.shape),
            _const_spec(w_keys.shape),
            _const_spec((FA_ROWS, 1)),
            _const_spec((HEAD_DIM, 1)), _const_spec((1, LANES)),
            _const_spec((HEAD_DIM, 1)), _const_spec((1, LANES)),
        ],
        out_specs=[o[0] for o in outs],
        out_shape=[o[1] for o in outs],
        scratch_shapes=[pltpu.VMEM((FA_ROWS, LANES), F32), pltpu.VMEM((tm, tm), BF16),
                        pltpu.VMEM((tm, d), BF16)],
        compiler_params=_params("arbitrary"),
        name="proj",
    )(shift, h2, h2, gain, w_feat, w_keys, fb, nfq, nfk, nsq, nsk)


def _fox_body(bounded_ref, q_ref, k_ref, vt_ref, o_ref, m_ref, acc_ref, st_ref):
    def attend(q_start, tq, n_full, mode):
        m_ref[:, :, :tq] = jnp.full((FOX_GROUP, 1, tq), NEG, F32)
        acc_ref[:, :tq] = jnp.zeros((FOX_GROUP * V_ROWS, tq), F32)

        everyone = slice(0, tq)

        def key_loader(start, n):
            return lambda h: k_ref[pl.ds(start, n), h * AUG:(h + 1) * AUG]

        def value_loader(start, n):
            return lambda h: vt_ref[h * V_ROWS:(h + 1) * V_ROWS, pl.ds(start, n)]

        def scores(keys, h, cols=everyone):
            return _dot(keys(h), q_ref[h * AUG:(h + 1) * AUG, pl.ds(q_start, tq)])

        def softmax(h, st, mask, cols):
            if mask is not None:
                st = jnp.where(mask, st, NEG)
            m_prev = m_ref[h, :, cols]
            m_new = jnp.maximum(m_prev, jnp.max(st, axis=0, keepdims=True))
            alpha = jnp.exp2(m_prev - m_new)
            m_ref[h, :, cols] = m_new
            return jnp.exp2(st - m_new).astype(BF16), alpha

        def shifted_probs(st, mask):
            if mask is not None:
                st = jnp.where(mask, st, NEG)
            return jnp.exp2(st).astype(BF16), None

        def weighted_values(h, values, p_alpha, cols):
            p, alpha = p_alpha
            rows = slice(h * V_ROWS, (h + 1) * V_ROWS)
            if alpha is None:
                acc_ref[rows, cols] += _dot(values(h), p)
            else:
                acc_ref[rows, cols] = alpha * acc_ref[rows, cols] + _dot(values(h), p)

        def run_heads(kblk, vt, st0, mask, cols=everyone, next_scores=None, fixed=False):
            def probs(h, st):
                return shifted_probs(st, mask) if fixed else softmax(h, st, mask, cols)

            st = [st0, scores(kblk, 1, cols), None, None]
            pa = probs(0, st[0])
            for h in range(FOX_GROUP):
                if h + 2 < FOX_GROUP:
                    st[h + 2] = scores(kblk, h + 2, cols)
                elif h + 2 == FOX_GROUP and next_scores is not None:
                    st_ref[...] = next_scores()
                weighted_values(h, vt, pa, cols)
                if h + 1 < FOX_GROUP:
                    pa = probs(h + 1, st[h + 1])

        def tile_start(j):
            return pl.multiple_of(BLOCK + j * FOX_TILE, BLOCK)

        def key_tile(j):
            return key_loader(tile_start(j), FOX_TILE)

        def full_tile(j, fixed):
            run_heads(key_tile(j), value_loader(tile_start(j), FOX_TILE), st_ref[...], None,
                      fixed=fixed, next_scores=lambda: scores(key_tile(j + 1), 0))

        def full_tiles(fixed):
            def tile_pair(jj, carry):
                full_tile(2 * jj, fixed)
                full_tile(2 * jj + 1, fixed)
                return carry

            lax.fori_loop(0, n_full // 2, tile_pair, 0)

            @pl.when(n_full % 2 == 1)
            def _():
                full_tile(n_full - 1, fixed)

        def diagonal_tile(st_diag0, next_scores, fixed):
            kr = lax.broadcasted_iota(jnp.int32, (BLOCK + tq, tq), 0)
            qc = lax.broadcasted_iota(jnp.int32, (BLOCK + tq, tq), 1)
            visible = ((kr >= PAD_FRONT) & (kr < BLOCK)) | ((kr >= BLOCK) & (kr - BLOCK <= qc))
            k_diag, v_diag = key_loader(q_start, tq), value_loader(q_start, tq)
            run_heads(lambda h: jnp.concatenate([k_lead(h), k_diag(h)], axis=0),
                      lambda h: jnp.concatenate([v_lead(h), v_diag(h)], axis=1),
                      jnp.concatenate([scores(k_lead, 0), st_diag0], axis=0), visible,
                      next_scores=next_scores, fixed=fixed)

        k_lead = key_loader(0, BLOCK)
        v_lead = value_loader(0, BLOCK)
        if mode in ("online", "shifted"):
            st_ref[...] = scores(key_tile(0), 0)
            full_tiles(mode == "shifted")
            diagonal_tile(st_ref[...], None, mode == "shifted")
        else:
            kpos0 = lax.broadcasted_iota(jnp.int32, (BLOCK, tq), 0)
            qc = lax.broadcasted_iota(jnp.int32, (BLOCK, tq), 1)
            visible = (kpos0 >= PAD_FRONT) & (kpos0 <= qc)
            for h in range(FOX_GROUP):
                weighted_values(h, v_lead, softmax(h, scores(k_lead, h), visible, everyone), everyone)
        for h in range(FOX_GROUP):
            denom = acc_ref[h * V_ROWS + HEAD_DIM:h * V_ROWS + HEAD_DIM + 1, :tq]
            o_ref[h * HEAD_DIM:(h + 1) * HEAD_DIM, pl.ds(q_start, tq)] = (
                acc_ref[h * V_ROWS:h * V_ROWS + HEAD_DIM, :tq] * (1.0 / denom)).astype(BF16)

    attend(0, BLOCK, 0, "lead")

    def query_tile(i, carry):
        q_start = pl.multiple_of(BLOCK + i * FOX_TILE, BLOCK)

        @pl.when(bounded_ref[0] == 1)
        def _():
            attend(q_start, FOX_TILE, i, "shifted")

        @pl.when(bounded_ref[0] != 1)
        def _():
            attend(q_start, FOX_TILE, i, "online")

        return carry

    lax.fori_loop(0, (q_ref.shape[1] - BLOCK) // FOX_TILE, query_tile, 0)


def _fox_logit_bound(q_gain, k_gain):
    bound = math.sqrt(HEAD_DIM) * LOG2E * jnp.max(jnp.abs(q_gain)) * jnp.max(jnp.abs(k_gain))
    ok = 2.0 * bound < FOX_MAX_SPREAD
    return ok.astype(jnp.int32).reshape(1), jnp.where(ok, bound, 0.0).astype(F32).reshape(1)


def _fox(bounded, qft, kf, vt, batch):
    rows = kf.shape[0]
    p = rows // batch
    n_grp = FOX_HEADS // FOX_GROUP
    width = FOX_GROUP * HEAD_DIM

    def feat(r):
        return pl.BlockSpec((r, p), lambda b, g: (g, b))

    return pl.pallas_call(
        _fox_body,
        grid=(batch, n_grp),
        in_specs=[pl.BlockSpec(memory_space=pltpu.SMEM),
                  feat(FOX_GROUP * AUG),
                  pl.BlockSpec((p, FOX_GROUP * AUG), lambda b, g: (b, g)),
                  feat(FOX_GROUP * V_ROWS)],
        out_specs=feat(width),
        out_shape=jax.ShapeDtypeStruct((FOX_W, rows), BF16),
        scratch_shapes=[
            pltpu.VMEM((FOX_GROUP, 1, FOX_TILE), F32),
            pltpu.VMEM((FOX_GROUP * V_ROWS, FOX_TILE), F32),
            pltpu.VMEM((FOX_TILE, FOX_TILE), F32),
        ],
        compiler_params=_params("arbitrary", "arbitrary"),
        name="fox",
    )(bounded, qft, kf, vt)


def _t5_bucket_np(dist):
    n = np.maximum(dist, 0)
    max_exact = N_BUCKETS // 2
    nf = np.maximum(n, 1).astype(np.float64)
    large = max_exact + (np.log(nf / max_exact) / math.log(MAX_DISTANCE / max_exact)
                         * (N_BUCKETS - max_exact)).astype(np.int64)
    large = np.minimum(large, N_BUCKETS - 1)
    return np.where(n < max_exact, n, large).astype(np.int32)


def _swa_static_buckets():
    k = np.arange(SWA_KEYS)[:, None]
    q = np.arange(BLOCK)[None, :]
    d = BLOCK + q - k
    bucket = np.where((d >= 0) & (d < WINDOW) & (k < 2 * BLOCK), _t5_bucket_np(d), -1)
    bucket = np.where(k >= 2 * BLOCK + PAD_FRONT, N_BUCKETS - 1, bucket)
    return np.ascontiguousarray(np.broadcast_to(bucket, (SWA_KEYS, BLOCK))).astype(np.int32)


def _swa_body(bound_ref, table_ref, sink_ref, bucket_ref, q_ref, k_ref, vt_ref, o_ref, bias_ref):
    g = pl.program_id(1)
    t = pl.program_id(2)
    cols = SWA_GROUP * BLOCK
    shift = bound_ref[1]

    @pl.when((pl.program_id(0) == 0) & (t == 0))
    def _():
        bucket = bucket_ref[...]
        slot = lax.broadcasted_iota(jnp.int32, (SWA_KEYS, BLOCK), 0)
        qpos = lax.broadcasted_iota(jnp.int32, (SWA_KEYS, BLOCK), 1)
        for r in range(SWA_GROUP):
            bias = jnp.full((SWA_KEYS, BLOCK), NEG, F32)
            for bkt in range(N_BUCKETS):
                bias = jnp.where(bucket == bkt,
                                 table_ref[bkt, g * SWA_GROUP + r] * LOG2E - shift, bias)
            bias_ref[g, 2, :, r * BLOCK:(r + 1) * BLOCK] = bias
            for n in range(2):
                band_ok = (slot >= PAD_FRONT - (n - 1) * BLOCK) & (slot < 2 * BLOCK)
                meta_ok = ((slot >= 2 * BLOCK + PAD_FRONT)
                           & (n * BLOCK + qpos - (slot - 2 * BLOCK) >= WINDOW))
                bias_ref[g, n, :, r * BLOCK:(r + 1) * BLOCK] = jnp.where(band_ok | meta_ok, bias, NEG)

    sink = jnp.concatenate(
        [jnp.full((1, BLOCK), sink_ref[g * SWA_GROUP + r] * LOG2E, F32) for r in range(SWA_GROUP)],
        axis=1)
    grp_rows = lax.broadcasted_iota(jnp.int32, (SWA_KVW, cols), 0) // HEAD_DIM
    v_rows = pl.ds(pl.multiple_of(g * HEAD_DIM, HEAD_DIM), HEAD_DIM)
    k_lead = k_ref[0:BLOCK, :]
    v_lead = vt_ref[v_rows, 0:BLOCK]

    n_blocks = q_ref.shape[1] // BLOCK

    def band(i):
        n = t * n_blocks + i
        return (n, pl.multiple_of(jnp.maximum(n - 1, 0) * BLOCK, BLOCK),
                pl.multiple_of(n * BLOCK, BLOCK))

    def scores(i):
        n, prev, cur = band(i)
        keys = jnp.concatenate(
            [k_ref[pl.ds(prev, BLOCK), :], k_ref[pl.ds(cur, BLOCK), :], k_lead], axis=0)
        q4 = jnp.concatenate(
            [q_ref[r * HEAD_DIM:(r + 1) * HEAD_DIM, i * BLOCK:(i + 1) * BLOCK]
             for r in range(SWA_GROUP)], axis=1)
        q8 = jnp.concatenate([q4, q4], axis=0)
        q8 = jnp.where(grp_rows == g, q8, jnp.zeros_like(q8))
        return _dot(keys, q8) + bias_ref[g, jnp.minimum(n, 2)]

    def finish(i, s, shifted):
        _, prev, cur = band(i)
        vals = jnp.concatenate(
            [vt_ref[v_rows, pl.ds(prev, BLOCK)], vt_ref[v_rows, pl.ds(cur, BLOCK)], v_lead], axis=1)
        vals = jnp.concatenate([vals, jnp.ones((V_ROWS - HEAD_DIM, SWA_KEYS), BF16)], axis=0)
        mx = shift if shifted else jnp.maximum(jnp.max(s, axis=0, keepdims=True), sink)
        pv = _dot(vals, jnp.exp2(s if shifted else s - mx).astype(BF16))
        denom = pv[HEAD_DIM:HEAD_DIM + 1] + jnp.exp2(sink - mx)
        out = pv[:HEAD_DIM] * (1.0 / denom)
        for r in range(SWA_GROUP):
            o_ref[r * HEAD_DIM:(r + 1) * HEAD_DIM, i * BLOCK:(i + 1) * BLOCK] = (
                out[:, r * BLOCK:(r + 1) * BLOCK].astype(BF16))

    def all_blocks(shifted):
        s = scores(0)
        for i in range(n_blocks):
            s_next = scores(i + 1) if i + 1 < n_blocks else None
            finish(i, s, shifted)
            s = s_next

    @pl.when(bound_ref[0] > 0.5)
    def _():
        all_blocks(True)

    @pl.when(bound_ref[0] <= 0.5)
    def _():
        all_blocks(False)


def _swa_logit_bound(q_gain, k_gain, table, sinks):
    b_qk = math.sqrt(HEAD_DIM) * LOG2E * jnp.max(jnp.abs(q_gain)) * jnp.max(jnp.abs(k_gain))
    top = b_qk + LOG2E * jnp.max(jnp.abs(table))
    shift = jnp.maximum(top, LOG2E * jnp.max(sinks))
    ok = (top + shift < FOX_MAX_SPREAD) & (shift - LOG2E * jnp.min(sinks) < FOX_MAX_SPREAD)
    return jnp.stack([ok.astype(F32), jnp.where(ok, shift, 0.0).astype(F32)])


def _swa(bound, table, sinks, qst, ks, vst, batch):
    rows = ks.shape[0]
    p = rows // batch
    n_t = SWA_STEPS
    swa_tile = p // n_t
    width = SWA_GROUP * HEAD_DIM
    bucket = jnp.asarray(_swa_static_buckets())
    smem = pl.BlockSpec(memory_space=pltpu.SMEM)
    tile = pl.BlockSpec((width, swa_tile), lambda b, g, t: (g, b * n_t + t))
    return pl.pallas_call(
        _swa_body,
        grid=(batch, SWA_KV_HEADS, n_t),
        in_specs=[smem, smem, smem, _const_spec((SWA_KEYS, BLOCK)), tile,
                  pl.BlockSpec((p, SWA_KVW), lambda b, g, t: (b, 0)),
                  pl.BlockSpec((SWA_KVW, p), lambda b, g, t: (0, b))],
        out_specs=tile,
        out_shape=jax.ShapeDtypeStruct((SWA_QW, rows), BF16),
        scratch_shapes=[pltpu.VMEM((SWA_KV_HEADS, 3, SWA_KEYS, SWA_GROUP * BLOCK), F32)],
        compiler_params=_params("arbitrary", "arbitrary", "arbitrary"),
        name="swa",
    )(bound, table, sinks, bucket, qst, ks, vst)


def _mix_body(h_ref, of_ref, os_ref, ga_ref, gb_ref, wf_ref, ws_ref, wo_ref, o_ref):
    y = (ga_ref[...].astype(F32) * _dot(wf_ref[...], of_ref[...])
         + gb_ref[...].astype(F32) * _dot(ws_ref[...], os_ref[...]))
    o_ref[...] = h_ref[...] + _dot_tn(y.astype(BF16), wo_ref[...])


def _mix_out(h2, o_fox, o_swa, ga, gb, wf_t, ws_t, wo):
    rows, d = h2.shape
    tm = next(t for t in MIX_TILES if rows % t == 0)

    def feat(r):
        return pl.BlockSpec((r, tm), lambda i: (0, i))

    tok = pl.BlockSpec((tm, d), lambda i: (i, 0))
    return pl.pallas_call(
        _mix_body,
        grid=(rows // tm,),
        in_specs=[tok, feat(FOX_W), feat(SWA_QW), feat(d), feat(d),
                  _const_spec(wf_t.shape), _const_spec(ws_t.shape), _const_spec(wo.shape)],
        out_specs=tok,
        out_shape=jax.ShapeDtypeStruct((rows, d), F32),
        compiler_params=_params("arbitrary"),
        name="mix_out",
    )(h2, o_fox, o_swa, ga, gb, wf_t, ws_t, wo)


def _w_in_body(w_ref, wf_ref, wk_ref, *, d_model):
    src = np.cumsum([0, FOX_W, FOX_W, FOX_W, FOX_HEADS, SWA_QW, SWA_KVW, SWA_KVW, d_model])
    qa, ka, va, fa, qb, kb, vb, ga, gb = (int(s) for s in src)

    def move(dst, lo, n):
        wf_ref[dst:dst + n] = w_ref[lo:lo + n].astype(BF16)

    move(_R_QA, qa, FOX_W)
    move(_R_VA, va, FOX_W)
    move(_R_QB, qb, SWA_QW)
    move(_R_VB, vb, SWA_KVW)
    move(_R_GA, ga, d_model)
    move(_R_GA + d_model, gb, d_model)
    wf_ref[_R_FA:_R_FA + FA_ROWS] = jnp.concatenate(
        [w_ref[fa:fa + FOX_HEADS], jnp.zeros((FA_ROWS - FOX_HEADS, w_ref.shape[1]), F32)],
        axis=0).astype(BF16)
    wk_ref[:, 0:FOX_W] = w_ref[ka:ka + FOX_W].T.astype(BF16)
    wk_ref[:, FOX_W:FOX_W + SWA_KVW] = w_ref[kb:kb + SWA_KVW].T.astype(BF16)


def _prep_w_in(w_in, layer):
    w_t = jnp.swapaxes(w_in, 1, 2)
    _, d_in, d = w_t.shape
    n_feat = _R_GA + 2 * d
    return pl.pallas_call(
        functools.partial(_w_in_body, d_model=d),
        grid=(1,),
        in_specs=[pl.BlockSpec((None, d_in, d), lambda i: (layer, 0, 0), pipeline_mode=pl.Buffered(1))],
        out_specs=[pl.BlockSpec((n_feat, d), lambda i: (0, 0)),
                   pl.BlockSpec((d, FOX_W + SWA_KVW), lambda i: (0, 0))],
        out_shape=[jax.ShapeDtypeStruct((n_feat, d), BF16),
                   jax.ShapeDtypeStruct((d, FOX_W + SWA_KVW), BF16)],
        compiler_params=_params("arbitrary"),
        name="prep_w_in",
    )(w_t)


def kernel(x, meta_tokens, rel_bias_table, ffn1_norm, ffn1_w_in, ffn1_w_out, mix_norm, w_in, forget_bias, fox_q_norm, fox_k_norm, swa_q_norm, swa_k_norm, swa_sinks, w_branch_fox, w_branch_swa, w_out, ffn2_norm, ffn2_w_in, ffn2_w_out):
    b, seq, d = x.shape
    depth = w_in.shape[0]
    p = BLOCK + seq
    assert seq % FOX_TILE == 0 and p % (SWA_STEPS * BLOCK) == 0 and (b * p) % ROW_TILE == 0
    lead = jnp.concatenate([jnp.zeros((PAD_FRONT, d), x.dtype), meta_tokens.astype(x.dtype)], axis=0)
    h = x.reshape(b * seq, d)
    table = rel_bias_table.astype(F32)

    def col(v):
        return v.astype(F32)[:, None]

    def twice(v):
        return jnp.tile(v.astype(F32), 2)[None]

    for l in range(depth):
        h = _ffn(h, ffn1_norm[l][None], ffn1_w_in, ffn1_w_out, l,
                 lead=(lead, b, seq) if l == 0 else None)
        w_feat, w_keys = _prep_w_in(w_in, l)
        fb = jnp.pad(col(forget_bias[l]), ((0, FA_ROWS - FOX_HEADS), (0, 0)))
        bounded, shift = _fox_logit_bound(fox_q_norm[l], fox_k_norm[l])
        qft, kf, vt, qst, ks, vst, ga, gb = _proj(
            shift, h, p, mix_norm[l][None], w_feat, w_keys, fb,
            col(fox_q_norm[l]), twice(fox_k_norm[l]), col(swa_q_norm[l]), twice(swa_k_norm[l]))
        o_fox = _fox(bounded, qft, kf, vt, b)
        sinks = swa_sinks[l].astype(F32)
        o_swa = _swa(_swa_logit_bound(swa_q_norm[l], swa_k_norm[l], table, sinks),
                     table, sinks, qst, ks, vst, b)
        h = _mix_out(h, o_fox, o_swa, ga, gb, w_branch_fox[l].astype(BF16).T,
                     w_branch_swa[l].astype(BF16).T, w_out[l].astype(BF16))
        last = l == depth - 1
        h = _ffn(h, ffn2_norm[l][None], ffn2_w_in, ffn2_w_out, l, (b, seq) if last else None)
    return h
```
